```python
import math
import jax, jax.numpy as jnp
from jax import lax
import numpy as np

D_MODEL = 1024
BATCH = 8
SEQ = 4096
DEPTH = 2

HEAD_DIM = 64
MOBA_HEADS = D_MODEL // (2 * HEAD_DIM)
MOBA_BLOCK = 256
MOBA_TOPK = 3
NSA_HEADS = D_MODEL // (2 * HEAD_DIM)
NSA_KV_GROUPS = 2
NSA_HEADS_PER_GROUP = NSA_HEADS // NSA_KV_GROUPS
NSA_CMP_BLOCK = 32
NSA_CMP_STRIDE = 16
NSA_CMP_HIDDEN = 256
NSA_SLC_BLOCK = 64
NSA_TOPN = 16
NSA_WINDOW = 512
NSA_FORCE_SCORE = 1e6
FOX_HEADS = D_MODEL // HEAD_DIM
D_FF = 4 * D_MODEL
REL_BUCKETS = 32
REL_MAX_DISTANCE = 1024
QUERY_BLOCK = 128
SEQ_ALIGN = MOBA_BLOCK
RMS_EPS = 1e-5
NEG_INF = -1e30

MOBA_W = MOBA_HEADS * HEAD_DIM
NSA_W = NSA_HEADS * HEAD_DIM
NSA_KV_W = NSA_KV_GROUPS * HEAD_DIM
EVEN_SPLITS = (MOBA_W, MOBA_W, MOBA_W, NSA_W) + (NSA_KV_W,) * 6 + (3 * NSA_HEADS,)
EVEN_IN = sum(EVEN_SPLITS)
FOX_W = FOX_HEADS * HEAD_DIM
ODD_SPLITS = (FOX_W, FOX_W, FOX_W, FOX_HEADS)
ODD_IN = sum(ODD_SPLITS)
N_BIAS_HEADS = MOBA_HEADS + NSA_HEADS

kernel_name = 'moba_nsa_fox_hybrid_trunk'


def _split(t, sizes):
    offs = [int(o) for o in np.cumsum(sizes)[:-1]]
    return jnp.split(t, offs, axis=-1)


def rmsnorm(x, g):
    xf = x.astype(jnp.float32)
    y = xf * lax.rsqrt(jnp.mean(xf * xf, axis=-1, keepdims=True) + RMS_EPS)
    return (y * g.astype(jnp.float32)).astype(x.dtype)


def masked_softmax(logits, mask):
    logits = jnp.where(mask, logits.astype(jnp.float32), NEG_INF)
    p = jax.nn.softmax(logits, axis=-1)
    return p * jnp.any(mask, axis=-1, keepdims=True)


def rel_bucket(dist):
    n = jnp.maximum(dist, 0)
    max_exact = REL_BUCKETS // 2
    nf = jnp.maximum(n, 1).astype(jnp.float32)
    large = max_exact + (jnp.log(nf / max_exact) / math.log(REL_MAX_DISTANCE / max_exact)
                         * (REL_BUCKETS - max_exact)).astype(jnp.int32)
    large = jnp.minimum(large, REL_BUCKETS - 1)
    return jnp.where(n < max_exact, n, large)


def moba_nsa_mixer(h, w_in, w_out, rel_bias, cmp_pos_k, cmp_pos_v,
                   cmp_k_w1, cmp_k_w2, cmp_v_w1, cmp_v_w2):
    B, S, _ = h.shape
    S_pad = -(-S // SEQ_ALIGN) * SEQ_ALIGN
    hp = jnp.pad(h, ((0, 0), (0, S_pad - S), (0, 0)))
    mq, mk, mv, nq, kc, vc, ksl, vsl, kwn, vwn, gz = _split(hp @ w_in, EVEN_SPLITS)
    scale = HEAD_DIM ** -0.5
    Q = QUERY_BLOCK
    G, J = NSA_KV_GROUPS, NSA_HEADS_PER_GROUP
    n_qc = S_pad // Q
    f32 = jnp.float32

    n_mb = S_pad // MOBA_BLOCK
    k_moba = min(MOBA_TOPK, n_mb)
    mq = mq.reshape(B, S_pad, MOBA_HEADS, HEAD_DIM).transpose(0, 2, 1, 3)
    mk = mk.reshape(B, n_mb, MOBA_BLOCK, MOBA_HEADS, HEAD_DIM).transpose(0, 3, 1, 2, 4)
    mv = mv.reshape(B, n_mb, MOBA_BLOCK, MOBA_HEADS, HEAD_DIM).transpose(0, 3, 1, 2, 4)
    k_mean = jnp.mean(mk, axis=3)

    nq = nq.reshape(B, S_pad, NSA_HEADS, HEAD_DIM).transpose(0, 2, 1, 3)
    n_cmp = (S_pad - NSA_CMP_BLOCK) // NSA_CMP_STRIDE + 1
    cmp_idx = np.arange(n_cmp)[:, None] * NSA_CMP_STRIDE + np.arange(NSA_CMP_BLOCK)[None, :]
    cmp_end = jnp.asarray(cmp_idx[:, -1])

    def compress(raw, pos, w1, w2):
        blocks = raw.reshape(B, S_pad, G, HEAD_DIM)[:, cmp_idx] + pos[:, None, :]
        blocks = blocks.transpose(0, 3, 1, 2, 4).reshape(B, G, n_cmp, NSA_CMP_BLOCK * HEAD_DIM)
        return jax.nn.silu(blocks @ w1) @ w2

    k_cmp = compress(kc, cmp_pos_k, cmp_k_w1, cmp_k_w2)
    v_cmp = compress(vc, cmp_pos_v, cmp_v_w1, cmp_v_w2)
    n_sb = S_pad // NSA_SLC_BLOCK
    n_sel = min(NSA_TOPN, n_sb)
    k_slc = ksl.reshape(B, n_sb, NSA_SLC_BLOCK, G, HEAD_DIM).transpose(0, 3, 1, 2, 4)
    v_slc = vsl.reshape(B, n_sb, NSA_SLC_BLOCK, G, HEAD_DIM).transpose(0, 3, 1, 2, 4)
    pad_w = ((0, 0), (0, 0), (NSA_WINDOW, 0), (0, 0))
    k_win = jnp.pad(kwn.reshape(B, S_pad, G, HEAD_DIM).transpose(0, 2, 1, 3), pad_w)
    v_win = jnp.pad(vwn.reshape(B, S_pad, G, HEAD_DIM).transpose(0, 2, 1, 3), pad_w)
    gates = jax.nn.sigmoid(gz.reshape(B, S_pad, NSA_HEADS, 3).transpose(0, 2, 1, 3))
    ci = np.arange(n_cmp)[:, None] * NSA_CMP_STRIDE
    sj = np.arange(n_sb)[None, :] * NSA_SLC_BLOCK
    overlap = jnp.asarray(((ci < sj + NSA_SLC_BLOCK) & (ci + NSA_CMP_BLOCK > sj)).astype(np.float32))

    table = rel_bias.T
    tb_moba = table[:MOBA_HEADS]
    tb_nsa = table[MOBA_HEADS:].reshape(G, J, REL_BUCKETS)
    h_m = jnp.arange(MOBA_HEADS)[:, None, None]
    g_i = jnp.arange(G)[:, None, None]
    g5 = jnp.arange(G)[:, None, None, None, None]
    j5 = jnp.arange(J)[None, :, None, None, None]

    def query_block(bc):
        b, c = bc
        q0 = c * Q
        t = q0 + jnp.arange(Q)
        qm = lax.dynamic_slice_in_dim(mq[b], q0, Q, axis=1)
        mk_b, mv_b = mk[b], mv[b]
        blk = q0 // MOBA_BLOCK
        route = jnp.einsum('hqd,hnd->hqn', qm, k_mean[b]).astype(f32)
        route = jnp.where(jnp.arange(n_mb) < blk, route, NEG_INF)
        _, sel = lax.top_k(route, k_moba)
        sel_ok = sel < blk
        k_sel = mk_b[h_m, sel]
        v_sel = mv_b[h_m, sel]
        k_own = lax.dynamic_index_in_dim(mk_b, blk, axis=1, keepdims=False)
        v_own = lax.dynamic_index_in_dim(mv_b, blk, axis=1, keepdims=False)
        d_sel = t[:, None, None] - (sel[..., None] * MOBA_BLOCK + jnp.arange(MOBA_BLOCK))
        d_own = t[:, None] - (blk * MOBA_BLOCK + jnp.arange(MOBA_BLOCK))[None, :]
        s_sel = (jnp.einsum('hqd,hqnkd->hqnk', qm, k_sel).astype(f32) * scale
                 + tb_moba[h_m[..., None], rel_bucket(d_sel)])
        s_own = (jnp.einsum('hqd,hkd->hqk', qm, k_own).astype(f32) * scale
                 + tb_moba[:, rel_bucket(d_own)])
        n_k = k_moba * MOBA_BLOCK
        m_sel = jnp.broadcast_to(sel_ok[..., None], d_sel.shape).reshape(MOBA_HEADS, Q, n_k)
        m_own = jnp.broadcast_to((d_own >= 0)[None], s_own.shape)
        p = masked_softmax(jnp.concatenate([s_sel.reshape(MOBA_HEADS, Q, n_k), s_own], axis=-1),
                           jnp.concatenate([m_sel, m_own], axis=-1))
        o_a = (jnp.einsum('hqnk,hqnkd->hqd', p[..., :n_k].reshape(MOBA_HEADS, Q, k_moba, MOBA_BLOCK), v_sel)
               + jnp.einsum('hqk,hkd->hqd', p[..., n_k:], v_own))
        qn = lax.dynamic_slice_in_dim(nq[b], q0, Q, axis=1).reshape(G, J, Q, HEAD_DIM)
        d_cmp = t[:, None] - cmp_end[None, :]
        s_cmp = (jnp.einsum('gjqd,gnd->gjqn', qn, k_cmp[b]).astype(f32) * scale
                 + tb_nsa[:, :, rel_bucket(d_cmp)])
        p_cmp = masked_softmax(s_cmp, (d_cmp >= 0)[None, None])
        o_cmp = jnp.einsum('gjqn,gnd->gjqd', p_cmp, v_cmp[b])
        imp = jnp.einsum('gjqn,nm->gqm', p_cmp, overlap)
        s_blk = t // NSA_SLC_BLOCK
        jb = jnp.arange(n_sb)[None, :]
        forced = (jb == 0) | (jb == s_blk[:, None]) | (jb == s_blk[:, None] - 1)
        imp = jnp.where(forced, imp + NSA_FORCE_SCORE, jnp.where(jb <= s_blk[:, None], imp, NEG_INF))
        _, sidx = lax.top_k(imp, n_sel)
        k_s = k_slc[b][g_i, sidx]
        v_s = v_slc[b][g_i, sidx]
        d_s = t[:, None, None] - (sidx[..., None] * NSA_SLC_BLOCK + jnp.arange(NSA_SLC_BLOCK))
        m_s = (d_s >= 0) & (sidx <= s_blk[:, None])[..., None]
        s_s = (jnp.einsum('gjqd,gqnkd->gjqnk', qn, k_s).astype(f32) * scale
               + tb_nsa[g5, j5, rel_bucket(d_s)[:, None]])
        n_ks = n_sel * NSA_SLC_BLOCK
        p_s = masked_softmax(s_s.reshape(G, J, Q, n_ks), m_s.reshape(G, 1, Q, n_ks))
        o_slc = jnp.einsum('gjqnk,gqnkd->gjqd', p_s.reshape(G, J, Q, n_sel, NSA_SLC_BLOCK), v_s)
        k_w = lax.dynamic_slice_in_dim(k_win[b], q0, Q + NSA_WINDOW, axis=1)
        v_w = lax.dynamic_slice_in_dim(v_win[b], q0, Q + NSA_WINDOW, axis=1)
        pos_w = q0 - NSA_WINDOW + jnp.arange(Q + NSA_WINDOW)
        d_w = t[:, None] - pos_w[None, :]
        m_w = (d_w >= 0) & (d_w < NSA_WINDOW) & (pos_w >= 0)[None, :]
        s_w = (jnp.einsum('gjqd,gkd->gjqk', qn, k_w).astype(f32) * scale
               + tb_nsa[:, :, rel_bucket(d_w)])
        p_w = masked_softmax(s_w, m_w[None, None])
        o_win = jnp.einsum('gjqk,gkd->gjqd', p_w, v_w)
        g = lax.dynamic_slice_in_dim(gates[b], q0, Q, axis=1).astype(f32)
        o_b = (g[..., 0:1] * o_cmp.reshape(NSA_HEADS, Q, HEAD_DIM)
               + g[..., 1:2] * o_slc.reshape(NSA_HEADS, Q, HEAD_DIM)
               + g[..., 2:3] * o_win.reshape(NSA_HEADS, Q, HEAD_DIM))
        return jnp.concatenate([o_a.astype(f32), o_b.astype(f32)], axis=0)

    b_idx = jnp.repeat(jnp.arange(B), n_qc)
    c_idx = jnp.tile(jnp.arange(n_qc), B)
    o = lax.map(query_block, (b_idx, c_idx))
    o = o.reshape(B, n_qc, MOBA_HEADS + NSA_HEADS, Q, HEAD_DIM).transpose(0, 1, 3, 2, 4)
    o = o.reshape(B, S_pad, MOBA_W + NSA_W)[:, :S]
    return o.astype(h.dtype) @ w_out


def fox_mixer(h, w_in, b_forget, w_out):
    B, S, _ = h.shape
    Q = QUERY_BLOCK
    scale = HEAD_DIM ** -0.5
    q, k, v, fz = _split(h @ w_in, ODD_SPLITS)
    q = q.reshape(B, S, FOX_HEADS, HEAD_DIM).transpose(0, 2, 1, 3)
    k = k.reshape(B, S, FOX_HEADS, HEAD_DIM).transpose(0, 2, 1, 3)
    v = v.reshape(B, S, FOX_HEADS, HEAD_DIM).transpose(0, 2, 1, 3)
    log_f = jax.nn.log_sigmoid((fz + b_forget).astype(jnp.float32))
    cum = jnp.cumsum(log_f, axis=1).transpose(0, 2, 1)
    key_pos = jnp.arange(S)

    def block(c):
        q0 = c * Q
        qc = lax.dynamic_slice_in_dim(q, q0, Q, axis=2)
        cq = lax.dynamic_slice_in_dim(cum, q0, Q, axis=2)
        t = q0 + jnp.arange(Q)
        s = (jnp.einsum('bhqd,bhkd->bhqk', qc, k).astype(jnp.float32) * scale
             + cq[..., None] - cum[:, :, None, :])
        p = masked_softmax(s, key_pos[None, :] <= t[:, None])
        return jnp.einsum('bhqk,bhkd->bhqd', p, v).astype(jnp.float32)

    o = lax.map(block, jnp.arange(S // Q))
    o = o.transpose(1, 0, 3, 2, 4).reshape(B, S, FOX_W)
    return o.astype(h.dtype) @ w_out


def sqrelu_mlp(h, w1, w2):
    a = jax.nn.relu(h @ w1)
    return (a * a) @ w2


def setup_inputs(seed: int = 0) -> dict:
    key = jax.random.key(seed)
    ks = jax.random.split(key, 20)
    f32 = jnp.float32

    def nrm(k, shape, scale):
        return jax.random.normal(k, shape, f32) * scale

    n_even = (DEPTH + 1) // 2
    n_odd = DEPTH // 2
    cmp_in = NSA_CMP_BLOCK * HEAD_DIM
    return {
        'x': nrm(ks[0], (BATCH, SEQ, D_MODEL), 1.0),
        'rel_bias': nrm(ks[1], (REL_BUCKETS, N_BIAS_HEADS), 0.5),
        'mix_norm': 1.0 + nrm(ks[2], (DEPTH, D_MODEL), 0.02),
        'mlp_norm': 1.0 + nrm(ks[3], (DEPTH, D_MODEL), 0.02),
        'even_w_in': nrm(ks[4], (n_even, D_MODEL, EVEN_IN), D_MODEL ** -0.5),
        'even_w_out': nrm(ks[5], (n_even, MOBA_W + NSA_W, D_MODEL), (MOBA_W + NSA_W) ** -0.5),
        'cmp_pos_k': nrm(ks[6], (n_even, NSA_CMP_BLOCK, HEAD_DIM), 0.2),
        'cmp_pos_v': nrm(ks[7], (n_even, NSA_CMP_BLOCK, HEAD_DIM), 0.2),
        'cmp_k_w1': nrm(ks[8], (n_even, cmp_in, NSA_CMP_HIDDEN), cmp_in ** -0.5),
        'cmp_k_w2': nrm(ks[9], (n_even, NSA_CMP_HIDDEN, HEAD_DIM), NSA_CMP_HIDDEN ** -0.5),
        'cmp_v_w1': nrm(ks[10], (n_even, cmp_in, NSA_CMP_HIDDEN), cmp_in ** -0.5),
        'cmp_v_w2': nrm(ks[11], (n_even, NSA_CMP_HIDDEN, HEAD_DIM), NSA_CMP_HIDDEN ** -0.5),
        'odd_w_in': nrm(ks[12], (n_odd, D_MODEL, ODD_IN), D_MODEL ** -0.5),
        'odd_b_forget': 3.0 + nrm(ks[13], (n_odd, FOX_HEADS), 0.5),
        'odd_w_out': nrm(ks[14], (n_odd, FOX_W, D_MODEL), FOX_W ** -0.5),
        'mlp_w1': nrm(ks[15], (DEPTH, D_MODEL, D_FF), D_MODEL ** -0.5),
        'mlp_w2': nrm(ks[16], (DEPTH, D_FF, D_MODEL), D_FF ** -0.5),
        'final_norm': 1.0 + nrm(ks[17], (D_MODEL,), 0.02),
    }


def reference(x, rel_bias, mix_norm, mlp_norm, even_w_in, even_w_out, cmp_pos_k, cmp_pos_v,
              cmp_k_w1, cmp_k_w2, cmp_v_w1, cmp_v_w2, odd_w_in, odd_b_forget, odd_w_out,
              mlp_w1, mlp_w2, final_norm):
    h = x
    for layer in range(DEPTH):
        hn = rmsnorm(h, mix_norm[layer])
        i = layer // 2
        if layer % 2 == 0:
            h = h + moba_nsa_mixer(hn, even_w_in[i], even_w_out[i], rel_bias,
                                   cmp_pos_k[i], cmp_pos_v[i], cmp_k_w1[i], cmp_k_w2[i],
                                   cmp_v_w1[i], cmp_v_w2[i])
        else:
            h = h + fox_mixer(hn, odd_w_in[i], odd_b_forget[i], odd_w_out[i])
        h = h + sqrelu_mlp(rmsnorm(h, mlp_norm[layer]), mlp_w1[layer], mlp_w2[layer])
    return rmsnorm(h, final_norm)
```

```python
import functools
import math

import numpy as np
import jax
import jax.numpy as jnp
from jax import lax
from jax.experimental import pallas as pl
from jax.experimental.pallas import tpu as pltpu

D_MODEL = 1024
HEAD_DIM = 64
MOBA_HEADS = 8
MOBA_BLOCK = 256
MOBA_TOPK = 3
NSA_HEADS = 8
NSA_KV_GROUPS = 2
NSA_HPG = NSA_HEADS // NSA_KV_GROUPS
NSA_CMP_BLOCK = 32
NSA_CMP_STRIDE = 16
NSA_CMP_HIDDEN = 256
NSA_SLC_BLOCK = 64
NSA_TOPN = 16
NSA_WINDOW = 512
NSA_FORCE_SCORE = 1e6
FOX_HEADS = 16
D_FF = 4 * D_MODEL
REL_BUCKETS = 32
REL_MAX_DISTANCE = 1024
RMS_EPS = 1e-5
NEG_INF = -1e30
SCALE = HEAD_DIM ** -0.5

MOBA_W = MOBA_HEADS * HEAD_DIM
NSA_W = NSA_HEADS * HEAD_DIM
NSA_KV_W = NSA_KV_GROUPS * HEAD_DIM
FOX_W = FOX_HEADS * HEAD_DIM

LANES = 128
TQ = 128
TK = 256
TM = 512
CH = 256
FF_CH = 512
VMEM_LIMIT = 56 * 1024 * 1024
NE_BIAS = -(-(REL_MAX_DISTANCE + TK - 1) // TQ)
NE_WIN = -(-(NSA_WINDOW + TK - 1) // TQ)
GZ_ROWS = 16

assert TK == 2 * TQ and MOBA_BLOCK == TK and TK % NSA_SLC_BLOCK == 0

F32 = jnp.float32
BF16 = jnp.bfloat16


def _dot(a, b):
    return jnp.dot(a, b, preferred_element_type=F32)


def _dot_nt(a, b):
    return lax.dot_general(a, b, (((1,), (1,)), ((), ())), preferred_element_type=F32)


def _dot_tn(a, b):
    return lax.dot_general(a, b, (((0,), (0,)), ((), ())), preferred_element_type=F32)


def _rmsnorm(x, g):
    ms = jnp.mean(x * x, axis=-1, keepdims=True)
    return x * lax.rsqrt(ms + RMS_EPS) * g


def _split3(x):
    a = x.astype(BF16)
    r = x - a.astype(F32)
    b = r.astype(BF16)
    c = (r - b.astype(F32)).astype(BF16)
    return a, b, c


def _const_spec(shape):
    nd = len(shape)
    return pl.BlockSpec(shape, lambda *_: (0,) * nd, pipeline_mode=pl.Buffered(1))


def _params(sem):
    return pltpu.CompilerParams(dimension_semantics=sem, vmem_limit_bytes=VMEM_LIMIT)


def _softmax_update(s, vt, m, l, acc):
    m_new = jnp.maximum(m, jnp.max(s, axis=0, keepdims=True))
    alpha = jnp.exp(m - m_new)
    p = jnp.exp(s - m_new)
    l = alpha * l + jnp.sum(p, axis=0, keepdims=True)
    acc = alpha * acc + _dot(vt, p.astype(BF16))
    return m_new, l, acc


def _init_state(n_heads):
    return tuple((jnp.full((1, TQ), NEG_INF, F32), jnp.zeros((1, TQ), F32),
                  jnp.zeros((HEAD_DIM, TQ), F32)) for _ in range(n_heads))


def _rank_select(val, idx, n_rows, k):
    cnt = jnp.zeros(val.shape, F32)
    for m in range(n_rows):
        vm = val[m:m + 1, :]
        beats = (vm > val) | ((vm == val) & (idx > m))
        cnt = cnt + jnp.where(beats, 1.0, 0.0)
    return cnt < k


def _inproj0_body(x_ref, g_ref, wrm_ref, wfm_ref, wgz_ref, rm_ref, fm_ref, gz_ref):
    xn = _rmsnorm(x_ref[...], g_ref[...]).astype(BF16)
    for c0 in range(0, rm_ref.shape[-1], CH):
        rm_ref[:, c0:c0 + CH] = _dot(xn, wrm_ref[:, c0:c0 + CH]).astype(BF16)
    for r0 in range(0, fm_ref.shape[2], CH):
        res = _dot_nt(wfm_ref[r0:r0 + CH, :], xn).astype(BF16)
        for t in range(TM // TK):
            fm_ref[0, t, r0:r0 + CH, :] = res[:, t * TK:(t + 1) * TK]
    gz_ref[0] = _dot_nt(wgz_ref[...], xn)


def _inproj0(x2, g, wrm, wfm, wgz, B, S):
    M = B * S
    nst = S // TM
    n_rm, n_fm, n_gz = wrm.shape[1], wfm.shape[0], wgz.shape[0]
    return pl.pallas_call(
        _inproj0_body,
        grid=(M // TM,),
        in_specs=[
            pl.BlockSpec((TM, D_MODEL), lambda i: (i, 0)),
            _const_spec((1, D_MODEL)),
            _const_spec((D_MODEL, n_rm)),
            _const_spec((n_fm, D_MODEL)),
            _const_spec((n_gz, D_MODEL)),
        ],
        out_specs=[
            pl.BlockSpec((TM, n_rm), lambda i: (i, 0)),
            pl.BlockSpec((1, TM // TK, n_fm, TK), lambda i: (i // nst, i % nst, 0, 0)),
            pl.BlockSpec((1, n_gz, TM), lambda i: (i // nst, 0, i % nst)),
        ],
        out_shape=[
            jax.ShapeDtypeStruct((M, n_rm), BF16),
            jax.ShapeDtypeStruct((B, S // TK, n_fm, TK), BF16),
            jax.ShapeDtypeStruct((B, n_gz, S), F32),
        ],
        compiler_params=_params(("parallel",)),
        name="inproj0",
    )(x2, g, wrm, wfm, wgz)


def _compress_body(rk_ref, rv_ref, pos_ref, w1_ref, w2k_ref, w2vt_ref, kc_ref, vct_ref):
    half = NSA_CMP_STRIDE * HEAD_DIM

    def hidden(r_ref, s):
        r = r_ref[0, 0]
        a = _dot(r, w1_ref[s, :half, :])
        b = _dot(r, w1_ref[s, half:, :])
        nxt = pltpu.roll(b, b.shape[0] - 1, axis=0)
        posb = _dot(pos_ref[s], w1_ref[s])[0:1]
        pre = a + nxt + posb
        return (pre * jax.nn.sigmoid(pre)).astype(BF16)

    kc_ref[0, 0] = _dot(hidden(rk_ref, 0), w2k_ref[...]).astype(BF16)
    vct_ref[0, 0] = _dot_nt(w2vt_ref[...], hidden(rv_ref, 1)).astype(BF16)


def _compress(r, pos, w1, w2k, w2vt, B, NC):
    G = NSA_KV_GROUPS
    half = NSA_CMP_STRIDE * HEAD_DIM
    return pl.pallas_call(
        _compress_body,
        grid=(B, G),
        in_specs=[
            pl.BlockSpec((1, 1, NC, half), lambda b, g: (b, g, 0, 0)),
            pl.BlockSpec((1, 1, NC, half), lambda b, g: (b, G + g, 0, 0)),
            _const_spec(pos.shape),
            _const_spec(w1.shape),
            _const_spec(w2k.shape),
            _const_spec(w2vt.shape),
        ],
        out_specs=[
            pl.BlockSpec((1, 1, NC, HEAD_DIM), lambda b, g: (b, g, 0, 0)),
            pl.BlockSpec((1, 1, HEAD_DIM, NC), lambda b, g: (b, g, 0, 0)),
        ],
        out_shape=[
            jax.ShapeDtypeStruct((B, G, NC, HEAD_DIM), BF16),
            jax.ShapeDtypeStruct((B, G, HEAD_DIM, NC), BF16),
        ],
        compiler_params=_params(("parallel", "parallel")),
        name="nsa_compress",
    )(r, r, pos, w1, w2k, w2vt)


def _moba_body(q_ref, k_ref, v_ref, t_ref, o_ref, kmean_ref, mask_ref, *, n_mb, topk):
    c = pl.program_id(2)
    blk = c // (TK // TQ)

    @pl.when(c == 0)
    def _():
        kmean_ref[...] = jnp.zeros_like(kmean_ref)
        for n in range(n_mb):
            kblk = k_ref[0, n * TK:(n + 1) * TK, :].astype(F32)
            kmean_ref[n:n + 1, :] = jnp.mean(kblk, axis=0, keepdims=True)

    q = q_ref[0, 0]
    rowi = lax.broadcasted_iota(jnp.int32, q.shape, 0)
    nidx = lax.broadcasted_iota(jnp.int32, (kmean_ref.shape[0], TQ), 0)
    km = _split3(kmean_ref[...])
    qpads = []
    for hh in range(2):
        qh = jnp.where(rowi // HEAD_DIM == hh, q, jnp.zeros_like(q))
        route = _dot(km[0], qh) + _dot(km[1], qh) + _dot(km[2], qh)
        route = jnp.where(nidx < blk, route, NEG_INF)
        sel = _rank_select(route, nidx, n_mb, topk) & (nidx < blk)
        mask_ref[hh] = jnp.where(sel | (nidx == blk), 0.0, NEG_INF)
        qpads.append(qh * SCALE)

    def step(n, state, near):
        k0 = pl.multiple_of(n * TK, TK)
        kt = k_ref[0, pl.ds(k0, TK), :]
        vt = v_ref[0, n]
        out = []
        for hh in range(2):
            s = _dot(kt, qpads[hh]) + mask_ref[hh, pl.ds(n, 1), :]
            if near:
                s = s + t_ref[hh, c - 2 * n]
            out.append(_softmax_update(s, vt[hh * HEAD_DIM:(hh + 1) * HEAD_DIM, :], *state[hh]))
        return tuple(out)

    n_lo = jnp.maximum(c - (NE_BIAS - 2), 0) // 2
    state = lax.fori_loop(0, blk - n_lo + 1, lambda i, st: step(blk - i, st, True), _init_state(2))
    state = lax.fori_loop(0, n_lo, lambda n, st: step(n, st, False), state)
    o_ref[0] = jnp.concatenate([acc / l for (_, l, acc) in state], axis=0).astype(BF16)


def _moba(fm, rm, tab, B, S):
    n_mb = S // MOBA_BLOCK
    n_pad = -(-n_mb // 16) * 16
    ne = tab.shape[1]
    body = functools.partial(_moba_body, n_mb=n_mb, topk=min(MOBA_TOPK, n_mb))
    return pl.pallas_call(
        body,
        grid=(B, MOBA_HEADS // 2, S // TQ),
        in_specs=[
            pl.BlockSpec((1, 1, 2 * HEAD_DIM, TQ), lambda b, p, c: (b, c // 2, p, c % 2)),
            pl.BlockSpec((1, S, 2 * HEAD_DIM), lambda b, p, c: (b, 0, p)),
            pl.BlockSpec((1, S // TK, 2 * HEAD_DIM, TK), lambda b, p, c: (b, 0, MOBA_HEADS // 2 + p, 0)),
            pl.BlockSpec((2, ne, TK, TQ), lambda b, p, c: (p, 0, 0, 0)),
        ],
        out_specs=pl.BlockSpec((1, 2 * HEAD_DIM, TQ), lambda b, p, c: (b, p, c)),
        out_shape=jax.ShapeDtypeStruct((B, MOBA_W, S), BF16),
        scratch_shapes=[pltpu.VMEM((n_pad, 2 * HEAD_DIM), F32), pltpu.VMEM((2, n_pad, TQ), F32)],
        compiler_params=_params(("parallel", "parallel", "arbitrary")),
        name="moba_attn",
    )(fm, rm, fm, tab)


def _nsa_body(q_ref, kc_ref, vct_ref, tc_ref, ovl_ref, exp_ref, ksl_ref, vsl_ref, kwn_ref, vwn_ref,
              tslc_ref, twin_ref, gz_ref, o_ref, *, n_sb, n_sel):
    g = pl.program_id(1)
    c = pl.program_id(2)
    blk = c // (TK // TQ)
    J = NSA_HPG

    q = q_ref[0, 0] * SCALE
    rowi = lax.broadcasted_iota(jnp.int32, (2 * HEAD_DIM, TQ), 0)
    qs = [q[j * HEAD_DIM:(j + 1) * HEAD_DIM, :] for j in range(J)]
    qpads = [jnp.where(rowi // HEAD_DIM == g, jnp.concatenate([qj, qj], axis=0), jnp.zeros((2 * HEAD_DIM, TQ), BF16))
             for qj in qs]

    kc = kc_ref[0, 0]
    vct = vct_ref[0, 0]
    o_cmp = []
    psum = jnp.zeros((kc.shape[0], TQ), F32)
    for j in range(J):
        s = _dot(kc, qs[j]) + tc_ref[j, 0]
        m = jnp.max(s, axis=0, keepdims=True)
        p = jnp.exp(s - m)
        l = jnp.sum(p, axis=0, keepdims=True)
        pn = p * jnp.where(m > 0.5 * NEG_INF, 1.0 / l, 0.0)
        o_cmp.append(_dot(vct, pn.astype(BF16)))
        psum = psum + pn

    ph = psum.astype(BF16)
    plo = (psum - ph.astype(F32)).astype(BF16)
    imp = _dot(ovl_ref[...], ph) + _dot(ovl_ref[...], plo)
    jb = lax.broadcasted_iota(jnp.int32, imp.shape, 0)
    t = c * TQ + lax.broadcasted_iota(jnp.int32, imp.shape, 1)
    sb = t // NSA_SLC_BLOCK
    forced = (jb == 0) | (jb == sb) | (jb == sb - 1)
    allowed = jb <= sb
    val = jnp.where(forced, imp + NSA_FORCE_SCORE, jnp.where(allowed, imp, NEG_INF))
    sel = _rank_select(val, jb, n_sb, n_sel) & allowed
    selb = jnp.where(sel, 0.0, NEG_INF).astype(BF16)

    def slc_step(n, state, near):
        k0 = pl.multiple_of(n * TK, TK)
        kt = ksl_ref[0, pl.ds(k0, TK), :]
        vt = vsl_ref[0, n]
        mt = _dot(exp_ref[pl.ds(k0, TK), :], selb)
        out = []
        for j in range(J):
            s = _dot(kt, qpads[j]) + mt
            if near:
                s = s + tslc_ref[j, c - 2 * n]
            out.append(_softmax_update(s, vt, *state[j]))
        return tuple(out)

    n_lo = jnp.maximum(c - (NE_BIAS - 2), 0) // 2
    st = lax.fori_loop(0, blk - n_lo + 1, lambda i, s_: slc_step(blk - i, s_, True), _init_state(J))
    st = lax.fori_loop(0, n_lo, lambda n, s_: slc_step(n, s_, False), st)
    o_slc = [acc / l for (_, l, acc) in st]

    def win_step(i, state):
        n = blk - i
        k0 = pl.multiple_of(n * TK, TK)
        kt = kwn_ref[0, pl.ds(k0, TK), :]
        vt = vwn_ref[0, n]
        out = []
        for j in range(J):
            s = _dot(kt, qpads[j]) + twin_ref[j, c - 2 * n]
            out.append(_softmax_update(s, vt, *state[j]))
        return tuple(out)

    w_lo = jnp.maximum(c - (NE_WIN - 2), 0) // 2
    st = lax.fori_loop(0, blk - w_lo + 1, win_step, _init_state(J))
    o_win = [acc / l for (_, l, acc) in st]

    gate = jax.nn.sigmoid(gz_ref[0])
    outs = []
    for j in range(J):
        outs.append(gate[j:j + 1, :] * o_cmp[j] + gate[J + j:J + j + 1, :] * o_slc[j]
                    + gate[2 * J + j:2 * J + j + 1, :] * o_win[j])
    o_ref[0] = jnp.concatenate(outs, axis=0).astype(BF16)


def _nsa(fm, rm, gz, kc, vct, tcmp, ovl, expand, tslc, twin, B, S, col_ksl, col_kwn, row_q, row_vsl, row_vwn):
    G, J = NSA_KV_GROUPS, NSA_HPG
    NC = kc.shape[2]
    n_sb = S // NSA_SLC_BLOCK
    body = functools.partial(_nsa_body, n_sb=n_sb, n_sel=min(NSA_TOPN, n_sb))
    qrows = J * HEAD_DIM
    return pl.pallas_call(
        body,
        grid=(B, G, S // TQ),
        in_specs=[
            pl.BlockSpec((1, 1, qrows, TQ), lambda b, g, c: (b, c // 2, row_q // qrows + g, c % 2)),
            pl.BlockSpec((1, 1, NC, HEAD_DIM), lambda b, g, c: (b, g, 0, 0)),
            pl.BlockSpec((1, 1, HEAD_DIM, NC), lambda b, g, c: (b, g, 0, 0)),
            pl.BlockSpec((J, 1, NC, TQ), lambda b, g, c: (g, c, 0, 0)),
            _const_spec(ovl.shape),
            _const_spec(expand.shape),
            pl.BlockSpec((1, S, LANES), lambda b, g, c: (b, 0, col_ksl // LANES)),
            pl.BlockSpec((1, S // TK, HEAD_DIM, TK), lambda b, g, c: (b, 0, row_vsl // HEAD_DIM + g, 0)),
            pl.BlockSpec((1, S, LANES), lambda b, g, c: (b, 0, col_kwn // LANES)),
            pl.BlockSpec((1, S // TK, HEAD_DIM, TK), lambda b, g, c: (b, 0, row_vwn // HEAD_DIM + g, 0)),
            pl.BlockSpec((J, tslc.shape[1], TK, TQ), lambda b, g, c: (MOBA_HEADS // J + g, 0, 0, 0)),
            pl.BlockSpec((J, twin.shape[1], TK, TQ), lambda b, g, c: (g, 0, 0, 0)),
            pl.BlockSpec((1, GZ_ROWS, TQ), lambda b, g, c: (b, g, c)),
        ],
        out_specs=pl.BlockSpec((1, qrows, TQ), lambda b, g, c: (b, g, c)),
        out_shape=jax.ShapeDtypeStruct((B, NSA_W, S), BF16),
        compiler_params=_params(("parallel", "parallel", "arbitrary")),
        name="nsa_attn",
    )(fm, kc, vct, tcmp, ovl, expand, rm, fm, rm, fm, tslc, twin, gz)


def _inproj1_body(x_ref, g_ref, wfm_ref, wk_ref, wf_ref, bf_ref, tri_ref, place_ref, fm_ref, ka_ref, carry_ref,
                  *, nst):
    i = pl.program_id(0)

    @pl.when(i % nst == 0)
    def _():
        carry_ref[...] = jnp.zeros_like(carry_ref)

    xf = _rmsnorm(x_ref[...], g_ref[...])
    xn = xf.astype(BF16)
    xlo = (xf - xn.astype(F32)).astype(BF16)
    for r0 in range(0, fm_ref.shape[2], CH):
        res = _dot_nt(wfm_ref[r0:r0 + CH, :], xn).astype(BF16)
        for t in range(TM // TK):
            fm_ref[0, t, r0:r0 + CH, :] = res[:, t * TK:(t + 1) * TK]

    fz = _dot(xn, wf_ref[0]) + _dot(xlo, wf_ref[0]) + _dot(xn, wf_ref[1]) + bf_ref[...]
    logf = jnp.minimum(fz, 0.0) - jnp.log(1.0 + jnp.exp(-jnp.abs(fz)))
    tri = tri_ref[...]
    h1, h2, h3 = _split3(logf)
    cum = _dot(tri, h1) + _dot(tri, h2) + _dot(tri, h3) + carry_ref[0:1, :]
    carry_ref[...] = jnp.broadcast_to(cum[TM - 1:TM, :], carry_ref.shape)
    c1, c2, c3 = _split3(cum)
    for c0 in range(0, ka_ref.shape[-1], FF_CH):
        ka = (_dot(xn, wk_ref[:, c0:c0 + FF_CH]) + _dot(c1, place_ref[0, :, c0:c0 + FF_CH])
              + _dot(c2, place_ref[1, :, c0:c0 + FF_CH]) + _dot(c3, place_ref[2, :, c0:c0 + FF_CH]))
        ka_ref[:, c0:c0 + FF_CH] = ka.astype(BF16)


def _inproj1(x2, g, wfm, wk, wf, bf, tri, place, B, S):
    M = B * S
    nst = S // TM
    n_fm, n_ka = wfm.shape[0], wk.shape[1]
    return pl.pallas_call(
        functools.partial(_inproj1_body, nst=nst),
        grid=(M // TM,),
        in_specs=[
            pl.BlockSpec((TM, D_MODEL), lambda i: (i, 0)),
            _const_spec((1, D_MODEL)),
            _const_spec(wfm.shape),
            _const_spec(wk.shape),
            _const_spec(wf.shape),
            _const_spec(bf.shape),
            _const_spec(tri.shape),
            _const_spec(place.shape),
        ],
        out_specs=[
            pl.BlockSpec((1, TM // TK, n_fm, TK), lambda i: (i // nst, i % nst, 0, 0)),
            pl.BlockSpec((TM, n_ka), lambda i: (i, 0)),
        ],
        out_shape=[
            jax.ShapeDtypeStruct((B, S // TK, n_fm, TK), BF16),
            jax.ShapeDtypeStruct((M, n_ka), BF16),
        ],
        scratch_shapes=[pltpu.VMEM((8, LANES), F32)],
        compiler_params=_params(("arbitrary",)),
        name="inproj1",
    )(x2, g, wfm, wk, wf, bf, tri, place)


def _fox_body(q_ref, k_ref, v_ref, cm_ref, o_ref):
    c = pl.program_id(2)
    blk = c // (TK // TQ)
    q = q_ref[0, 0] * SCALE
    qa = jnp.concatenate([q, jnp.ones((LANES - HEAD_DIM, TQ), BF16)], axis=0)

    def tile(n):
        k0 = pl.multiple_of(n * TK, TK)
        return _dot(k_ref[0, pl.ds(k0, TK), :], qa), v_ref[0, n]

    s, vt = tile(blk)
    s = s + cm_ref[c % 2]
    m = jnp.max(s, axis=0, keepdims=True)
    p = jnp.exp(s - m)
    state = (m, jnp.sum(p, axis=0, keepdims=True), _dot(vt, p.astype(BF16)))

    def step(i, st):
        s_, vt_ = tile(blk - 1 - i)
        return _softmax_update(s_, vt_, *st)

    _, l, acc = lax.fori_loop(0, blk, step, state)
    o_ref[0] = (acc / l).astype(BF16)


def _fox(fm, ka, cmask, B, S):
    return pl.pallas_call(
        _fox_body,
        grid=(B, FOX_HEADS, S // TQ),
        in_specs=[
            pl.BlockSpec((1, 1, HEAD_DIM, TQ), lambda b, h, c: (b, c // 2, h, c % 2)),
            pl.BlockSpec((1, S, LANES), lambda b, h, c: (b, 0, h)),
            pl.BlockSpec((1, S // TK, HEAD_DIM, TK), lambda b, h, c: (b, 0, FOX_HEADS + h, 0)),
            _const_spec(cmask.shape),
        ],
        out_specs=pl.BlockSpec((1, HEAD_DIM, TQ), lambda b, h, c: (b, h, c)),
        out_shape=jax.ShapeDtypeStruct((B, FOX_W, S), BF16),
        compiler_params=_params(("parallel", "parallel", "arbitrary")),
        name="fox_attn",
    )(fm, ka, fm, cmask)


def _post_body(*refs, n_parts, final):
    o_refs = refs[:n_parts]
    h_ref, wo_ref, g_ref, w1_ref, w2_ref = refs[n_parts:n_parts + 5]
    gf_ref = refs[n_parts + 5] if final else None
    out_ref, hn_ref = refs[-2:]
    h1 = h_ref[...]
    r0 = 0
    for o_ref in o_refs:
        nf = o_ref.shape[1]
        h1 = h1 + _dot_tn(o_ref[0], wo_ref[r0:r0 + nf, :])
        r0 += nf
    out_ref[...] = h1
    hn_ref[...] = _rmsnorm(out_ref[...], g_ref[...]).astype(BF16)
    for c0 in range(0, D_FF, FF_CH):
        a = jnp.maximum(_dot(hn_ref[...], w1_ref[:, c0:c0 + FF_CH]), 0.0)
        out_ref[...] += _dot((a * a).astype(BF16), w2_ref[c0:c0 + FF_CH, :])
    if final:
        out_ref[...] = _rmsnorm(out_ref[...], gf_ref[...])


def _post(o_parts, h2, wo, g, w1, w2, gf, B, S):
    M = B * S
    nst = S // TM
    final = gf is not None
    in_specs = [pl.BlockSpec((1, o.shape[1], TM), lambda i: (i // nst, 0, i % nst)) for o in o_parts]
    in_specs += [
        pl.BlockSpec((TM, D_MODEL), lambda i: (i, 0)),
        _const_spec(wo.shape),
        _const_spec((1, D_MODEL)),
        _const_spec(w1.shape),
        _const_spec(w2.shape),
    ]
    args = list(o_parts) + [h2, wo, g, w1, w2]
    if final:
        in_specs.append(_const_spec((1, D_MODEL)))
        args.append(gf)
    return pl.pallas_call(
        functools.partial(_post_body, n_parts=len(o_parts), final=final),
        grid=(M // TM,),
        in_specs=in_specs,
        out_specs=pl.BlockSpec((TM, D_MODEL), lambda i: (i, 0)),
        out_shape=jax.ShapeDtypeStruct((M, D_MODEL), F32),
        scratch_shapes=[pltpu.VMEM((TM, D_MODEL), BF16)],
        compiler_params=_params(("parallel",)),
        name="post_final" if final else "post",
    )(*args)


def _rel_bucket(dist):
    n = jnp.maximum(dist, 0)
    max_exact = REL_BUCKETS // 2
    nf = jnp.maximum(n, 1).astype(jnp.float32)
    large = max_exact + (jnp.log(nf / max_exact) / math.log(REL_MAX_DISTANCE / max_exact)
                         * (REL_BUCKETS - max_exact)).astype(jnp.int32)
    large = jnp.minimum(large, REL_BUCKETS - 1)
    return jnp.where(n < max_exact, n, large)


def _bias_tables(rel_bias, S):
    n_dist = max(S, NE_BIAS * TQ + TQ) + NSA_CMP_BLOCK
    table = rel_bias.T
    bv = table[:, _rel_bucket(jnp.arange(n_dist))]
    far = table[:, REL_BUCKETS - 1][:, None, None, None]
    e = np.arange(NE_BIAS)[:, None, None]
    d = e * TQ + np.arange(TQ)[None, None, :] - np.arange(TK)[None, :, None]
    vals = bv[:, np.clip(d, 0, n_dist - 1)]
    tile = jnp.where(d >= 0, vals - far, NEG_INF)
    dw = d[:NE_WIN]
    twin = jnp.where((dw >= 0) & (dw < NSA_WINDOW), vals[MOBA_HEADS:, :NE_WIN], NEG_INF)
    n_c = S // NSA_CMP_STRIDE
    cmp_end = np.arange(n_c) * NSA_CMP_STRIDE + NSA_CMP_BLOCK - 1
    dc = (np.arange(S // TQ)[:, None, None] * TQ + np.arange(TQ)[None, None, :]) - cmp_end[None, :, None]
    tcmp = jnp.where(dc >= 0, bv[MOBA_HEADS:, np.clip(dc, 0, n_dist - 1)], NEG_INF)
    return tile, twin, tcmp


def _selection_constants(S):
    n_c = S // NSA_CMP_STRIDE
    n_cmp = (S - NSA_CMP_BLOCK) // NSA_CMP_STRIDE + 1
    n_sb = S // NSA_SLC_BLOCK
    ci = np.arange(n_c)[None, :] * NSA_CMP_STRIDE
    sj = np.arange(n_sb)[:, None] * NSA_SLC_BLOCK
    ovl = (ci < sj + NSA_SLC_BLOCK) & (ci + NSA_CMP_BLOCK > sj) & (np.arange(n_c)[None, :] < n_cmp)
    expand = (np.arange(S)[:, None] // NSA_SLC_BLOCK) == np.arange(n_sb)[None, :]
    return jnp.asarray(ovl, BF16), jnp.asarray(expand, BF16)


def _causal_tiles():
    e = np.arange(2)[:, None, None]
    d = e * TQ + np.arange(TQ)[None, None, :] - np.arange(TK)[None, :, None]
    return jnp.asarray(np.where(d >= 0, 0.0, NEG_INF), F32)


def kernel(x, rel_bias, mix_norm, mlp_norm, even_w_in, even_w_out, cmp_pos_k, cmp_pos_v, cmp_k_w1, cmp_k_w2,
           cmp_v_w1, cmp_v_w2, odd_w_in, odd_b_forget, odd_w_out, mlp_w1, mlp_w2, final_norm):
    B, S, D = x.shape
    assert D == D_MODEL and S % TM == 0
    G, J = NSA_KV_GROUPS, NSA_HPG
    h = x.reshape(B * S, D)

    offs = np.cumsum((MOBA_W, MOBA_W, MOBA_W, NSA_W) + (NSA_KV_W,) * 6)
    mq_w, mk_w, mv_w, nq_w, kc_w, vc_w, ksl_w, vsl_w, kwn_w, vwn_w, gz_w = jnp.split(even_w_in[0], offs, axis=1)
    wrm = jnp.concatenate([mk_w, kc_w, vc_w, ksl_w, kwn_w], axis=1).astype(BF16)
    col_kcvc, col_ksl, col_kwn = MOBA_W, MOBA_W + 2 * NSA_KV_W, MOBA_W + 3 * NSA_KV_W
    wfm = jnp.concatenate([mq_w, mv_w, nq_w, vsl_w, vwn_w], axis=1).T.astype(BF16)
    row_nq, row_vsl, row_vwn = 2 * MOBA_W, 2 * MOBA_W + NSA_W, 2 * MOBA_W + NSA_W + NSA_KV_W
    gzw = gz_w.T.reshape(G, J, 3, D).transpose(0, 2, 1, 3).reshape(G, 3 * J, D)
    gzw = jnp.pad(gzw, ((0, 0), (0, GZ_ROWS - 3 * J), (0, 0))).reshape(G * GZ_ROWS, D).astype(BF16)

    rm, fm, gz = _inproj0(h, mix_norm[0][None, :], wrm, wfm, gzw, B, S)
    rm = rm.reshape(B, S, -1)

    tile, twin, tcmp = _bias_tables(rel_bias, S)
    ovl, expand = _selection_constants(S)

    o_moba = _moba(fm, rm, tile, B, S)

    n_c = S // NSA_CMP_STRIDE
    r = rm[:, :, col_kcvc:col_kcvc + 2 * NSA_KV_W].reshape(B, n_c, NSA_CMP_STRIDE, 2 * G, HEAD_DIM)
    r = r.transpose(0, 3, 1, 2, 4).reshape(B, 2 * G, n_c, NSA_CMP_STRIDE * HEAD_DIM)
    pos = jnp.stack([cmp_pos_k[0].reshape(1, -1), cmp_pos_v[0].reshape(1, -1)])
    pos = jnp.pad(pos, ((0, 0), (0, 7), (0, 0))).astype(BF16)
    w1c = jnp.stack([cmp_k_w1[0], cmp_v_w1[0]]).astype(BF16)
    kc, vct = _compress(r, pos, w1c, cmp_k_w2[0].astype(BF16), cmp_v_w2[0].T.astype(BF16), B, n_c)

    o_nsa = _nsa(fm, rm, gz, kc, vct, tcmp, ovl, expand, tile, twin, B, S,
                 col_ksl, col_kwn, row_nq, row_vsl, row_vwn)

    h = _post([o_moba, o_nsa], h, even_w_out[0].astype(BF16), mlp_norm[0][None, :],
              mlp_w1[0].astype(BF16), mlp_w2[0].astype(BF16), None, B, S)

    q_w, k_w, v_w, f_w = jnp.split(odd_w_in[0], np.cumsum((FOX_W, FOX_W, FOX_W)), axis=1)
    wfm1 = jnp.concatenate([q_w, v_w], axis=1).T.astype(BF16)
    wk = jnp.pad(k_w.reshape(D, FOX_HEADS, HEAD_DIM), ((0, 0), (0, 0), (0, LANES - HEAD_DIM)))
    wk = wk.reshape(D, FOX_HEADS * LANES).astype(BF16)
    f_w = jnp.pad(f_w, ((0, 0), (0, LANES - FOX_HEADS)))
    f_hi = f_w.astype(BF16)
    wf = jnp.stack([f_hi, (f_w - f_hi.astype(F32)).astype(BF16)])
    bf = jnp.pad(odd_b_forget[0], (0, LANES - FOX_HEADS))[None, :]
    tri = jnp.asarray(np.tril(np.ones((TM, TM))), BF16)
    place = np.zeros((3, LANES, FOX_HEADS * LANES), np.float32)
    for term in range(3):
        place[term, np.arange(FOX_HEADS), np.arange(FOX_HEADS) * LANES + HEAD_DIM + term] = -1.0
    fm1, ka = _inproj1(h, mix_norm[1][None, :], wfm1, wk, wf, bf, tri, jnp.asarray(place, BF16), B, S)
    o_fox = _fox(fm1, ka.reshape(B, S, -1), _causal_tiles(), B, S)

    h = _post([o_fox], h, odd_w_out[0].astype(BF16), mlp_norm[1][None, :],
              mlp_w1[1].astype(BF16), mlp_w2[1].astype(BF16), final_norm[None, :], B, S)
    return h.reshape(B, S, D)
```

```python
import functools
import math

import numpy as np
import jax
import jax.numpy as jnp
from jax import lax
from jax.experimental import pallas as pl
from jax.experimental.pallas import tpu as pltpu

D_MODEL = 1024
HEAD_DIM = 64
MOBA_HEADS = 8
MOBA_BLOCK = 256
MOBA_TOPK = 3
NSA_HEADS = 8
NSA_KV_GROUPS = 2
NSA_HPG = NSA_HEADS // NSA_KV_GROUPS
NSA_CMP_BLOCK = 32
NSA_CMP_STRIDE = 16
NSA_CMP_HIDDEN = 256
NSA_SLC_BLOCK = 64
NSA_TOPN = 16
NSA_WINDOW = 512
NSA_FORCE_SCORE = 1e6
FOX_HEADS = 16
D_FF = 4 * D_MODEL
REL_BUCKETS = 32
REL_MAX_DISTANCE = 1024
RMS_EPS = 1e-5
NEG_INF = -1e30
SCALE = HEAD_DIM ** -0.5

MOBA_W = MOBA_HEADS * HEAD_DIM
NSA_W = NSA_HEADS * HEAD_DIM
NSA_KV_W = NSA_KV_GROUPS * HEAD_DIM
FOX_W = FOX_HEADS * HEAD_DIM

LANES = 128
TQ = 128
TK = 256
TM = 512
CH = 256
FF_CH = 512
VMEM_LIMIT = 56 * 1024 * 1024
NE_BIAS = -(-(REL_MAX_DISTANCE + TK - 1) // TQ)
NE_WIN = -(-(NSA_WINDOW + TK - 1) // TQ)
GZ_ROWS = 16

assert TK == 2 * TQ and MOBA_BLOCK == TK and TK % NSA_SLC_BLOCK == 0

F32 = jnp.float32
BF16 = jnp.bfloat16


def _dot(a, b):
    return jnp.dot(a, b, preferred_element_type=F32)


def _dot_nt(a, b):
    return lax.dot_general(a, b, (((1,), (1,)), ((), ())), preferred_element_type=F32)


def _dot_tn(a, b):
    return lax.dot_general(a, b, (((0,), (0,)), ((), ())), preferred_element_type=F32)


def _rmsnorm(x, g):
    ms = jnp.mean(x * x, axis=-1, keepdims=True)
    return x * lax.rsqrt(ms + RMS_EPS) * g


def _split3(x):
    a = x.astype(BF16)
    r = x - a.astype(F32)
    b = r.astype(BF16)
    c = (r - b.astype(F32)).astype(BF16)
    return a, b, c


def _const_spec(shape):
    nd = len(shape)
    return pl.BlockSpec(shape, lambda *_: (0,) * nd, pipeline_mode=pl.Buffered(1))


def _params(sem):
    return pltpu.CompilerParams(dimension_semantics=sem, vmem_limit_bytes=VMEM_LIMIT)


def _softmax_update(s, vt, m, l, acc):
    m_new = jnp.maximum(m, jnp.max(s, axis=0, keepdims=True))
    alpha = jnp.exp(m - m_new)
    p = jnp.exp(s - m_new)
    l = alpha * l + jnp.sum(p, axis=0, keepdims=True)
    acc = alpha * acc + _dot(vt, p.astype(BF16))
    return m_new, l, acc


def _init_state(n_heads):
    return tuple((jnp.full((1, TQ), NEG_INF, F32), jnp.zeros((1, TQ), F32),
                  jnp.zeros((HEAD_DIM, TQ), F32)) for _ in range(n_heads))


def _rank_select(val, idx, n_rows, k):
    cnt = jnp.zeros(val.shape, F32)
    for m in range(n_rows):
        vm = val[m:m + 1, :]
        beats = (vm > val) | ((vm == val) & (idx > m))
        cnt = cnt + jnp.where(beats, 1.0, 0.0)
    return cnt < k


def _inproj0_body(x_ref, g_ref, wrm_ref, wfm_ref, wgz_ref, rm_ref, fm_ref, gz_ref):
    xn = _rmsnorm(x_ref[...], g_ref[...]).astype(BF16)
    for c0 in range(0, rm_ref.shape[-1], CH):
        rm_ref[:, c0:c0 + CH] = _dot(xn, wrm_ref[:, c0:c0 + CH]).astype(BF16)
    for r0 in range(0, fm_ref.shape[2], CH):
        res = _dot_nt(wfm_ref[r0:r0 + CH, :], xn).astype(BF16)
        for t in range(TM // TK):
            fm_ref[0, t, r0:r0 + CH, :] = res[:, t * TK:(t + 1) * TK]
    gz_ref[0] = _dot_nt(wgz_ref[...], xn)


def _inproj0(x2, g, wrm, wfm, wgz, B, S):
    M = B * S
    nst = S // TM
    n_rm, n_fm, n_gz = wrm.shape[1], wfm.shape[0], wgz.shape[0]
    return pl.pallas_call(
        _inproj0_body,
        grid=(M // TM,),
        in_specs=[
            pl.BlockSpec((TM, D_MODEL), lambda i: (i, 0)),
            _const_spec((1, D_MODEL)),
            _const_spec((D_MODEL, n_rm)),
            _const_spec((n_fm, D_MODEL)),
            _const_spec((n_gz, D_MODEL)),
        ],
        out_specs=[
            pl.BlockSpec((TM, n_rm), lambda i: (i, 0)),
            pl.BlockSpec((1, TM // TK, n_fm, TK), lambda i: (i // nst, i % nst, 0, 0)),
            pl.BlockSpec((1, n_gz, TM), lambda i: (i // nst, 0, i % nst)),
        ],
        out_shape=[
            jax.ShapeDtypeStruct((M, n_rm), BF16),
            jax.ShapeDtypeStruct((B, S // TK, n_fm, TK), BF16),
            jax.ShapeDtypeStruct((B, n_gz, S), F32),
        ],
        compiler_params=_params(("parallel",)),
        name="inproj0",
    )(x2, g, wrm, wfm, wgz)


def _compress_body(rk_ref, rv_ref, pos_ref, w1_ref, w2k_ref, w2vt_ref, kc_ref, vct_ref):
    half = NSA_CMP_STRIDE * HEAD_DIM

    def hidden(r_ref, s):
        r = r_ref[0, 0]
        a = _dot(r, w1_ref[s, :half, :])
        b = _dot(r, w1_ref[s, half:, :])
        nxt = pltpu.roll(b, b.shape[0] - 1, axis=0)
        posb = _dot(pos_ref[s], w1_ref[s])[0:1]
        pre = a + nxt + posb
        return (pre * jax.nn.sigmoid(pre)).astype(BF16)

    kc_ref[0, 0] = _dot(hidden(rk_ref, 0), w2k_ref[...]).astype(BF16)
    vct_ref[0, 0] = _dot_nt(w2vt_ref[...], hidden(rv_ref, 1)).astype(BF16)


def _compress(r, pos, w1, w2k, w2vt, B, NC):
    G = NSA_KV_GROUPS
    half = NSA_CMP_STRIDE * HEAD_DIM
    return pl.pallas_call(
        _compress_body,
        grid=(B, G),
        in_specs=[
            pl.BlockSpec((1, 1, NC, half), lambda b, g: (b, g, 0, 0)),
            pl.BlockSpec((1, 1, NC, half), lambda b, g: (b, G + g, 0, 0)),
            _const_spec(pos.shape),
            _const_spec(w1.shape),
            _const_spec(w2k.shape),
            _const_spec(w2vt.shape),
        ],
        out_specs=[
            pl.BlockSpec((1, 1, NC, HEAD_DIM), lambda b, g: (b, g, 0, 0)),
            pl.BlockSpec((1, 1, HEAD_DIM, NC), lambda b, g: (b, g, 0, 0)),
        ],
        out_shape=[
            jax.ShapeDtypeStruct((B, G, NC, HEAD_DIM), BF16),
            jax.ShapeDtypeStruct((B, G, HEAD_DIM, NC), BF16),
        ],
        compiler_params=_params(("parallel", "parallel")),
        name="nsa_compress",
    )(r, r, pos, w1, w2k, w2vt)


def _moba_body(q_ref, k_ref, v_ref, t_ref, o_ref, kmean_ref, mask_ref, *, n_mb, topk):
    c = pl.program_id(2)
    blk = c // (TK // TQ)

    @pl.when(c == 0)
    def _():
        kmean_ref[...] = jnp.zeros_like(kmean_ref)
        for n in range(n_mb):
            kblk = k_ref[0, n * TK:(n + 1) * TK, :].astype(F32)
            kmean_ref[n:n + 1, :] = jnp.mean(kblk, axis=0, keepdims=True)

    q = q_ref[0, 0]
    rowi = lax.broadcasted_iota(jnp.int32, q.shape, 0)
    nidx = lax.broadcasted_iota(jnp.int32, (kmean_ref.shape[0], TQ), 0)
    km = _split3(kmean_ref[...])
    qpads = []
    for hh in range(2):
        qh = jnp.where(rowi // HEAD_DIM == hh, q, jnp.zeros_like(q))
        route = _dot(km[0], qh) + _dot(km[1], qh) + _dot(km[2], qh)
        route = jnp.where(nidx < blk, route, NEG_INF)
        sel = _rank_select(route, nidx, n_mb, topk) & (nidx < blk)
        mask_ref[hh] = jnp.where(sel | (nidx == blk), 0.0, NEG_INF)
        qpads.append(qh * SCALE)

    def step(n, state, near):
        k0 = pl.multiple_of(n * TK, TK)
        kt = k_ref[0, pl.ds(k0, TK), :]
        vt = v_ref[0, n]
        out = []
        for hh in range(2):
            s = _dot(kt, qpads[hh]) + mask_ref[hh, pl.ds(n, 1), :]
            if near:
                s = s + t_ref[hh, c - 2 * n]
            out.append(_softmax_update(s, vt[hh * HEAD_DIM:(hh + 1) * HEAD_DIM, :], *state[hh]))
        return tuple(out)

    n_lo = jnp.maximum(c - (NE_BIAS - 2), 0) // 2
    state = lax.fori_loop(0, blk - n_lo + 1, lambda i, st: step(blk - i, st, True), _init_state(2))
    state = lax.fori_loop(0, n_lo, lambda n, st: step(n, st, False), state)
    o_ref[0] = jnp.concatenate([acc / l for (_, l, acc) in state], axis=0).astype(BF16)


def _moba(fm, rm, tab, B, S):
    n_mb = S // MOBA_BLOCK
    n_pad = -(-n_mb // 16) * 16
    ne = tab.shape[1]
    body = functools.partial(_moba_body, n_mb=n_mb, topk=min(MOBA_TOPK, n_mb))
    return pl.pallas_call(
        body,
        grid=(B, MOBA_HEADS // 2, S // TQ),
        in_specs=[
            pl.BlockSpec((1, 1, 2 * HEAD_DIM, TQ), lambda b, p, c: (b, c // 2, p, c % 2)),
            pl.BlockSpec((1, S, 2 * HEAD_DIM), lambda b, p, c: (b, 0, p)),
            pl.BlockSpec((1, S // TK, 2 * HEAD_DIM, TK), lambda b, p, c: (b, 0, MOBA_HEADS // 2 + p, 0)),
            pl.BlockSpec((2, ne, TK, TQ), lambda b, p, c: (p, 0, 0, 0)),
        ],
        out_specs=pl.BlockSpec((1, 2 * HEAD_DIM, TQ), lambda b, p, c: (b, p, c)),
        out_shape=jax.ShapeDtypeStruct((B, MOBA_W, S), BF16),
        scratch_shapes=[pltpu.VMEM((n_pad, 2 * HEAD_DIM), F32), pltpu.VMEM((2, n_pad, TQ), F32)],
        compiler_params=_params(("parallel", "parallel", "arbitrary")),
        name="moba_attn",
    )(fm, rm, fm, tab)


def _nsa_body(q_ref, kc_ref, vct_ref, tc_ref, ovl_ref, exp_ref, ksl_ref, vsl_ref, kwn_ref, vwn_ref,
              tslc_ref, twin_ref, gz_ref, o_ref, *, n_sb, n_sel):
    g = pl.program_id(1)
    c = pl.program_id(2)
    blk = c // (TK // TQ)
    J = NSA_HPG

    q = q_ref[0, 0] * SCALE
    rowi = lax.broadcasted_iota(jnp.int32, (2 * HEAD_DIM, TQ), 0)
    qs = [q[j * HEAD_DIM:(j + 1) * HEAD_DIM, :] for j in range(J)]
    qpads = [jnp.where(rowi // HEAD_DIM == g, jnp.concatenate([qj, qj], axis=0), jnp.zeros((2 * HEAD_DIM, TQ), BF16))
             for qj in qs]

    kc = kc_ref[0, 0]
    vct = vct_ref[0, 0]
    o_cmp = []
    psum = jnp.zeros((kc.shape[0], TQ), F32)
    for j in range(J):
        s = _dot(kc, qs[j]) + tc_ref[j, 0]
        m = jnp.max(s, axis=0, keepdims=True)
        p = jnp.exp(s - m)
        l = jnp.sum(p, axis=0, keepdims=True)
        pn = p * jnp.where(m > 0.5 * NEG_INF, 1.0 / l, 0.0)
        o_cmp.append(_dot(vct, pn.astype(BF16)))
        psum = psum + pn

    ph = psum.astype(BF16)
    plo = (psum - ph.astype(F32)).astype(BF16)
    imp = _dot(ovl_ref[...], ph) + _dot(ovl_ref[...], plo)
    jb = lax.broadcasted_iota(jnp.int32, imp.shape, 0)
    t = c * TQ + lax.broadcasted_iota(jnp.int32, imp.shape, 1)
    sb = t // NSA_SLC_BLOCK
    forced = (jb == 0) | (jb == sb) | (jb == sb - 1)
    allowed = jb <= sb
    val = jnp.where(forced, imp + NSA_FORCE_SCORE, jnp.where(allowed, imp, NEG_INF))
    sel = _rank_select(val, jb, n_sb, n_sel) & allowed
    selb = jnp.where(sel, 0.0, NEG_INF).astype(BF16)

    def slc_step(n, state, near):
        k0 = pl.multiple_of(n * TK, TK)
        kt = ksl_ref[0, pl.ds(k0, TK), :]
        vt = vsl_ref[0, n]
        mt = _dot(exp_ref[pl.ds(k0, TK), :], selb)
        out = []
        for j in range(J):
            s = _dot(kt, qpads[j]) + mt
            if near:
                s = s + tslc_ref[j, c - 2 * n]
            out.append(_softmax_update(s, vt, *state[j]))
        return tuple(out)

    n_lo = jnp.maximum(c - (NE_BIAS - 2), 0) // 2
    st = lax.fori_loop(0, blk - n_lo + 1, lambda i, s_: slc_step(blk - i, s_, True), _init_state(J))
    st = lax.fori_loop(0, n_lo, lambda n, s_: slc_step(n, s_, False), st)
    o_slc = [acc / l for (_, l, acc) in st]

    def win_step(i, state):
        n = blk - i
        k0 = pl.multiple_of(n * TK, TK)
        kt = kwn_ref[0, pl.ds(k0, TK), :]
        vt = vwn_ref[0, n]
        out = []
        for j in range(J):
            s = _dot(kt, qpads[j]) + twin_ref[j, c - 2 * n]
            out.append(_softmax_update(s, vt, *state[j]))
        return tuple(out)

    w_lo = jnp.maximum(c - (NE_WIN - 2), 0) // 2
    st = lax.fori_loop(0, blk - w_lo + 1, win_step, _init_state(J))
    o_win = [acc / l for (_, l, acc) in st]

    gate = jax.nn.sigmoid(gz_ref[0])
    outs = []
    for j in range(J):
        outs.append(gate[j:j + 1, :] * o_cmp[j] + gate[J + j:J + j + 1, :] * o_slc[j]
                    + gate[2 * J + j:2 * J + j + 1, :] * o_win[j])
    o_ref[0] = jnp.concatenate(outs, axis=0).astype(BF16)


def _nsa(fm, rm, gz, kc, vct, tcmp, ovl, expand, tslc, twin, B, S, col_ksl, col_kwn, row_q, row_vsl, row_vwn):
    G, J = NSA_KV_GROUPS, NSA_HPG
    NC = kc.shape[2]
    n_sb = S // NSA_SLC_BLOCK
    body = functools.partial(_nsa_body, n_sb=n_sb, n_sel=min(NSA_TOPN, n_sb))
    qrows = J * HEAD_DIM
    return pl.pallas_call(
        body,
        grid=(B, G, S // TQ),
        in_specs=[
            pl.BlockSpec((1, 1, qrows, TQ), lambda b, g, c: (b, c // 2, row_q // qrows + g, c % 2)),
            pl.BlockSpec((1, 1, NC, HEAD_DIM), lambda b, g, c: (b, g, 0, 0)),
            pl.BlockSpec((1, 1, HEAD_DIM, NC), lambda b, g, c: (b, g, 0, 0)),
            pl.BlockSpec((J, 1, NC, TQ), lambda b, g, c: (g, c, 0, 0)),
            _const_spec(ovl.shape),
            _const_spec(expand.shape),
            pl.BlockSpec((1, S, LANES), lambda b, g, c: (b, 0, col_ksl // LANES)),
            pl.BlockSpec((1, S // TK, HEAD_DIM, TK), lambda b, g, c: (b, 0, row_vsl // HEAD_DIM + g, 0)),
            pl.BlockSpec((1, S, LANES), lambda b, g, c: (b, 0, col_kwn // LANES)),
            pl.BlockSpec((1, S // TK, HEAD_DIM, TK), lambda b, g, c: (b, 0, row_vwn // HEAD_DIM + g, 0)),
            pl.BlockSpec((J, tslc.shape[1], TK, TQ), lambda b, g, c: (MOBA_HEADS // J + g, 0, 0, 0)),
            pl.BlockSpec((J, twin.shape[1], TK, TQ), lambda b, g, c: (g, 0, 0, 0)),
            pl.BlockSpec((1, GZ_ROWS, TQ), lambda b, g, c: (b, g, c)),
        ],
        out_specs=pl.BlockSpec((1, qrows, TQ), lambda b, g, c: (b, g, c)),
        out_shape=jax.ShapeDtypeStruct((B, NSA_W, S), BF16),
        compiler_params=_params(("parallel", "parallel", "arbitrary")),
        name="nsa_attn",
    )(fm, kc, vct, tcmp, ovl, expand, rm, fm, rm, fm, tslc, twin, gz)


def _inproj1_body(x_ref, g_ref, wfm_ref, wk_ref, wf_ref, bf_ref, tri_ref, place_ref, fm_ref, ka_ref, carry_ref,
                  *, nst):
    i = pl.program_id(0)

    @pl.when(i % nst == 0)
    def _():
        carry_ref[...] = jnp.zeros_like(carry_ref)

    xf = _rmsnorm(x_ref[...], g_ref[...])
    xn = xf.astype(BF16)
    xlo = (xf - xn.astype(F32)).astype(BF16)
    for r0 in range(0, fm_ref.shape[2], CH):
        res = _dot_nt(wfm_ref[r0:r0 + CH, :], xn).astype(BF16)
        for t in range(TM // TK):
            fm_ref[0, t, r0:r0 + CH, :] = res[:, t * TK:(t + 1) * TK]

    fz = _dot(xn, wf_ref[0]) + _dot(xlo, wf_ref[0]) + _dot(xn, wf_ref[1]) + bf_ref[...]
    logf = jnp.minimum(fz, 0.0) - jnp.log(1.0 + jnp.exp(-jnp.abs(fz)))
    tri = tri_ref[...]
    h1, h2, h3 = _split3(logf)
    cum = _dot(tri, h1) + _dot(tri, h2) + _dot(tri, h3) + carry_ref[0:1, :]
    carry_ref[...] = jnp.broadcast_to(cum[TM - 1:TM, :], carry_ref.shape)
    c1, c2, c3 = _split3(cum)
    for c0 in range(0, ka_ref.shape[-1], FF_CH):
        ka = (_dot(xn, wk_ref[:, c0:c0 + FF_CH]) + _dot(c1, place_ref[0, :, c0:c0 + FF_CH])
              + _dot(c2, place_ref[1, :, c0:c0 + FF_CH]) + _dot(c3, place_ref[2, :, c0:c0 + FF_CH]))
        ka_ref[:, c0:c0 + FF_CH] = ka.astype(BF16)


def _inproj1(x2, g, wfm, wk, wf, bf, tri, place, B, S):
    M = B * S
    nst = S // TM
    n_fm, n_ka = wfm.shape[0], wk.shape[1]
    return pl.pallas_call(
        functools.partial(_inproj1_body, nst=nst),
        grid=(M // TM,),
        in_specs=[
            pl.BlockSpec((TM, D_MODEL), lambda i: (i, 0)),
            _const_spec((1, D_MODEL)),
            _const_spec(wfm.shape),
            _const_spec(wk.shape),
            _const_spec(wf.shape),
            _const_spec(bf.shape),
            _const_spec(tri.shape),
            _const_spec(place.shape),
        ],
        out_specs=[
            pl.BlockSpec((1, TM // TK, n_fm, TK), lambda i: (i // nst, i % nst, 0, 0)),
            pl.BlockSpec((TM, n_ka), lambda i: (i, 0)),
        ],
        out_shape=[
            jax.ShapeDtypeStruct((B, S // TK, n_fm, TK), BF16),
            jax.ShapeDtypeStruct((M, n_ka), BF16),
        ],
        scratch_shapes=[pltpu.VMEM((8, LANES), F32)],
        compiler_params=_params(("arbitrary",)),
        name="inproj1",
    )(x2, g, wfm, wk, wf, bf, tri, place)


def _fox_body(q_ref, k_ref, v_ref, cm_ref, o_ref):
    c = pl.program_id(2)
    blk = c // (TK // TQ)
    q = q_ref[0, 0] * SCALE
    qa = jnp.concatenate([q, jnp.ones((LANES - HEAD_DIM, TQ), BF16)], axis=0)

    def tile(n):
        k0 = pl.multiple_of(n * TK, TK)
        return _dot(k_ref[0, pl.ds(k0, TK), :], qa), v_ref[0, n]

    s, vt = tile(blk)
    s = s + cm_ref[c % 2]
    m = jnp.max(s, axis=0, keepdims=True)
    p = jnp.exp(s - m)
    state = (m, jnp.sum(p, axis=0, keepdims=True), _dot(vt, p.astype(BF16)))

    def step(i, st):
        s_, vt_ = tile(blk - 1 - i)
        return _softmax_update(s_, vt_, *st)

    _, l, acc = lax.fori_loop(0, blk, step, state)
    o_ref[0] = (acc / l).astype(BF16)


def _fox(fm, ka, cmask, B, S):
    return pl.pallas_call(
        _fox_body,
        grid=(B, FOX_HEADS, S // TQ),
        in_specs=[
            pl.BlockSpec((1, 1, HEAD_DIM, TQ), lambda b, h, c: (b, c // 2, h, c % 2)),
            pl.BlockSpec((1, S, LANES), lambda b, h, c: (b, 0, h)),
            pl.BlockSpec((1, S // TK, HEAD_DIM, TK), lambda b, h, c: (b, 0, FOX_HEADS + h, 0)),
            _const_spec(cmask.shape),
        ],
        out_specs=pl.BlockSpec((1, HEAD_DIM, TQ), lambda b, h, c: (b, h, c)),
        out_shape=jax.ShapeDtypeStruct((B, FOX_W, S), BF16),
        compiler_params=_params(("parallel", "parallel", "arbitrary")),
        name="fox_attn",
    )(fm, ka, fm, cmask)


def _post_body(*refs, n_parts, final):
    o_refs = refs[:n_parts]
    h_ref, wo_ref, g_ref, w1_ref, w2_ref = refs[n_parts:n_parts + 5]
    gf_ref = refs[n_parts + 5] if final else None
    out_ref, hn_ref = refs[-2:]
    h1 = h_ref[...]
    r0 = 0
    for o_ref in o_refs:
        nf = o_ref.shape[1]
        h1 = h1 + _dot_tn(o_ref[0], wo_ref[r0:r0 + nf, :])
        r0 += nf
    out_ref[...] = h1
    hn_ref[...] = _rmsnorm(out_ref[...], g_ref[...]).astype(BF16)
    for c0 in range(0, D_FF, FF_CH):
        a = jnp.maximum(_dot(hn_ref[...], w1_ref[:, c0:c0 + FF_CH]), 0.0)
        out_ref[...] += _dot((a * a).astype(BF16), w2_ref[c0:c0 + FF_CH, :])
    if final:
        out_ref[...] = _rmsnorm(out_ref[...], gf_ref[...])


def _post(o_parts, h2, wo, g, w1, w2, gf, B, S):
    M = B * S
    nst = S // TM
    final = gf is not None
    in_specs = [pl.BlockSpec((1, o.shape[1], TM), lambda i: (i // nst, 0, i % nst)) for o in o_parts]
    in_specs += [
        pl.BlockSpec((TM, D_MODEL), lambda i: (i, 0)),
        _const_spec(wo.shape),
        _const_spec((1, D_MODEL)),
        _const_spec(w1.shape),
        _const_spec(w2.shape),
    ]
    args = list(o_parts) + [h2, wo, g, w1, w2]
    if final:
        in_specs.append(_const_spec((1, D_MODEL)))
        args.append(gf)
    return pl.pallas_call(
        functools.partial(_post_body, n_parts=len(o_parts), final=final),
        grid=(M // TM,),
        in_specs=in_specs,
        out_specs=pl.BlockSpec((TM, D_MODEL), lambda i: (i, 0)),
        out_shape=jax.ShapeDtypeStruct((M, D_MODEL), F32),
        scratch_shapes=[pltpu.VMEM((TM, D_MODEL), BF16)],
        compiler_params=_params(("parallel",)),
        name="post_final" if final else "post",
    )(*args)


def _rel_bucket(dist):
    n = jnp.maximum(dist, 0)
    max_exact = REL_BUCKETS // 2
    nf = jnp.maximum(n, 1).astype(jnp.float32)
    large = max_exact + (jnp.log(nf / max_exact) / math.log(REL_MAX_DISTANCE / max_exact)
                         * (REL_BUCKETS - max_exact)).astype(jnp.int32)
    large = jnp.minimum(large, REL_BUCKETS - 1)
    return jnp.where(n < max_exact, n, large)


def _bias_tables(rel_bias, S):
    n_heads = rel_bias.shape[1]
    table = rel_bias.T

    def bias_of(dist):
        bkt = _rel_bucket(jnp.asarray(dist))[None, :]
        out = jnp.zeros((n_heads, len(dist)), F32)
        for b in range(REL_BUCKETS):
            out = jnp.where(bkt == b, table[:, b:b + 1], out)
        return jnp.where(jnp.asarray(dist)[None, :] >= 0, out, NEG_INF)

    def shifted_rows(w, n_rows, step):
        p = w.shape[1]
        return jnp.tile(w, (1, n_rows))[:, :n_rows * (p - step)].reshape(w.shape[0], n_rows, p - step)

    w = bias_of(np.arange(-(TK - 1), NE_BIAS * TQ + TQ))
    toep = shifted_rows(w, TK, 1)[:, :, TK - 1:TK - 1 + NE_BIAS * TQ]
    vals = toep.reshape(n_heads, TK, NE_BIAS, TQ).transpose(0, 2, 1, 3)
    d = (np.arange(NE_BIAS)[:, None, None] * TQ + np.arange(TQ)[None, None, :] - np.arange(TK)[None, :, None])
    far = table[:, REL_BUCKETS - 1][:, None, None, None]
    tile = jnp.where(d >= 0, vals - far, NEG_INF)
    dw = d[:NE_WIN]
    twin = jnp.where((dw >= 0) & (dw < NSA_WINDOW), vals[MOBA_HEADS:, :NE_WIN], NEG_INF)
    n_c = S // NSA_CMP_STRIDE
    last = NSA_CMP_STRIDE * (n_c - 1)
    w2 = bias_of(np.arange(2 * S + NSA_CMP_STRIDE) - last - (NSA_CMP_BLOCK - 1))[MOBA_HEADS:]
    g = shifted_rows(w2, n_c, NSA_CMP_STRIDE)[:, :, last:last + S]
    tcmp = g.reshape(-1, n_c, S // TQ, TQ).transpose(0, 2, 1, 3)
    return tile, twin, tcmp


def _selection_constants(S):
    n_c = S // NSA_CMP_STRIDE
    n_cmp = (S - NSA_CMP_BLOCK) // NSA_CMP_STRIDE + 1
    n_sb = S // NSA_SLC_BLOCK
    ci = np.arange(n_c)[None, :] * NSA_CMP_STRIDE
    sj = np.arange(n_sb)[:, None] * NSA_SLC_BLOCK
    ovl = (ci < sj + NSA_SLC_BLOCK) & (ci + NSA_CMP_BLOCK > sj) & (np.arange(n_c)[None, :] < n_cmp)
    expand = (np.arange(S)[:, None] // NSA_SLC_BLOCK) == np.arange(n_sb)[None, :]
    return jnp.asarray(ovl, BF16), jnp.asarray(expand, BF16)


def _causal_tiles():
    e = np.arange(2)[:, None, None]
    d = e * TQ + np.arange(TQ)[None, None, :] - np.arange(TK)[None, :, None]
    return jnp.asarray(np.where(d >= 0, 0.0, NEG_INF), F32)


def kernel(x, rel_bias, mix_norm, mlp_norm, even_w_in, even_w_out, cmp_pos_k, cmp_pos_v, cmp_k_w1, cmp_k_w2,
           cmp_v_w1, cmp_v_w2, odd_w_in, odd_b_forget, odd_w_out, mlp_w1, mlp_w2, final_norm):
    B, S, D = x.shape
    assert D == D_MODEL and S % TM == 0
    G, J = NSA_KV_GROUPS, NSA_HPG
    h = x.reshape(B * S, D)

    offs = np.cumsum((MOBA_W, MOBA_W, MOBA_W, NSA_W) + (NSA_KV_W,) * 6)
    mq_w, mk_w, mv_w, nq_w, kc_w, vc_w, ksl_w, vsl_w, kwn_w, vwn_w, gz_w = jnp.split(even_w_in[0], offs, axis=1)
    wrm = jnp.concatenate([mk_w, kc_w, vc_w, ksl_w, kwn_w], axis=1).astype(BF16)
    col_kcvc, col_ksl, col_kwn = MOBA_W, MOBA_W + 2 * NSA_KV_W, MOBA_W + 3 * NSA_KV_W
    wfm = jnp.concatenate([mq_w, mv_w, nq_w, vsl_w, vwn_w], axis=1).T.astype(BF16)
    row_nq, row_vsl, row_vwn = 2 * MOBA_W, 2 * MOBA_W + NSA_W, 2 * MOBA_W + NSA_W + NSA_KV_W
    gzw = gz_w.T.reshape(G, J, 3, D).transpose(0, 2, 1, 3).reshape(G, 3 * J, D)
    gzw = jnp.pad(gzw, ((0, 0), (0, GZ_ROWS - 3 * J), (0, 0))).reshape(G * GZ_ROWS, D).astype(BF16)

    rm, fm, gz = _inproj0(h, mix_norm[0][None, :], wrm, wfm, gzw, B, S)
    rm = rm.reshape(B, S, -1)

    tile, twin, tcmp = _bias_tables(rel_bias, S)
    ovl, expand = _selection_constants(S)

    o_moba = _moba(fm, rm, tile, B, S)

    n_c = S // NSA_CMP_STRIDE
    r = rm[:, :, col_kcvc:col_kcvc + 2 * NSA_KV_W].reshape(B, n_c, NSA_CMP_STRIDE, 2 * G, HEAD_DIM)
    r = r.transpose(0, 3, 1, 2, 4).reshape(B, 2 * G, n_c, NSA_CMP_STRIDE * HEAD_DIM)
    pos = jnp.stack([cmp_pos_k[0].reshape(1, -1), cmp_pos_v[0].reshape(1, -1)])
    pos = jnp.pad(pos, ((0, 0), (0, 7), (0, 0))).astype(BF16)
    w1c = jnp.stack([cmp_k_w1[0], cmp_v_w1[0]]).astype(BF16)
    kc, vct = _compress(r, pos, w1c, cmp_k_w2[0].astype(BF16), cmp_v_w2[0].T.astype(BF16), B, n_c)

    o_nsa = _nsa(fm, rm, gz, kc, vct, tcmp, ovl, expand, tile, twin, B, S,
                 col_ksl, col_kwn, row_nq, row_vsl, row_vwn)

    h = _post([o_moba, o_nsa], h, even_w_out[0].astype(BF16), mlp_norm[0][None, :],
              mlp_w1[0].astype(BF16), mlp_w2[0].astype(BF16), None, B, S)

    q_w, k_w, v_w, f_w = jnp.split(odd_w_in[0], np.cumsum((FOX_W, FOX_W, FOX_W)), axis=1)
    wfm1 = jnp.concatenate([q_w, v_w], axis=1).T.astype(BF16)
    wk = jnp.pad(k_w.reshape(D, FOX_HEADS, HEAD_DIM), ((0, 0), (0, 0), (0, LANES - HEAD_DIM)))
    wk = wk.reshape(D, FOX_HEADS * LANES).astype(BF16)
    f_w = jnp.pad(f_w, ((0, 0), (0, LANES - FOX_HEADS)))
    f_hi = f_w.astype(BF16)
    wf = jnp.stack([f_hi, (f_w - f_hi.astype(F32)).astype(BF16)])
    bf = jnp.pad(odd_b_forget[0], (0, LANES - FOX_HEADS))[None, :]
    tri = jnp.asarray(np.tril(np.ones((TM, TM))), BF16)
    place = np.zeros((3, LANES, FOX_HEADS * LANES), np.float32)
    for term in range(3):
        place[term, np.arange(FOX_HEADS), np.arange(FOX_HEADS) * LANES + HEAD_DIM + term] = -1.0
    fm1, ka = _inproj1(h, mix_norm[1][None, :], wfm1, wk, wf, bf, tri, jnp.asarray(place, BF16), B, S)
    o_fox = _fox(fm1, ka.reshape(B, S, -1), _causal_tiles(), B, S)

    h = _post([o_fox], h, odd_w_out[0].astype(BF16), mlp_norm[1][None, :],
              mlp_w1[1].astype(BF16), mlp_w2[1].astype(BF16), final_norm[None, :], B, S)
    return h.reshape(B, S, D)
```

```python
import functools
import math

import numpy as np
import jax
import jax.numpy as jnp
from jax import lax
from jax.experimental import pallas as pl
from jax.experimental.pallas import tpu as pltpu

D_MODEL = 1024
HEAD_DIM = 64
MOBA_HEADS = 8
MOBA_BLOCK = 256
MOBA_TOPK = 3
NSA_HEADS = 8
NSA_KV_GROUPS = 2
NSA_HPG = NSA_HEADS // NSA_KV_GROUPS
NSA_CMP_BLOCK = 32
NSA_CMP_STRIDE = 16
NSA_CMP_HIDDEN = 256
NSA_SLC_BLOCK = 64
NSA_TOPN = 16
NSA_WINDOW = 512
NSA_FORCE_SCORE = 1e6
FOX_HEADS = 16
D_FF = 4 * D_MODEL
REL_BUCKETS = 32
REL_MAX_DISTANCE = 1024
RMS_EPS = 1e-5
NEG_INF = -1e30
SCALE = HEAD_DIM ** -0.5

MOBA_W = MOBA_HEADS * HEAD_DIM
NSA_W = NSA_HEADS * HEAD_DIM
NSA_KV_W = NSA_KV_GROUPS * HEAD_DIM
FOX_W = FOX_HEADS * HEAD_DIM

LANES = 128
TQ = 128
TK = 256
TM = 512
CH = 256
FF_CH = 512
VMEM_LIMIT = 56 * 1024 * 1024
NE_BIAS = -(-(REL_MAX_DISTANCE + TK - 1) // TQ)
NE_WIN = -(-(NSA_WINDOW + TK - 1) // TQ)
GZ_ROWS = 16
HPS = 4

assert TK == 2 * TQ and MOBA_BLOCK == TK and TK % NSA_SLC_BLOCK == 0

F32 = jnp.float32
BF16 = jnp.bfloat16


def _dot(a, b):
    return jnp.dot(a, b, preferred_element_type=F32)


def _dot_nt(a, b):
    return lax.dot_general(a, b, (((1,), (1,)), ((), ())), preferred_element_type=F32)


def _dot_tn(a, b):
    return lax.dot_general(a, b, (((0,), (0,)), ((), ())), preferred_element_type=F32)


def _rmsnorm(x, g):
    ms = jnp.mean(x * x, axis=-1, keepdims=True)
    return x * lax.rsqrt(ms + RMS_EPS) * g


def _split3(x):
    a = x.astype(BF16)
    r = x - a.astype(F32)
    b = r.astype(BF16)
    c = (r - b.astype(F32)).astype(BF16)
    return a, b, c


def _const_spec(shape):
    nd = len(shape)
    return pl.BlockSpec(shape, lambda *_: (0,) * nd, pipeline_mode=pl.Buffered(1))


def _params(sem):
    return pltpu.CompilerParams(dimension_semantics=sem, vmem_limit_bytes=VMEM_LIMIT)


def _attend(n_tiles, tile_of, qk_fn, fix_fn, v_fn, s_scr, p_scr, n_heads, first_fix=None):
    for idx, s in enumerate(qk_fn(tile_of(0))):
        s_scr[idx] = s if first_fix is None else first_fix(idx, s)
    p_scr[...] = jnp.zeros_like(p_scr)
    last = n_tiles - 1
    init = tuple((jnp.full((1, TQ), NEG_INF, F32), jnp.zeros((1, TQ), F32), jnp.zeros((HEAD_DIM, TQ), F32),
                  jnp.ones((1, TQ), F32)) for _ in range(n_heads))

    def body(i, state):
        n = tile_of(i)
        n_prev = tile_of(jnp.maximum(i - 1, 0))
        n_next = tile_of(jnp.minimum(i + 1, last))
        pv = [_dot(v_fn(h, n_prev), p_scr[h]) for h in range(n_heads)]
        s_next = qk_fn(n_next)
        out = []
        for h in range(n_heads):
            m, l, acc, a_prev = state[h]
            s = fix_fn(h, n, s_scr)
            m_new = jnp.maximum(m, jnp.max(s, axis=0, keepdims=True))
            alpha = jnp.exp(m - m_new)
            p = jnp.exp(s - m_new)
            l = alpha * l + jnp.sum(p, axis=0, keepdims=True)
            acc = a_prev * acc + pv[h]
            p_scr[h] = p.astype(BF16)
            out.append((m_new, l, acc, alpha))
        for idx, s in enumerate(s_next):
            s_scr[idx] = s
        return tuple(out)

    state = lax.fori_loop(0, n_tiles, body, init)
    n_last = tile_of(last)
    return [(a_prev * acc + _dot(v_fn(h, n_last), p_scr[h])) / l
            for h, (_, l, acc, a_prev) in enumerate(state)]


def _rank_select(val, idx, n_rows, k):
    cnt = jnp.zeros(val.shape, F32)
    for m in range(n_rows):
        vm = val[m:m + 1, :]
        beats = (vm > val) | ((vm == val) & (idx > m))
        cnt = cnt + jnp.where(beats, 1.0, 0.0)
    return cnt < k


def _inproj0_body(x_ref, g_ref, wrm_ref, wfm_ref, wgz_ref, rm_ref, fm_ref, gz_ref):
    xn = _rmsnorm(x_ref[...], g_ref[...]).astype(BF16)
    for c0 in range(0, rm_ref.shape[-1], CH):
        rm_ref[:, c0:c0 + CH] = _dot(xn, wrm_ref[:, c0:c0 + CH]).astype(BF16)
    for r0 in range(0, fm_ref.shape[2], CH):
        res = _dot_nt(wfm_ref[r0:r0 + CH, :], xn).astype(BF16)
        for t in range(TM // TK):
            fm_ref[0, t, r0:r0 + CH, :] = res[:, t * TK:(t + 1) * TK]
    gz_ref[0] = _dot_nt(wgz_ref[...], xn)


def _inproj0(x2, g, wrm, wfm, wgz, B, S):
    M = B * S
    nst = S // TM
    n_rm, n_fm, n_gz = wrm.shape[1], wfm.shape[0], wgz.shape[0]
    return pl.pallas_call(
        _inproj0_body,
        grid=(M // TM,),
        in_specs=[
            pl.BlockSpec((TM, D_MODEL), lambda i: (i, 0)),
            _const_spec((1, D_MODEL)),
            _const_spec((D_MODEL, n_rm)),
            _const_spec((n_fm, D_MODEL)),
            _const_spec((n_gz, D_MODEL)),
        ],
        out_specs=[
            pl.BlockSpec((TM, n_rm), lambda i: (i, 0)),
            pl.BlockSpec((1, TM // TK, n_fm, TK), lambda i: (i // nst, i % nst, 0, 0)),
            pl.BlockSpec((1, n_gz, TM), lambda i: (i // nst, 0, i % nst)),
        ],
        out_shape=[
            jax.ShapeDtypeStruct((M, n_rm), BF16),
            jax.ShapeDtypeStruct((B, S // TK, n_fm, TK), BF16),
            jax.ShapeDtypeStruct((B, n_gz, S), F32),
        ],
        compiler_params=_params(("parallel",)),
        name="inproj0",
    )(x2, g, wrm, wfm, wgz)


def _compress_body(rk_ref, rv_ref, pos_ref, w1_ref, w2k_ref, w2vt_ref, kc_ref, vct_ref):
    half = NSA_CMP_STRIDE * HEAD_DIM

    def hidden(r_ref, s):
        r = r_ref[0, 0]
        a = _dot(r, w1_ref[s, :half, :])
        b = _dot(r, w1_ref[s, half:, :])
        nxt = pltpu.roll(b, 1, axis=0)
        posb = _dot(pos_ref[s], w1_ref[s])[0:1]
        pre = a + nxt + posb
        return (pre * jax.nn.sigmoid(pre)).astype(BF16)

    kc_ref[0, 0] = _dot(hidden(rk_ref, 0), w2k_ref[...]).astype(BF16)
    vct_ref[0, 0] = _dot_nt(w2vt_ref[...], hidden(rv_ref, 1)).astype(BF16)


def _compress(r, pos, w1, w2k, w2vt, B, NC):
    G = NSA_KV_GROUPS
    half = NSA_CMP_STRIDE * HEAD_DIM
    return pl.pallas_call(
        _compress_body,
        grid=(B, G),
        in_specs=[
            pl.BlockSpec((1, 1, NC, half), lambda b, g: (b, g, 0, 0)),
            pl.BlockSpec((1, 1, NC, half), lambda b, g: (b, G + g, 0, 0)),
            _const_spec(pos.shape),
            _const_spec(w1.shape),
            _const_spec(w2k.shape),
            _const_spec(w2vt.shape),
        ],
        out_specs=[
            pl.BlockSpec((1, 1, NC, HEAD_DIM), lambda b, g: (b, g, 0, 0)),
            pl.BlockSpec((1, 1, HEAD_DIM, NC), lambda b, g: (b, g, 0, 0)),
        ],
        out_shape=[
            jax.ShapeDtypeStruct((B, G, NC, HEAD_DIM), BF16),
            jax.ShapeDtypeStruct((B, G, HEAD_DIM, NC), BF16),
        ],
        compiler_params=_params(("parallel", "parallel")),
        name="nsa_compress",
    )(r, r, pos, w1, w2k, w2vt)


def _moba_body(q_ref, k_ref, v_ref, t_ref, o_ref, kmean_ref, mask_ref, s_scr, p_scr, *, n_mb, topk):
    c = pl.program_id(2)
    blk = c // (TK // TQ)

    @pl.when(c == 0)
    def _():
        kmean_ref[...] = jnp.zeros_like(kmean_ref)
        for n in range(n_mb):
            kblk = k_ref[0, n * TK:(n + 1) * TK, :].astype(F32)
            kmean_ref[n:n + 1, :] = jnp.mean(kblk, axis=0, keepdims=True)

    q = q_ref[0, 0]
    rowi = lax.broadcasted_iota(jnp.int32, (LANES, TQ), 0)
    nidx = lax.broadcasted_iota(jnp.int32, (kmean_ref.shape[0], TQ), 0)
    km = _split3(kmean_ref[...])
    qpads = []
    for h in range(HPS):
        lo = (h // 2) * LANES
        qpair = q[lo:lo + LANES, :]
        qh = jnp.where(rowi // HEAD_DIM == h % 2, qpair, jnp.zeros_like(qpair))
        route = sum(_dot(part[:, lo:lo + LANES], qh) for part in km)
        route = jnp.where(nidx < blk, route, NEG_INF)
        sel = _rank_select(route, nidx, n_mb, topk) & (nidx < blk)
        mask_ref[h] = jnp.where(sel | (nidx == blk), 0.0, NEG_INF)
        qpads.append(qh * SCALE)

    def qk_fn(n):
        kt = k_ref[0, pl.ds(pl.multiple_of(n * TK, TK), TK), :]
        return [_dot(kt[:, (h // 2) * LANES:(h // 2 + 1) * LANES], qpads[h]) for h in range(HPS)]

    def fix_fn(h, n, s_scr):
        return s_scr[h] + mask_ref[h, pl.ds(n, 1), :] + t_ref[h, jnp.minimum(c - 2 * n, NE_BIAS)]

    def v_fn(h, n):
        return v_ref[0, n, h * HEAD_DIM:(h + 1) * HEAD_DIM, :]

    outs = _attend(blk + 1, lambda i: blk - i, qk_fn, fix_fn, v_fn, s_scr, p_scr, HPS)
    o_ref[0] = jnp.concatenate(outs, axis=0).astype(BF16)


def _moba(fm, rm, tab, B, S):
    n_mb = S // MOBA_BLOCK
    n_pad = -(-n_mb // 16) * 16
    ne = tab.shape[1]
    rows = HPS * HEAD_DIM
    body = functools.partial(_moba_body, n_mb=n_mb, topk=min(MOBA_TOPK, n_mb))
    return pl.pallas_call(
        body,
        grid=(B, MOBA_HEADS // HPS, S // TQ),
        in_specs=[
            pl.BlockSpec((1, 1, rows, TQ), lambda b, p, c: (b, c // 2, p, c % 2)),
            pl.BlockSpec((1, S, rows), lambda b, p, c: (b, 0, p)),
            pl.BlockSpec((1, S // TK, rows, TK), lambda b, p, c: (b, 0, MOBA_HEADS // HPS + p, 0)),
            pl.BlockSpec((HPS, ne, TK, TQ), lambda b, p, c: (p, 0, 0, 0)),
        ],
        out_specs=pl.BlockSpec((1, rows, TQ), lambda b, p, c: (b, p, c)),
        out_shape=jax.ShapeDtypeStruct((B, MOBA_W, S), BF16),
        scratch_shapes=[pltpu.VMEM((n_pad, rows), F32), pltpu.VMEM((HPS, n_pad, TQ), F32),
                        pltpu.VMEM((HPS, TK, TQ), F32), pltpu.VMEM((HPS, TK, TQ), BF16)],
        compiler_params=_params(("parallel", "parallel", "arbitrary")),
        name="moba_attn",
    )(fm, rm, fm, tab)


def _nsa_body(q_ref, kc_ref, vct_ref, fc_ref, ovl_ref, exp_ref, ksl_ref, vsl_ref, kwn_ref, vwn_ref,
              tslc_ref, twin_ref, gz_ref, o_ref, s_scr, p_scr, *, n_sb, n_sel):
    g = pl.program_id(1)
    c = pl.program_id(2)
    blk = c // (TK // TQ)
    J = NSA_HPG

    q = q_ref[0, 0] * SCALE
    rowi = lax.broadcasted_iota(jnp.int32, (2 * HEAD_DIM, TQ), 0)
    qs = [q[j * HEAD_DIM:(j + 1) * HEAD_DIM, :] for j in range(J)]
    qpads = [jnp.where(rowi // HEAD_DIM == g, jnp.concatenate([qj, qj], axis=0), jnp.zeros((2 * HEAD_DIM, TQ), BF16))
             for qj in qs]

    kc = kc_ref[0, 0]
    vct = vct_ref[0, 0]
    o_cmp = []
    psum = jnp.zeros((kc.shape[0], TQ), F32)
    c0 = pl.multiple_of(c * (TQ // NSA_CMP_STRIDE), TQ // NSA_CMP_STRIDE)
    for j in range(J):
        s = _dot(kc, qs[j]) + fc_ref[j, pl.ds(c0, kc.shape[0]), :]
        m = jnp.max(s, axis=0, keepdims=True)
        p = jnp.exp(s - m)
        l = jnp.sum(p, axis=0, keepdims=True)
        pn = p * jnp.where(m > 0.5 * NEG_INF, 1.0 / l, 0.0)
        o_cmp.append(_dot(vct, pn.astype(BF16)))
        psum = psum + pn

    ph = psum.astype(BF16)
    plo = (psum - ph.astype(F32)).astype(BF16)
    imp = _dot(ovl_ref[...], ph) + _dot(ovl_ref[...], plo)
    jb = lax.broadcasted_iota(jnp.int32, imp.shape, 0)
    t = c * TQ + lax.broadcasted_iota(jnp.int32, imp.shape, 1)
    sb = t // NSA_SLC_BLOCK
    forced = (jb == 0) | (jb == sb) | (jb == sb - 1)
    allowed = jb <= sb
    val = jnp.where(forced, imp + NSA_FORCE_SCORE, jnp.where(allowed, imp, NEG_INF))
    sel = _rank_select(val, jb, n_sb, n_sel) & allowed
    selb = jnp.where(sel, 0.0, NEG_INF).astype(BF16)

    def slc_qk(n):
        k0 = pl.multiple_of(n * TK, TK)
        kt = ksl_ref[0, pl.ds(k0, TK), :]
        return [_dot(kt, qpads[j]) for j in range(J)] + [_dot(exp_ref[pl.ds(k0, TK), :], selb)]

    def slc_fix(j, n, s_ref):
        return s_ref[j] + s_ref[J] + tslc_ref[j, jnp.minimum(c - 2 * n, NE_BIAS)]

    o_slc = _attend(blk + 1, lambda i: blk - i, slc_qk, slc_fix, lambda j, n: vsl_ref[0, n], s_scr, p_scr, J)

    def win_qk(n):
        kt = kwn_ref[0, pl.ds(pl.multiple_of(n * TK, TK), TK), :]
        return [_dot(kt, qpads[j]) for j in range(J)]

    def win_fix(j, n, s_ref):
        return s_ref[j] + twin_ref[j, c - 2 * n]

    w_lo = jnp.maximum(c - (NE_WIN - 2), 0) // 2
    o_win = _attend(blk - w_lo + 1, lambda i: blk - i, win_qk, win_fix, lambda j, n: vwn_ref[0, n], s_scr, p_scr, J)

    gate = jax.nn.sigmoid(gz_ref[0])
    outs = []
    for j in range(J):
        outs.append(gate[j:j + 1, :] * o_cmp[j] + gate[J + j:J + j + 1, :] * o_slc[j]
                    + gate[2 * J + j:2 * J + j + 1, :] * o_win[j])
    o_ref[0] = jnp.concatenate(outs, axis=0).astype(BF16)


def _nsa(fm, rm, gz, kc, vct, tcmp, ovl, expand, tslc, twin, B, S, col_ksl, col_kwn, row_q, row_vsl, row_vwn):
    G, J = NSA_KV_GROUPS, NSA_HPG
    NC = kc.shape[2]
    n_sb = S // NSA_SLC_BLOCK
    body = functools.partial(_nsa_body, n_sb=n_sb, n_sel=min(NSA_TOPN, n_sb))
    qrows = J * HEAD_DIM
    return pl.pallas_call(
        body,
        grid=(B, G, S // TQ),
        in_specs=[
            pl.BlockSpec((1, 1, qrows, TQ), lambda b, g, c: (b, c // 2, row_q // qrows + g, c % 2)),
            pl.BlockSpec((1, 1, NC, HEAD_DIM), lambda b, g, c: (b, g, 0, 0)),
            pl.BlockSpec((1, 1, HEAD_DIM, NC), lambda b, g, c: (b, g, 0, 0)),
            pl.BlockSpec((J, tcmp.shape[1], TQ), lambda b, g, c: (g, 0, 0)),
            _const_spec(ovl.shape),
            _const_spec(expand.shape),
            pl.BlockSpec((1, S, LANES), lambda b, g, c: (b, 0, col_ksl // LANES)),
            pl.BlockSpec((1, S // TK, HEAD_DIM, TK), lambda b, g, c: (b, 0, row_vsl // HEAD_DIM + g, 0)),
            pl.BlockSpec((1, S, LANES), lambda b, g, c: (b, 0, col_kwn // LANES)),
            pl.BlockSpec((1, S // TK, HEAD_DIM, TK), lambda b, g, c: (b, 0, row_vwn // HEAD_DIM + g, 0)),
            pl.BlockSpec((J, tslc.shape[1], TK, TQ), lambda b, g, c: (MOBA_HEADS // J + g, 0, 0, 0)),
            pl.BlockSpec((J, twin.shape[1], TK, TQ), lambda b, g, c: (g, 0, 0, 0)),
            pl.BlockSpec((1, GZ_ROWS, TQ), lambda b, g, c: (b, g, c)),
        ],
        out_specs=pl.BlockSpec((1, qrows, TQ), lambda b, g, c: (b, g, c)),
        out_shape=jax.ShapeDtypeStruct((B, NSA_W, S), BF16),
        scratch_shapes=[pltpu.VMEM((J + 1, TK, TQ), F32), pltpu.VMEM((J, TK, TQ), BF16)],
        compiler_params=_params(("parallel", "parallel", "arbitrary")),
        name="nsa_attn",
    )(fm, kc, vct, tcmp, ovl, expand, rm, fm, rm, fm, tslc, twin, gz)


def _inproj1_body(x_ref, g_ref, wfm_ref, wk_ref, wf_ref, bf_ref, tri_ref, place_ref, fm_ref, ka_ref, carry_ref,
                  *, nst):
    i = pl.program_id(0)

    @pl.when(i % nst == 0)
    def _():
        carry_ref[...] = jnp.zeros_like(carry_ref)

    xf = _rmsnorm(x_ref[...], g_ref[...])
    xn = xf.astype(BF16)
    xlo = (xf - xn.astype(F32)).astype(BF16)
    for r0 in range(0, fm_ref.shape[2], CH):
        res = _dot_nt(wfm_ref[r0:r0 + CH, :], xn).astype(BF16)
        for t in range(TM // TK):
            fm_ref[0, t, r0:r0 + CH, :] = res[:, t * TK:(t + 1) * TK]

    fz = _dot(xn, wf_ref[0]) + _dot(xlo, wf_ref[0]) + _dot(xn, wf_ref[1]) + bf_ref[...]
    logf = jnp.minimum(fz, 0.0) - jnp.log(1.0 + jnp.exp(-jnp.abs(fz)))
    tri = tri_ref[...]
    h1, h2, h3 = _split3(logf)
    cum = _dot(tri, h1) + _dot(tri, h2) + _dot(tri, h3) + carry_ref[0:1, :]
    carry_ref[...] = jnp.broadcast_to(cum[TM - 1:TM, :], carry_ref.shape)
    c1, c2, c3 = _split3(cum)
    for c0 in range(0, ka_ref.shape[-1], FF_CH):
        ka = (_dot(xn, wk_ref[:, c0:c0 + FF_CH]) + _dot(c1, place_ref[0, :, c0:c0 + FF_CH])
              + _dot(c2, place_ref[1, :, c0:c0 + FF_CH]) + _dot(c3, place_ref[2, :, c0:c0 + FF_CH]))
        ka_ref[:, c0:c0 + FF_CH] = ka.astype(BF16)


def _inproj1(x2, g, wfm, wk, wf, bf, tri, place, B, S):
    M = B * S
    nst = S // TM
    n_fm, n_ka = wfm.shape[0], wk.shape[1]
    return pl.pallas_call(
        functools.partial(_inproj1_body, nst=nst),
        grid=(M // TM,),
        in_specs=[
            pl.BlockSpec((TM, D_MODEL), lambda i: (i, 0)),
            _const_spec((1, D_MODEL)),
            _const_spec(wfm.shape),
            _const_spec(wk.shape),
            _const_spec(wf.shape),
            _const_spec(bf.shape),
            _const_spec(tri.shape),
            _const_spec(place.shape),
        ],
        out_specs=[
            pl.BlockSpec((1, TM // TK, n_fm, TK), lambda i: (i // nst, i % nst, 0, 0)),
            pl.BlockSpec((TM, n_ka), lambda i: (i, 0)),
        ],
        out_shape=[
            jax.ShapeDtypeStruct((B, S // TK, n_fm, TK), BF16),
            jax.ShapeDtypeStruct((M, n_ka), BF16),
        ],
        scratch_shapes=[pltpu.VMEM((8, LANES), F32)],
        compiler_params=_params(("arbitrary",)),
        name="inproj1",
    )(x2, g, wfm, wk, wf, bf, tri, place)


def _fox_body(q_ref, k_ref, v_ref, cm_ref, o_ref, s_scr, p_scr):
    c = pl.program_id(2)
    blk = c // (TK // TQ)
    q = q_ref[0, 0] * SCALE
    ones = jnp.ones((LANES - HEAD_DIM, TQ), BF16)
    qas = [jnp.concatenate([q[h * HEAD_DIM:(h + 1) * HEAD_DIM, :], ones], axis=0) for h in range(HPS)]

    def qk_fn(n):
        kt = k_ref[0, pl.ds(pl.multiple_of(n * TK, TK), TK), :]
        return [_dot(kt[:, h * LANES:(h + 1) * LANES], qas[h]) for h in range(HPS)]

    causal = cm_ref[c % 2]
    outs = _attend(blk + 1, lambda i: blk - i, qk_fn, lambda h, n, s_ref: s_ref[h],
                   lambda h, n: v_ref[0, n, h * HEAD_DIM:(h + 1) * HEAD_DIM, :], s_scr, p_scr, HPS,
                   first_fix=lambda idx, s: s + causal)
    o_ref[0] = jnp.concatenate(outs, axis=0).astype(BF16)


def _fox(fm, ka, cmask, B, S):
    rows = HPS * HEAD_DIM
    return pl.pallas_call(
        _fox_body,
        grid=(B, FOX_HEADS // HPS, S // TQ),
        in_specs=[
            pl.BlockSpec((1, 1, rows, TQ), lambda b, h, c: (b, c // 2, h, c % 2)),
            pl.BlockSpec((1, S, HPS * LANES), lambda b, h, c: (b, 0, h)),
            pl.BlockSpec((1, S // TK, rows, TK), lambda b, h, c: (b, 0, FOX_HEADS // HPS + h, 0)),
            _const_spec(cmask.shape),
        ],
        out_specs=pl.BlockSpec((1, rows, TQ), lambda b, h, c: (b, h, c)),
        out_shape=jax.ShapeDtypeStruct((B, FOX_W, S), BF16),
        scratch_shapes=[pltpu.VMEM((HPS, TK, TQ), F32), pltpu.VMEM((HPS, TK, TQ), BF16)],
        compiler_params=_params(("parallel", "parallel", "arbitrary")),
        name="fox_attn",
    )(fm, ka, fm, cmask)


def _post_body(*refs, n_parts, final):
    o_refs = refs[:n_parts]
    h_ref, wo_ref, g_ref, w1_ref, w2_ref = refs[n_parts:n_parts + 5]
    gf_ref = refs[n_parts + 5] if final else None
    out_ref, hn_ref = refs[-2:]
    h1 = h_ref[...]
    r0 = 0
    for o_ref in o_refs:
        nf = o_ref.shape[1]
        h1 = h1 + _dot_tn(o_ref[0], wo_ref[r0:r0 + nf, :])
        r0 += nf
    out_ref[...] = h1
    hn_ref[...] = _rmsnorm(out_ref[...], g_ref[...]).astype(BF16)
    for c0 in range(0, D_FF, FF_CH):
        a = jnp.maximum(_dot(hn_ref[...], w1_ref[:, c0:c0 + FF_CH]), 0.0)
        out_ref[...] += _dot((a * a).astype(BF16), w2_ref[c0:c0 + FF_CH, :])
    if final:
        out_ref[...] = _rmsnorm(out_ref[...], gf_ref[...])


def _post(o_parts, h2, wo, g, w1, w2, gf, B, S):
    M = B * S
    nst = S // TM
    final = gf is not None
    in_specs = [pl.BlockSpec((1, o.shape[1], TM), lambda i: (i // nst, 0, i % nst)) for o in o_parts]
    in_specs += [
        pl.BlockSpec((TM, D_MODEL), lambda i: (i, 0)),
        _const_spec(wo.shape),
        _const_spec((1, D_MODEL)),
        _const_spec(w1.shape),
        _const_spec(w2.shape),
    ]
    args = list(o_parts) + [h2, wo, g, w1, w2]
    if final:
        in_specs.append(_const_spec((1, D_MODEL)))
        args.append(gf)
    return pl.pallas_call(
        functools.partial(_post_body, n_parts=len(o_parts), final=final),
        grid=(M // TM,),
        in_specs=in_specs,
        out_specs=pl.BlockSpec((TM, D_MODEL), lambda i: (i, 0)),
        out_shape=jax.ShapeDtypeStruct((M, D_MODEL), F32),
        scratch_shapes=[pltpu.VMEM((TM, D_MODEL), BF16)],
        compiler_params=_params(("parallel",)),
        name="post_final" if final else "post",
    )(*args)


def _rel_bucket(dist):
    n = jnp.maximum(dist, 0)
    max_exact = REL_BUCKETS // 2
    nf = jnp.maximum(n, 1).astype(jnp.float32)
    large = max_exact + (jnp.log(nf / max_exact) / math.log(REL_MAX_DISTANCE / max_exact)
                         * (REL_BUCKETS - max_exact)).astype(jnp.int32)
    large = jnp.minimum(large, REL_BUCKETS - 1)
    return jnp.where(n < max_exact, n, large)


def _bias_tables(rel_bias, S):
    n_heads = rel_bias.shape[1]
    table = rel_bias.T

    def bias_of(dist):
        bkt = _rel_bucket(jnp.asarray(dist))[None, :]
        out = jnp.zeros((n_heads, len(dist)), F32)
        for b in range(REL_BUCKETS):
            out = jnp.where(bkt == b, table[:, b:b + 1], out)
        return jnp.where(jnp.asarray(dist)[None, :] >= 0, out, NEG_INF)

    def shifted_rows(w, n_rows, step):
        p = w.shape[1]
        return jnp.tile(w, (1, n_rows))[:, :n_rows * (p - step)].reshape(w.shape[0], n_rows, p - step)

    w = bias_of(np.arange(-(TK - 1), NE_BIAS * TQ + TQ))
    toep = shifted_rows(w, TK, 1)[:, :, TK - 1:TK - 1 + NE_BIAS * TQ]
    vals = toep.reshape(n_heads, TK, NE_BIAS, TQ).transpose(0, 2, 1, 3)
    d = (np.arange(NE_BIAS)[:, None, None] * TQ + np.arange(TQ)[None, None, :] - np.arange(TK)[None, :, None])
    far = table[:, REL_BUCKETS - 1][:, None, None, None]
    tile = jnp.where(d >= 0, vals - far, NEG_INF)
    tile = jnp.concatenate([tile, jnp.zeros_like(tile[:, :1])], axis=1)
    dw = d[:NE_WIN]
    twin = jnp.where((dw >= 0) & (dw < NSA_WINDOW), vals[MOBA_HEADS:, :NE_WIN], NEG_INF)
    n_c = S // NSA_CMP_STRIDE
    u = np.arange(2 * n_c)[:, None]
    dc = np.arange(TQ)[None, :] + NSA_CMP_STRIDE * (u - (n_c - 1)) - (NSA_CMP_BLOCK - 1)
    fcmp = bias_of(dc.reshape(-1))[MOBA_HEADS:].reshape(-1, 2 * n_c, TQ)
    return tile, twin, fcmp


def _selection_constants(S):
    n_c = S // NSA_CMP_STRIDE
    n_cmp = (S - NSA_CMP_BLOCK) // NSA_CMP_STRIDE + 1
    n_sb = S // NSA_SLC_BLOCK
    ci = np.arange(n_c)[None, :] * NSA_CMP_STRIDE
    sj = np.arange(n_sb)[:, None] * NSA_SLC_BLOCK
    ovl = (ci < sj + NSA_SLC_BLOCK) & (ci + NSA_CMP_BLOCK > sj) & (np.arange(n_c)[None, :] < n_cmp)
    ovl = ovl[:, ::-1]
    expand =(np.arange(S)[:, None] // NSA_SLC_BLOCK) == np.arange(n_sb)[None, :]
    return jnp.asarray(ovl, BF16), jnp.asarray(expand, BF16)


def _causal_tiles():
    e = np.arange(2)[:, None, None]
    d = e * TQ + np.arange(TQ)[None, None, :] - np.arange(TK)[None, :, None]
    return jnp.asarray(np.where(d >= 0, 0.0, NEG_INF), F32)


def kernel(x, rel_bias, mix_norm, mlp_norm, even_w_in, even_w_out, cmp_pos_k, cmp_pos_v, cmp_k_w1, cmp_k_w2,
           cmp_v_w1, cmp_v_w2, odd_w_in, odd_b_forget, odd_w_out, mlp_w1, mlp_w2, final_norm):
    B, S, D = x.shape
    assert D == D_MODEL and S % TM == 0
    G, J = NSA_KV_GROUPS, NSA_HPG
    h = x.reshape(B * S, D)

    offs = np.cumsum((MOBA_W, MOBA_W, MOBA_W, NSA_W) + (NSA_KV_W,) * 6)
    mq_w, mk_w, mv_w, nq_w, kc_w, vc_w, ksl_w, vsl_w, kwn_w, vwn_w, gz_w = jnp.split(even_w_in[0], offs, axis=1)
    wrm = jnp.concatenate([mk_w, kc_w, vc_w, ksl_w, kwn_w], axis=1).astype(BF16)
    col_kcvc, col_ksl, col_kwn = MOBA_W, MOBA_W + 2 * NSA_KV_W, MOBA_W + 3 * NSA_KV_W
    wfm = jnp.concatenate([mq_w, mv_w, nq_w, vsl_w, vwn_w], axis=1).T.astype(BF16)
    row_nq, row_vsl, row_vwn = 2 * MOBA_W, 2 * MOBA_W + NSA_W, 2 * MOBA_W + NSA_W + NSA_KV_W
    gzw = gz_w.T.reshape(G, J, 3, D).transpose(0, 2, 1, 3).reshape(G, 3 * J, D)
    gzw = jnp.pad(gzw, ((0, 0), (0, GZ_ROWS - 3 * J), (0, 0))).reshape(G * GZ_ROWS, D).astype(BF16)

    rm, fm, gz = _inproj0(h, mix_norm[0][None, :], wrm, wfm, gzw, B, S)
    rm = rm.reshape(B, S, -1)

    tile, twin, tcmp = _bias_tables(rel_bias, S)
    ovl, expand = _selection_constants(S)

    o_moba = _moba(fm, rm, tile, B, S)

    n_c = S // NSA_CMP_STRIDE
    r = rm[:, :, col_kcvc:col_kcvc + 2 * NSA_KV_W].reshape(B, n_c, NSA_CMP_STRIDE, 2 * G, HEAD_DIM)
    r = r.transpose(0, 3, 1, 2, 4).reshape(B, 2 * G, n_c, NSA_CMP_STRIDE * HEAD_DIM)[:, :, ::-1]
    pos = jnp.stack([cmp_pos_k[0].reshape(1, -1), cmp_pos_v[0].reshape(1, -1)])
    pos = jnp.pad(pos, ((0, 0), (0, 7), (0, 0))).astype(BF16)
    w1c = jnp.stack([cmp_k_w1[0], cmp_v_w1[0]]).astype(BF16)
    kc, vct = _compress(r, pos, w1c, cmp_k_w2[0].astype(BF16), cmp_v_w2[0].T.astype(BF16), B, n_c)

    o_nsa = _nsa(fm, rm, gz, kc, vct, tcmp, ovl, expand, tile, twin, B, S,
                 col_ksl, col_kwn, row_nq, row_vsl, row_vwn)

    h = _post([o_moba, o_nsa], h, even_w_out[0].astype(BF16), mlp_norm[0][None, :],
              mlp_w1[0].astype(BF16), mlp_w2[0].astype(BF16), None, B, S)

    q_w, k_w, v_w, f_w = jnp.split(odd_w_in[0], np.cumsum((FOX_W, FOX_W, FOX_W)), axis=1)
    wfm1 = jnp.concatenate([q_w, v_w], axis=1).T.astype(BF16)
    wk = jnp.pad(k_w.reshape(D, FOX_HEADS, HEAD_DIM), ((0, 0), (0, 0), (0, LANES - HEAD_DIM)))
    wk = wk.reshape(D, FOX_HEADS * LANES).astype(BF16)
    f_w = jnp.pad(f_w, ((0, 0), (0, LANES - FOX_HEADS)))
    f_hi = f_w.astype(BF16)
    wf = jnp.stack([f_hi, (f_w - f_hi.astype(F32)).astype(BF16)])
    bf = jnp.pad(odd_b_forget[0], (0, LANES - FOX_HEADS))[None, :]
    tri = jnp.asarray(np.tril(np.ones((TM, TM))), BF16)
    place = np.zeros((3, LANES, FOX_HEADS * LANES), np.float32)
    for term in range(3):
        place[term, np.arange(FOX_HEADS), np.arange(FOX_HEADS) * LANES + HEAD_DIM + term] = -1.0
    fm1, ka = _inproj1(h, mix_norm[1][None, :], wfm1, wk, wf, bf, tri, jnp.asarray(place, BF16), B, S)
    o_fox = _fox(fm1, ka.reshape(B, S, -1), _causal_tiles(), B, S)

    h = _post([o_fox], h, odd_w_out[0].astype(BF16), mlp_norm[1][None, :],
              mlp_w1[1].astype(BF16), mlp_w2[1].astype(BF16), final_norm[None, :], B, S)
    return h.reshape(B, S, D)
```

```python
import functools
import math

import numpy as np
import jax
import jax.numpy as jnp
from jax import lax
from jax.experimental import pallas as pl
from jax.experimental.pallas import tpu as pltpu

D_MODEL = 1024
HEAD_DIM = 64
MOBA_HEADS = 8
MOBA_BLOCK = 256
MOBA_TOPK = 3
NSA_HEADS = 8
NSA_KV_GROUPS = 2
NSA_HPG = NSA_HEADS // NSA_KV_GROUPS
NSA_CMP_BLOCK = 32
NSA_CMP_STRIDE = 16
NSA_CMP_HIDDEN = 256
NSA_SLC_BLOCK = 64
NSA_TOPN = 16
NSA_WINDOW = 512
NSA_FORCE_SCORE = 1e6
FOX_HEADS = 16
D_FF = 4 * D_MODEL
REL_BUCKETS = 32
REL_MAX_DISTANCE = 1024
RMS_EPS = 1e-5
NEG_INF = -1e30
SCALE = HEAD_DIM ** -0.5

MOBA_W = MOBA_HEADS * HEAD_DIM
NSA_W = NSA_HEADS * HEAD_DIM
NSA_KV_W = NSA_KV_GROUPS * HEAD_DIM
FOX_W = FOX_HEADS * HEAD_DIM

LANES = 128
TQ = 256
TK = 256
QPK = TK // TQ
TM = 512
CH = 256
FF_CH = 512
VMEM_LIMIT = 56 * 1024 * 1024
NE_BIAS = -(-(REL_MAX_DISTANCE + TK - 1) // TQ)
NE_WIN = -(-(NSA_WINDOW + TK - 1) // TQ)
GZ_ROWS = 16
HPS = 4
SPT = TK // NSA_SLC_BLOCK

assert TK % TQ == 0 and MOBA_BLOCK == TK and TK % NSA_SLC_BLOCK == 0

F32 = jnp.float32
BF16 = jnp.bfloat16


def _dot(a, b):
    return jnp.dot(a, b, preferred_element_type=F32)


def _dot_nt(a, b):
    return lax.dot_general(a, b, (((1,), (1,)), ((), ())), preferred_element_type=F32)


def _dot_tn(a, b):
    return lax.dot_general(a, b, (((0,), (0,)), ((), ())), preferred_element_type=F32)


def _rmsnorm(x, g):
    ms = jnp.mean(x * x, axis=-1, keepdims=True)
    return x * lax.rsqrt(ms + RMS_EPS) * g


def _split3(x):
    a = x.astype(BF16)
    r = x - a.astype(F32)
    b = r.astype(BF16)
    c = (r - b.astype(F32)).astype(BF16)
    return a, b, c


def _const_spec(shape):
    nd = len(shape)
    return pl.BlockSpec(shape, lambda *_: (0,) * nd, pipeline_mode=pl.Buffered(1))


def _params(sem):
    return pltpu.CompilerParams(dimension_semantics=sem, vmem_limit_bytes=VMEM_LIMIT)


def _attend(n_tiles, tile_of, qk_fn, fix_fn, v_fn, s_scr, p_scr, n_heads, first_fix=None):
    for idx, s in enumerate(qk_fn(tile_of(0))):
        s_scr[idx] = s if first_fix is None else first_fix(idx, s)
    p_scr[...] = jnp.zeros_like(p_scr)
    last = n_tiles - 1
    init = tuple((jnp.full((1, TQ), NEG_INF, F32), jnp.zeros((1, TQ), F32), jnp.zeros((HEAD_DIM, TQ), F32),
                  jnp.ones((1, TQ), F32)) for _ in range(n_heads))

    def body(i, state):
        n = tile_of(i)
        n_prev = tile_of(jnp.maximum(i - 1, 0))
        n_next = tile_of(jnp.minimum(i + 1, last))
        pv = [_dot(v_fn(h, n_prev), p_scr[h]) for h in range(n_heads)]
        s_next = qk_fn(n_next)
        out = []
        for h in range(n_heads):
            m, l, acc, a_prev = state[h]
            s = fix_fn(h, n, s_scr)
            m_new = jnp.maximum(m, jnp.max(s, axis=0, keepdims=True))
            alpha = jnp.exp(m - m_new)
            p = jnp.exp(s - m_new)
            l = alpha * l + jnp.sum(p, axis=0, keepdims=True)
            acc = a_prev * acc + pv[h]
            p_scr[h] = p.astype(BF16)
            out.append((m_new, l, acc, alpha))
        for idx, s in enumerate(s_next):
            s_scr[idx] = s
        return tuple(out)

    state = lax.fori_loop(0, n_tiles, body, init)
    n_last = tile_of(last)
    return [(a_prev * acc + _dot(v_fn(h, n_last), p_scr[h])) / l
            for h, (_, l, acc, a_prev) in enumerate(state)]


def _rank_select(val, idx, n_rows, k):
    cnt = jnp.zeros(val.shape, F32)
    for m in range(n_rows):
        vm = val[m:m + 1, :]
        beats = (vm > val) | ((vm == val) & (idx > m))
        cnt = cnt + jnp.where(beats, 1.0, 0.0)
    return cnt < k


def _inproj0_body(x_ref, g_ref, wrm_ref, wfm_ref, wgz_ref, rm_ref, fm_ref, gz_ref):
    xn = _rmsnorm(x_ref[...], g_ref[...]).astype(BF16)
    for c0 in range(0, rm_ref.shape[-1], CH):
        rm_ref[:, c0:c0 + CH] = _dot(xn, wrm_ref[:, c0:c0 + CH]).astype(BF16)
    for r0 in range(0, fm_ref.shape[2], CH):
        res = _dot_nt(wfm_ref[r0:r0 + CH, :], xn).astype(BF16)
        for t in range(TM // TK):
            fm_ref[0, t, r0:r0 + CH, :] = res[:, t * TK:(t + 1) * TK]
    gz_ref[0] = _dot_nt(wgz_ref[...], xn)


def _inproj0(x2, g, wrm, wfm, wgz, B, S):
    M = B * S
    nst = S // TM
    n_rm, n_fm, n_gz = wrm.shape[1], wfm.shape[0], wgz.shape[0]
    return pl.pallas_call(
        _inproj0_body,
        grid=(M // TM,),
        in_specs=[
            pl.BlockSpec((TM, D_MODEL), lambda i: (i, 0)),
            _const_spec((1, D_MODEL)),
            _const_spec((D_MODEL, n_rm)),
            _const_spec((n_fm, D_MODEL)),
            _const_spec((n_gz, D_MODEL)),
        ],
        out_specs=[
            pl.BlockSpec((TM, n_rm), lambda i: (i, 0)),
            pl.BlockSpec((1, TM // TK, n_fm, TK), lambda i: (i // nst, i % nst, 0, 0)),
            pl.BlockSpec((1, n_gz, TM), lambda i: (i // nst, 0, i % nst)),
        ],
        out_shape=[
            jax.ShapeDtypeStruct((M, n_rm), BF16),
            jax.ShapeDtypeStruct((B, S // TK, n_fm, TK), BF16),
            jax.ShapeDtypeStruct((B, n_gz, S), F32),
        ],
        compiler_params=_params(("parallel",)),
        name="inproj0",
    )(x2, g, wrm, wfm, wgz)


def _compress_body(rk_ref, rv_ref, pos_ref, w1_ref, w2k_ref, w2vt_ref, flip_ref, kc_ref, vct_ref):
    half = NSA_CMP_STRIDE * HEAD_DIM

    def hidden(r_ref, s):
        r = r_ref[0, 0]
        a = _dot(r, w1_ref[s, :half, :])
        b = _dot(r, w1_ref[s, half:, :])
        nxt = pltpu.roll(b, b.shape[0] - 1, axis=0)
        posb = _dot(pos_ref[s], w1_ref[s])[0:1]
        pre = a + nxt + posb
        act = (pre * jax.nn.sigmoid(pre)).astype(BF16)
        return _dot(flip_ref[...], act).astype(BF16)

    kc_ref[0, 0] = _dot(hidden(rk_ref, 0), w2k_ref[...]).astype(BF16)
    vct_ref[0, 0] = _dot_nt(w2vt_ref[...], hidden(rv_ref, 1)).astype(BF16)


def _compress(r, pos, w1, w2k, w2vt, B, NC):
    G = NSA_KV_GROUPS
    half = NSA_CMP_STRIDE * HEAD_DIM
    return pl.pallas_call(
        _compress_body,
        grid=(B, G),
        in_specs=[
            pl.BlockSpec((1, 1, NC, half), lambda b, g: (b, g, 0, 0)),
            pl.BlockSpec((1, 1, NC, half), lambda b, g: (b, G + g, 0, 0)),
            _const_spec(pos.shape),
            _const_spec(w1.shape),
            _const_spec(w2k.shape),
            _const_spec(w2vt.shape),
            _const_spec((NC, NC)),
        ],
        out_specs=[
            pl.BlockSpec((1, 1, NC, HEAD_DIM), lambda b, g: (b, g, 0, 0)),
            pl.BlockSpec((1, 1, HEAD_DIM, NC), lambda b, g: (b, g, 0, 0)),
        ],
        out_shape=[
            jax.ShapeDtypeStruct((B, G, NC, HEAD_DIM), BF16),
            jax.ShapeDtypeStruct((B, G, HEAD_DIM, NC), BF16),
        ],
        compiler_params=_params(("parallel", "parallel")),
        name="nsa_compress",
    )(r, r, pos, w1, w2k, w2vt, jnp.asarray(np.eye(NC)[::-1], BF16))


def _moba_body(q_ref, k_ref, v_ref, t_ref, o_ref, kmean_ref, mask_ref, s_scr, p_scr, *, n_mb, topk):
    c = pl.program_id(2)
    blk = c // QPK

    @pl.when(c == 0)
    def _():
        kmean_ref[...] = jnp.zeros_like(kmean_ref)
        for n in range(n_mb):
            kblk = k_ref[0, n * TK:(n + 1) * TK, :].astype(F32)
            kmean_ref[n:n + 1, :] = jnp.mean(kblk, axis=0, keepdims=True)

    q = q_ref[0, 0]
    rowi = lax.broadcasted_iota(jnp.int32, (LANES, TQ), 0)
    nidx = lax.broadcasted_iota(jnp.int32, (kmean_ref.shape[0], TQ), 0)
    km = _split3(kmean_ref[...])
    qpads = []
    for h in range(HPS):
        lo = (h // 2) * LANES
        qpair = q[lo:lo + LANES, :]
        qh = jnp.where(rowi // HEAD_DIM == h % 2, qpair, jnp.zeros_like(qpair))
        route = sum(_dot(part[:, lo:lo + LANES], qh) for part in km)
        route = jnp.where(nidx < blk, route, NEG_INF)
        sel = _rank_select(route, nidx, n_mb, topk) & (nidx < blk)
        mask_ref[h] = jnp.where(sel | (nidx == blk), 0.0, NEG_INF)
        qpads.append(qh * SCALE)

    def qk_fn(n):
        kt = k_ref[0, pl.ds(pl.multiple_of(n * TK, TK), TK), :]
        return [_dot(kt[:, (h // 2) * LANES:(h // 2 + 1) * LANES], qpads[h]) for h in range(HPS)]

    def fix_fn(h, n, s_scr):
        return s_scr[h] + mask_ref[h, pl.ds(n, 1), :] + t_ref[h, jnp.minimum(c - QPK * n, NE_BIAS)]

    def v_fn(h, n):
        return v_ref[0, n, h * HEAD_DIM:(h + 1) * HEAD_DIM, :]

    outs = _attend(blk + 1, lambda i: blk - i, qk_fn, fix_fn, v_fn, s_scr, p_scr, HPS)
    o_ref[0] = jnp.concatenate(outs, axis=0).astype(BF16)


def _moba(fm, rm, tab, B, S):
    n_mb = S // MOBA_BLOCK
    n_pad = -(-n_mb // 16) * 16
    ne = tab.shape[1]
    rows = HPS * HEAD_DIM
    body = functools.partial(_moba_body, n_mb=n_mb, topk=min(MOBA_TOPK, n_mb))
    return pl.pallas_call(
        body,
        grid=(B, MOBA_HEADS // HPS, S // TQ),
        in_specs=[
            pl.BlockSpec((1, 1, rows, TQ), lambda b, p, c: (b, c // QPK, p, c % QPK)),
            pl.BlockSpec((1, S, rows), lambda b, p, c: (b, 0, p)),
            pl.BlockSpec((1, S // TK, rows, TK), lambda b, p, c: (b, 0, MOBA_HEADS // HPS + p, 0)),
            pl.BlockSpec((HPS, ne, TK, TQ), lambda b, p, c: (p, 0, 0, 0)),
        ],
        out_specs=pl.BlockSpec((1, rows, TQ), lambda b, p, c: (b, p, c)),
        out_shape=jax.ShapeDtypeStruct((B, MOBA_W, S), BF16),
        scratch_shapes=[pltpu.VMEM((n_pad, rows), F32), pltpu.VMEM((HPS, n_pad, TQ), F32),
                        pltpu.VMEM((HPS, TK, TQ), F32), pltpu.VMEM((HPS, TK, TQ), BF16)],
        compiler_params=_params(("parallel", "parallel", "arbitrary")),
        name="moba_attn",
    )(fm, rm, fm, tab)


def _nsa_body(q_ref, kc_ref, vct_ref, fc_ref, ovl_ref, ksl_ref, vsl_ref, kwn_ref, vwn_ref,
              tslc_ref, twin_ref, gz_ref, o_ref, s_scr, p_scr, s_win, p_win, sel_ref, *, n_sb, n_sel):
    g = pl.program_id(1)
    c = pl.program_id(2)
    blk = c // QPK
    J = NSA_HPG

    q = q_ref[0, 0] * SCALE
    rowi = lax.broadcasted_iota(jnp.int32, (2 * HEAD_DIM, TQ), 0)
    qs = [q[j * HEAD_DIM:(j + 1) * HEAD_DIM, :] for j in range(J)]
    qpads = [jnp.where(rowi // HEAD_DIM == g, jnp.concatenate([qj, qj], axis=0), jnp.zeros((2 * HEAD_DIM, TQ), BF16))
             for qj in qs]

    kc = kc_ref[0, 0]
    vct = vct_ref[0, 0]
    o_cmp = []
    psum = jnp.zeros((kc.shape[0], TQ), F32)
    c0 = pl.multiple_of(c * (TQ // NSA_CMP_STRIDE), TQ // NSA_CMP_STRIDE)
    for j in range(J):
        s = _dot(kc, qs[j]) + fc_ref[j, pl.ds(c0, kc.shape[0]), :]
        m = jnp.max(s, axis=0, keepdims=True)
        p = jnp.exp(s - m)
        l = jnp.sum(p, axis=0, keepdims=True)
        pn = p * jnp.where(m > 0.5 * NEG_INF, 1.0 / l, 0.0)
        o_cmp.append(_dot(vct, pn.astype(BF16)))
        psum = psum + pn

    ph = psum.astype(BF16)
    plo = (psum - ph.astype(F32)).astype(BF16)
    imp = _dot(ovl_ref[...], ph) + _dot(ovl_ref[...], plo)
    jb = lax.broadcasted_iota(jnp.int32, imp.shape, 0)
    t = c * TQ + lax.broadcasted_iota(jnp.int32, imp.shape, 1)
    sb = t // NSA_SLC_BLOCK
    forced = (jb == 0) | (jb == sb) | (jb == sb - 1)
    allowed = jb <= sb
    val = jnp.where(forced, imp + NSA_FORCE_SCORE, jnp.where(allowed, imp, NEG_INF))
    sel = _rank_select(val, jb, n_sb, n_sel) & allowed
    selb = jnp.where(sel, 0.0, NEG_INF)
    pad = jnp.zeros((sel_ref.shape[1] - SPT, TQ), F32)
    for n in range(n_sb // SPT):
        sel_ref[n] = jnp.concatenate([selb[n * SPT:(n + 1) * SPT, :], pad], axis=0)

    def slc_qk(n):
        kt = ksl_ref[0, pl.ds(pl.multiple_of(n * TK, TK), TK), :]
        return [_dot(kt, qpads[j]) for j in range(J)]

    def slc_fix(j, n, s_ref):
        rows = sel_ref[n]
        mask = jnp.concatenate([jnp.broadcast_to(rows[b:b + 1, :], (NSA_SLC_BLOCK, TQ)) for b in range(SPT)], axis=0)
        return s_ref[j] + mask + tslc_ref[j, jnp.minimum(c - QPK * n, NE_BIAS)]

    o_slc = _attend(blk + 1, lambda i: blk - i, slc_qk, slc_fix, lambda j, n: vsl_ref[0, n], s_scr, p_scr, J)

    def win_qk(n):
        kt = kwn_ref[0, pl.ds(pl.multiple_of(n * TK, TK), TK), :]
        return [_dot(kt, qpads[j]) for j in range(J)]

    def win_fix(j, n, s_ref):
        return s_ref[j] + twin_ref[j, c - QPK * n]

    w_lo = jnp.maximum(c - NE_WIN + QPK, 0) // QPK
    o_win = _attend(blk - w_lo + 1, lambda i: blk - i, win_qk, win_fix, lambda j, n: vwn_ref[0, n], s_win, p_win, J)

    gate = jax.nn.sigmoid(gz_ref[0])
    outs = []
    for j in range(J):
        outs.append(gate[j:j + 1, :] * o_cmp[j] + gate[J + j:J + j + 1, :] * o_slc[j]
                    + gate[2 * J + j:2 * J + j + 1, :] * o_win[j])
    o_ref[0] = jnp.concatenate(outs, axis=0).astype(BF16)


def _nsa(fm, rm, gz, kc, vct, tcmp, ovl, tslc, twin, B, S, col_ksl, col_kwn, row_q, row_vsl, row_vwn):
    G, J = NSA_KV_GROUPS, NSA_HPG
    NC = kc.shape[2]
    n_sb = S // NSA_SLC_BLOCK
    body = functools.partial(_nsa_body, n_sb=n_sb, n_sel=min(NSA_TOPN, n_sb))
    qrows = J * HEAD_DIM
    return pl.pallas_call(
        body,
        grid=(B, G, S // TQ),
        in_specs=[
            pl.BlockSpec((1, 1, qrows, TQ), lambda b, g, c: (b, c // QPK, row_q // qrows + g, c % QPK)),
            pl.BlockSpec((1, 1, NC, HEAD_DIM), lambda b, g, c: (b, g, 0, 0)),
            pl.BlockSpec((1, 1, HEAD_DIM, NC), lambda b, g, c: (b, g, 0, 0)),
            pl.BlockSpec((J, tcmp.shape[1], TQ), lambda b, g, c: (g, 0, 0)),
            _const_spec(ovl.shape),
            pl.BlockSpec((1, S, LANES), lambda b, g, c: (b, 0, col_ksl // LANES)),
            pl.BlockSpec((1, S // TK, HEAD_DIM, TK), lambda b, g, c: (b, 0, row_vsl // HEAD_DIM + g, 0)),
            pl.BlockSpec((1, S, LANES), lambda b, g, c: (b, 0, col_kwn // LANES)),
            pl.BlockSpec((1, S // TK, HEAD_DIM, TK), lambda b, g, c: (b, 0, row_vwn // HEAD_DIM + g, 0)),
            pl.BlockSpec((J, tslc.shape[1], TK, TQ), lambda b, g, c: (MOBA_HEADS // J + g, 0, 0, 0)),
            pl.BlockSpec((J, twin.shape[1], TK, TQ), lambda b, g, c: (g, 0, 0, 0)),
            pl.BlockSpec((1, GZ_ROWS, TQ), lambda b, g, c: (b, g, c)),
        ],
        out_specs=pl.BlockSpec((1, qrows, TQ), lambda b, g, c: (b, g, c)),
        out_shape=jax.ShapeDtypeStruct((B, NSA_W, S), BF16),
        scratch_shapes=[pltpu.VMEM((J, TK, TQ), F32), pltpu.VMEM((J, TK, TQ), BF16),
                        pltpu.VMEM((J, TK, TQ), F32), pltpu.VMEM((J, TK, TQ), BF16),
                        pltpu.VMEM((S // TK, 8, TQ), F32)],
        compiler_params=_params(("parallel", "parallel", "arbitrary")),
        name="nsa_attn",
    )(fm, kc, vct, tcmp, ovl, rm, fm, rm, fm, tslc, twin, gz)


def _inproj1_body(x_ref, g_ref, wfm_ref, wk_ref, wf_ref, bf_ref, tri_ref, place_ref, fm_ref, ka_ref, carry_ref,
                  *, nst):
    i = pl.program_id(0)

    @pl.when(i % nst == 0)
    def _():
        carry_ref[...] = jnp.zeros_like(carry_ref)

    xf = _rmsnorm(x_ref[...], g_ref[...])
    xn = xf.astype(BF16)
    xlo = (xf - xn.astype(F32)).astype(BF16)
    for r0 in range(0, fm_ref.shape[2], CH):
        res = _dot_nt(wfm_ref[r0:r0 + CH, :], xn).astype(BF16)
        for t in range(TM // TK):
            fm_ref[0, t, r0:r0 + CH, :] = res[:, t * TK:(t + 1) * TK]

    fz = _dot(xn, wf_ref[0]) + _dot(xlo, wf_ref[0]) + _dot(xn, wf_ref[1]) + bf_ref[...]
    logf = jnp.minimum(fz, 0.0) - jnp.log(1.0 + jnp.exp(-jnp.abs(fz)))
    tri = tri_ref[...]
    h1, h2, h3 = _split3(logf)
    cum = _dot(tri, h1) + _dot(tri, h2) + _dot(tri, h3) + carry_ref[0:1, :]
    carry_ref[...] = jnp.broadcast_to(cum[TM - 1:TM, :], carry_ref.shape)
    c1, c2, c3 = _split3(cum)
    for c0 in range(0, ka_ref.shape[-1], FF_CH):
        ka = (_dot(xn, wk_ref[:, c0:c0 + FF_CH]) + _dot(c1, place_ref[0, :, c0:c0 + FF_CH])
              + _dot(c2, place_ref[1, :, c0:c0 + FF_CH]) + _dot(c3, place_ref[2, :, c0:c0 + FF_CH]))
        ka_ref[:, c0:c0 + FF_CH] = ka.astype(BF16)


def _inproj1(x2, g, wfm, wk, wf, bf, tri, place, B, S):
    M = B * S
    nst = S // TM
    n_fm, n_ka = wfm.shape[0], wk.shape[1]
    return pl.pallas_call(
        functools.partial(_inproj1_body, nst=nst),
        grid=(M // TM,),
        in_specs=[
            pl.BlockSpec((TM, D_MODEL), lambda i: (i, 0)),
            _const_spec((1, D_MODEL)),
            _const_spec(wfm.shape),
            _const_spec(wk.shape),
            _const_spec(wf.shape),
            _const_spec(bf.shape),
            _const_spec(tri.shape),
            _const_spec(place.shape),
        ],
        out_specs=[
            pl.BlockSpec((1, TM // TK, n_fm, TK), lambda i: (i // nst, i % nst, 0, 0)),
            pl.BlockSpec((TM, n_ka), lambda i: (i, 0)),
        ],
        out_shape=[
            jax.ShapeDtypeStruct((B, S // TK, n_fm, TK), BF16),
            jax.ShapeDtypeStruct((M, n_ka), BF16),
        ],
        scratch_shapes=[pltpu.VMEM((8, LANES), F32)],
        compiler_params=_params(("arbitrary",)),
        name="inproj1",
    )(x2, g, wfm, wk, wf, bf, tri, place)


def _fox_body(q_ref, k_ref, v_ref, cm_ref, o_ref, s_scr, p_scr):
    c = pl.program_id(2)
    blk = c // QPK
    q = q_ref[0, 0] * SCALE
    ones = jnp.ones((LANES - HEAD_DIM, TQ), BF16)
    qas = [jnp.concatenate([q[h * HEAD_DIM:(h + 1) * HEAD_DIM, :], ones], axis=0) for h in range(HPS)]

    def qk_fn(n):
        kt = k_ref[0, pl.ds(pl.multiple_of(n * TK, TK), TK), :]
        return [_dot(kt[:, h * LANES:(h + 1) * LANES], qas[h]) for h in range(HPS)]

    causal = cm_ref[c % QPK]
    outs = _attend(blk + 1, lambda i: blk - i, qk_fn, lambda h, n, s_ref: s_ref[h],
                   lambda h, n: v_ref[0, n, h * HEAD_DIM:(h + 1) * HEAD_DIM, :], s_scr, p_scr, HPS,
                   first_fix=lambda idx, s: s + causal)
    o_ref[0] = jnp.concatenate(outs, axis=0).astype(BF16)


def _fox(fm, ka, cmask, B, S):
    rows = HPS * HEAD_DIM
    return pl.pallas_call(
        _fox_body,
        grid=(B, FOX_HEADS // HPS, S // TQ),
        in_specs=[
            pl.BlockSpec((1, 1, rows, TQ), lambda b, h, c: (b, c // QPK, h, c % QPK)),
            pl.BlockSpec((1, S, HPS * LANES), lambda b, h, c: (b, 0, h)),
            pl.BlockSpec((1, S // TK, rows, TK), lambda b, h, c: (b, 0, FOX_HEADS // HPS + h, 0)),
            _const_spec(cmask.shape),
        ],
        out_specs=pl.BlockSpec((1, rows, TQ), lambda b, h, c: (b, h, c)),
        out_shape=jax.ShapeDtypeStruct((B, FOX_W, S), BF16),
        scratch_shapes=[pltpu.VMEM((HPS, TK, TQ), F32), pltpu.VMEM((HPS, TK, TQ), BF16)],
        compiler_params=_params(("parallel", "parallel", "arbitrary")),
        name="fox_attn",
    )(fm, ka, fm, cmask)


def _post_body(*refs, n_parts, final):
    o_refs = refs[:n_parts]
    h_ref, wo_ref, g_ref, w1_ref, w2_ref = refs[n_parts:n_parts + 5]
    gf_ref = refs[n_parts + 5] if final else None
    out_ref, hn_ref = refs[-2:]
    h1 = h_ref[...]
    r0 = 0
    for o_ref in o_refs:
        nf = o_ref.shape[1]
        h1 = h1 + _dot_tn(o_ref[0], wo_ref[r0:r0 + nf, :])
        r0 += nf
    out_ref[...] = h1
    hn_ref[...] = _rmsnorm(out_ref[...], g_ref[...]).astype(BF16)
    for c0 in range(0, D_FF, FF_CH):
        a = jnp.maximum(_dot(hn_ref[...], w1_ref[:, c0:c0 + FF_CH]), 0.0)
        out_ref[...] += _dot((a * a).astype(BF16), w2_ref[c0:c0 + FF_CH, :])
    if final:
        out_ref[...] = _rmsnorm(out_ref[...], gf_ref[...])


def _post(o_parts, h2, wo, g, w1, w2, gf, B, S):
    M = B * S
    nst = S // TM
    final = gf is not None
    in_specs = [pl.BlockSpec((1, o.shape[1], TM), lambda i: (i // nst, 0, i % nst)) for o in o_parts]
    in_specs += [
        pl.BlockSpec((TM, D_MODEL), lambda i: (i, 0)),
        _const_spec(wo.shape),
        _const_spec((1, D_MODEL)),
        _const_spec(w1.shape),
        _const_spec(w2.shape),
    ]
    args = list(o_parts) + [h2, wo, g, w1, w2]
    if final:
        in_specs.append(_const_spec((1, D_MODEL)))
        args.append(gf)
    return pl.pallas_call(
        functools.partial(_post_body, n_parts=len(o_parts), final=final),
        grid=(M // TM,),
        in_specs=in_specs,
        out_specs=pl.BlockSpec((TM, D_MODEL), lambda i: (i, 0)),
        out_shape=jax.ShapeDtypeStruct((M, D_MODEL), F32),
        scratch_shapes=[pltpu.VMEM((TM, D_MODEL), BF16)],
        compiler_params=_params(("parallel",)),
        name="post_final" if final else "post",
    )(*args)


def _rel_bucket(dist):
    n = jnp.maximum(dist, 0)
    max_exact = REL_BUCKETS // 2
    nf = jnp.maximum(n, 1).astype(jnp.float32)
    large = max_exact + (jnp.log(nf / max_exact) / math.log(REL_MAX_DISTANCE / max_exact)
                         * (REL_BUCKETS - max_exact)).astype(jnp.int32)
    large = jnp.minimum(large, REL_BUCKETS - 1)
    return jnp.where(n < max_exact, n, large)


def _bias_tables(rel_bias, S):
    n_heads = rel_bias.shape[1]
    table = rel_bias.T

    def bias_of(dist):
        bkt = _rel_bucket(jnp.asarray(dist))[None, :]
        out = jnp.zeros((n_heads, len(dist)), F32)
        for b in range(REL_BUCKETS):
            out = jnp.where(bkt == b, table[:, b:b + 1], out)
        return jnp.where(jnp.asarray(dist)[None, :] >= 0, out, NEG_INF)

    def shifted_rows(w, n_rows, step):
        p = w.shape[1]
        return jnp.tile(w, (1, n_rows))[:, :n_rows * (p - step)].reshape(w.shape[0], n_rows, p - step)

    w = bias_of(np.arange(-(TK - 1), NE_BIAS * TQ + TQ))
    toep = shifted_rows(w, TK, 1)[:, :, TK - 1:TK - 1 + NE_BIAS * TQ]
    vals = toep.reshape(n_heads, TK, NE_BIAS, TQ).transpose(0, 2, 1, 3)
    d = (np.arange(NE_BIAS)[:, None, None] * TQ + np.arange(TQ)[None, None, :] - np.arange(TK)[None, :, None])
    far = table[:, REL_BUCKETS - 1][:, None, None, None]
    tile = jnp.where(d >= 0, vals - far, NEG_INF)
    tile = jnp.concatenate([tile, jnp.zeros_like(tile[:, :1])], axis=1)
    dw = d[:NE_WIN]
    twin = jnp.where((dw >= 0) & (dw < NSA_WINDOW), vals[MOBA_HEADS:, :NE_WIN], NEG_INF)
    n_c = S // NSA_CMP_STRIDE
    u = np.arange(2 * n_c)[:, None]
    dc = np.arange(TQ)[None, :] + NSA_CMP_STRIDE * (u - (n_c - 1)) - (NSA_CMP_BLOCK - 1)
    fcmp = bias_of(dc.reshape(-1))[MOBA_HEADS:].reshape(-1, 2 * n_c, TQ)
    return tile, twin, fcmp


def _selection_constants(S):
    n_c = S // NSA_CMP_STRIDE
    n_cmp = (S - NSA_CMP_BLOCK) // NSA_CMP_STRIDE + 1
    n_sb = S // NSA_SLC_BLOCK
    ci = np.arange(n_c)[None, :] * NSA_CMP_STRIDE
    sj = np.arange(n_sb)[:, None] * NSA_SLC_BLOCK
    ovl = (ci < sj + NSA_SLC_BLOCK) & (ci + NSA_CMP_BLOCK > sj) & (np.arange(n_c)[None, :] < n_cmp)
    ovl = ovl[:, ::-1]
    return jnp.asarray(ovl, BF16)


def _causal_tiles():
    e = np.arange(QPK)[:, None, None]
    d = e * TQ + np.arange(TQ)[None, None, :] - np.arange(TK)[None, :, None]
    return jnp.asarray(np.where(d >= 0, 0.0, NEG_INF), F32)


def kernel(x, rel_bias, mix_norm, mlp_norm, even_w_in, even_w_out, cmp_pos_k, cmp_pos_v, cmp_k_w1, cmp_k_w2,
           cmp_v_w1, cmp_v_w2, odd_w_in, odd_b_forget, odd_w_out, mlp_w1, mlp_w2, final_norm):
    B, S, D = x.shape
    assert D == D_MODEL and S % TM == 0
    G, J = NSA_KV_GROUPS, NSA_HPG
    h = x.reshape(B * S, D)

    offs = np.cumsum((MOBA_W, MOBA_W, MOBA_W, NSA_W) + (NSA_KV_W,) * 6)
    mq_w, mk_w, mv_w, nq_w, kc_w, vc_w, ksl_w, vsl_w, kwn_w, vwn_w, gz_w = jnp.split(even_w_in[0], offs, axis=1)
    wrm = jnp.concatenate([mk_w, kc_w, vc_w, ksl_w, kwn_w], axis=1).astype(BF16)
    col_kcvc, col_ksl, col_kwn = MOBA_W, MOBA_W + 2 * NSA_KV_W, MOBA_W + 3 * NSA_KV_W
    wfm = jnp.concatenate([mq_w, mv_w, nq_w, vsl_w, vwn_w], axis=1).T.astype(BF16)
    row_nq, row_vsl, row_vwn = 2 * MOBA_W, 2 * MOBA_W + NSA_W, 2 * MOBA_W + NSA_W + NSA_KV_W
    gzw = gz_w.T.reshape(G, J, 3, D).transpose(0, 2, 1, 3).reshape(G, 3 * J, D)
    gzw = jnp.pad(gzw, ((0, 0), (0, GZ_ROWS - 3 * J), (0, 0))).reshape(G * GZ_ROWS, D).astype(BF16)

    rm, fm, gz = _inproj0(h, mix_norm[0][None, :], wrm, wfm, gzw, B, S)
    rm = rm.reshape(B, S, -1)

    tile, twin, tcmp = _bias_tables(rel_bias, S)
    ovl = _selection_constants(S)

    o_moba = _moba(fm, rm, tile, B, S)

    n_c = S // NSA_CMP_STRIDE
    r = rm[:, :, col_kcvc:col_kcvc + 2 * NSA_KV_W].reshape(B, n_c, NSA_CMP_STRIDE, 2 * G, HEAD_DIM)
    r = r.transpose(0, 3, 1, 2, 4).reshape(B, 2 * G, n_c, NSA_CMP_STRIDE * HEAD_DIM)
    pos = jnp.stack([cmp_pos_k[0].reshape(1, -1), cmp_pos_v[0].reshape(1, -1)])
    pos = jnp.pad(pos, ((0, 0), (0, 7), (0, 0))).astype(BF16)
    w1c = jnp.stack([cmp_k_w1[0], cmp_v_w1[0]]).astype(BF16)
    kc, vct = _compress(r, pos, w1c, cmp_k_w2[0].astype(BF16), cmp_v_w2[0].T.astype(BF16), B, n_c)

    o_nsa = _nsa(fm, rm, gz, kc, vct, tcmp, ovl, tile, twin, B, S,
                 col_ksl, col_kwn, row_nq, row_vsl, row_vwn)

    h = _post([o_moba, o_nsa], h, even_w_out[0].astype(BF16), mlp_norm[0][None, :],
              mlp_w1[0].astype(BF16), mlp_w2[0].astype(BF16), None, B, S)

    q_w, k_w, v_w, f_w = jnp.split(odd_w_in[0], np.cumsum((FOX_W, FOX_W, FOX_W)), axis=1)
    wfm1 = jnp.concatenate([q_w, v_w], axis=1).T.astype(BF16)
    wk = jnp.pad(k_w.reshape(D, FOX_HEADS, HEAD_DIM), ((0, 0), (0, 0), (0, LANES - HEAD_DIM)))
    wk = wk.reshape(D, FOX_HEADS * LANES).astype(BF16)
    f_w = jnp.pad(f_w, ((0, 0), (0, LANES - FOX_HEADS)))
    f_hi = f_w.astype(BF16)
    wf = jnp.stack([f_hi, (f_w - f_hi.astype(F32)).astype(BF16)])
    bf = jnp.pad(odd_b_forget[0], (0, LANES - FOX_HEADS))[None, :]
    tri = jnp.asarray(np.tril(np.ones((TM, TM))), BF16)
    place = np.zeros((3, LANES, FOX_HEADS * LANES), np.float32)
    for term in range(3):
        place[term, np.arange(FOX_HEADS), np.arange(FOX_HEADS) * LANES + HEAD_DIM + term] = -1.0
    fm1, ka = _inproj1(h, mix_norm[1][None, :], wfm1, wk, wf, bf, tri, jnp.asarray(place, BF16), B, S)
    o_fox = _fox(fm1, ka.reshape(B, S, -1), _causal_tiles(), B, S)

    h = _post([o_fox], h, odd_w_out[0].astype(BF16), mlp_norm[1][None, :],
              mlp_w1[1].astype(BF16), mlp_w2[1].astype(BF16), final_norm[None, :], B, S)
    return h.reshape(B, S, D)
```

```python
import functools
import math

import numpy as np
import jax
import jax.numpy as jnp
from jax import lax
from jax.experimental import pallas as pl
from jax.experimental.pallas import tpu as pltpu

D_MODEL = 1024
HEAD_DIM = 64
MOBA_HEADS = 8
MOBA_BLOCK = 256
MOBA_TOPK = 3
NSA_HEADS = 8
NSA_KV_GROUPS = 2
NSA_HPG = NSA_HEADS // NSA_KV_GROUPS
NSA_CMP_BLOCK = 32
NSA_CMP_STRIDE = 16
NSA_CMP_HIDDEN = 256
NSA_SLC_BLOCK = 64
NSA_TOPN = 16
NSA_WINDOW = 512
NSA_FORCE_SCORE = 1e6
FOX_HEADS = 16
D_FF = 4 * D_MODEL
REL_BUCKETS = 32
REL_MAX_DISTANCE = 1024
RMS_EPS = 1e-5
NEG_INF = -1e30
SCALE = HEAD_DIM ** -0.5

MOBA_W = MOBA_HEADS * HEAD_DIM
NSA_W = NSA_HEADS * HEAD_DIM
NSA_KV_W = NSA_KV_GROUPS * HEAD_DIM
FOX_W = FOX_HEADS * HEAD_DIM

LANES = 128
TQ = 256
TK = 256
QPK = TK // TQ
TM = 512
CH = 256
FF_CH = 512
VMEM_LIMIT = 56 * 1024 * 1024
NE_BIAS = -(-(REL_MAX_DISTANCE + TK - 1) // TQ)
NE_WIN = -(-(NSA_WINDOW + TK - 1) // TQ)
GZ_ROWS = 16
HPS = 8
SPT = TK // NSA_SLC_BLOCK

assert TK % TQ == 0 and MOBA_BLOCK == TK and TK % NSA_SLC_BLOCK == 0

F32 = jnp.float32
BF16 = jnp.bfloat16


def _dot(a, b):
    return jnp.dot(a, b, preferred_element_type=F32)


def _dot_nt(a, b):
    return lax.dot_general(a, b, (((1,), (1,)), ((), ())), preferred_element_type=F32)


def _dot_tn(a, b):
    return lax.dot_general(a, b, (((0,), (0,)), ((), ())), preferred_element_type=F32)


def _rmsnorm(x, g):
    ms = jnp.mean(x * x, axis=-1, keepdims=True)
    return x * lax.rsqrt(ms + RMS_EPS) * g


def _split3(x):
    a = x.astype(BF16)
    r = x - a.astype(F32)
    b = r.astype(BF16)
    c = (r - b.astype(F32)).astype(BF16)
    return a, b, c


def _const_spec(shape):
    nd = len(shape)
    return pl.BlockSpec(shape, lambda *_: (0,) * nd, pipeline_mode=pl.Buffered(1))


def _params(sem):
    return pltpu.CompilerParams(dimension_semantics=sem, vmem_limit_bytes=VMEM_LIMIT)


def _attend(n_tiles, tile_of, qk_fn, fix_fn, v_fn, s_scr, acc_scr, n_heads, first_fix=None):
    for h, s in enumerate(qk_fn(tile_of(0))):
        s_scr[h] = s if first_fix is None else first_fix(h, s)
    acc_scr[...] = jnp.zeros_like(acc_scr)
    last = n_tiles - 1
    init = tuple((jnp.full((1, TQ), NEG_INF, F32), jnp.zeros((1, TQ), F32)) for _ in range(n_heads))

    def body(i, state):
        n = tile_of(i)
        s_next = qk_fn(tile_of(jnp.minimum(i + 1, last)))
        out = []
        for h in range(n_heads):
            m, l = state[h]
            s = fix_fn(h, n, s_scr[h])
            m_new = jnp.maximum(m, jnp.max(s, axis=0, keepdims=True))
            alpha = jnp.exp(m - m_new)
            p = jnp.exp(s - m_new)
            l = alpha * l + jnp.sum(p, axis=0, keepdims=True)
            acc_scr[h] = alpha * acc_scr[h] + _dot(v_fn(h, n), p.astype(BF16))
            out.append((m_new, l))
        for h, s in enumerate(s_next):
            s_scr[h] = s
        return tuple(out)

    state = lax.fori_loop(0, n_tiles, body, init)
    return [acc_scr[h] / l for h, (_, l) in enumerate(state)]


def _rank_select(val, idx, n_rows, k):
    cnt = jnp.zeros(val.shape, F32)
    for m in range(n_rows):
        vm = val[m:m + 1, :]
        beats = (vm > val) | ((vm == val) & (idx > m))
        cnt = cnt + jnp.where(beats, 1.0, 0.0)
    return cnt < k


def _inproj0_body(x_ref, g_ref, wrm_ref, wfm_ref, wgz_ref, rm_ref, fm_ref, gz_ref):
    xn = _rmsnorm(x_ref[...], g_ref[...]).astype(BF16)
    for c0 in range(0, rm_ref.shape[-1], CH):
        rm_ref[:, c0:c0 + CH] = _dot(xn, wrm_ref[:, c0:c0 + CH]).astype(BF16)
    for r0 in range(0, fm_ref.shape[2], CH):
        res = _dot_nt(wfm_ref[r0:r0 + CH, :], xn).astype(BF16)
        for t in range(TM // TK):
            fm_ref[0, t, r0:r0 + CH, :] = res[:, t * TK:(t + 1) * TK]
    gz_ref[0] = _dot_nt(wgz_ref[...], xn)


def _inproj0(x2, g, wrm, wfm, wgz, B, S):
    M = B * S
    nst = S // TM
    n_rm, n_fm, n_gz = wrm.shape[1], wfm.shape[0], wgz.shape[0]
    return pl.pallas_call(
        _inproj0_body,
        grid=(M // TM,),
        in_specs=[
            pl.BlockSpec((TM, D_MODEL), lambda i: (i, 0)),
            _const_spec((1, D_MODEL)),
            _const_spec((D_MODEL, n_rm)),
            _const_spec((n_fm, D_MODEL)),
            _const_spec((n_gz, D_MODEL)),
        ],
        out_specs=[
            pl.BlockSpec((TM, n_rm), lambda i: (i, 0)),
            pl.BlockSpec((1, TM // TK, n_fm, TK), lambda i: (i // nst, i % nst, 0, 0)),
            pl.BlockSpec((1, n_gz, TM), lambda i: (i // nst, 0, i % nst)),
        ],
        out_shape=[
            jax.ShapeDtypeStruct((M, n_rm), BF16),
            jax.ShapeDtypeStruct((B, S // TK, n_fm, TK), BF16),
            jax.ShapeDtypeStruct((B, n_gz, S), F32),
        ],
        compiler_params=_params(("parallel",)),
        name="inproj0",
    )(x2, g, wrm, wfm, wgz)


def _compress_body(rk_ref, rv_ref, pos_ref, w1_ref, w2k_ref, w2vt_ref, flip_ref, kc_ref, vct_ref):
    half = NSA_CMP_STRIDE * HEAD_DIM

    def hidden(r_ref, s):
        r = r_ref[0, 0]
        a = _dot(r, w1_ref[s, :half, :])
        b = _dot(r, w1_ref[s, half:, :])
        nxt = pltpu.roll(b, b.shape[0] - 1, axis=0)
        posb = _dot(pos_ref[s], w1_ref[s])[0:1]
        pre = a + nxt + posb
        act = (pre * jax.nn.sigmoid(pre)).astype(BF16)
        return _dot(flip_ref[...], act).astype(BF16)

    kc_ref[0, 0] = _dot(hidden(rk_ref, 0), w2k_ref[...]).astype(BF16)
    vct_ref[0, 0] = _dot_nt(w2vt_ref[...], hidden(rv_ref, 1)).astype(BF16)


def _compress(r, pos, w1, w2k, w2vt, B, NC):
    G = NSA_KV_GROUPS
    half = NSA_CMP_STRIDE * HEAD_DIM
    return pl.pallas_call(
        _compress_body,
        grid=(B, G),
        in_specs=[
            pl.BlockSpec((1, 1, NC, half), lambda b, g: (b, g, 0, 0)),
            pl.BlockSpec((1, 1, NC, half), lambda b, g: (b, G + g, 0, 0)),
            _const_spec(pos.shape),
            _const_spec(w1.shape),
            _const_spec(w2k.shape),
            _const_spec(w2vt.shape),
            _const_spec((NC, NC)),
        ],
        out_specs=[
            pl.BlockSpec((1, 1, NC, HEAD_DIM), lambda b, g: (b, g, 0, 0)),
            pl.BlockSpec((1, 1, HEAD_DIM, NC), lambda b, g: (b, g, 0, 0)),
        ],
        out_shape=[
            jax.ShapeDtypeStruct((B, G, NC, HEAD_DIM), BF16),
            jax.ShapeDtypeStruct((B, G, HEAD_DIM, NC), BF16),
        ],
        compiler_params=_params(("parallel", "parallel")),
        name="nsa_compress",
    )(r, r, pos, w1, w2k, w2vt, jnp.asarray(np.eye(NC)[::-1], BF16))


def _moba_body(q_ref, k_ref, v_ref, t_ref, o_ref, kmean_ref, mask_ref, s_scr, acc_scr, *, n_mb, topk):
    c = pl.program_id(2)
    blk = c // QPK

    @pl.when(c == 0)
    def _():
        kmean_ref[...] = jnp.zeros_like(kmean_ref)
        for n in range(n_mb):
            kblk = k_ref[0, n * TK:(n + 1) * TK, :].astype(F32)
            kmean_ref[n:n + 1, :] = jnp.mean(kblk, axis=0, keepdims=True)

    q = q_ref[0, 0]
    rowi = lax.broadcasted_iota(jnp.int32, (LANES, TQ), 0)
    nidx = lax.broadcasted_iota(jnp.int32, (kmean_ref.shape[0], TQ), 0)
    km = _split3(kmean_ref[...])
    qpads = []
    for h in range(HPS):
        lo = (h // 2) * LANES
        qpair = q[lo:lo + LANES, :]
        qh = jnp.where(rowi // HEAD_DIM == h % 2, qpair, jnp.zeros_like(qpair))
        route = sum(_dot(part[:, lo:lo + LANES], qh) for part in km)
        route = jnp.where(nidx < blk, route, NEG_INF)
        sel = _rank_select(route, nidx, n_mb, topk) & (nidx < blk)
        mask_ref[h] = jnp.where(sel | (nidx == blk), 0.0, NEG_INF)
        qpads.append(qh * SCALE)

    def qk_fn(n):
        rows = pl.ds(pl.multiple_of(n * TK, TK), TK)
        return [_dot(k_ref[0, rows, (h // 2) * LANES:(h // 2 + 1) * LANES], qpads[h]) for h in range(HPS)]

    def fix_fn(h, n, s):
        return s + mask_ref[h, pl.ds(n, 1), :] + t_ref[h, jnp.minimum(c - QPK * n, NE_BIAS)]

    def v_fn(h, n):
        return v_ref[0, n, h * HEAD_DIM:(h + 1) * HEAD_DIM, :]

    outs = _attend(blk + 1, lambda i: blk - i, qk_fn, fix_fn, v_fn, s_scr, acc_scr, HPS)
    o_ref[0] = jnp.concatenate(outs, axis=0).astype(BF16)


def _moba(fm, rm, tab, B, S):
    n_mb = S // MOBA_BLOCK
    n_pad = -(-n_mb // 16) * 16
    ne = tab.shape[1]
    rows = HPS * HEAD_DIM
    body = functools.partial(_moba_body, n_mb=n_mb, topk=min(MOBA_TOPK, n_mb))
    return pl.pallas_call(
        body,
        grid=(B, MOBA_HEADS // HPS, S // TQ),
        in_specs=[
            pl.BlockSpec((1, 1, rows, TQ), lambda b, p, c: (b, c // QPK, p, c % QPK)),
            pl.BlockSpec((1, S, rows), lambda b, p, c: (b, 0, p)),
            pl.BlockSpec((1, S // TK, rows, TK), lambda b, p, c: (b, 0, MOBA_HEADS // HPS + p, 0)),
            pl.BlockSpec((HPS, ne, TK, TQ), lambda b, p, c: (p, 0, 0, 0), pipeline_mode=pl.Buffered(1)),
        ],
        out_specs=pl.BlockSpec((1, rows, TQ), lambda b, p, c: (b, p, c)),
        out_shape=jax.ShapeDtypeStruct((B, MOBA_W, S), BF16),
        scratch_shapes=[pltpu.VMEM((n_pad, rows), F32), pltpu.VMEM((HPS, n_pad, TQ), F32),
                        pltpu.VMEM((HPS, TK, TQ), F32), pltpu.VMEM((HPS, HEAD_DIM, TQ), F32)],
        compiler_params=_params(("parallel", "parallel", "arbitrary")),
        name="moba_attn",
    )(fm, rm, fm, tab)


def _nsa_body(q_ref, kc_ref, vct_ref, fc_ref, ovl_ref, ksl_ref, vsl_ref, kwn_ref, vwn_ref,
              tslc_ref, twin_ref, gz_ref, o_ref, s_scr, acc_scr, s_win, acc_win, sel_ref, *, n_sb, n_sel):
    c = pl.program_id(1)
    blk = c // QPK
    G, J = NSA_KV_GROUPS, NSA_HPG
    H = G * J

    q = q_ref[0, 0] * SCALE
    qs = [q[h * HEAD_DIM:(h + 1) * HEAD_DIM, :] for h in range(H)]
    zero = jnp.zeros((HEAD_DIM, TQ), BF16)
    qpads = [jnp.concatenate([zero] * (h // J) + [qs[h]] + [zero] * (G - 1 - h // J), axis=0) for h in range(H)]

    c0 = pl.multiple_of(c * (TQ // NSA_CMP_STRIDE), TQ // NSA_CMP_STRIDE)
    o_cmp = []
    for g in range(G):
        kc = kc_ref[0, g]
        vct = vct_ref[0, g]
        psum = jnp.zeros((kc.shape[0], TQ), F32)
        for h in range(g * J, (g + 1) * J):
            s = _dot(kc, qs[h]) + fc_ref[h, pl.ds(c0, kc.shape[0]), :]
            m = jnp.max(s, axis=0, keepdims=True)
            p = jnp.exp(s - m)
            l = jnp.sum(p, axis=0, keepdims=True)
            pn = p * jnp.where(m > 0.5 * NEG_INF, 1.0 / l, 0.0)
            o_cmp.append(_dot(vct, pn.astype(BF16)))
            psum = psum + pn

        ph = psum.astype(BF16)
        plo = (psum - ph.astype(F32)).astype(BF16)
        imp = _dot(ovl_ref[...], ph) + _dot(ovl_ref[...], plo)
        jb = lax.broadcasted_iota(jnp.int32, imp.shape, 0)
        t = c * TQ + lax.broadcasted_iota(jnp.int32, imp.shape, 1)
        sb = t // NSA_SLC_BLOCK
        forced = (jb == 0) | (jb == sb) | (jb == sb - 1)
        allowed = jb <= sb
        val = jnp.where(forced, imp + NSA_FORCE_SCORE, jnp.where(allowed, imp, NEG_INF))
        sel = _rank_select(val, jb, n_sb, n_sel) & allowed
        selb = jnp.where(sel, 0.0, NEG_INF)
        pad = jnp.zeros((sel_ref.shape[2] - SPT, TQ), F32)
        for n in range(n_sb // SPT):
            sel_ref[g, n] = jnp.concatenate([selb[n * SPT:(n + 1) * SPT, :], pad], axis=0)

    def slc_qk(n):
        kt = ksl_ref[0, pl.ds(pl.multiple_of(n * TK, TK), TK), :]
        return [_dot(kt, qpads[h]) for h in range(H)]

    def slc_fix(h, n, s):
        rows = sel_ref[h // J, n]
        mask = jnp.concatenate([jnp.broadcast_to(rows[b:b + 1, :], (NSA_SLC_BLOCK, TQ)) for b in range(SPT)], axis=0)
        return s + mask + tslc_ref[h, jnp.minimum(c - QPK * n, NE_BIAS)]

    def slc_v(h, n):
        return vsl_ref[0, n, (h // J) * HEAD_DIM:(h // J + 1) * HEAD_DIM, :]

    o_slc = _attend(blk + 1, lambda i: blk - i, slc_qk, slc_fix, slc_v, s_scr, acc_scr, H)

    def win_qk(n):
        kt = kwn_ref[0, pl.ds(pl.multiple_of(n * TK, TK), TK), :]
        return [_dot(kt, qpads[h]) for h in range(H)]

    def win_fix(h, n, s):
        return s + twin_ref[h, c - QPK * n]

    def win_v(h, n):
        return vwn_ref[0, n, (h // J) * HEAD_DIM:(h // J + 1) * HEAD_DIM, :]

    w_lo = jnp.maximum(c - NE_WIN + QPK, 0) // QPK
    o_win = _attend(blk - w_lo + 1, lambda i: blk - i, win_qk, win_fix, win_v, s_win, acc_win, H)

    gate = jax.nn.sigmoid(gz_ref[0])
    outs = []
    for h in range(H):
        r = (h // J) * GZ_ROWS + h % J
        outs.append(gate[r:r + 1, :] * o_cmp[h] + gate[r + J:r + J + 1, :] * o_slc[h]
                    + gate[r + 2 * J:r + 2 * J + 1, :] * o_win[h])
    o_ref[0] = jnp.concatenate(outs, axis=0).astype(BF16)


def _nsa(fm, rm, gz, kc, vct, tcmp, ovl, tslc, twin, B, S, col_ksl, col_kwn, row_q, row_vsl, row_vwn):
    G, J = NSA_KV_GROUPS, NSA_HPG
    NC = kc.shape[2]
    n_sb = S // NSA_SLC_BLOCK
    body = functools.partial(_nsa_body, n_sb=n_sb, n_sel=min(NSA_TOPN, n_sb))
    H = G * J
    kvrows = G * HEAD_DIM
    one = pl.Buffered(1)
    return pl.pallas_call(
        body,
        grid=(B, S // TQ),
        in_specs=[
            pl.BlockSpec((1, 1, NSA_W, TQ), lambda b, c: (b, c // QPK, row_q // NSA_W, c % QPK)),
            pl.BlockSpec((1, G, NC, HEAD_DIM), lambda b, c: (b, 0, 0, 0)),
            pl.BlockSpec((1, G, HEAD_DIM, NC), lambda b, c: (b, 0, 0, 0)),
            _const_spec(tcmp.shape),
            _const_spec(ovl.shape),
            pl.BlockSpec((1, S, LANES), lambda b, c: (b, 0, col_ksl // LANES)),
            pl.BlockSpec((1, S // TK, kvrows, TK), lambda b, c: (b, 0, row_vsl // kvrows, 0)),
            pl.BlockSpec((1, S, LANES), lambda b, c: (b, 0, col_kwn // LANES)),
            pl.BlockSpec((1, S // TK, kvrows, TK), lambda b, c: (b, 0, row_vwn // kvrows, 0)),
            pl.BlockSpec((H, tslc.shape[1], TK, TQ), lambda b, c: (MOBA_HEADS // H, 0, 0, 0), pipeline_mode=one),
            _const_spec(twin.shape),
            pl.BlockSpec((1, G * GZ_ROWS, TQ), lambda b, c: (b, 0, c)),
        ],
        out_specs=pl.BlockSpec((1, NSA_W, TQ), lambda b, c: (b, 0, c)),
        out_shape=jax.ShapeDtypeStruct((B, NSA_W, S), BF16),
        scratch_shapes=[pltpu.VMEM((H, TK, TQ), F32), pltpu.VMEM((H, HEAD_DIM, TQ), F32),
                        pltpu.VMEM((H, TK, TQ), F32), pltpu.VMEM((H, HEAD_DIM, TQ), F32),
                        pltpu.VMEM((G, S // TK, 8, TQ), F32)],
        compiler_params=_params(("parallel", "arbitrary")),
        name="nsa_attn",
    )(fm, kc, vct, tcmp, ovl, rm, fm, rm, fm, tslc, twin, gz)


def _inproj1_body(x_ref, g_ref, wfm_ref, wk_ref, wf_ref, bf_ref, tri_ref, place_ref, fm_ref, ka_ref, carry_ref,
                  *, nst):
    i = pl.program_id(0)

    @pl.when(i % nst == 0)
    def _():
        carry_ref[...] = jnp.zeros_like(carry_ref)

    xf = _rmsnorm(x_ref[...], g_ref[...])
    xn = xf.astype(BF16)
    xlo = (xf - xn.astype(F32)).astype(BF16)
    for r0 in range(0, fm_ref.shape[2], CH):
        res = _dot_nt(wfm_ref[r0:r0 + CH, :], xn).astype(BF16)
        for t in range(TM // TK):
            fm_ref[0, t, r0:r0 + CH, :] = res[:, t * TK:(t + 1) * TK]

    fz = _dot(xn, wf_ref[0]) + _dot(xlo, wf_ref[0]) + _dot(xn, wf_ref[1]) + bf_ref[...]
    logf = jnp.minimum(fz, 0.0) - jnp.log(1.0 + jnp.exp(-jnp.abs(fz)))
    tri = tri_ref[...]
    h1, h2, h3 = _split3(logf)
    cum = _dot(tri, h1) + _dot(tri, h2) + _dot(tri, h3) + carry_ref[0:1, :]
    carry_ref[...] = jnp.broadcast_to(cum[TM - 1:TM, :], carry_ref.shape)
    c1, c2, c3 = _split3(cum)
    for c0 in range(0, ka_ref.shape[-1], FF_CH):
        ka = (_dot(xn, wk_ref[:, c0:c0 + FF_CH]) + _dot(c1, place_ref[0, :, c0:c0 + FF_CH])
              + _dot(c2, place_ref[1, :, c0:c0 + FF_CH]) + _dot(c3, place_ref[2, :, c0:c0 + FF_CH]))
        ka_ref[:, c0:c0 + FF_CH] = ka.astype(BF16)


def _inproj1(x2, g, wfm, wk, wf, bf, tri, place, B, S):
    M = B * S
    nst = S // TM
    n_fm, n_ka = wfm.shape[0], wk.shape[1]
    return pl.pallas_call(
        functools.partial(_inproj1_body, nst=nst),
        grid=(M // TM,),
        in_specs=[
            pl.BlockSpec((TM, D_MODEL), lambda i: (i, 0)),
            _const_spec((1, D_MODEL)),
            _const_spec(wfm.shape),
            _const_spec(wk.shape),
            _const_spec(wf.shape),
            _const_spec(bf.shape),
            _const_spec(tri.shape),
            _const_spec(place.shape),
        ],
        out_specs=[
            pl.BlockSpec((1, TM // TK, n_fm, TK), lambda i: (i // nst, i % nst, 0, 0)),
            pl.BlockSpec((TM, n_ka), lambda i: (i, 0)),
        ],
        out_shape=[
            jax.ShapeDtypeStruct((B, S // TK, n_fm, TK), BF16),
            jax.ShapeDtypeStruct((M, n_ka), BF16),
        ],
        scratch_shapes=[pltpu.VMEM((8, LANES), F32)],
        compiler_params=_params(("arbitrary",)),
        name="inproj1",
    )(x2, g, wfm, wk, wf, bf, tri, place)


def _fox_body(q_ref, k_ref, v_ref, cm_ref, o_ref, s_scr, acc_scr):
    c = pl.program_id(2)
    blk = c // QPK
    q = q_ref[0, 0] * SCALE
    ones = jnp.ones((LANES - HEAD_DIM, TQ), BF16)
    qas = [jnp.concatenate([q[h * HEAD_DIM:(h + 1) * HEAD_DIM, :], ones], axis=0) for h in range(HPS)]

    def qk_fn(n):
        rows = pl.ds(pl.multiple_of(n * TK, TK), TK)
        return [_dot(k_ref[0, rows, h * LANES:(h + 1) * LANES], qas[h]) for h in range(HPS)]

    causal = cm_ref[c % QPK]
    outs = _attend(blk + 1, lambda i: blk - i, qk_fn, lambda h, n, s: s,
                   lambda h, n: v_ref[0, n, h * HEAD_DIM:(h + 1) * HEAD_DIM, :], s_scr, acc_scr, HPS,
                   first_fix=lambda idx, s: s + causal)
    o_ref[0] = jnp.concatenate(outs, axis=0).astype(BF16)


def _fox(fm, ka, cmask, B, S):
    rows = HPS * HEAD_DIM
    return pl.pallas_call(
        _fox_body,
        grid=(B, FOX_HEADS // HPS, S // TQ),
        in_specs=[
            pl.BlockSpec((1, 1, rows, TQ), lambda b, h, c: (b, c // QPK, h, c % QPK)),
            pl.BlockSpec((1, S, HPS * LANES), lambda b, h, c: (b, 0, h)),
            pl.BlockSpec((1, S // TK, rows, TK), lambda b, h, c: (b, 0, FOX_HEADS // HPS + h, 0)),
            _const_spec(cmask.shape),
        ],
        out_specs=pl.BlockSpec((1, rows, TQ), lambda b, h, c: (b, h, c)),
        out_shape=jax.ShapeDtypeStruct((B, FOX_W, S), BF16),
        scratch_shapes=[pltpu.VMEM((HPS, TK, TQ), F32), pltpu.VMEM((HPS, HEAD_DIM, TQ), F32)],
        compiler_params=_params(("parallel", "parallel", "arbitrary")),
        name="fox_attn",
    )(fm, ka, fm, cmask)


def _post_body(*refs, n_parts, final):
    o_refs = refs[:n_parts]
    h_ref, wo_ref, g_ref, w1_ref, w2_ref = refs[n_parts:n_parts + 5]
    gf_ref = refs[n_parts + 5] if final else None
    out_ref, hn_ref = refs[-2:]
    h1 = h_ref[...]
    r0 = 0
    for o_ref in o_refs:
        nf = o_ref.shape[1]
        h1 = h1 + _dot_tn(o_ref[0], wo_ref[r0:r0 + nf, :])
        r0 += nf
    out_ref[...] = h1
    hn_ref[...] = _rmsnorm(out_ref[...], g_ref[...]).astype(BF16)
    for c0 in range(0, D_FF, FF_CH):
        a = jnp.maximum(_dot(hn_ref[...], w1_ref[:, c0:c0 + FF_CH]), 0.0)
        out_ref[...] += _dot((a * a).astype(BF16), w2_ref[c0:c0 + FF_CH, :])
    if final:
        out_ref[...] = _rmsnorm(out_ref[...], gf_ref[...])


def _post(o_parts, h2, wo, g, w1, w2, gf, B, S):
    M = B * S
    nst = S // TM
    final = gf is not None
    in_specs = [pl.BlockSpec((1, o.shape[1], TM), lambda i: (i // nst, 0, i % nst)) for o in o_parts]
    in_specs += [
        pl.BlockSpec((TM, D_MODEL), lambda i: (i, 0)),
        _const_spec(wo.shape),
        _const_spec((1, D_MODEL)),
        _const_spec(w1.shape),
        _const_spec(w2.shape),
    ]
    args = list(o_parts) + [h2, wo, g, w1, w2]
    if final:
        in_specs.append(_const_spec((1, D_MODEL)))
        args.append(gf)
    return pl.pallas_call(
        functools.partial(_post_body, n_parts=len(o_parts), final=final),
        grid=(M // TM,),
        in_specs=in_specs,
        out_specs=pl.BlockSpec((TM, D_MODEL), lambda i: (i, 0)),
        out_shape=jax.ShapeDtypeStruct((M, D_MODEL), F32),
        scratch_shapes=[pltpu.VMEM((TM, D_MODEL), BF16)],
        compiler_params=_params(("parallel",)),
        name="post_final" if final else "post",
    )(*args)


def _rel_bucket(dist):
    n = jnp.maximum(dist, 0)
    max_exact = REL_BUCKETS // 2
    nf = jnp.maximum(n, 1).astype(jnp.float32)
    large = max_exact + (jnp.log(nf / max_exact) / math.log(REL_MAX_DISTANCE / max_exact)
                         * (REL_BUCKETS - max_exact)).astype(jnp.int32)
    large = jnp.minimum(large, REL_BUCKETS - 1)
    return jnp.where(n < max_exact, n, large)


def _bias_tables(rel_bias, S):
    n_heads = rel_bias.shape[1]
    table = rel_bias.T

    def bias_of(dist):
        bkt = _rel_bucket(jnp.asarray(dist))[None, :]
        out = jnp.zeros((n_heads, len(dist)), F32)
        for b in range(REL_BUCKETS):
            out = jnp.where(bkt == b, table[:, b:b + 1], out)
        return jnp.where(jnp.asarray(dist)[None, :] >= 0, out, NEG_INF)

    def shifted_rows(w, n_rows, step):
        p = w.shape[1]
        return jnp.tile(w, (1, n_rows))[:, :n_rows * (p - step)].reshape(w.shape[0], n_rows, p - step)

    w = bias_of(np.arange(-(TK - 1), NE_BIAS * TQ + TQ))
    toep = shifted_rows(w, TK, 1)[:, :, TK - 1:TK - 1 + NE_BIAS * TQ]
    vals = toep.reshape(n_heads, TK, NE_BIAS, TQ).transpose(0, 2, 1, 3)
    d = (np.arange(NE_BIAS)[:, None, None] * TQ + np.arange(TQ)[None, None, :] - np.arange(TK)[None, :, None])
    far = table[:, REL_BUCKETS - 1][:, None, None, None]
    tile = jnp.where(d >= 0, vals - far, NEG_INF)
    tile = jnp.concatenate([tile, jnp.zeros_like(tile[:, :1])], axis=1)
    dw = d[:NE_WIN]
    twin = jnp.where((dw >= 0) & (dw < NSA_WINDOW), vals[MOBA_HEADS:, :NE_WIN], NEG_INF)
    n_c = S // NSA_CMP_STRIDE
    u = np.arange(2 * n_c)[:, None]
    dc = np.arange(TQ)[None, :] + NSA_CMP_STRIDE * (u - (n_c - 1)) - (NSA_CMP_BLOCK - 1)
    fcmp = bias_of(dc.reshape(-1))[MOBA_HEADS:].reshape(-1, 2 * n_c, TQ)
    return tile, twin, fcmp


def _selection_constants(S):
    n_c = S // NSA_CMP_STRIDE
    n_cmp = (S - NSA_CMP_BLOCK) // NSA_CMP_STRIDE + 1
    n_sb = S // NSA_SLC_BLOCK
    ci = np.arange(n_c)[None, :] * NSA_CMP_STRIDE
    sj = np.arange(n_sb)[:, None] * NSA_SLC_BLOCK
    ovl = (ci < sj + NSA_SLC_BLOCK) & (ci + NSA_CMP_BLOCK > sj) & (np.arange(n_c)[None, :] < n_cmp)
    ovl = ovl[:, ::-1]
    return jnp.asarray(ovl, BF16)


def _causal_tiles():
    e = np.arange(QPK)[:, None, None]
    d = e * TQ + np.arange(TQ)[None, None, :] - np.arange(TK)[None, :, None]
    return jnp.asarray(np.where(d >= 0, 0.0, NEG_INF), F32)


def kernel(x, rel_bias, mix_norm, mlp_norm, even_w_in, even_w_out, cmp_pos_k, cmp_pos_v, cmp_k_w1, cmp_k_w2,
           cmp_v_w1, cmp_v_w2, odd_w_in, odd_b_forget, odd_w_out, mlp_w1, mlp_w2, final_norm):
    B, S, D = x.shape
    assert D == D_MODEL and S % TM == 0
    G, J = NSA_KV_GROUPS, NSA_HPG
    h = x.reshape(B * S, D)

    offs = np.cumsum((MOBA_W, MOBA_W, MOBA_W, NSA_W) + (NSA_KV_W,) * 6)
    mq_w, mk_w, mv_w, nq_w, kc_w, vc_w, ksl_w, vsl_w, kwn_w, vwn_w, gz_w = jnp.split(even_w_in[0], offs, axis=1)
    wrm = jnp.concatenate([mk_w, kc_w, vc_w, ksl_w, kwn_w], axis=1).astype(BF16)
    col_kcvc, col_ksl, col_kwn = MOBA_W, MOBA_W + 2 * NSA_KV_W, MOBA_W + 3 * NSA_KV_W
    wfm = jnp.concatenate([mq_w, mv_w, nq_w, vsl_w, vwn_w], axis=1).T.astype(BF16)
    row_nq, row_vsl, row_vwn = 2 * MOBA_W, 2 * MOBA_W + NSA_W, 2 * MOBA_W + NSA_W + NSA_KV_W
    gzw = gz_w.T.reshape(G, J, 3, D).transpose(0, 2, 1, 3).reshape(G, 3 * J, D)
    gzw = jnp.pad(gzw, ((0, 0), (0, GZ_ROWS - 3 * J), (0, 0))).reshape(G * GZ_ROWS, D).astype(BF16)

    rm, fm, gz = _inproj0(h, mix_norm[0][None, :], wrm, wfm, gzw, B, S)
    rm = rm.reshape(B, S, -1)

    tile, twin, tcmp = _bias_tables(rel_bias, S)
    ovl = _selection_constants(S)

    o_moba = _moba(fm, rm, tile, B, S)

    n_c = S // NSA_CMP_STRIDE
    r = rm[:, :, col_kcvc:col_kcvc + 2 * NSA_KV_W].reshape(B, n_c, NSA_CMP_STRIDE, 2 * G, HEAD_DIM)
    r = r.transpose(0, 3, 1, 2, 4).reshape(B, 2 * G, n_c, NSA_CMP_STRIDE * HEAD_DIM)
    pos = jnp.stack([cmp_pos_k[0].reshape(1, -1), cmp_pos_v[0].reshape(1, -1)])
    pos = jnp.pad(pos, ((0, 0), (0, 7), (0, 0))).astype(BF16)
    w1c = jnp.stack([cmp_k_w1[0], cmp_v_w1[0]]).astype(BF16)
    kc, vct = _compress(r, pos, w1c, cmp_k_w2[0].astype(BF16), cmp_v_w2[0].T.astype(BF16), B, n_c)

    o_nsa = _nsa(fm, rm, gz, kc, vct, tcmp, ovl, tile, twin, B, S,
                 col_ksl, col_kwn, row_nq, row_vsl, row_vwn)

    h = _post([o_moba, o_nsa], h, even_w_out[0].astype(BF16), mlp_norm[0][None, :],
              mlp_w1[0].astype(BF16), mlp_w2[0].astype(BF16), None, B, S)

    q_w, k_w, v_w, f_w = jnp.split(odd_w_in[0], np.cumsum((FOX_W, FOX_W, FOX_W)), axis=1)
    wfm1 = jnp.concatenate([q_w, v_w], axis=1).T.astype(BF16)
    wk = jnp.pad(k_w.reshape(D, FOX_HEADS, HEAD_DIM), ((0, 0), (0, 0), (0, LANES - HEAD_DIM)))
    wk = wk.reshape(D, FOX_HEADS * LANES).astype(BF16)
    f_w = jnp.pad(f_w, ((0, 0), (0, LANES - FOX_HEADS)))
    f_hi = f_w.astype(BF16)
    wf = jnp.stack([f_hi, (f_w - f_hi.astype(F32)).astype(BF16)])
    bf = jnp.pad(odd_b_forget[0], (0, LANES - FOX_HEADS))[None, :]
    tri = jnp.asarray(np.tril(np.ones((TM, TM))), BF16)
    place = np.zeros((3, LANES, FOX_HEADS * LANES), np.float32)
    for term in range(3):
        place[term, np.arange(FOX_HEADS), np.arange(FOX_HEADS) * LANES + HEAD_DIM + term] = -1.0
    fm1, ka = _inproj1(h, mix_norm[1][None, :], wfm1, wk, wf, bf, tri, jnp.asarray(place, BF16), B, S)
    o_fox = _fox(fm1, ka.reshape(B, S, -1), _causal_tiles(), B, S)

    h = _post([o_fox], h, odd_w_out[0].astype(BF16), mlp_norm[1][None, :],
              mlp_w1[1].astype(BF16), mlp_w2[1].astype(BF16), final_norm[None, :], B, S)
    return h.reshape(B, S, D)
```

```python
import functools
import math

import numpy as np
import jax
import jax.numpy as jnp
from jax import lax
from jax.experimental import pallas as pl
from jax.experimental.pallas import tpu as pltpu

D_MODEL = 1024
HEAD_DIM = 64
MOBA_HEADS = 8
MOBA_BLOCK = 256
MOBA_TOPK = 3
NSA_HEADS = 8
NSA_KV_GROUPS = 2
NSA_HPG = NSA_HEADS // NSA_KV_GROUPS
NSA_CMP_BLOCK = 32
NSA_CMP_STRIDE = 16
NSA_CMP_HIDDEN = 256
NSA_SLC_BLOCK = 64
NSA_TOPN = 16
NSA_WINDOW = 512
NSA_FORCE_SCORE = 1e6
FOX_HEADS = 16
D_FF = 4 * D_MODEL
REL_BUCKETS = 32
REL_MAX_DISTANCE = 1024
RMS_EPS = 1e-5
NEG_INF = -1e30
SCALE = HEAD_DIM ** -0.5

MOBA_W = MOBA_HEADS * HEAD_DIM
NSA_W = NSA_HEADS * HEAD_DIM
NSA_KV_W = NSA_KV_GROUPS * HEAD_DIM
FOX_W = FOX_HEADS * HEAD_DIM

LANES = 128
TQ = 256
TK = 256
QPK = TK // TQ
TM = 512
CH = 256
FF_CH = 512
VMEM_LIMIT = 56 * 1024 * 1024
NE_BIAS = -(-(REL_MAX_DISTANCE + TK - 1) // TQ)
NE_WIN = -(-(NSA_WINDOW + TK - 1) // TQ)
GZ_ROWS = 16
HPS = 8
SPT = TK // NSA_SLC_BLOCK
ACC_ROWS = HEAD_DIM + 16
ROW_CHUNK = 64
LOG2E = math.log2(math.e)

assert TK % TQ == 0 and MOBA_BLOCK == TK and TK % NSA_SLC_BLOCK == 0

F32 = jnp.float32
BF16 = jnp.bfloat16


def _dot(a, b):
    return jnp.dot(a, b, preferred_element_type=F32)


def _dot_nt(a, b):
    return lax.dot_general(a, b, (((1,), (1,)), ((), ())), preferred_element_type=F32)


def _dot_tn(a, b):
    return lax.dot_general(a, b, (((0,), (0,)), ((), ())), preferred_element_type=F32)


def _rmsnorm(x, g):
    ms = jnp.mean(x * x, axis=-1, keepdims=True)
    return x * lax.rsqrt(ms + RMS_EPS) * g


def _split3(x):
    a = x.astype(BF16)
    r = x - a.astype(F32)
    b = r.astype(BF16)
    c = (r - b.astype(F32)).astype(BF16)
    return a, b, c


def _const_spec(shape):
    nd = len(shape)
    return pl.BlockSpec(shape, lambda *_: (0,) * nd, pipeline_mode=pl.Buffered(1))


def _params(sem):
    return pltpu.CompilerParams(dimension_semantics=sem, vmem_limit_bytes=VMEM_LIMIT)


def _attend(n_tiles, tile_of, qk_fn, fix_fn, v_fn, s_scr, acc_scr, n_heads, first_fix=None):
    def put_scores(h, n, fix):
        s = qk_fn(h, n)
        s_scr[h] = s if fix is None else fix(h, n, s)

    for h in range(n_heads):
        put_scores(h, tile_of(0), fix_fn if first_fix is None else first_fix)
    acc_scr[...] = jnp.zeros_like(acc_scr)
    last = n_tiles - 1
    ones = jnp.ones((ACC_ROWS - HEAD_DIM, TK), BF16)
    chunks = range(0, TK, ROW_CHUNK)

    def body(i, ms):
        n = tile_of(i)
        n_next = tile_of(jnp.minimum(i + 1, last))
        out = []
        for h in range(n_heads):
            mx = s_scr[h, 0:ROW_CHUNK, :]
            for r0 in chunks[1:]:
                mx = jnp.maximum(mx, s_scr[h, r0:r0 + ROW_CHUNK, :])
            m_new = jnp.maximum(ms[h], jnp.max(mx, axis=0, keepdims=True))
            alpha = jnp.exp2(ms[h] - m_new)
            p = jnp.concatenate([jnp.exp2(s_scr[h, r0:r0 + ROW_CHUNK, :] - m_new).astype(BF16) for r0 in chunks],
                                axis=0)
            va = jnp.concatenate([v_fn(h, n), ones], axis=0)
            acc_scr[h] = alpha * acc_scr[h] + _dot(va, p)
            out.append(m_new)
            put_scores(h, n_next, fix_fn)
        return tuple(out)

    lax.fori_loop(0, n_tiles, body, tuple(jnp.full((1, TQ), NEG_INF, F32) for _ in range(n_heads)))
    return [acc_scr[h, :HEAD_DIM, :] / acc_scr[h, HEAD_DIM:HEAD_DIM + 1, :] for h in range(n_heads)]


def _rank_select(val, idx, n_rows, k):
    cnt = jnp.zeros(val.shape, F32)
    for m in range(n_rows):
        vm = val[m:m + 1, :]
        beats = (vm > val) | ((vm == val) & (idx > m))
        cnt = cnt + jnp.where(beats, 1.0, 0.0)
    return cnt < k


def _inproj0_body(x_ref, g_ref, wrm_ref, wfm_ref, wgz_ref, rm_ref, fm_ref, gz_ref):
    xn = _rmsnorm(x_ref[...], g_ref[...]).astype(BF16)
    for c0 in range(0, rm_ref.shape[-1], CH):
        rm_ref[:, c0:c0 + CH] = _dot(xn, wrm_ref[:, c0:c0 + CH]).astype(BF16)
    for r0 in range(0, fm_ref.shape[2], CH):
        res = _dot_nt(wfm_ref[r0:r0 + CH, :], xn).astype(BF16)
        for t in range(TM // TK):
            fm_ref[0, t, r0:r0 + CH, :] = res[:, t * TK:(t + 1) * TK]
    gz_ref[0] = _dot_nt(wgz_ref[...], xn)


def _inproj0(x2, g, wrm, wfm, wgz, B, S):
    M = B * S
    nst = S // TM
    n_rm, n_fm, n_gz = wrm.shape[1], wfm.shape[0], wgz.shape[0]
    return pl.pallas_call(
        _inproj0_body,
        grid=(M // TM,),
        in_specs=[
            pl.BlockSpec((TM, D_MODEL), lambda i: (i, 0)),
            _const_spec((1, D_MODEL)),
            _const_spec((D_MODEL, n_rm)),
            _const_spec((n_fm, D_MODEL)),
            _const_spec((n_gz, D_MODEL)),
        ],
        out_specs=[
            pl.BlockSpec((TM, n_rm), lambda i: (i, 0)),
            pl.BlockSpec((1, TM // TK, n_fm, TK), lambda i: (i // nst, i % nst, 0, 0)),
            pl.BlockSpec((1, n_gz, TM), lambda i: (i // nst, 0, i % nst)),
        ],
        out_shape=[
            jax.ShapeDtypeStruct((M, n_rm), BF16),
            jax.ShapeDtypeStruct((B, S // TK, n_fm, TK), BF16),
            jax.ShapeDtypeStruct((B, n_gz, S), F32),
        ],
        compiler_params=_params(("parallel",)),
        name="inproj0",
    )(x2, g, wrm, wfm, wgz)


def _compress_body(rk_ref, rv_ref, pos_ref, w1_ref, w2k_ref, w2vt_ref, flip_ref, kc_ref, vct_ref):
    half = NSA_CMP_STRIDE * HEAD_DIM

    def hidden(r_ref, s):
        r = r_ref[0, 0]
        a = _dot(r, w1_ref[s, :half, :])
        b = _dot(r, w1_ref[s, half:, :])
        nxt = pltpu.roll(b, b.shape[0] - 1, axis=0)
        posb = _dot(pos_ref[s], w1_ref[s])[0:1]
        pre = a + nxt + posb
        act = (pre * jax.nn.sigmoid(pre)).astype(BF16)
        return _dot(flip_ref[...], act).astype(BF16)

    kc_ref[0, 0] = _dot(hidden(rk_ref, 0), w2k_ref[...]).astype(BF16)
    vct_ref[0, 0] = _dot_nt(w2vt_ref[...], hidden(rv_ref, 1)).astype(BF16)


def _compress(r, pos, w1, w2k, w2vt, B, NC):
    G = NSA_KV_GROUPS
    half = NSA_CMP_STRIDE * HEAD_DIM
    return pl.pallas_call(
        _compress_body,
        grid=(B, G),
        in_specs=[
            pl.BlockSpec((1, 1, NC, half), lambda b, g: (b, g, 0, 0)),
            pl.BlockSpec((1, 1, NC, half), lambda b, g: (b, G + g, 0, 0)),
            _const_spec(pos.shape),
            _const_spec(w1.shape),
            _const_spec(w2k.shape),
            _const_spec(w2vt.shape),
            _const_spec((NC, NC)),
        ],
        out_specs=[
            pl.BlockSpec((1, 1, NC, HEAD_DIM), lambda b, g: (b, g, 0, 0)),
            pl.BlockSpec((1, 1, HEAD_DIM, NC), lambda b, g: (b, g, 0, 0)),
        ],
        out_shape=[
            jax.ShapeDtypeStruct((B, G, NC, HEAD_DIM), BF16),
            jax.ShapeDtypeStruct((B, G, HEAD_DIM, NC), BF16),
        ],
        compiler_params=_params(("parallel", "parallel")),
        name="nsa_compress",
    )(r, r, pos, w1, w2k, w2vt, jnp.asarray(np.eye(NC)[::-1], BF16))


def _moba_body(q_ref, k_ref, v_ref, t_ref, o_ref, kmean_ref, mask_ref, s_scr, acc_scr, *, n_mb, topk):
    c = pl.program_id(2)
    blk = c // QPK

    @pl.when(c == 0)
    def _():
        kmean_ref[...] = jnp.zeros_like(kmean_ref)
        for n in range(n_mb):
            kblk = k_ref[0, n * TK:(n + 1) * TK, :].astype(F32)
            kmean_ref[n:n + 1, :] = jnp.mean(kblk, axis=0, keepdims=True)

    q = q_ref[0, 0]
    rowi = lax.broadcasted_iota(jnp.int32, (LANES, TQ), 0)
    nidx = lax.broadcasted_iota(jnp.int32, (kmean_ref.shape[0], TQ), 0)
    km = _split3(kmean_ref[...])
    qpads = []
    for h in range(HPS):
        lo = (h // 2) * LANES
        qpair = q[lo:lo + LANES, :]
        qh = jnp.where(rowi // HEAD_DIM == h % 2, qpair, jnp.zeros_like(qpair))
        route = sum(_dot(part[:, lo:lo + LANES], qh) for part in km)
        route = jnp.where(nidx < blk, route, NEG_INF)
        sel = _rank_select(route, nidx, n_mb, topk) & (nidx < blk)
        mask_ref[h] = jnp.where(sel | (nidx == blk), 0.0, NEG_INF)
        qpads.append(qh)

    def qk_fn(h, n):
        rows = pl.ds(pl.multiple_of(n * TK, TK), TK)
        return _dot(k_ref[0, rows, (h // 2) * LANES:(h // 2 + 1) * LANES], qpads[h])

    def fix_fn(h, n, s):
        return s + mask_ref[h, pl.ds(n, 1), :] + t_ref[h, jnp.minimum(c - QPK * n, NE_BIAS)]

    def v_fn(h, n):
        return v_ref[0, n, h * HEAD_DIM:(h + 1) * HEAD_DIM, :]

    outs = _attend(blk + 1, lambda i: blk - i, qk_fn, fix_fn, v_fn, s_scr, acc_scr, HPS)
    o_ref[0] = jnp.concatenate(outs, axis=0).astype(BF16)


def _moba(fm, rm, tab, B, S):
    n_mb = S // MOBA_BLOCK
    n_pad = -(-n_mb // 16) * 16
    ne = tab.shape[1]
    rows = HPS * HEAD_DIM
    body = functools.partial(_moba_body, n_mb=n_mb, topk=min(MOBA_TOPK, n_mb))
    return pl.pallas_call(
        body,
        grid=(B, MOBA_HEADS // HPS, S // TQ),
        in_specs=[
            pl.BlockSpec((1, 1, rows, TQ), lambda b, p, c: (b, c // QPK, p, c % QPK)),
            pl.BlockSpec((1, S, rows), lambda b, p, c: (b, 0, p)),
            pl.BlockSpec((1, S // TK, rows, TK), lambda b, p, c: (b, 0, MOBA_HEADS // HPS + p, 0)),
            pl.BlockSpec((HPS, ne, TK, TQ), lambda b, p, c: (p, 0, 0, 0), pipeline_mode=pl.Buffered(1)),
        ],
        out_specs=pl.BlockSpec((1, rows, TQ), lambda b, p, c: (b, p, c)),
        out_shape=jax.ShapeDtypeStruct((B, MOBA_W, S), BF16),
        scratch_shapes=[pltpu.VMEM((n_pad, rows), F32), pltpu.VMEM((HPS, n_pad, TQ), F32),
                        pltpu.VMEM((HPS, TK, TQ), F32), pltpu.VMEM((HPS, ACC_ROWS, TQ), F32)],
        compiler_params=_params(("parallel", "parallel", "arbitrary")),
        name="moba_attn",
    )(fm, rm, fm, tab)


def _nsa_body(q_ref, kc_ref, vct_ref, fc_ref, ovl_ref, ksl_ref, vsl_ref, kwn_ref, vwn_ref,
              tslc_ref, twin_ref, gz_ref, o_ref, s_scr, acc_scr, s_win, acc_win, sel_ref, *, n_sb, n_sel):
    c = pl.program_id(1)
    blk = c // QPK
    G, J = NSA_KV_GROUPS, NSA_HPG
    H = G * J

    q = q_ref[0, 0]
    qs = [q[h * HEAD_DIM:(h + 1) * HEAD_DIM, :] for h in range(H)]
    zero = jnp.zeros((HEAD_DIM, TQ), BF16)
    qpads = [jnp.concatenate([zero] * (h // J) + [qs[h]] + [zero] * (G - 1 - h // J), axis=0) for h in range(H)]

    c0 = pl.multiple_of(c * (TQ // NSA_CMP_STRIDE), TQ // NSA_CMP_STRIDE)
    o_cmp = []
    for g in range(G):
        kc = kc_ref[0, g]
        vct = vct_ref[0, g]
        psum = jnp.zeros((kc.shape[0], TQ), F32)
        for h in range(g * J, (g + 1) * J):
            s = _dot(kc, qs[h]) + fc_ref[h, pl.ds(c0, kc.shape[0]), :]
            m = jnp.max(s, axis=0, keepdims=True)
            p = jnp.exp2(s - m)
            l = jnp.sum(p, axis=0, keepdims=True)
            pn = p * jnp.where(m > 0.5 * NEG_INF, 1.0 / l, 0.0)
            o_cmp.append(_dot(vct, pn.astype(BF16)))
            psum = psum + pn

        ph = psum.astype(BF16)
        plo = (psum - ph.astype(F32)).astype(BF16)
        imp = _dot(ovl_ref[...], ph) + _dot(ovl_ref[...], plo)
        jb = lax.broadcasted_iota(jnp.int32, imp.shape, 0)
        t = c * TQ + lax.broadcasted_iota(jnp.int32, imp.shape, 1)
        sb = t // NSA_SLC_BLOCK
        forced = (jb == 0) | (jb == sb) | (jb == sb - 1)
        allowed = jb <= sb
        val = jnp.where(forced, imp + NSA_FORCE_SCORE, jnp.where(allowed, imp, NEG_INF))
        sel = _rank_select(val, jb, n_sb, n_sel) & allowed
        selb = jnp.where(sel, 0.0, NEG_INF)
        pad = jnp.zeros((sel_ref.shape[2] - SPT, TQ), F32)
        for n in range(n_sb // SPT):
            sel_ref[g, n] = jnp.concatenate([selb[n * SPT:(n + 1) * SPT, :], pad], axis=0)

    def slc_qk(h, n):
        return _dot(ksl_ref[0, pl.ds(pl.multiple_of(n * TK, TK), TK), :], qpads[h])

    def slc_fix(h, n, s):
        rows = sel_ref[h // J, n]
        mask = jnp.concatenate([jnp.broadcast_to(rows[b:b + 1, :], (NSA_SLC_BLOCK, TQ)) for b in range(SPT)], axis=0)
        return s + mask + tslc_ref[h, jnp.minimum(c - QPK * n, NE_BIAS)]

    def slc_v(h, n):
        return vsl_ref[0, n, (h // J) * HEAD_DIM:(h // J + 1) * HEAD_DIM, :]

    o_slc = _attend(blk + 1, lambda i: blk - i, slc_qk, slc_fix, slc_v, s_scr, acc_scr, H)

    def win_qk(h, n):
        return _dot(kwn_ref[0, pl.ds(pl.multiple_of(n * TK, TK), TK), :], qpads[h])

    def win_fix(h, n, s):
        return s + twin_ref[h, c - QPK * n]

    def win_v(h, n):
        return vwn_ref[0, n, (h // J) * HEAD_DIM:(h // J + 1) * HEAD_DIM, :]

    w_lo = jnp.maximum(c - NE_WIN + QPK, 0) // QPK
    o_win = _attend(blk - w_lo + 1, lambda i: blk - i, win_qk, win_fix, win_v, s_win, acc_win, H)

    gate = jax.nn.sigmoid(gz_ref[0])
    outs = []
    for h in range(H):
        r = (h // J) * GZ_ROWS + h % J
        outs.append(gate[r:r + 1, :] * o_cmp[h] + gate[r + J:r + J + 1, :] * o_slc[h]
                    + gate[r + 2 * J:r + 2 * J + 1, :] * o_win[h])
    o_ref[0] = jnp.concatenate(outs, axis=0).astype(BF16)


def _nsa(fm, rm, gz, kc, vct, tcmp, ovl, tslc, twin, B, S, col_ksl, col_kwn, row_q, row_vsl, row_vwn):
    G, J = NSA_KV_GROUPS, NSA_HPG
    NC = kc.shape[2]
    n_sb = S // NSA_SLC_BLOCK
    body = functools.partial(_nsa_body, n_sb=n_sb, n_sel=min(NSA_TOPN, n_sb))
    H = G * J
    kvrows = G * HEAD_DIM
    one = pl.Buffered(1)
    return pl.pallas_call(
        body,
        grid=(B, S // TQ),
        in_specs=[
            pl.BlockSpec((1, 1, NSA_W, TQ), lambda b, c: (b, c // QPK, row_q // NSA_W, c % QPK)),
            pl.BlockSpec((1, G, NC, HEAD_DIM), lambda b, c: (b, 0, 0, 0)),
            pl.BlockSpec((1, G, HEAD_DIM, NC), lambda b, c: (b, 0, 0, 0)),
            _const_spec(tcmp.shape),
            _const_spec(ovl.shape),
            pl.BlockSpec((1, S, LANES), lambda b, c: (b, 0, col_ksl // LANES)),
            pl.BlockSpec((1, S // TK, kvrows, TK), lambda b, c: (b, 0, row_vsl // kvrows, 0)),
            pl.BlockSpec((1, S, LANES), lambda b, c: (b, 0, col_kwn // LANES)),
            pl.BlockSpec((1, S // TK, kvrows, TK), lambda b, c: (b, 0, row_vwn // kvrows, 0)),
            pl.BlockSpec((H, tslc.shape[1], TK, TQ), lambda b, c: (MOBA_HEADS // H, 0, 0, 0), pipeline_mode=one),
            _const_spec(twin.shape),
            pl.BlockSpec((1, G * GZ_ROWS, TQ), lambda b, c: (b, 0, c)),
        ],
        out_specs=pl.BlockSpec((1, NSA_W, TQ), lambda b, c: (b, 0, c)),
        out_shape=jax.ShapeDtypeStruct((B, NSA_W, S), BF16),
        scratch_shapes=[pltpu.VMEM((H, TK, TQ), F32), pltpu.VMEM((H, ACC_ROWS, TQ), F32),
                        pltpu.VMEM((H, TK, TQ), F32), pltpu.VMEM((H, ACC_ROWS, TQ), F32),
                        pltpu.VMEM((G, S // TK, 8, TQ), F32)],
        compiler_params=_params(("parallel", "arbitrary")),
        name="nsa_attn",
    )(fm, kc, vct, tcmp, ovl, rm, fm, rm, fm, tslc, twin, gz)


def _inproj1_body(x_ref, g_ref, wfm_ref, wk_ref, wf_ref, bf_ref, tri_ref, place_ref, fm_ref, ka_ref, carry_ref,
                  *, nst):
    i = pl.program_id(0)

    @pl.when(i % nst == 0)
    def _():
        carry_ref[...] = jnp.zeros_like(carry_ref)

    xf = _rmsnorm(x_ref[...], g_ref[...])
    xn = xf.astype(BF16)
    xlo = (xf - xn.astype(F32)).astype(BF16)
    for r0 in range(0, fm_ref.shape[2], CH):
        res = _dot_nt(wfm_ref[r0:r0 + CH, :], xn).astype(BF16)
        for t in range(TM // TK):
            fm_ref[0, t, r0:r0 + CH, :] = res[:, t * TK:(t + 1) * TK]

    fz = _dot(xn, wf_ref[0]) + _dot(xlo, wf_ref[0]) + _dot(xn, wf_ref[1]) + bf_ref[...]
    logf = jnp.minimum(fz, 0.0) - jnp.log(1.0 + jnp.exp(-jnp.abs(fz)))
    tri = tri_ref[...]
    h1, h2, h3 = _split3(logf)
    cum = _dot(tri, h1) + _dot(tri, h2) + _dot(tri, h3) + carry_ref[0:1, :]
    carry_ref[...] = jnp.broadcast_to(cum[TM - 1:TM, :], carry_ref.shape)
    c1, c2, c3 = _split3(cum * LOG2E)
    for c0 in range(0, ka_ref.shape[-1], FF_CH):
        ka = (_dot(xn, wk_ref[:, c0:c0 + FF_CH]) + _dot(c1, place_ref[0, :, c0:c0 + FF_CH])
              + _dot(c2, place_ref[1, :, c0:c0 + FF_CH]) + _dot(c3, place_ref[2, :, c0:c0 + FF_CH]))
        ka_ref[:, c0:c0 + FF_CH] = ka.astype(BF16)


def _inproj1(x2, g, wfm, wk, wf, bf, tri, place, B, S):
    M = B * S
    nst = S // TM
    n_fm, n_ka = wfm.shape[0], wk.shape[1]
    return pl.pallas_call(
        functools.partial(_inproj1_body, nst=nst),
        grid=(M // TM,),
        in_specs=[
            pl.BlockSpec((TM, D_MODEL), lambda i: (i, 0)),
            _const_spec((1, D_MODEL)),
            _const_spec(wfm.shape),
            _const_spec(wk.shape),
            _const_spec(wf.shape),
            _const_spec(bf.shape),
            _const_spec(tri.shape),
            _const_spec(place.shape),
        ],
        out_specs=[
            pl.BlockSpec((1, TM // TK, n_fm, TK), lambda i: (i // nst, i % nst, 0, 0)),
            pl.BlockSpec((TM, n_ka), lambda i: (i, 0)),
        ],
        out_shape=[
            jax.ShapeDtypeStruct((B, S // TK, n_fm, TK), BF16),
            jax.ShapeDtypeStruct((M, n_ka), BF16),
        ],
        scratch_shapes=[pltpu.VMEM((8, LANES), F32)],
        compiler_params=_params(("arbitrary",)),
        name="inproj1",
    )(x2, g, wfm, wk, wf, bf, tri, place)


def _fox_body(q_ref, k_ref, v_ref, cm_ref, o_ref, s_scr, acc_scr):
    c = pl.program_id(2)
    blk = c // QPK
    q = q_ref[0, 0]
    ones = jnp.ones((LANES - HEAD_DIM, TQ), BF16)
    qas = [jnp.concatenate([q[h * HEAD_DIM:(h + 1) * HEAD_DIM, :], ones], axis=0) for h in range(HPS)]

    def qk_fn(h, n):
        rows = pl.ds(pl.multiple_of(n * TK, TK), TK)
        return _dot(k_ref[0, rows, h * LANES:(h + 1) * LANES], qas[h])

    causal = cm_ref[c % QPK]
    outs = _attend(blk + 1, lambda i: blk - i, qk_fn, None,
                   lambda h, n: v_ref[0, n, h * HEAD_DIM:(h + 1) * HEAD_DIM, :], s_scr, acc_scr, HPS,
                   first_fix=lambda h, n, s: s + causal)
    o_ref[0] = jnp.concatenate(outs, axis=0).astype(BF16)


def _fox(fm, ka, cmask, B, S):
    rows = HPS * HEAD_DIM
    return pl.pallas_call(
        _fox_body,
        grid=(B, FOX_HEADS // HPS, S // TQ),
        in_specs=[
            pl.BlockSpec((1, 1, rows, TQ), lambda b, h, c: (b, c // QPK, h, c % QPK)),
            pl.BlockSpec((1, S, HPS * LANES), lambda b, h, c: (b, 0, h)),
            pl.BlockSpec((1, S // TK, rows, TK), lambda b, h, c: (b, 0, FOX_HEADS // HPS + h, 0)),
            _const_spec(cmask.shape),
        ],
        out_specs=pl.BlockSpec((1, rows, TQ), lambda b, h, c: (b, h, c)),
        out_shape=jax.ShapeDtypeStruct((B, FOX_W, S), BF16),
        scratch_shapes=[pltpu.VMEM((HPS, TK, TQ), F32), pltpu.VMEM((HPS, ACC_ROWS, TQ), F32)],
        compiler_params=_params(("parallel", "parallel", "arbitrary")),
        name="fox_attn",
    )(fm, ka, fm, cmask)


def _post_body(*refs, n_parts, final):
    o_refs = refs[:n_parts]
    h_ref, wo_ref, g_ref, w1_ref, w2_ref = refs[n_parts:n_parts + 5]
    gf_ref = refs[n_parts + 5] if final else None
    out_ref, hn_ref = refs[-2:]
    h1 = h_ref[...]
    r0 = 0
    for o_ref in o_refs:
        nf = o_ref.shape[1]
        h1 = h1 + _dot_tn(o_ref[0], wo_ref[r0:r0 + nf, :])
        r0 += nf
    out_ref[...] = h1
    hn_ref[...] = _rmsnorm(out_ref[...], g_ref[...]).astype(BF16)
    for c0 in range(0, D_FF, FF_CH):
        a = jnp.maximum(_dot(hn_ref[...], w1_ref[:, c0:c0 + FF_CH]), 0.0)
        out_ref[...] += _dot((a * a).astype(BF16), w2_ref[c0:c0 + FF_CH, :])
    if final:
        out_ref[...] = _rmsnorm(out_ref[...], gf_ref[...])


def _post(o_parts, h2, wo, g, w1, w2, gf, B, S):
    M = B * S
    nst = S // TM
    final = gf is not None
    in_specs = [pl.BlockSpec((1, o.shape[1], TM), lambda i: (i // nst, 0, i % nst)) for o in o_parts]
    in_specs += [
        pl.BlockSpec((TM, D_MODEL), lambda i: (i, 0)),
        _const_spec(wo.shape),
        _const_spec((1, D_MODEL)),
        _const_spec(w1.shape),
        _const_spec(w2.shape),
    ]
    args = list(o_parts) + [h2, wo, g, w1, w2]
    if final:
        in_specs.append(_const_spec((1, D_MODEL)))
        args.append(gf)
    return pl.pallas_call(
        functools.partial(_post_body, n_parts=len(o_parts), final=final),
        grid=(M // TM,),
        in_specs=in_specs,
        out_specs=pl.BlockSpec((TM, D_MODEL), lambda i: (i, 0)),
        out_shape=jax.ShapeDtypeStruct((M, D_MODEL), F32),
        scratch_shapes=[pltpu.VMEM((TM, D_MODEL), BF16)],
        compiler_params=_params(("parallel",)),
        name="post_final" if final else "post",
    )(*args)


def _rel_bucket(dist):
    n = jnp.maximum(dist, 0)
    max_exact = REL_BUCKETS // 2
    nf = jnp.maximum(n, 1).astype(jnp.float32)
    large = max_exact + (jnp.log(nf / max_exact) / math.log(REL_MAX_DISTANCE / max_exact)
                         * (REL_BUCKETS - max_exact)).astype(jnp.int32)
    large = jnp.minimum(large, REL_BUCKETS - 1)
    return jnp.where(n < max_exact, n, large)


def _bias_tables(rel_bias, S):
    n_heads = rel_bias.shape[1]
    table = rel_bias.T * LOG2E

    def bias_of(dist):
        bkt = _rel_bucket(jnp.asarray(dist))[None, :]
        out = jnp.zeros((n_heads, len(dist)), F32)
        for b in range(REL_BUCKETS):
            out = jnp.where(bkt == b, table[:, b:b + 1], out)
        return jnp.where(jnp.asarray(dist)[None, :] >= 0, out, NEG_INF)

    def shifted_rows(w, n_rows, step):
        p = w.shape[1]
        return jnp.tile(w, (1, n_rows))[:, :n_rows * (p - step)].reshape(w.shape[0], n_rows, p - step)

    w = bias_of(np.arange(-(TK - 1), NE_BIAS * TQ + TQ))
    toep = shifted_rows(w, TK, 1)[:, :, TK - 1:TK - 1 + NE_BIAS * TQ]
    vals = toep.reshape(n_heads, TK, NE_BIAS, TQ).transpose(0, 2, 1, 3)
    d = (np.arange(NE_BIAS)[:, None, None] * TQ + np.arange(TQ)[None, None, :] - np.arange(TK)[None, :, None])
    far = table[:, REL_BUCKETS - 1][:, None, None, None]
    tile = jnp.where(d >= 0, vals - far, NEG_INF)
    tile = jnp.concatenate([tile, jnp.zeros_like(tile[:, :1])], axis=1)
    dw = d[:NE_WIN]
    twin = jnp.where((dw >= 0) & (dw < NSA_WINDOW), vals[MOBA_HEADS:, :NE_WIN], NEG_INF)
    n_c = S // NSA_CMP_STRIDE
    u = np.arange(2 * n_c)[:, None]
    dc = np.arange(TQ)[None, :] + NSA_CMP_STRIDE * (u - (n_c - 1)) - (NSA_CMP_BLOCK - 1)
    fcmp = bias_of(dc.reshape(-1))[MOBA_HEADS:].reshape(-1, 2 * n_c, TQ)
    return tile, twin, fcmp


def _selection_constants(S):
    n_c = S // NSA_CMP_STRIDE
    n_cmp = (S - NSA_CMP_BLOCK) // NSA_CMP_STRIDE + 1
    n_sb = S // NSA_SLC_BLOCK
    ci = np.arange(n_c)[None, :] * NSA_CMP_STRIDE
    sj = np.arange(n_sb)[:, None] * NSA_SLC_BLOCK
    ovl = (ci < sj + NSA_SLC_BLOCK) & (ci + NSA_CMP_BLOCK > sj) & (np.arange(n_c)[None, :] < n_cmp)
    ovl = ovl[:, ::-1]
    return jnp.asarray(ovl, BF16)


def _causal_tiles():
    e = np.arange(QPK)[:, None, None]
    d = e * TQ + np.arange(TQ)[None, None, :] - np.arange(TK)[None, :, None]
    return jnp.asarray(np.where(d >= 0, 0.0, NEG_INF), F32)


def kernel(x, rel_bias, mix_norm, mlp_norm, even_w_in, even_w_out, cmp_pos_k, cmp_pos_v, cmp_k_w1, cmp_k_w2,
           cmp_v_w1, cmp_v_w2, odd_w_in, odd_b_forget, odd_w_out, mlp_w1, mlp_w2, final_norm):
    B, S, D = x.shape
    assert D == D_MODEL and S % TM == 0
    G, J = NSA_KV_GROUPS, NSA_HPG
    h = x.reshape(B * S, D)

    offs = np.cumsum((MOBA_W, MOBA_W, MOBA_W, NSA_W) + (NSA_KV_W,) * 6)
    mq_w, mk_w, mv_w, nq_w, kc_w, vc_w, ksl_w, vsl_w, kwn_w, vwn_w, gz_w = jnp.split(even_w_in[0], offs, axis=1)
    wrm = jnp.concatenate([mk_w, kc_w, vc_w, ksl_w, kwn_w], axis=1).astype(BF16)
    col_kcvc, col_ksl, col_kwn = MOBA_W, MOBA_W + 2 * NSA_KV_W, MOBA_W + 3 * NSA_KV_W
    qs = SCALE * LOG2E
    wfm = jnp.concatenate([mq_w * qs, mv_w, nq_w * qs, vsl_w, vwn_w], axis=1).T.astype(BF16)
    row_nq, row_vsl, row_vwn = 2 * MOBA_W, 2 * MOBA_W + NSA_W, 2 * MOBA_W + NSA_W + NSA_KV_W
    gzw = gz_w.T.reshape(G, J, 3, D).transpose(0, 2, 1, 3).reshape(G, 3 * J, D)
    gzw = jnp.pad(gzw, ((0, 0), (0, GZ_ROWS - 3 * J), (0, 0))).reshape(G * GZ_ROWS, D).astype(BF16)

    rm, fm, gz = _inproj0(h, mix_norm[0][None, :], wrm, wfm, gzw, B, S)
    rm = rm.reshape(B, S, -1)

    tile, twin, tcmp = _bias_tables(rel_bias, S)
    ovl = _selection_constants(S)

    o_moba = _moba(fm, rm, tile, B, S)

    n_c = S // NSA_CMP_STRIDE
    r = rm[:, :, col_kcvc:col_kcvc + 2 * NSA_KV_W].reshape(B, n_c, NSA_CMP_STRIDE, 2 * G, HEAD_DIM)
    r = r.transpose(0, 3, 1, 2, 4).reshape(B, 2 * G, n_c, NSA_CMP_STRIDE * HEAD_DIM)
    pos = jnp.stack([cmp_pos_k[0].reshape(1, -1), cmp_pos_v[0].reshape(1, -1)])
    pos = jnp.pad(pos, ((0, 0), (0, 7), (0, 0))).astype(BF16)
    w1c = jnp.stack([cmp_k_w1[0], cmp_v_w1[0]]).astype(BF16)
    kc, vct = _compress(r, pos, w1c, cmp_k_w2[0].astype(BF16), cmp_v_w2[0].T.astype(BF16), B, n_c)

    o_nsa = _nsa(fm, rm, gz, kc, vct, tcmp, ovl, tile, twin, B, S,
                 col_ksl, col_kwn, row_nq, row_vsl, row_vwn)

    h = _post([o_moba, o_nsa], h, even_w_out[0].astype(BF16), mlp_norm[0][None, :],
              mlp_w1[0].astype(BF16), mlp_w2[0].astype(BF16), None, B, S)

    q_w, k_w, v_w, f_w = jnp.split(odd_w_in[0], np.cumsum((FOX_W, FOX_W, FOX_W)), axis=1)
    wfm1 = jnp.concatenate([q_w * qs, v_w], axis=1).T.astype(BF16)
    wk = jnp.pad(k_w.reshape(D, FOX_HEADS, HEAD_DIM), ((0, 0), (0, 0), (0, LANES - HEAD_DIM)))
    wk = wk.reshape(D, FOX_HEADS * LANES).astype(BF16)
    f_w = jnp.pad(f_w, ((0, 0), (0, LANES - FOX_HEADS)))
    f_hi = f_w.astype(BF16)
    wf = jnp.stack([f_hi, (f_w - f_hi.astype(F32)).astype(BF16)])
    bf = jnp.pad(odd_b_forget[0], (0, LANES - FOX_HEADS))[None, :]
    tri = jnp.asarray(np.tril(np.ones((TM, TM))), BF16)
    place = np.zeros((3, LANES, FOX_HEADS * LANES), np.float32)
    for term in range(3):
        place[term, np.arange(FOX_HEADS), np.arange(FOX_HEADS) * LANES + HEAD_DIM + term] = -1.0
    fm1, ka = _inproj1(h, mix_norm[1][None, :], wfm1, wk, wf, bf, tri, jnp.asarray(place, BF16), B, S)
    o_fox = _fox(fm1, ka.reshape(B, S, -1), _causal_tiles(), B, S)

    h = _post([o_fox], h, odd_w_out[0].astype(BF16), mlp_norm[1][None, :],
              mlp_w1[1].astype(BF16), mlp_w2[1].astype(BF16), final_norm[None, :], B, S)
    return h.reshape(B, S, D)
```

```python
import functools
import math

import numpy as np
import jax
import jax.numpy as jnp
from jax import lax
from jax.experimental import pallas as pl
from jax.experimental.pallas import tpu as pltpu

D_MODEL = 1024
HEAD_DIM = 64
MOBA_HEADS = 8
MOBA_BLOCK = 256
MOBA_TOPK = 3
NSA_HEADS = 8
NSA_KV_GROUPS = 2
NSA_HPG = NSA_HEADS // NSA_KV_GROUPS
NSA_CMP_BLOCK = 32
NSA_CMP_STRIDE = 16
NSA_CMP_HIDDEN = 256
NSA_SLC_BLOCK = 64
NSA_TOPN = 16
NSA_WINDOW = 512
NSA_FORCE_SCORE = 1e6
FOX_HEADS = 16
D_FF = 4 * D_MODEL
REL_BUCKETS = 32
REL_MAX_DISTANCE = 1024
RMS_EPS = 1e-5
NEG_INF = -1e30
SCALE = HEAD_DIM ** -0.5

MOBA_W = MOBA_HEADS * HEAD_DIM
NSA_W = NSA_HEADS * HEAD_DIM
NSA_KV_W = NSA_KV_GROUPS * HEAD_DIM
FOX_W = FOX_HEADS * HEAD_DIM

LANES = 128
TQ = 256
TK = 256
QPK = TK // TQ
TM = 512
CH = 256
FF_CH = 512
VMEM_LIMIT = 56 * 1024 * 1024
NE_BIAS = -(-(REL_MAX_DISTANCE + TK - 1) // TQ)
NE_WIN = -(-(NSA_WINDOW + TK - 1) // TQ)
GZ_ROWS = 16
HPS = 8
SPT = TK // NSA_SLC_BLOCK
ACC_ROWS = HEAD_DIM + 16
ROW_CHUNK = 64
LOG2E = math.log2(math.e)

assert TK % TQ == 0 and MOBA_BLOCK == TK and TK % NSA_SLC_BLOCK == 0

F32 = jnp.float32
BF16 = jnp.bfloat16


def _dot(a, b):
    return jnp.dot(a, b, preferred_element_type=F32)


def _dot_nt(a, b):
    return lax.dot_general(a, b, (((1,), (1,)), ((), ())), preferred_element_type=F32)


def _dot_tn(a, b):
    return lax.dot_general(a, b, (((0,), (0,)), ((), ())), preferred_element_type=F32)


def _rmsnorm(x, g):
    ms = jnp.mean(x * x, axis=-1, keepdims=True)
    return x * lax.rsqrt(ms + RMS_EPS) * g


def _split3(x):
    a = x.astype(BF16)
    r = x - a.astype(F32)
    b = r.astype(BF16)
    c = (r - b.astype(F32)).astype(BF16)
    return a, b, c


def _const_spec(shape):
    nd = len(shape)
    return pl.BlockSpec(shape, lambda *_: (0,) * nd, pipeline_mode=pl.Buffered(1))


def _params(sem):
    return pltpu.CompilerParams(dimension_semantics=sem, vmem_limit_bytes=VMEM_LIMIT)


def _attend(n_tiles, tile_of, qk_fn, fix_fn, v_fn, s_scr, acc_scr, n_heads, first_fix=None):
    def put_scores(h, n, fix):
        s = qk_fn(h, n)
        s_scr[h] = s if fix is None else fix(h, n, s)

    for h in range(n_heads):
        put_scores(h, tile_of(0), fix_fn if first_fix is None else first_fix)
    acc_scr[...] = jnp.zeros_like(acc_scr)
    last = n_tiles - 1
    ones = jnp.ones((ACC_ROWS - HEAD_DIM, TK), BF16)
    chunks = range(0, TK, ROW_CHUNK)

    def body(i, ms):
        n = tile_of(i)
        n_next = tile_of(jnp.minimum(i + 1, last))
        out = []
        for h in range(n_heads):
            mx = s_scr[h, 0:ROW_CHUNK, :]
            for r0 in chunks[1:]:
                mx = jnp.maximum(mx, s_scr[h, r0:r0 + ROW_CHUNK, :])
            m_new = jnp.maximum(ms[h], jnp.max(mx, axis=0, keepdims=True))
            alpha = jnp.exp2(ms[h] - m_new)
            p = jnp.concatenate([jnp.exp2(s_scr[h, r0:r0 + ROW_CHUNK, :] - m_new).astype(BF16) for r0 in chunks],
                                axis=0)
            va = jnp.concatenate([v_fn(h, n), ones], axis=0)
            acc_scr[h] = alpha * acc_scr[h] + _dot(va, p)
            out.append(m_new)
            put_scores(h, n_next, fix_fn)
        return tuple(out)

    lax.fori_loop(0, n_tiles, body, tuple(jnp.full((1, TQ), NEG_INF, F32) for _ in range(n_heads)))
    return [acc_scr[h, :HEAD_DIM, :] / acc_scr[h, HEAD_DIM:HEAD_DIM + 1, :] for h in range(n_heads)]


def _rank_select(val, idx, n_rows, k):
    cnt = jnp.zeros(val.shape, F32)
    for m in range(n_rows):
        vm = val[m:m + 1, :]
        beats = (vm > val) | ((vm == val) & (idx > m))
        cnt = cnt + jnp.where(beats, 1.0, 0.0)
    return cnt < k


def _inproj0_body(x_ref, g_ref, wrm_ref, wfm_ref, wgz_ref, rm_ref, fm_ref, gz_ref):
    xn = _rmsnorm(x_ref[...], g_ref[...]).astype(BF16)
    for c0 in range(0, rm_ref.shape[-1], CH):
        rm_ref[:, c0:c0 + CH] = _dot(xn, wrm_ref[:, c0:c0 + CH]).astype(BF16)
    for r0 in range(0, fm_ref.shape[2], CH):
        res = _dot_nt(wfm_ref[r0:r0 + CH, :], xn).astype(BF16)
        for t in range(TM // TK):
            fm_ref[0, t, r0:r0 + CH, :] = res[:, t * TK:(t + 1) * TK]
    gz_ref[0] = _dot_nt(wgz_ref[...], xn)


def _inproj0(x2, g, wrm, wfm, wgz, B, S):
    M = B * S
    nst = S // TM
    n_rm, n_fm, n_gz = wrm.shape[1], wfm.shape[0], wgz.shape[0]
    return pl.pallas_call(
        _inproj0_body,
        grid=(M // TM,),
        in_specs=[
            pl.BlockSpec((TM, D_MODEL), lambda i: (i, 0)),
            _const_spec((1, D_MODEL)),
            _const_spec((D_MODEL, n_rm)),
            _const_spec((n_fm, D_MODEL)),
            _const_spec((n_gz, D_MODEL)),
        ],
        out_specs=[
            pl.BlockSpec((TM, n_rm), lambda i: (i, 0)),
            pl.BlockSpec((1, TM // TK, n_fm, TK), lambda i: (i // nst, i % nst, 0, 0)),
            pl.BlockSpec((1, n_gz, TM), lambda i: (i // nst, 0, i % nst)),
        ],
        out_shape=[
            jax.ShapeDtypeStruct((M, n_rm), BF16),
            jax.ShapeDtypeStruct((B, S // TK, n_fm, TK), BF16),
            jax.ShapeDtypeStruct((B, n_gz, S), F32),
        ],
        compiler_params=_params(("parallel",)),
        name="inproj0",
    )(x2, g, wrm, wfm, wgz)


def _compress_body(rk_ref, rv_ref, pos_ref, w1_ref, w2k_ref, w2vt_ref, flip_ref, kc_ref, vct_ref):
    half = NSA_CMP_STRIDE * HEAD_DIM

    def hidden(r_ref, s):
        r = r_ref[0, 0]
        a = _dot(r, w1_ref[s, :half, :])
        b = _dot(r, w1_ref[s, half:, :])
        nxt = pltpu.roll(b, b.shape[0] - 1, axis=0)
        posb = _dot(pos_ref[s], w1_ref[s])[0:1]
        pre = a + nxt + posb
        act = (pre * jax.nn.sigmoid(pre)).astype(BF16)
        return _dot(flip_ref[...], act).astype(BF16)

    kc_ref[0, 0] = _dot(hidden(rk_ref, 0), w2k_ref[...]).astype(BF16)
    vct_ref[0, 0] = _dot_nt(w2vt_ref[...], hidden(rv_ref, 1)).astype(BF16)


def _compress(r, pos, w1, w2k, w2vt, B, NC):
    G = NSA_KV_GROUPS
    half = NSA_CMP_STRIDE * HEAD_DIM
    return pl.pallas_call(
        _compress_body,
        grid=(B, G),
        in_specs=[
            pl.BlockSpec((1, 1, NC, half), lambda b, g: (b, g, 0, 0)),
            pl.BlockSpec((1, 1, NC, half), lambda b, g: (b, G + g, 0, 0)),
            _const_spec(pos.shape),
            _const_spec(w1.shape),
            _const_spec(w2k.shape),
            _const_spec(w2vt.shape),
            _const_spec((NC, NC)),
        ],
        out_specs=[
            pl.BlockSpec((1, 1, NC, HEAD_DIM), lambda b, g: (b, g, 0, 0)),
            pl.BlockSpec((1, 1, HEAD_DIM, NC), lambda b, g: (b, g, 0, 0)),
        ],
        out_shape=[
            jax.ShapeDtypeStruct((B, G, NC, HEAD_DIM), BF16),
            jax.ShapeDtypeStruct((B, G, HEAD_DIM, NC), BF16),
        ],
        compiler_params=_params(("parallel", "parallel")),
        name="nsa_compress",
    )(r, r, pos, w1, w2k, w2vt, jnp.asarray(np.eye(NC)[::-1], BF16))


def _moba_body(q_ref, k_ref, v_ref, t_ref, o_ref, kmean_ref, mask_ref, s_scr, acc_scr, *, n_mb, topk):
    c = pl.program_id(2)
    blk = c // QPK

    @pl.when(c == 0)
    def _():
        kmean_ref[...] = jnp.zeros_like(kmean_ref)
        for n in range(n_mb):
            kblk = k_ref[0, n * TK:(n + 1) * TK, :].astype(F32)
            kmean_ref[n:n + 1, :] = jnp.mean(kblk, axis=0, keepdims=True)

    q = q_ref[0, 0]
    rowi = lax.broadcasted_iota(jnp.int32, (LANES, TQ), 0)
    nidx = lax.broadcasted_iota(jnp.int32, (kmean_ref.shape[0], TQ), 0)
    km = _split3(kmean_ref[...])
    qpads = []
    for h in range(HPS):
        lo = (h // 2) * LANES
        qpair = q[lo:lo + LANES, :]
        qh = jnp.where(rowi // HEAD_DIM == h % 2, qpair, jnp.zeros_like(qpair))
        route = sum(_dot(part[:, lo:lo + LANES], qh) for part in km)
        route = jnp.where(nidx < blk, route, NEG_INF)
        sel = _rank_select(route, nidx, n_mb, topk) & (nidx < blk)
        mask_ref[h] = jnp.where(sel | (nidx == blk), 0.0, NEG_INF)
        qpads.append(qh)

    def qk_fn(h, n):
        rows = pl.ds(pl.multiple_of(n * TK, TK), TK)
        return _dot(k_ref[0, rows, (h // 2) * LANES:(h // 2 + 1) * LANES], qpads[h])

    def fix_fn(h, n, s):
        return s + mask_ref[h, pl.ds(n, 1), :] + t_ref[h, jnp.minimum(c - QPK * n, NE_BIAS)]

    def v_fn(h, n):
        return v_ref[0, n, h * HEAD_DIM:(h + 1) * HEAD_DIM, :]

    outs = _attend(blk + 1, lambda i: blk - i, qk_fn, fix_fn, v_fn, s_scr, acc_scr, HPS)
    o_ref[0] = jnp.concatenate(outs, axis=0).astype(BF16)


def _moba(fm, rm, tab, B, S):
    n_mb = S // MOBA_BLOCK
    n_pad = -(-n_mb // 16) * 16
    ne = tab.shape[1]
    rows = HPS * HEAD_DIM
    body = functools.partial(_moba_body, n_mb=n_mb, topk=min(MOBA_TOPK, n_mb))
    return pl.pallas_call(
        body,
        grid=(B, MOBA_HEADS // HPS, S // TQ),
        in_specs=[
            pl.BlockSpec((1, 1, rows, TQ), lambda b, p, c: (b, c // QPK, p, c % QPK)),
            pl.BlockSpec((1, S, rows), lambda b, p, c: (b, 0, p)),
            pl.BlockSpec((1, S // TK, rows, TK), lambda b, p, c: (b, 0, MOBA_HEADS // HPS + p, 0)),
            pl.BlockSpec((HPS, ne, TK, TQ), lambda b, p, c: (p, 0, 0, 0), pipeline_mode=pl.Buffered(1)),
        ],
        out_specs=pl.BlockSpec((1, rows, TQ), lambda b, p, c: (b, p, c)),
        out_shape=jax.ShapeDtypeStruct((B, MOBA_W, S), BF16),
        scratch_shapes=[pltpu.VMEM((n_pad, rows), F32), pltpu.VMEM((HPS, n_pad, TQ), F32),
                        pltpu.VMEM((HPS, TK, TQ), F32), pltpu.VMEM((HPS, ACC_ROWS, TQ), F32)],
        compiler_params=_params(("parallel", "parallel", "arbitrary")),
        name="moba_attn",
    )(fm, rm, fm, tab)


def _nsa_body(q_ref, kc_ref, vct_ref, fc_ref, ovl_ref, ksl_ref, vsl_ref, kwn_ref, vwn_ref,
              tslc_ref, twin_ref, gz_ref, o_ref, s_scr, acc_scr, s_win, acc_win, sel_ref, val_ref, *, n_sb, n_sel):
    c = pl.program_id(1)
    blk = c // QPK
    G, J = NSA_KV_GROUPS, NSA_HPG
    H = G * J

    q = q_ref[0, 0]
    qs = [q[h * HEAD_DIM:(h + 1) * HEAD_DIM, :] for h in range(H)]
    zero = jnp.zeros((HEAD_DIM, TQ), BF16)
    qpads = [jnp.concatenate([zero] * (h // J) + [qs[h]] + [zero] * (G - 1 - h // J), axis=0) for h in range(H)]

    c0 = pl.multiple_of(c * (TQ // NSA_CMP_STRIDE), TQ // NSA_CMP_STRIDE)
    o_cmp = []
    for g in range(G):
        kc = kc_ref[0, g]
        vct = vct_ref[0, g]
        psum = jnp.zeros((kc.shape[0], TQ), F32)
        for h in range(g * J, (g + 1) * J):
            s = _dot(kc, qs[h]) + fc_ref[h, pl.ds(c0, kc.shape[0]), :]
            m = jnp.max(s, axis=0, keepdims=True)
            p = jnp.exp2(s - m)
            l = jnp.sum(p, axis=0, keepdims=True)
            pn = p * jnp.where(m > 0.5 * NEG_INF, 1.0 / l, 0.0)
            o_cmp.append(_dot(vct, pn.astype(BF16)))
            psum = psum + pn

        ph = psum.astype(BF16)
        plo = (psum - ph.astype(F32)).astype(BF16)
        imp = _dot(ovl_ref[...], ph) + _dot(ovl_ref[...], plo)
        jb = lax.broadcasted_iota(jnp.int32, imp.shape, 0)
        t = c * TQ + lax.broadcasted_iota(jnp.int32, imp.shape, 1)
        sb = t // NSA_SLC_BLOCK
        forced = (jb == 0) | (jb == sb) | (jb == sb - 1)
        allowed = jb <= sb
        val = jnp.where(forced, imp + NSA_FORCE_SCORE, jnp.where(allowed, imp, NEG_INF))
        val_ref[g] = val
        live = (c + 1) * (TQ // NSA_SLC_BLOCK)

        def count(i, cnt, g=g, val=val, jb=jb):
            for r in range(SPT):
                m = i * SPT + r
                vm = val_ref[g, pl.ds(m, 1), :]
                cnt = cnt + jnp.where((vm > val) | ((vm == val) & (jb > m)), 1.0, 0.0)
            return cnt

        cnt = lax.fori_loop(0, live // SPT, count, jnp.zeros(val.shape, F32))
        sel = (cnt < n_sel) & allowed
        selb = jnp.where(sel, 0.0, NEG_INF)
        pad = jnp.zeros((sel_ref.shape[2] - SPT, TQ), F32)
        for n in range(n_sb // SPT):
            sel_ref[g, n] = jnp.concatenate([selb[n * SPT:(n + 1) * SPT, :], pad], axis=0)

    def slc_qk(h, n):
        return _dot(ksl_ref[0, pl.ds(pl.multiple_of(n * TK, TK), TK), :], qpads[h])

    def slc_fix(h, n, s):
        rows = sel_ref[h // J, n]
        mask = jnp.concatenate([jnp.broadcast_to(rows[b:b + 1, :], (NSA_SLC_BLOCK, TQ)) for b in range(SPT)], axis=0)
        return s + mask + tslc_ref[h, jnp.minimum(c - QPK * n, NE_BIAS)]

    def slc_v(h, n):
        return vsl_ref[0, n, (h // J) * HEAD_DIM:(h // J + 1) * HEAD_DIM, :]

    o_slc = _attend(blk + 1, lambda i: blk - i, slc_qk, slc_fix, slc_v, s_scr, acc_scr, H)

    def win_qk(h, n):
        return _dot(kwn_ref[0, pl.ds(pl.multiple_of(n * TK, TK), TK), :], qpads[h])

    def win_fix(h, n, s):
        return s + twin_ref[h, c - QPK * n]

    def win_v(h, n):
        return vwn_ref[0, n, (h // J) * HEAD_DIM:(h // J + 1) * HEAD_DIM, :]

    w_lo = jnp.maximum(c - NE_WIN + QPK, 0) // QPK
    o_win = _attend(blk - w_lo + 1, lambda i: blk - i, win_qk, win_fix, win_v, s_win, acc_win, H)

    gate = jax.nn.sigmoid(gz_ref[0])
    outs = []
    for h in range(H):
        r = (h // J) * GZ_ROWS + h % J
        outs.append(gate[r:r + 1, :] * o_cmp[h] + gate[r + J:r + J + 1, :] * o_slc[h]
                    + gate[r + 2 * J:r + 2 * J + 1, :] * o_win[h])
    o_ref[0] = jnp.concatenate(outs, axis=0).astype(BF16)


def _nsa(fm, rm, gz, kc, vct, tcmp, ovl, tslc, twin, B, S, col_ksl, col_kwn, row_q, row_vsl, row_vwn):
    G, J = NSA_KV_GROUPS, NSA_HPG
    NC = kc.shape[2]
    n_sb = S // NSA_SLC_BLOCK
    body = functools.partial(_nsa_body, n_sb=n_sb, n_sel=min(NSA_TOPN, n_sb))
    H = G * J
    kvrows = G * HEAD_DIM
    one = pl.Buffered(1)
    return pl.pallas_call(
        body,
        grid=(B, S // TQ),
        in_specs=[
            pl.BlockSpec((1, 1, NSA_W, TQ), lambda b, c: (b, c // QPK, row_q // NSA_W, c % QPK)),
            pl.BlockSpec((1, G, NC, HEAD_DIM), lambda b, c: (b, 0, 0, 0)),
            pl.BlockSpec((1, G, HEAD_DIM, NC), lambda b, c: (b, 0, 0, 0)),
            _const_spec(tcmp.shape),
            _const_spec(ovl.shape),
            pl.BlockSpec((1, S, LANES), lambda b, c: (b, 0, col_ksl // LANES)),
            pl.BlockSpec((1, S // TK, kvrows, TK), lambda b, c: (b, 0, row_vsl // kvrows, 0)),
            pl.BlockSpec((1, S, LANES), lambda b, c: (b, 0, col_kwn // LANES)),
            pl.BlockSpec((1, S // TK, kvrows, TK), lambda b, c: (b, 0, row_vwn // kvrows, 0)),
            pl.BlockSpec((H, tslc.shape[1], TK, TQ), lambda b, c: (MOBA_HEADS // H, 0, 0, 0), pipeline_mode=one),
            _const_spec(twin.shape),
            pl.BlockSpec((1, G * GZ_ROWS, TQ), lambda b, c: (b, 0, c)),
        ],
        out_specs=pl.BlockSpec((1, NSA_W, TQ), lambda b, c: (b, 0, c)),
        out_shape=jax.ShapeDtypeStruct((B, NSA_W, S), BF16),
        scratch_shapes=[pltpu.VMEM((H, TK, TQ), F32), pltpu.VMEM((H, ACC_ROWS, TQ), F32),
                        pltpu.VMEM((H, TK, TQ), F32), pltpu.VMEM((H, ACC_ROWS, TQ), F32),
                        pltpu.VMEM((G, S // TK, 8, TQ), F32), pltpu.VMEM((G, n_sb, TQ), F32)],
        compiler_params=_params(("parallel", "arbitrary")),
        name="nsa_attn",
    )(fm, kc, vct, tcmp, ovl, rm, fm, rm, fm, tslc, twin, gz)


def _inproj1_body(x_ref, g_ref, wfm_ref, wk_ref, wf_ref, bf_ref, tri_ref, place_ref, fm_ref, ka_ref, carry_ref,
                  *, nst):
    i = pl.program_id(0)

    @pl.when(i % nst == 0)
    def _():
        carry_ref[...] = jnp.zeros_like(carry_ref)

    xf = _rmsnorm(x_ref[...], g_ref[...])
    xn = xf.astype(BF16)
    xlo = (xf - xn.astype(F32)).astype(BF16)
    for r0 in range(0, fm_ref.shape[2], CH):
        res = _dot_nt(wfm_ref[r0:r0 + CH, :], xn).astype(BF16)
        for t in range(TM // TK):
            fm_ref[0, t, r0:r0 + CH, :] = res[:, t * TK:(t + 1) * TK]

    fz = _dot(xn, wf_ref[0]) + _dot(xlo, wf_ref[0]) + _dot(xn, wf_ref[1]) + bf_ref[...]
    logf = jnp.minimum(fz, 0.0) - jnp.log(1.0 + jnp.exp(-jnp.abs(fz)))
    tri = tri_ref[...]
    h1, h2, h3 = _split3(logf)
    cum = _dot(tri, h1) + _dot(tri, h2) + _dot(tri, h3) + carry_ref[0:1, :]
    carry_ref[...] = jnp.broadcast_to(cum[TM - 1:TM, :], carry_ref.shape)
    c1, c2, c3 = _split3(cum * LOG2E)
    for c0 in range(0, ka_ref.shape[-1], FF_CH):
        ka = (_dot(xn, wk_ref[:, c0:c0 + FF_CH]) + _dot(c1, place_ref[0, :, c0:c0 + FF_CH])
              + _dot(c2, place_ref[1, :, c0:c0 + FF_CH]) + _dot(c3, place_ref[2, :, c0:c0 + FF_CH]))
        ka_ref[:, c0:c0 + FF_CH] = ka.astype(BF16)


def _inproj1(x2, g, wfm, wk, wf, bf, tri, place, B, S):
    M = B * S
    nst = S // TM
    n_fm, n_ka = wfm.shape[0], wk.shape[1]
    return pl.pallas_call(
        functools.partial(_inproj1_body, nst=nst),
        grid=(M // TM,),
        in_specs=[
            pl.BlockSpec((TM, D_MODEL), lambda i: (i, 0)),
            _const_spec((1, D_MODEL)),
            _const_spec(wfm.shape),
            _const_spec(wk.shape),
            _const_spec(wf.shape),
            _const_spec(bf.shape),
            _const_spec(tri.shape),
            _const_spec(place.shape),
        ],
        out_specs=[
            pl.BlockSpec((1, TM // TK, n_fm, TK), lambda i: (i // nst, i % nst, 0, 0)),
            pl.BlockSpec((TM, n_ka), lambda i: (i, 0)),
        ],
        out_shape=[
            jax.ShapeDtypeStruct((B, S // TK, n_fm, TK), BF16),
            jax.ShapeDtypeStruct((M, n_ka), BF16),
        ],
        scratch_shapes=[pltpu.VMEM((8, LANES), F32)],
        compiler_params=_params(("arbitrary",)),
        name="inproj1",
    )(x2, g, wfm, wk, wf, bf, tri, place)


def _fox_body(q_ref, k_ref, v_ref, cm_ref, o_ref, s_scr, acc_scr):
    c = pl.program_id(2)
    blk = c // QPK
    q = q_ref[0, 0]
    ones = jnp.ones((LANES - HEAD_DIM, TQ), BF16)
    qas = [jnp.concatenate([q[h * HEAD_DIM:(h + 1) * HEAD_DIM, :], ones], axis=0) for h in range(HPS)]

    def qk_fn(h, n):
        rows = pl.ds(pl.multiple_of(n * TK, TK), TK)
        return _dot(k_ref[0, rows, h * LANES:(h + 1) * LANES], qas[h])

    causal = cm_ref[c % QPK]
    outs = _attend(blk + 1, lambda i: blk - i, qk_fn, None,
                   lambda h, n: v_ref[0, n, h * HEAD_DIM:(h + 1) * HEAD_DIM, :], s_scr, acc_scr, HPS,
                   first_fix=lambda h, n, s: s + causal)
    o_ref[0] = jnp.concatenate(outs, axis=0).astype(BF16)


def _fox(fm, ka, cmask, B, S):
    rows = HPS * HEAD_DIM
    return pl.pallas_call(
        _fox_body,
        grid=(B, FOX_HEADS // HPS, S // TQ),
        in_specs=[
            pl.BlockSpec((1, 1, rows, TQ), lambda b, h, c: (b, c // QPK, h, c % QPK)),
            pl.BlockSpec((1, S, HPS * LANES), lambda b, h, c: (b, 0, h)),
            pl.BlockSpec((1, S // TK, rows, TK), lambda b, h, c: (b, 0, FOX_HEADS // HPS + h, 0)),
            _const_spec(cmask.shape),
        ],
        out_specs=pl.BlockSpec((1, rows, TQ), lambda b, h, c: (b, h, c)),
        out_shape=jax.ShapeDtypeStruct((B, FOX_W, S), BF16),
        scratch_shapes=[pltpu.VMEM((HPS, TK, TQ), F32), pltpu.VMEM((HPS, ACC_ROWS, TQ), F32)],
        compiler_params=_params(("parallel", "parallel", "arbitrary")),
        name="fox_attn",
    )(fm, ka, fm, cmask)


def _post_body(*refs, n_parts, final):
    o_refs = refs[:n_parts]
    h_ref, wo_ref, g_ref, w1_ref, w2_ref = refs[n_parts:n_parts + 5]
    gf_ref = refs[n_parts + 5] if final else None
    out_ref, hn_ref = refs[-2:]
    h1 = h_ref[...]
    r0 = 0
    for o_ref in o_refs:
        nf = o_ref.shape[1]
        h1 = h1 + _dot_tn(o_ref[0], wo_ref[r0:r0 + nf, :])
        r0 += nf
    out_ref[...] = h1
    hn_ref[...] = _rmsnorm(out_ref[...], g_ref[...]).astype(BF16)
    for c0 in range(0, D_FF, FF_CH):
        a = jnp.maximum(_dot(hn_ref[...], w1_ref[:, c0:c0 + FF_CH]), 0.0)
        out_ref[...] += _dot((a * a).astype(BF16), w2_ref[c0:c0 + FF_CH, :])
    if final:
        out_ref[...] = _rmsnorm(out_ref[...], gf_ref[...])


def _post(o_parts, h2, wo, g, w1, w2, gf, B, S):
    M = B * S
    nst = S // TM
    final = gf is not None
    in_specs = [pl.BlockSpec((1, o.shape[1], TM), lambda i: (i // nst, 0, i % nst)) for o in o_parts]
    in_specs += [
        pl.BlockSpec((TM, D_MODEL), lambda i: (i, 0)),
        _const_spec(wo.shape),
        _const_spec((1, D_MODEL)),
        _const_spec(w1.shape),
        _const_spec(w2.shape),
    ]
    args = list(o_parts) + [h2, wo, g, w1, w2]
    if final:
        in_specs.append(_const_spec((1, D_MODEL)))
        args.append(gf)
    return pl.pallas_call(
        functools.partial(_post_body, n_parts=len(o_parts), final=final),
        grid=(M // TM,),
        in_specs=in_specs,
        out_specs=pl.BlockSpec((TM, D_MODEL), lambda i: (i, 0)),
        out_shape=jax.ShapeDtypeStruct((M, D_MODEL), F32),
        scratch_shapes=[pltpu.VMEM((TM, D_MODEL), BF16)],
        compiler_params=_params(("parallel",)),
        name="post_final" if final else "post",
    )(*args)


def _rel_bucket(dist):
    n = jnp.maximum(dist, 0)
    max_exact = REL_BUCKETS // 2
    nf = jnp.maximum(n, 1).astype(jnp.float32)
    large = max_exact + (jnp.log(nf / max_exact) / math.log(REL_MAX_DISTANCE / max_exact)
                         * (REL_BUCKETS - max_exact)).astype(jnp.int32)
    large = jnp.minimum(large, REL_BUCKETS - 1)
    return jnp.where(n < max_exact, n, large)


def _bias_tables(rel_bias, S):
    n_heads = rel_bias.shape[1]
    table = rel_bias.T * LOG2E

    def bias_of(dist):
        bkt = _rel_bucket(jnp.asarray(dist))[None, :]
        out = jnp.zeros((n_heads, len(dist)), F32)
        for b in range(REL_BUCKETS):
            out = jnp.where(bkt == b, table[:, b:b + 1], out)
        return jnp.where(jnp.asarray(dist)[None, :] >= 0, out, NEG_INF)

    d = (np.arange(NE_BIAS)[:, None, None] * TQ + np.arange(TQ)[None, None, :] - np.arange(TK)[None, :, None])
    vals = bias_of(d.reshape(-1)).reshape((n_heads,) + d.shape)
    far = table[:, REL_BUCKETS - 1][:, None, None, None]
    tile = jnp.where(d >= 0, vals - far, NEG_INF)
    tile = jnp.concatenate([tile, jnp.zeros_like(tile[:, :1])], axis=1)
    dw = d[:NE_WIN]
    twin = jnp.where((dw >= 0) & (dw < NSA_WINDOW), vals[MOBA_HEADS:, :NE_WIN], NEG_INF)
    n_c = S // NSA_CMP_STRIDE
    u = np.arange(2 * n_c)[:, None]
    dc = np.arange(TQ)[None, :] + NSA_CMP_STRIDE * (u - (n_c - 1)) - (NSA_CMP_BLOCK - 1)
    fcmp = bias_of(dc.reshape(-1))[MOBA_HEADS:].reshape(-1, 2 * n_c, TQ)
    return tile, twin, fcmp


def _selection_constants(S):
    n_c = S // NSA_CMP_STRIDE
    n_cmp = (S - NSA_CMP_BLOCK) // NSA_CMP_STRIDE + 1
    n_sb = S // NSA_SLC_BLOCK
    ci = np.arange(n_c)[None, :] * NSA_CMP_STRIDE
    sj = np.arange(n_sb)[:, None] * NSA_SLC_BLOCK
    ovl = (ci < sj + NSA_SLC_BLOCK) & (ci + NSA_CMP_BLOCK > sj) & (np.arange(n_c)[None, :] < n_cmp)
    ovl = ovl[:, ::-1]
    return jnp.asarray(ovl, BF16)


def _causal_tiles():
    e = np.arange(QPK)[:, None, None]
    d = e * TQ + np.arange(TQ)[None, None, :] - np.arange(TK)[None, :, None]
    return jnp.asarray(np.where(d >= 0, 0.0, NEG_INF), F32)


def kernel(x, rel_bias, mix_norm, mlp_norm, even_w_in, even_w_out, cmp_pos_k, cmp_pos_v, cmp_k_w1, cmp_k_w2,
           cmp_v_w1, cmp_v_w2, odd_w_in, odd_b_forget, odd_w_out, mlp_w1, mlp_w2, final_norm):
    B, S, D = x.shape
    assert D == D_MODEL and S % TM == 0
    G, J = NSA_KV_GROUPS, NSA_HPG
    h = x.reshape(B * S, D)

    offs = np.cumsum((MOBA_W, MOBA_W, MOBA_W, NSA_W) + (NSA_KV_W,) * 6)
    mq_w, mk_w, mv_w, nq_w, kc_w, vc_w, ksl_w, vsl_w, kwn_w, vwn_w, gz_w = jnp.split(even_w_in[0], offs, axis=1)
    wrm = jnp.concatenate([mk_w, kc_w, vc_w, ksl_w, kwn_w], axis=1).astype(BF16)
    col_kcvc, col_ksl, col_kwn = MOBA_W, MOBA_W + 2 * NSA_KV_W, MOBA_W + 3 * NSA_KV_W
    qs = SCALE * LOG2E
    wfm = jnp.concatenate([mq_w * qs, mv_w, nq_w * qs, vsl_w, vwn_w], axis=1).T.astype(BF16)
    row_nq, row_vsl, row_vwn = 2 * MOBA_W, 2 * MOBA_W + NSA_W, 2 * MOBA_W + NSA_W + NSA_KV_W
    gzw = gz_w.T.reshape(G, J, 3, D).transpose(0, 2, 1, 3).reshape(G, 3 * J, D)
    gzw = jnp.pad(gzw, ((0, 0), (0, GZ_ROWS - 3 * J), (0, 0))).reshape(G * GZ_ROWS, D).astype(BF16)

    rm, fm, gz = _inproj0(h, mix_norm[0][None, :], wrm, wfm, gzw, B, S)
    rm = rm.reshape(B, S, -1)

    tile, twin, tcmp = _bias_tables(rel_bias, S)
    ovl = _selection_constants(S)

    o_moba = _moba(fm, rm, tile, B, S)

    n_c = S // NSA_CMP_STRIDE
    r = rm[:, :, col_kcvc:col_kcvc + 2 * NSA_KV_W].reshape(B, n_c, NSA_CMP_STRIDE, 2 * G, HEAD_DIM)
    r = r.transpose(0, 3, 1, 2, 4).reshape(B, 2 * G, n_c, NSA_CMP_STRIDE * HEAD_DIM)
    pos = jnp.stack([cmp_pos_k[0].reshape(1, -1), cmp_pos_v[0].reshape(1, -1)])
    pos = jnp.pad(pos, ((0, 0), (0, 7), (0, 0))).astype(BF16)
    w1c = jnp.stack([cmp_k_w1[0], cmp_v_w1[0]]).astype(BF16)
    kc, vct = _compress(r, pos, w1c, cmp_k_w2[0].astype(BF16), cmp_v_w2[0].T.astype(BF16), B, n_c)

    o_nsa = _nsa(fm, rm, gz, kc, vct, tcmp, ovl, tile, twin, B, S,
                 col_ksl, col_kwn, row_nq, row_vsl, row_vwn)

    h = _post([o_moba, o_nsa], h, even_w_out[0].astype(BF16), mlp_norm[0][None, :],
              mlp_w1[0].astype(BF16), mlp_w2[0].astype(BF16), None, B, S)

    q_w, k_w, v_w, f_w = jnp.split(odd_w_in[0], np.cumsum((FOX_W, FOX_W, FOX_W)), axis=1)
    wfm1 = jnp.concatenate([q_w * qs, v_w], axis=1).T.astype(BF16)
    wk = jnp.pad(k_w.reshape(D, FOX_HEADS, HEAD_DIM), ((0, 0), (0, 0), (0, LANES - HEAD_DIM)))
    wk = wk.reshape(D, FOX_HEADS * LANES).astype(BF16)
    f_w = jnp.pad(f_w, ((0, 0), (0, LANES - FOX_HEADS)))
    f_hi = f_w.astype(BF16)
    wf = jnp.stack([f_hi, (f_w - f_hi.astype(F32)).astype(BF16)])
    bf = jnp.pad(odd_b_forget[0], (0, LANES - FOX_HEADS))[None, :]
    tri = jnp.asarray(np.tril(np.ones((TM, TM))), BF16)
    place = np.zeros((3, LANES, FOX_HEADS * LANES), np.float32)
    for term in range(3):
        place[term, np.arange(FOX_HEADS), np.arange(FOX_HEADS) * LANES + HEAD_DIM + term] = -1.0
    fm1, ka = _inproj1(h, mix_norm[1][None, :], wfm1, wk, wf, bf, tri, jnp.asarray(place, BF16), B, S)
    o_fox = _fox(fm1, ka.reshape(B, S, -1), _causal_tiles(), B, S)

    h = _post([o_fox], h, odd_w_out[0].astype(BF16), mlp_norm[1][None, :],
              mlp_w1[1].astype(BF16), mlp_w2[1].astype(BF16), final_norm[None, :], B, S)
    return h.reshape(B, S, D)
```

```python
import functools
import math

import numpy as np
import jax
import jax.numpy as jnp
from jax import lax
from jax.experimental import pallas as pl
from jax.experimental.pallas import tpu as pltpu

D_MODEL = 1024
HEAD_DIM = 64
MOBA_HEADS = 8
MOBA_BLOCK = 256
MOBA_TOPK = 3
NSA_HEADS = 8
NSA_KV_GROUPS = 2
NSA_HPG = NSA_HEADS // NSA_KV_GROUPS
NSA_CMP_BLOCK = 32
NSA_CMP_STRIDE = 16
NSA_CMP_HIDDEN = 256
NSA_SLC_BLOCK = 64
NSA_TOPN = 16
NSA_WINDOW = 512
NSA_FORCE_SCORE = 1e6
FOX_HEADS = 16
D_FF = 4 * D_MODEL
REL_BUCKETS = 32
REL_MAX_DISTANCE = 1024
RMS_EPS = 1e-5
NEG_INF = -1e30
SCALE = HEAD_DIM ** -0.5

MOBA_W = MOBA_HEADS * HEAD_DIM
NSA_W = NSA_HEADS * HEAD_DIM
NSA_KV_W = NSA_KV_GROUPS * HEAD_DIM
FOX_W = FOX_HEADS * HEAD_DIM

LANES = 128
TQ = 256
TK = 256
QPK = TK // TQ
TM = 512
CH = 256
FF_CH = 512
VMEM_LIMIT = 56 * 1024 * 1024
NE_BIAS = -(-(REL_MAX_DISTANCE + TK - 1) // TQ)
NE_WIN = -(-(NSA_WINDOW + TK - 1) // TQ)
GZ_ROWS = 16
HPS = 8
FOX_HPS = 16
SPT = TK // NSA_SLC_BLOCK
ACC_ROWS = HEAD_DIM + 16
ROW_CHUNK = 64
LOG2E = math.log2(math.e)

assert TK % TQ == 0 and MOBA_BLOCK == TK and TK % NSA_SLC_BLOCK == 0

F32 = jnp.float32
BF16 = jnp.bfloat16


def _dot(a, b):
    return jnp.dot(a, b, preferred_element_type=F32)


def _dot_nt(a, b):
    return lax.dot_general(a, b, (((1,), (1,)), ((), ())), preferred_element_type=F32)


def _dot_tn(a, b):
    return lax.dot_general(a, b, (((0,), (0,)), ((), ())), preferred_element_type=F32)


def _rmsnorm(x, g):
    ms = jnp.mean(x * x, axis=-1, keepdims=True)
    return x * lax.rsqrt(ms + RMS_EPS) * g


def _split3(x):
    a = x.astype(BF16)
    r = x - a.astype(F32)
    b = r.astype(BF16)
    c = (r - b.astype(F32)).astype(BF16)
    return a, b, c


def _const_spec(shape):
    nd = len(shape)
    return pl.BlockSpec(shape, lambda *_: (0,) * nd, pipeline_mode=pl.Buffered(1))


def _params(sem):
    return pltpu.CompilerParams(dimension_semantics=sem, vmem_limit_bytes=VMEM_LIMIT)


def _attend(n_tiles, tile_of, qk_fn, fix_fn, v_fn, s_scr, acc_scr, n_heads, first_fix=None):
    def put_scores(h, n, fix):
        s = qk_fn(h, n)
        s_scr[h] = s if fix is None else fix(h, n, s)

    for h in range(n_heads):
        put_scores(h, tile_of(0), fix_fn if first_fix is None else first_fix)
    acc_scr[...] = jnp.zeros_like(acc_scr)
    last = n_tiles - 1
    ones = jnp.ones((ACC_ROWS - HEAD_DIM, TK), BF16)
    chunks = range(0, TK, ROW_CHUNK)

    def body(i, ms):
        n = tile_of(i)
        n_next = tile_of(jnp.minimum(i + 1, last))
        out = []
        for h in range(n_heads):
            mx = s_scr[h, 0:ROW_CHUNK, :]
            for r0 in chunks[1:]:
                mx = jnp.maximum(mx, s_scr[h, r0:r0 + ROW_CHUNK, :])
            m_new = jnp.maximum(ms[h], jnp.max(mx, axis=0, keepdims=True))
            alpha = jnp.exp2(ms[h] - m_new)
            p = jnp.concatenate([jnp.exp2(s_scr[h, r0:r0 + ROW_CHUNK, :] - m_new).astype(BF16) for r0 in chunks],
                                axis=0)
            va = jnp.concatenate([v_fn(h, n), ones], axis=0)
            acc_scr[h] = alpha * acc_scr[h] + _dot(va, p)
            out.append(m_new)
            put_scores(h, n_next, fix_fn)
        return tuple(out)

    lax.fori_loop(0, n_tiles, body, tuple(jnp.full((1, TQ), NEG_INF, F32) for _ in range(n_heads)))
    return [acc_scr[h, :HEAD_DIM, :] / acc_scr[h, HEAD_DIM:HEAD_DIM + 1, :] for h in range(n_heads)]


def _rank_select(val, idx, n_rows, k):
    cnt = jnp.zeros(val.shape, F32)
    for m in range(n_rows):
        vm = val[m:m + 1, :]
        beats = (vm > val) | ((vm == val) & (idx > m))
        cnt = cnt + jnp.where(beats, 1.0, 0.0)
    return cnt < k


def _inproj0_body(x_ref, g_ref, wrm_ref, wfm_ref, wgz_ref, rm_ref, fm_ref, gz_ref):
    xn = _rmsnorm(x_ref[...], g_ref[...]).astype(BF16)
    for c0 in range(0, rm_ref.shape[-1], CH):
        rm_ref[:, c0:c0 + CH] = _dot(xn, wrm_ref[:, c0:c0 + CH]).astype(BF16)
    for r0 in range(0, fm_ref.shape[2], CH):
        res = _dot_nt(wfm_ref[r0:r0 + CH, :], xn).astype(BF16)
        for t in range(TM // TK):
            fm_ref[0, t, r0:r0 + CH, :] = res[:, t * TK:(t + 1) * TK]
    gz_ref[0] = _dot_nt(wgz_ref[...], xn)


def _inproj0(x2, g, wrm, wfm, wgz, B, S):
    M = B * S
    nst = S // TM
    n_rm, n_fm, n_gz = wrm.shape[1], wfm.shape[0], wgz.shape[0]
    return pl.pallas_call(
        _inproj0_body,
        grid=(M // TM,),
        in_specs=[
            pl.BlockSpec((TM, D_MODEL), lambda i: (i, 0)),
            _const_spec((1, D_MODEL)),
            _const_spec((D_MODEL, n_rm)),
            _const_spec((n_fm, D_MODEL)),
            _const_spec((n_gz, D_MODEL)),
        ],
        out_specs=[
            pl.BlockSpec((TM, n_rm), lambda i: (i, 0)),
            pl.BlockSpec((1, TM // TK, n_fm, TK), lambda i: (i // nst, i % nst, 0, 0)),
            pl.BlockSpec((1, n_gz, TM), lambda i: (i // nst, 0, i % nst)),
        ],
        out_shape=[
            jax.ShapeDtypeStruct((M, n_rm), BF16),
            jax.ShapeDtypeStruct((B, S // TK, n_fm, TK), BF16),
            jax.ShapeDtypeStruct((B, n_gz, S), F32),
        ],
        compiler_params=_params(("parallel",)),
        name="inproj0",
    )(x2, g, wrm, wfm, wgz)


def _compress_body(rk_ref, rv_ref, pos_ref, w1_ref, w2k_ref, w2vt_ref, flip_ref, kc_ref, vct_ref):
    half = NSA_CMP_STRIDE * HEAD_DIM

    def hidden(r_ref, s):
        r = r_ref[0, 0]
        a = _dot(r, w1_ref[s, :half, :])
        b = _dot(r, w1_ref[s, half:, :])
        nxt = pltpu.roll(b, b.shape[0] - 1, axis=0)
        posb = _dot(pos_ref[s], w1_ref[s])[0:1]
        pre = a + nxt + posb
        act = (pre * jax.nn.sigmoid(pre)).astype(BF16)
        return _dot(flip_ref[...], act).astype(BF16)

    kc_ref[0, 0] = _dot(hidden(rk_ref, 0), w2k_ref[...]).astype(BF16)
    vct_ref[0, 0] = _dot_nt(w2vt_ref[...], hidden(rv_ref, 1)).astype(BF16)


def _compress(r, pos, w1, w2k, w2vt, B, NC):
    G = NSA_KV_GROUPS
    half = NSA_CMP_STRIDE * HEAD_DIM
    return pl.pallas_call(
        _compress_body,
        grid=(B, G),
        in_specs=[
            pl.BlockSpec((1, 1, NC, half), lambda b, g: (b, g, 0, 0)),
            pl.BlockSpec((1, 1, NC, half), lambda b, g: (b, G + g, 0, 0)),
            _const_spec(pos.shape),
            _const_spec(w1.shape),
            _const_spec(w2k.shape),
            _const_spec(w2vt.shape),
            _const_spec((NC, NC)),
        ],
        out_specs=[
            pl.BlockSpec((1, 1, NC, HEAD_DIM), lambda b, g: (b, g, 0, 0)),
            pl.BlockSpec((1, 1, HEAD_DIM, NC), lambda b, g: (b, g, 0, 0)),
        ],
        out_shape=[
            jax.ShapeDtypeStruct((B, G, NC, HEAD_DIM), BF16),
            jax.ShapeDtypeStruct((B, G, HEAD_DIM, NC), BF16),
        ],
        compiler_params=_params(("parallel", "parallel")),
        name="nsa_compress",
    )(r, r, pos, w1, w2k, w2vt, jnp.asarray(np.eye(NC)[::-1], BF16))


def _moba_body(q_ref, k_ref, v_ref, t_ref, o_ref, kmean_ref, mask_ref, s_scr, acc_scr, *, n_mb, topk):
    c = pl.program_id(2)
    blk = c // QPK

    @pl.when(c == 0)
    def _():
        kmean_ref[...] = jnp.zeros_like(kmean_ref)
        for n in range(n_mb):
            kblk = k_ref[0, n * TK:(n + 1) * TK, :].astype(F32)
            kmean_ref[n:n + 1, :] = jnp.mean(kblk, axis=0, keepdims=True)

    q = q_ref[0, 0]
    rowi = lax.broadcasted_iota(jnp.int32, (LANES, TQ), 0)
    nidx = lax.broadcasted_iota(jnp.int32, (kmean_ref.shape[0], TQ), 0)
    km = _split3(kmean_ref[...])
    qpads = []
    for h in range(HPS):
        lo = (h // 2) * LANES
        qpair = q[lo:lo + LANES, :]
        qh = jnp.where(rowi // HEAD_DIM == h % 2, qpair, jnp.zeros_like(qpair))
        route = sum(_dot(part[:, lo:lo + LANES], qh) for part in km)
        route = jnp.where(nidx < blk, route, NEG_INF)
        sel = _rank_select(route, nidx, n_mb, topk) & (nidx < blk)
        mask_ref[h] = jnp.where(sel | (nidx == blk), 0.0, NEG_INF)
        qpads.append(qh)

    def qk_fn(h, n):
        rows = pl.ds(pl.multiple_of(n * TK, TK), TK)
        return _dot(k_ref[0, rows, (h // 2) * LANES:(h // 2 + 1) * LANES], qpads[h])

    def fix_fn(h, n, s):
        return s + mask_ref[h, pl.ds(n, 1), :] + t_ref[h, jnp.minimum(c - QPK * n, NE_BIAS)]

    def v_fn(h, n):
        return v_ref[0, n, h * HEAD_DIM:(h + 1) * HEAD_DIM, :]

    outs = _attend(blk + 1, lambda i: blk - i, qk_fn, fix_fn, v_fn, s_scr, acc_scr, HPS)
    o_ref[0] = jnp.concatenate(outs, axis=0).astype(BF16)


def _moba(fm, rm, tab, B, S):
    n_mb = S // MOBA_BLOCK
    n_pad = -(-n_mb // 16) * 16
    ne = tab.shape[1]
    rows = HPS * HEAD_DIM
    body = functools.partial(_moba_body, n_mb=n_mb, topk=min(MOBA_TOPK, n_mb))
    return pl.pallas_call(
        body,
        grid=(B, MOBA_HEADS // HPS, S // TQ),
        in_specs=[
            pl.BlockSpec((1, 1, rows, TQ), lambda b, p, c: (b, c // QPK, p, c % QPK)),
            pl.BlockSpec((1, S, rows), lambda b, p, c: (b, 0, p)),
            pl.BlockSpec((1, S // TK, rows, TK), lambda b, p, c: (b, 0, MOBA_HEADS // HPS + p, 0)),
            pl.BlockSpec((HPS, ne, TK, TQ), lambda b, p, c: (p, 0, 0, 0), pipeline_mode=pl.Buffered(1)),
        ],
        out_specs=pl.BlockSpec((1, rows, TQ), lambda b, p, c: (b, p, c)),
        out_shape=jax.ShapeDtypeStruct((B, MOBA_W, S), BF16),
        scratch_shapes=[pltpu.VMEM((n_pad, rows), F32), pltpu.VMEM((HPS, n_pad, TQ), F32),
                        pltpu.VMEM((HPS, TK, TQ), F32), pltpu.VMEM((HPS, ACC_ROWS, TQ), F32)],
        compiler_params=_params(("parallel", "parallel", "arbitrary")),
        name="moba_attn",
    )(fm, rm, fm, tab)


def _nsa_body(q_ref, kc_ref, vct_ref, fc_ref, ovl_ref, ksl_ref, vsl_ref, kwn_ref, vwn_ref,
              tslc_ref, twin_ref, gz_ref, o_ref, s_scr, acc_scr, s_win, acc_win, sel_ref, *, n_sb, n_sel):
    c = pl.program_id(1)
    blk = c // QPK
    G, J = NSA_KV_GROUPS, NSA_HPG
    H = G * J

    q = q_ref[0, 0]
    qs = [q[h * HEAD_DIM:(h + 1) * HEAD_DIM, :] for h in range(H)]
    zero = jnp.zeros((HEAD_DIM, TQ), BF16)
    qpads = [jnp.concatenate([zero] * (h // J) + [qs[h]] + [zero] * (G - 1 - h // J), axis=0) for h in range(H)]

    c0 = pl.multiple_of(c * (TQ // NSA_CMP_STRIDE), TQ // NSA_CMP_STRIDE)
    o_cmp = []
    for g in range(G):
        kc = kc_ref[0, g]
        vct = vct_ref[0, g]
        psum = jnp.zeros((kc.shape[0], TQ), F32)
        for h in range(g * J, (g + 1) * J):
            s = _dot(kc, qs[h]) + fc_ref[h, pl.ds(c0, kc.shape[0]), :]
            m = jnp.max(s, axis=0, keepdims=True)
            p = jnp.exp2(s - m)
            l = jnp.sum(p, axis=0, keepdims=True)
            pn = p * jnp.where(m > 0.5 * NEG_INF, 1.0 / l, 0.0)
            o_cmp.append(_dot(vct, pn.astype(BF16)))
            psum = psum + pn

        ph = psum.astype(BF16)
        plo = (psum - ph.astype(F32)).astype(BF16)
        imp = _dot(ovl_ref[...], ph) + _dot(ovl_ref[...], plo)
        jb = lax.broadcasted_iota(jnp.int32, imp.shape, 0)
        t = c * TQ + lax.broadcasted_iota(jnp.int32, imp.shape, 1)
        sb = t // NSA_SLC_BLOCK
        forced = (jb == 0) | (jb == sb) | (jb == sb - 1)
        allowed = jb <= sb
        val = jnp.where(forced, imp + NSA_FORCE_SCORE, jnp.where(allowed, imp, NEG_INF))
        sel = _rank_select(val, jb, n_sb, n_sel) & allowed
        selb = jnp.where(sel, 0.0, NEG_INF)
        pad = jnp.zeros((sel_ref.shape[2] - SPT, TQ), F32)
        for n in range(n_sb // SPT):
            sel_ref[g, n] = jnp.concatenate([selb[n * SPT:(n + 1) * SPT, :], pad], axis=0)

    def slc_qk(h, n):
        return _dot(ksl_ref[0, pl.ds(pl.multiple_of(n * TK, TK), TK), :], qpads[h])

    def slc_fix(h, n, s):
        rows = sel_ref[h // J, n]
        mask = jnp.concatenate([jnp.broadcast_to(rows[b:b + 1, :], (NSA_SLC_BLOCK, TQ)) for b in range(SPT)], axis=0)
        return s + mask + tslc_ref[h, jnp.minimum(c - QPK * n, NE_BIAS)]

    def slc_v(h, n):
        return vsl_ref[0, n, (h // J) * HEAD_DIM:(h // J + 1) * HEAD_DIM, :]

    o_slc = _attend(blk + 1, lambda i: blk - i, slc_qk, slc_fix, slc_v, s_scr, acc_scr, H)

    def win_qk(h, n):
        return _dot(kwn_ref[0, pl.ds(pl.multiple_of(n * TK, TK), TK), :], qpads[h])

    def win_fix(h, n, s):
        return s + twin_ref[h, c - QPK * n]

    def win_v(h, n):
        return vwn_ref[0, n, (h // J) * HEAD_DIM:(h // J + 1) * HEAD_DIM, :]

    w_lo = jnp.maximum(c - NE_WIN + QPK, 0) // QPK
    o_win = _attend(blk - w_lo + 1, lambda i: blk - i, win_qk, win_fix, win_v, s_win, acc_win, H)

    gate = jax.nn.sigmoid(gz_ref[0])
    outs = []
    for h in range(H):
        r = (h // J) * GZ_ROWS + h % J
        outs.append(gate[r:r + 1, :] * o_cmp[h] + gate[r + J:r + J + 1, :] * o_slc[h]
                    + gate[r + 2 * J:r + 2 * J + 1, :] * o_win[h])
    o_ref[0] = jnp.concatenate(outs, axis=0).astype(BF16)


def _nsa(fm, rm, gz, kc, vct, tcmp, ovl, tslc, twin, B, S, col_ksl, col_kwn, row_q, row_vsl, row_vwn):
    G, J = NSA_KV_GROUPS, NSA_HPG
    NC = kc.shape[2]
    n_sb = S // NSA_SLC_BLOCK
    body = functools.partial(_nsa_body, n_sb=n_sb, n_sel=min(NSA_TOPN, n_sb))
    H = G * J
    kvrows = G * HEAD_DIM
    one = pl.Buffered(1)
    return pl.pallas_call(
        body,
        grid=(B, S // TQ),
        in_specs=[
            pl.BlockSpec((1, 1, NSA_W, TQ), lambda b, c: (b, c // QPK, row_q // NSA_W, c % QPK)),
            pl.BlockSpec((1, G, NC, HEAD_DIM), lambda b, c: (b, 0, 0, 0)),
            pl.BlockSpec((1, G, HEAD_DIM, NC), lambda b, c: (b, 0, 0, 0)),
            _const_spec(tcmp.shape),
            _const_spec(ovl.shape),
            pl.BlockSpec((1, S, LANES), lambda b, c: (b, 0, col_ksl // LANES)),
            pl.BlockSpec((1, S // TK, kvrows, TK), lambda b, c: (b, 0, row_vsl // kvrows, 0)),
            pl.BlockSpec((1, S, LANES), lambda b, c: (b, 0, col_kwn // LANES)),
            pl.BlockSpec((1, S // TK, kvrows, TK), lambda b, c: (b, 0, row_vwn // kvrows, 0)),
            pl.BlockSpec((H, tslc.shape[1], TK, TQ), lambda b, c: (MOBA_HEADS // H, 0, 0, 0), pipeline_mode=one),
            _const_spec(twin.shape),
            pl.BlockSpec((1, G * GZ_ROWS, TQ), lambda b, c: (b, 0, c)),
        ],
        out_specs=pl.BlockSpec((1, NSA_W, TQ), lambda b, c: (b, 0, c)),
        out_shape=jax.ShapeDtypeStruct((B, NSA_W, S), BF16),
        scratch_shapes=[pltpu.VMEM((H, TK, TQ), F32), pltpu.VMEM((H, ACC_ROWS, TQ), F32),
                        pltpu.VMEM((H, TK, TQ), F32), pltpu.VMEM((H, ACC_ROWS, TQ), F32),
                        pltpu.VMEM((G, S // TK, 8, TQ), F32)],
        compiler_params=_params(("parallel", "arbitrary")),
        name="nsa_attn",
    )(fm, kc, vct, tcmp, ovl, rm, fm, rm, fm, tslc, twin, gz)


def _inproj1_body(x_ref, g_ref, wfm_ref, wk_ref, wf_ref, bf_ref, tri_ref, place_ref, fm_ref, ka_ref, carry_ref,
                  *, nst):
    i = pl.program_id(0)

    @pl.when(i % nst == 0)
    def _():
        carry_ref[...] = jnp.zeros_like(carry_ref)

    xf = _rmsnorm(x_ref[...], g_ref[...])
    xn = xf.astype(BF16)
    xlo = (xf - xn.astype(F32)).astype(BF16)
    for r0 in range(0, fm_ref.shape[2], CH):
        res = _dot_nt(wfm_ref[r0:r0 + CH, :], xn).astype(BF16)
        for t in range(TM // TK):
            fm_ref[0, t, r0:r0 + CH, :] = res[:, t * TK:(t + 1) * TK]

    fz = _dot(xn, wf_ref[0]) + _dot(xlo, wf_ref[0]) + _dot(xn, wf_ref[1]) + bf_ref[...]
    logf = jnp.minimum(fz, 0.0) - jnp.log(1.0 + jnp.exp(-jnp.abs(fz)))
    tri = tri_ref[...]
    h1, h2, h3 = _split3(logf)
    cum = _dot(tri, h1) + _dot(tri, h2) + _dot(tri, h3) + carry_ref[0:1, :]
    carry_ref[...] = jnp.broadcast_to(cum[TM - 1:TM, :], carry_ref.shape)
    c1, c2, c3 = _split3(cum * LOG2E)
    for c0 in range(0, ka_ref.shape[-1], FF_CH):
        ka = (_dot(xn, wk_ref[:, c0:c0 + FF_CH]) + _dot(c1, place_ref[0, :, c0:c0 + FF_CH])
              + _dot(c2, place_ref[1, :, c0:c0 + FF_CH]) + _dot(c3, place_ref[2, :, c0:c0 + FF_CH]))
        ka_ref[:, c0:c0 + FF_CH] = ka.astype(BF16)


def _inproj1(x2, g, wfm, wk, wf, bf, tri, place, B, S):
    M = B * S
    nst = S // TM
    n_fm, n_ka = wfm.shape[0], wk.shape[1]
    return pl.pallas_call(
        functools.partial(_inproj1_body, nst=nst),
        grid=(M // TM,),
        in_specs=[
            pl.BlockSpec((TM, D_MODEL), lambda i: (i, 0)),
            _const_spec((1, D_MODEL)),
            _const_spec(wfm.shape),
            _const_spec(wk.shape),
            _const_spec(wf.shape),
            _const_spec(bf.shape),
            _const_spec(tri.shape),
            _const_spec(place.shape),
        ],
        out_specs=[
            pl.BlockSpec((1, TM // TK, n_fm, TK), lambda i: (i // nst, i % nst, 0, 0)),
            pl.BlockSpec((TM, n_ka), lambda i: (i, 0)),
        ],
        out_shape=[
            jax.ShapeDtypeStruct((B, S // TK, n_fm, TK), BF16),
            jax.ShapeDtypeStruct((M, n_ka), BF16),
        ],
        scratch_shapes=[pltpu.VMEM((8, LANES), F32)],
        compiler_params=_params(("arbitrary",)),
        name="inproj1",
    )(x2, g, wfm, wk, wf, bf, tri, place)


def _fox_body(q_ref, k_ref, v_ref, cm_ref, o_ref, s_scr, acc_scr):
    c = pl.program_id(2)
    blk = c // QPK
    q = q_ref[0, 0]
    ones = jnp.ones((LANES - HEAD_DIM, TQ), BF16)
    qas = [jnp.concatenate([q[h * HEAD_DIM:(h + 1) * HEAD_DIM, :], ones], axis=0) for h in range(FOX_HPS)]

    def qk_fn(h, n):
        rows = pl.ds(pl.multiple_of(n * TK, TK), TK)
        return _dot(k_ref[0, rows, h * LANES:(h + 1) * LANES], qas[h])

    causal = cm_ref[c % QPK]
    outs = _attend(blk + 1, lambda i: blk - i, qk_fn, None,
                   lambda h, n: v_ref[0, n, h * HEAD_DIM:(h + 1) * HEAD_DIM, :], s_scr, acc_scr, FOX_HPS,
                   first_fix=lambda h, n, s: s + causal)
    o_ref[0] = jnp.concatenate(outs, axis=0).astype(BF16)


def _fox(fm, ka, cmask, B, S):
    rows = FOX_HPS * HEAD_DIM
    return pl.pallas_call(
        _fox_body,
        grid=(B, FOX_HEADS // FOX_HPS, S // TQ),
        in_specs=[
            pl.BlockSpec((1, 1, rows, TQ), lambda b, h, c: (b, c // QPK, h, c % QPK)),
            pl.BlockSpec((1, S, FOX_HPS * LANES), lambda b, h, c: (b, 0, h), pipeline_mode=pl.Buffered(1)),
            pl.BlockSpec((1, S // TK, rows, TK), lambda b, h, c: (b, 0, FOX_HEADS // FOX_HPS + h, 0),
                         pipeline_mode=pl.Buffered(1)),
            _const_spec(cmask.shape),
        ],
        out_specs=pl.BlockSpec((1, rows, TQ), lambda b, h, c: (b, h, c)),
        out_shape=jax.ShapeDtypeStruct((B, FOX_W, S), BF16),
        scratch_shapes=[pltpu.VMEM((FOX_HPS, TK, TQ), F32), pltpu.VMEM((FOX_HPS, ACC_ROWS, TQ), F32)],
        compiler_params=_params(("parallel", "parallel", "arbitrary")),
        name="fox_attn",
    )(fm, ka, fm, cmask)


def _post_body(*refs, n_parts, final):
    o_refs = refs[:n_parts]
    h_ref, wo_ref, g_ref, w1_ref, w2_ref = refs[n_parts:n_parts + 5]
    gf_ref = refs[n_parts + 5] if final else None
    out_ref, hn_ref = refs[-2:]
    h1 = h_ref[...]
    r0 = 0
    for o_ref in o_refs:
        nf = o_ref.shape[1]
        h1 = h1 + _dot_tn(o_ref[0], wo_ref[r0:r0 + nf, :])
        r0 += nf
    out_ref[...] = h1
    hn_ref[...] = _rmsnorm(out_ref[...], g_ref[...]).astype(BF16)
    for c0 in range(0, D_FF, FF_CH):
        a = jnp.maximum(_dot(hn_ref[...], w1_ref[:, c0:c0 + FF_CH]), 0.0)
        out_ref[...] += _dot((a * a).astype(BF16), w2_ref[c0:c0 + FF_CH, :])
    if final:
        out_ref[...] = _rmsnorm(out_ref[...], gf_ref[...])


def _post(o_parts, h2, wo, g, w1, w2, gf, B, S):
    M = B * S
    nst = S // TM
    final = gf is not None
    in_specs = [pl.BlockSpec((1, o.shape[1], TM), lambda i: (i // nst, 0, i % nst)) for o in o_parts]
    in_specs += [
        pl.BlockSpec((TM, D_MODEL), lambda i: (i, 0)),
        _const_spec(wo.shape),
        _const_spec((1, D_MODEL)),
        _const_spec(w1.shape),
        _const_spec(w2.shape),
    ]
    args = list(o_parts) + [h2, wo, g, w1, w2]
    if final:
        in_specs.append(_const_spec((1, D_MODEL)))
        args.append(gf)
    return pl.pallas_call(
        functools.partial(_post_body, n_parts=len(o_parts), final=final),
        grid=(M // TM,),
        in_specs=in_specs,
        out_specs=pl.BlockSpec((TM, D_MODEL), lambda i: (i, 0)),
        out_shape=jax.ShapeDtypeStruct((M, D_MODEL), F32),
        scratch_shapes=[pltpu.VMEM((TM, D_MODEL), BF16)],
        compiler_params=_params(("parallel",)),
        name="post_final" if final else "post",
    )(*args)


def _rel_bucket(dist):
    n = jnp.maximum(dist, 0)
    max_exact = REL_BUCKETS // 2
    nf = jnp.maximum(n, 1).astype(jnp.float32)
    large = max_exact + (jnp.log(nf / max_exact) / math.log(REL_MAX_DISTANCE / max_exact)
                         * (REL_BUCKETS - max_exact)).astype(jnp.int32)
    large = jnp.minimum(large, REL_BUCKETS - 1)
    return jnp.where(n < max_exact, n, large)


def _bias_tables(rel_bias, S):
    n_heads = rel_bias.shape[1]
    table = rel_bias.T * LOG2E

    def bias_of(dist):
        tab = table.reshape((n_heads, REL_BUCKETS) + (1,) * dist.ndim)
        bkt = _rel_bucket(jnp.asarray(dist))[None]
        out = jnp.zeros((n_heads,) + dist.shape, F32)
        for b in range(REL_BUCKETS):
            out = jnp.where(bkt == b, tab[:, b], out)
        return jnp.where(jnp.asarray(dist)[None] >= 0, out, NEG_INF)

    d = (np.arange(NE_BIAS)[:, None, None] * TQ + np.arange(TQ)[None, None, :] - np.arange(TK)[None, :, None])
    vals = bias_of(d)
    far = table[:, REL_BUCKETS - 1][:, None, None, None]
    tile = jnp.where(d >= 0, vals - far, NEG_INF)
    tile = jnp.concatenate([tile, jnp.zeros_like(tile[:, :1])], axis=1)
    dw = d[:NE_WIN]
    twin = jnp.where((dw >= 0) & (dw < NSA_WINDOW), vals[MOBA_HEADS:, :NE_WIN], NEG_INF)
    n_c = S // NSA_CMP_STRIDE
    u = np.arange(2 * n_c)[:, None]
    dc = np.arange(TQ)[None, :] + NSA_CMP_STRIDE * (u - (n_c - 1)) - (NSA_CMP_BLOCK - 1)
    fcmp = bias_of(dc)[MOBA_HEADS:]
    return tile, twin, fcmp


def _selection_constants(S):
    n_c = S // NSA_CMP_STRIDE
    n_cmp = (S - NSA_CMP_BLOCK) // NSA_CMP_STRIDE + 1
    n_sb = S // NSA_SLC_BLOCK
    ci = np.arange(n_c)[None, :] * NSA_CMP_STRIDE
    sj = np.arange(n_sb)[:, None] * NSA_SLC_BLOCK
    ovl = (ci < sj + NSA_SLC_BLOCK) & (ci + NSA_CMP_BLOCK > sj) & (np.arange(n_c)[None, :] < n_cmp)
    ovl = ovl[:, ::-1]
    return jnp.asarray(ovl, BF16)


def _causal_tiles():
    e = np.arange(QPK)[:, None, None]
    d = e * TQ + np.arange(TQ)[None, None, :] - np.arange(TK)[None, :, None]
    return jnp.asarray(np.where(d >= 0, 0.0, NEG_INF), F32)


def kernel(x, rel_bias, mix_norm, mlp_norm, even_w_in, even_w_out, cmp_pos_k, cmp_pos_v, cmp_k_w1, cmp_k_w2,
           cmp_v_w1, cmp_v_w2, odd_w_in, odd_b_forget, odd_w_out, mlp_w1, mlp_w2, final_norm):
    B, S, D = x.shape
    assert D == D_MODEL and S % TM == 0
    G, J = NSA_KV_GROUPS, NSA_HPG
    h = x.reshape(B * S, D)

    offs = np.cumsum((MOBA_W, MOBA_W, MOBA_W, NSA_W) + (NSA_KV_W,) * 6)
    mq_w, mk_w, mv_w, nq_w, kc_w, vc_w, ksl_w, vsl_w, kwn_w, vwn_w, gz_w = jnp.split(even_w_in[0], offs, axis=1)
    wrm = jnp.concatenate([mk_w, kc_w, vc_w, ksl_w, kwn_w], axis=1).astype(BF16)
    col_kcvc, col_ksl, col_kwn = MOBA_W, MOBA_W + 2 * NSA_KV_W, MOBA_W + 3 * NSA_KV_W
    qs = SCALE * LOG2E
    wfm = jnp.concatenate([mq_w * qs, mv_w, nq_w * qs, vsl_w, vwn_w], axis=1).T.astype(BF16)
    row_nq, row_vsl, row_vwn = 2 * MOBA_W, 2 * MOBA_W + NSA_W, 2 * MOBA_W + NSA_W + NSA_KV_W
    gzw = gz_w.T.reshape(G, J, 3, D).transpose(0, 2, 1, 3).reshape(G, 3 * J, D)
    gzw = jnp.pad(gzw, ((0, 0), (0, GZ_ROWS - 3 * J), (0, 0))).reshape(G * GZ_ROWS, D).astype(BF16)

    rm, fm, gz = _inproj0(h, mix_norm[0][None, :], wrm, wfm, gzw, B, S)
    rm = rm.reshape(B, S, -1)

    tile, twin, tcmp = _bias_tables(rel_bias, S)
    ovl = _selection_constants(S)

    o_moba = _moba(fm, rm, tile, B, S)

    n_c = S // NSA_CMP_STRIDE
    r = rm[:, :, col_kcvc:col_kcvc + 2 * NSA_KV_W].reshape(B, n_c, NSA_CMP_STRIDE, 2 * G, HEAD_DIM)
    r = r.transpose(0, 3, 1, 2, 4).reshape(B, 2 * G, n_c, NSA_CMP_STRIDE * HEAD_DIM)
    pos = jnp.stack([cmp_pos_k[0].reshape(1, -1), cmp_pos_v[0].reshape(1, -1)])
    pos = jnp.pad(pos, ((0, 0), (0, 7), (0, 0))).astype(BF16)
    w1c = jnp.stack([cmp_k_w1[0], cmp_v_w1[0]]).astype(BF16)
    kc, vct = _compress(r, pos, w1c, cmp_k_w2[0].astype(BF16), cmp_v_w2[0].T.astype(BF16), B, n_c)

    o_nsa = _nsa(fm, rm, gz, kc, vct, tcmp, ovl, tile, twin, B, S,
                 col_ksl, col_kwn, row_nq, row_vsl, row_vwn)

    h = _post([o_moba, o_nsa], h, even_w_out[0].astype(BF16), mlp_norm[0][None, :],
              mlp_w1[0].astype(BF16), mlp_w2[0].astype(BF16), None, B, S)

    q_w, k_w, v_w, f_w = jnp.split(odd_w_in[0], np.cumsum((FOX_W, FOX_W, FOX_W)), axis=1)
    wfm1 = jnp.concatenate([q_w * qs, v_w], axis=1).T.astype(BF16)
    wk = jnp.pad(k_w.reshape(D, FOX_HEADS, HEAD_DIM), ((0, 0), (0, 0), (0, LANES - HEAD_DIM)))
    wk = wk.reshape(D, FOX_HEADS * LANES).astype(BF16)
    f_w = jnp.pad(f_w, ((0, 0), (0, LANES - FOX_HEADS)))
    f_hi = f_w.astype(BF16)
    wf = jnp.stack([f_hi, (f_w - f_hi.astype(F32)).astype(BF16)])
    bf = jnp.pad(odd_b_forget[0], (0, LANES - FOX_HEADS))[None, :]
    tri = jnp.asarray(np.tril(np.ones((TM, TM))), BF16)
    place = np.zeros((3, LANES, FOX_HEADS * LANES), np.float32)
    for term in range(3):
        place[term, np.arange(FOX_HEADS), np.arange(FOX_HEADS) * LANES + HEAD_DIM + term] = -1.0
    fm1, ka = _inproj1(h, mix_norm[1][None, :], wfm1, wk, wf, bf, tri, jnp.asarray(place, BF16), B, S)
    o_fox = _fox(fm1, ka.reshape(B, S, -1), _causal_tiles(), B, S)

    h = _post([o_fox], h, odd_w_out[0].astype(BF16), mlp_norm[1][None, :],
              mlp_w1[1].astype(BF16), mlp_w2[1].astype(BF16), final_norm[None, :], B, S)
    return h.reshape(B, S, D)
```

```python
import functools
import math

import numpy as np
import jax
import jax.numpy as jnp
from jax import lax
from jax.experimental import pallas as pl
from jax.experimental.pallas import tpu as pltpu

D_MODEL = 1024
HEAD_DIM = 64
MOBA_HEADS = 8
MOBA_BLOCK = 256
MOBA_TOPK = 3
NSA_HEADS = 8
NSA_KV_GROUPS = 2
NSA_HPG = NSA_HEADS // NSA_KV_GROUPS
NSA_CMP_BLOCK = 32
NSA_CMP_STRIDE = 16
NSA_CMP_HIDDEN = 256
NSA_SLC_BLOCK = 64
NSA_TOPN = 16
NSA_WINDOW = 512
NSA_FORCE_SCORE = 1e6
FOX_HEADS = 16
D_FF = 4 * D_MODEL
REL_BUCKETS = 32
REL_MAX_DISTANCE = 1024
RMS_EPS = 1e-5
NEG_INF = -1e30
SCALE = HEAD_DIM ** -0.5

MOBA_W = MOBA_HEADS * HEAD_DIM
NSA_W = NSA_HEADS * HEAD_DIM
NSA_KV_W = NSA_KV_GROUPS * HEAD_DIM
FOX_W = FOX_HEADS * HEAD_DIM

LANES = 128
TQ = 256
TK = 512
TB = 256
QPK = TK // TQ
KSUB = TK // TB
TM = 512
CH = 256
FF_CH = 512
VMEM_LIMIT = 56 * 1024 * 1024
NE_BIAS = -(-(REL_MAX_DISTANCE + TB - 1) // TQ)
NE_WIN = -(-(NSA_WINDOW + TB - 1) // TQ)
GZ_ROWS = 16
HPS = 8
FOX_HPS = 16
SPT = TK // NSA_SLC_BLOCK
ACC_ROWS = HEAD_DIM + 16
ROW_CHUNK = 64
LOG2E = math.log2(math.e)

assert TK % TQ == 0 and TK % TB == 0 and TQ == TB and MOBA_BLOCK == TB and TB % NSA_SLC_BLOCK == 0 and SPT <= 8

F32 = jnp.float32
BF16 = jnp.bfloat16


def _dot(a, b):
    return jnp.dot(a, b, preferred_element_type=F32)


def _dot_nt(a, b):
    return lax.dot_general(a, b, (((1,), (1,)), ((), ())), preferred_element_type=F32)


def _dot_tn(a, b):
    return lax.dot_general(a, b, (((0,), (0,)), ((), ())), preferred_element_type=F32)


def _rmsnorm(x, g):
    ms = jnp.mean(x * x, axis=-1, keepdims=True)
    return x * lax.rsqrt(ms + RMS_EPS) * g


def _split3(x):
    a = x.astype(BF16)
    r = x - a.astype(F32)
    b = r.astype(BF16)
    c = (r - b.astype(F32)).astype(BF16)
    return a, b, c


def _const_spec(shape):
    nd = len(shape)
    return pl.BlockSpec(shape, lambda *_: (0,) * nd, pipeline_mode=pl.Buffered(1))


def _params(sem):
    return pltpu.CompilerParams(dimension_semantics=sem, vmem_limit_bytes=VMEM_LIMIT)


def _attend(n_tiles, tile_of, qk_fn, fix_fn, v_fn, s_scr, acc_scr, n_heads, first_fix=None):
    def put_scores(h, n, fix):
        s = qk_fn(h, n)
        s_scr[h] = s if fix is None else fix(h, n, s)

    for h in range(n_heads):
        put_scores(h, tile_of(0), fix_fn if first_fix is None else first_fix)
    acc_scr[...] = jnp.zeros_like(acc_scr)
    last = n_tiles - 1
    ones = jnp.ones((ACC_ROWS - HEAD_DIM, TK), BF16)
    chunks = range(0, TK, ROW_CHUNK)

    def body(i, ms):
        n = tile_of(i)
        n_next = tile_of(jnp.minimum(i + 1, last))
        out = []
        for h in range(n_heads):
            mx = s_scr[h, 0:ROW_CHUNK, :]
            for r0 in chunks[1:]:
                mx = jnp.maximum(mx, s_scr[h, r0:r0 + ROW_CHUNK, :])
            m_new = jnp.maximum(ms[h], jnp.max(mx, axis=0, keepdims=True))
            alpha = jnp.exp2(ms[h] - m_new)
            p = jnp.concatenate([jnp.exp2(s_scr[h, r0:r0 + ROW_CHUNK, :] - m_new).astype(BF16) for r0 in chunks],
                                axis=0)
            va = jnp.concatenate([v_fn(h, n), ones], axis=0)
            acc_scr[h] = alpha * acc_scr[h] + _dot(va, p)
            out.append(m_new)
            put_scores(h, n_next, fix_fn)
        return tuple(out)

    lax.fori_loop(0, n_tiles, body, tuple(jnp.full((1, TQ), NEG_INF, F32) for _ in range(n_heads)))
    return [acc_scr[h, :HEAD_DIM, :] / acc_scr[h, HEAD_DIM:HEAD_DIM + 1, :] for h in range(n_heads)]


def _rank_select(val, idx, n_rows, k):
    cnt = jnp.zeros(val.shape, F32)
    for m in range(n_rows):
        vm = val[m:m + 1, :]
        beats = (vm > val) | ((vm == val) & (idx > m))
        cnt = cnt + jnp.where(beats, 1.0, 0.0)
    return cnt < k


def _inproj0_body(x_ref, g_ref, wrm_ref, wfm_ref, wgz_ref, rm_ref, fm_ref, gz_ref):
    xn = _rmsnorm(x_ref[...], g_ref[...]).astype(BF16)
    for c0 in range(0, rm_ref.shape[-1], CH):
        rm_ref[:, c0:c0 + CH] = _dot(xn, wrm_ref[:, c0:c0 + CH]).astype(BF16)
    for r0 in range(0, fm_ref.shape[2], CH):
        res = _dot_nt(wfm_ref[r0:r0 + CH, :], xn).astype(BF16)
        for t in range(TM // TK):
            fm_ref[0, t, r0:r0 + CH, :] = res[:, t * TK:(t + 1) * TK]
    gz_ref[0] = _dot_nt(wgz_ref[...], xn)


def _inproj0(x2, g, wrm, wfm, wgz, B, S):
    M = B * S
    nst = S // TM
    n_rm, n_fm, n_gz = wrm.shape[1], wfm.shape[0], wgz.shape[0]
    return pl.pallas_call(
        _inproj0_body,
        grid=(M // TM,),
        in_specs=[
            pl.BlockSpec((TM, D_MODEL), lambda i: (i, 0)),
            _const_spec((1, D_MODEL)),
            _const_spec((D_MODEL, n_rm)),
            _const_spec((n_fm, D_MODEL)),
            _const_spec((n_gz, D_MODEL)),
        ],
        out_specs=[
            pl.BlockSpec((TM, n_rm), lambda i: (i, 0)),
            pl.BlockSpec((1, TM // TK, n_fm, TK), lambda i: (i // nst, i % nst, 0, 0)),
            pl.BlockSpec((1, n_gz, TM), lambda i: (i // nst, 0, i % nst)),
        ],
        out_shape=[
            jax.ShapeDtypeStruct((M, n_rm), BF16),
            jax.ShapeDtypeStruct((B, S // TK, n_fm, TK), BF16),
            jax.ShapeDtypeStruct((B, n_gz, S), F32),
        ],
        compiler_params=_params(("parallel",)),
        name="inproj0",
    )(x2, g, wrm, wfm, wgz)


def _compress_body(rk_ref, rv_ref, pos_ref, w1_ref, w2k_ref, w2vt_ref, flip_ref, kc_ref, vct_ref):
    half = NSA_CMP_STRIDE * HEAD_DIM

    def hidden(r_ref, s):
        r = r_ref[0, 0]
        a = _dot(r, w1_ref[s, :half, :])
        b = _dot(r, w1_ref[s, half:, :])
        nxt = pltpu.roll(b, b.shape[0] - 1, axis=0)
        posb = _dot(pos_ref[s], w1_ref[s])[0:1]
        pre = a + nxt + posb
        act = (pre * jax.nn.sigmoid(pre)).astype(BF16)
        return _dot(flip_ref[...], act).astype(BF16)

    kc_ref[0, 0] = _dot(hidden(rk_ref, 0), w2k_ref[...]).astype(BF16)
    vct_ref[0, 0] = _dot_nt(w2vt_ref[...], hidden(rv_ref, 1)).astype(BF16)


def _compress(r, pos, w1, w2k, w2vt, B, NC):
    G = NSA_KV_GROUPS
    half = NSA_CMP_STRIDE * HEAD_DIM
    return pl.pallas_call(
        _compress_body,
        grid=(B, G),
        in_specs=[
            pl.BlockSpec((1, 1, NC, half), lambda b, g: (b, g, 0, 0)),
            pl.BlockSpec((1, 1, NC, half), lambda b, g: (b, G + g, 0, 0)),
            _const_spec(pos.shape),
            _const_spec(w1.shape),
            _const_spec(w2k.shape),
            _const_spec(w2vt.shape),
            _const_spec((NC, NC)),
        ],
        out_specs=[
            pl.BlockSpec((1, 1, NC, HEAD_DIM), lambda b, g: (b, g, 0, 0)),
            pl.BlockSpec((1, 1, HEAD_DIM, NC), lambda b, g: (b, g, 0, 0)),
        ],
        out_shape=[
            jax.ShapeDtypeStruct((B, G, NC, HEAD_DIM), BF16),
            jax.ShapeDtypeStruct((B, G, HEAD_DIM, NC), BF16),
        ],
        compiler_params=_params(("parallel", "parallel")),
        name="nsa_compress",
    )(r, r, pos, w1, w2k, w2vt, jnp.asarray(np.eye(NC)[::-1], BF16))


def _moba_body(q_ref, k_ref, v_ref, t_ref, o_ref, kmean_ref, mask_ref, s_scr, acc_scr, *, n_mb, topk):
    c = pl.program_id(2)
    blk = c // QPK
    qblk = c * TQ // MOBA_BLOCK

    @pl.when(c == 0)
    def _():
        kmean_ref[...] = jnp.zeros_like(kmean_ref)
        for n in range(n_mb):
            kblk = k_ref[0, n * MOBA_BLOCK:(n + 1) * MOBA_BLOCK, :].astype(F32)
            kmean_ref[n:n + 1, :] = jnp.mean(kblk, axis=0, keepdims=True)

    q = q_ref[0, 0]
    rowi = lax.broadcasted_iota(jnp.int32, (LANES, TQ), 0)
    nidx = lax.broadcasted_iota(jnp.int32, (kmean_ref.shape[0], TQ), 0)
    km = _split3(kmean_ref[...])
    qpads = []
    for h in range(HPS):
        lo = (h // 2) * LANES
        qpair = q[lo:lo + LANES, :]
        qh = jnp.where(rowi // HEAD_DIM == h % 2, qpair, jnp.zeros_like(qpair))
        route = sum(_dot(part[:, lo:lo + LANES], qh) for part in km)
        route = jnp.where(nidx < qblk, route, NEG_INF)
        sel = _rank_select(route, nidx, n_mb, topk) & (nidx < qblk)
        mask_ref[h] = jnp.where(sel | (nidx == qblk), 0.0, NEG_INF)
        qpads.append(qh)

    def qk_fn(h, n):
        rows = pl.ds(pl.multiple_of(n * TK, TK), TK)
        return _dot(k_ref[0, rows, (h // 2) * LANES:(h // 2 + 1) * LANES], qpads[h])

    def fix_fn(h, n, s):
        parts = []
        for u in range(KSUB):
            b = KSUB * n + u
            parts.append(mask_ref[h, pl.ds(b, 1), :] + t_ref[h, jnp.clip(c - b, -1, NE_BIAS) + 1])
        return s + jnp.concatenate(parts, axis=0)

    def v_fn(h, n):
        return v_ref[0, n, h * HEAD_DIM:(h + 1) * HEAD_DIM, :]

    outs = _attend(blk + 1, lambda i: blk - i, qk_fn, fix_fn, v_fn, s_scr, acc_scr, HPS)
    o_ref[0] = jnp.concatenate(outs, axis=0).astype(BF16)


def _moba(fm, rm, tab, B, S):
    n_mb = S // MOBA_BLOCK
    n_pad = -(-n_mb // 16) * 16
    ne = tab.shape[1]
    rows = HPS * HEAD_DIM
    body = functools.partial(_moba_body, n_mb=n_mb, topk=min(MOBA_TOPK, n_mb))
    return pl.pallas_call(
        body,
        grid=(B, MOBA_HEADS // HPS, S // TQ),
        in_specs=[
            pl.BlockSpec((1, 1, rows, TQ), lambda b, p, c: (b, c // QPK, p, c % QPK)),
            pl.BlockSpec((1, S, rows), lambda b, p, c: (b, 0, p)),
            pl.BlockSpec((1, S // TK, rows, TK), lambda b, p, c: (b, 0, MOBA_HEADS // HPS + p, 0)),
            pl.BlockSpec((HPS, ne, TB, TQ), lambda b, p, c: (p, 0, 0, 0), pipeline_mode=pl.Buffered(1)),
        ],
        out_specs=pl.BlockSpec((1, rows, TQ), lambda b, p, c: (b, p, c)),
        out_shape=jax.ShapeDtypeStruct((B, MOBA_W, S), BF16),
        scratch_shapes=[pltpu.VMEM((n_pad, rows), F32), pltpu.VMEM((HPS, n_pad, TQ), F32),
                        pltpu.VMEM((HPS, TK, TQ), F32), pltpu.VMEM((HPS, ACC_ROWS, TQ), F32)],
        compiler_params=_params(("parallel", "parallel", "arbitrary")),
        name="moba_attn",
    )(fm, rm, fm, tab)


def _nsa_body(q_ref, kc_ref, vct_ref, fc_ref, ovl_ref, ksl_ref, vsl_ref, kwn_ref, vwn_ref,
              tslc_ref, twin_ref, gz_ref, o_ref, s_scr, acc_scr, s_win, acc_win, sel_ref, *, n_sb, n_sel):
    c = pl.program_id(1)
    blk = c // QPK
    G, J = NSA_KV_GROUPS, NSA_HPG
    H = G * J

    q = q_ref[0, 0]
    qs = [q[h * HEAD_DIM:(h + 1) * HEAD_DIM, :] for h in range(H)]
    zero = jnp.zeros((HEAD_DIM, TQ), BF16)
    qpads = [jnp.concatenate([zero] * (h // J) + [qs[h]] + [zero] * (G - 1 - h // J), axis=0) for h in range(H)]

    c0 = pl.multiple_of(c * (TQ // NSA_CMP_STRIDE), TQ // NSA_CMP_STRIDE)
    o_cmp = []
    for g in range(G):
        kc = kc_ref[0, g]
        vct = vct_ref[0, g]
        psum = jnp.zeros((kc.shape[0], TQ), F32)
        for h in range(g * J, (g + 1) * J):
            s = _dot(kc, qs[h]) + fc_ref[h, pl.ds(c0, kc.shape[0]), :]
            m = jnp.max(s, axis=0, keepdims=True)
            p = jnp.exp2(s - m)
            l = jnp.sum(p, axis=0, keepdims=True)
            pn = p * jnp.where(m > 0.5 * NEG_INF, 1.0 / l, 0.0)
            o_cmp.append(_dot(vct, pn.astype(BF16)))
            psum = psum + pn

        ph = psum.astype(BF16)
        plo = (psum - ph.astype(F32)).astype(BF16)
        imp = _dot(ovl_ref[...], ph) + _dot(ovl_ref[...], plo)
        jb = lax.broadcasted_iota(jnp.int32, imp.shape, 0)
        t = c * TQ + lax.broadcasted_iota(jnp.int32, imp.shape, 1)
        sb = t // NSA_SLC_BLOCK
        forced = (jb == 0) | (jb == sb) | (jb == sb - 1)
        allowed = jb <= sb
        val = jnp.where(forced, imp + NSA_FORCE_SCORE, jnp.where(allowed, imp, NEG_INF))
        sel = _rank_select(val, jb, n_sb, n_sel) & allowed
        selb = jnp.where(sel, 0.0, NEG_INF)
        for n in range(n_sb // SPT):
            slab = selb[n * SPT:(n + 1) * SPT, :]
            if SPT < sel_ref.shape[2]:
                slab = jnp.concatenate([slab, jnp.zeros((sel_ref.shape[2] - SPT, TQ), F32)], axis=0)
            sel_ref[g, n] = slab

    def slc_qk(h, n):
        return _dot(ksl_ref[0, pl.ds(pl.multiple_of(n * TK, TK), TK), :], qpads[h])

    def slc_fix(h, n, s):
        rows = sel_ref[h // J, n]
        mask = jnp.concatenate([jnp.broadcast_to(rows[b:b + 1, :], (NSA_SLC_BLOCK, TQ)) for b in range(SPT)], axis=0)
        bias = jnp.concatenate([tslc_ref[h, jnp.clip(c - (KSUB * n + u), -1, NE_BIAS) + 1] for u in range(KSUB)],
                               axis=0)
        return s + mask + bias

    def slc_v(h, n):
        return vsl_ref[0, n, (h // J) * HEAD_DIM:(h // J + 1) * HEAD_DIM, :]

    o_slc = _attend(blk + 1, lambda i: blk - i, slc_qk, slc_fix, slc_v, s_scr, acc_scr, H)

    def win_qk(h, n):
        return _dot(kwn_ref[0, pl.ds(pl.multiple_of(n * TK, TK), TK), :], qpads[h])

    def win_fix(h, n, s):
        return s + jnp.concatenate([twin_ref[h, jnp.clip(c - (KSUB * n + u), -1, NE_WIN) + 1] for u in range(KSUB)],
                                   axis=0)

    def win_v(h, n):
        return vwn_ref[0, n, (h // J) * HEAD_DIM:(h // J + 1) * HEAD_DIM, :]

    w_lo = jnp.maximum(c - NE_WIN + 1, 0) // KSUB
    o_win = _attend(blk - w_lo + 1, lambda i: blk - i, win_qk, win_fix, win_v, s_win, acc_win, H)

    gate = jax.nn.sigmoid(gz_ref[0])
    outs = []
    for h in range(H):
        r = (h // J) * GZ_ROWS + h % J
        outs.append(gate[r:r + 1, :] * o_cmp[h] + gate[r + J:r + J + 1, :] * o_slc[h]
                    + gate[r + 2 * J:r + 2 * J + 1, :] * o_win[h])
    o_ref[0] = jnp.concatenate(outs, axis=0).astype(BF16)


def _nsa(fm, rm, gz, kc, vct, tcmp, ovl, tslc, twin, B, S, col_ksl, col_kwn, row_q, row_vsl, row_vwn):
    G, J = NSA_KV_GROUPS, NSA_HPG
    NC = kc.shape[2]
    n_sb = S // NSA_SLC_BLOCK
    body = functools.partial(_nsa_body, n_sb=n_sb, n_sel=min(NSA_TOPN, n_sb))
    H = G * J
    kvrows = G * HEAD_DIM
    one = pl.Buffered(1)
    return pl.pallas_call(
        body,
        grid=(B, S // TQ),
        in_specs=[
            pl.BlockSpec((1, 1, NSA_W, TQ), lambda b, c: (b, c // QPK, row_q // NSA_W, c % QPK)),
            pl.BlockSpec((1, G, NC, HEAD_DIM), lambda b, c: (b, 0, 0, 0)),
            pl.BlockSpec((1, G, HEAD_DIM, NC), lambda b, c: (b, 0, 0, 0)),
            _const_spec(tcmp.shape),
            _const_spec(ovl.shape),
            pl.BlockSpec((1, S, LANES), lambda b, c: (b, 0, col_ksl // LANES)),
            pl.BlockSpec((1, S // TK, kvrows, TK), lambda b, c: (b, 0, row_vsl // kvrows, 0)),
            pl.BlockSpec((1, S, LANES), lambda b, c: (b, 0, col_kwn // LANES)),
            pl.BlockSpec((1, S // TK, kvrows, TK), lambda b, c: (b, 0, row_vwn // kvrows, 0)),
            pl.BlockSpec((H, tslc.shape[1], TB, TQ), lambda b, c: (MOBA_HEADS // H, 0, 0, 0), pipeline_mode=one),
            _const_spec(twin.shape),
            pl.BlockSpec((1, G * GZ_ROWS, TQ), lambda b, c: (b, 0, c)),
        ],
        out_specs=pl.BlockSpec((1, NSA_W, TQ), lambda b, c: (b, 0, c)),
        out_shape=jax.ShapeDtypeStruct((B, NSA_W, S), BF16),
        scratch_shapes=[pltpu.VMEM((H, TK, TQ), F32), pltpu.VMEM((H, ACC_ROWS, TQ), F32),
                        pltpu.VMEM((H, TK, TQ), F32), pltpu.VMEM((H, ACC_ROWS, TQ), F32),
                        pltpu.VMEM((G, S // TK, 8, TQ), F32)],
        compiler_params=_params(("parallel", "arbitrary")),
        name="nsa_attn",
    )(fm, kc, vct, tcmp, ovl, rm, fm, rm, fm, tslc, twin, gz)


def _inproj1_body(x_ref, g_ref, wfm_ref, wk_ref, wf_ref, bf_ref, tri_ref, place_ref, fm_ref, ka_ref, carry_ref,
                  *, nst):
    i = pl.program_id(0)

    @pl.when(i % nst == 0)
    def _():
        carry_ref[...] = jnp.zeros_like(carry_ref)

    xf = _rmsnorm(x_ref[...], g_ref[...])
    xn = xf.astype(BF16)
    xlo = (xf - xn.astype(F32)).astype(BF16)
    for r0 in range(0, fm_ref.shape[2], CH):
        res = _dot_nt(wfm_ref[r0:r0 + CH, :], xn).astype(BF16)
        for t in range(TM // TK):
            fm_ref[0, t, r0:r0 + CH, :] = res[:, t * TK:(t + 1) * TK]

    fz = _dot(xn, wf_ref[0]) + _dot(xlo, wf_ref[0]) + _dot(xn, wf_ref[1]) + bf_ref[...]
    logf = jnp.minimum(fz, 0.0) - jnp.log(1.0 + jnp.exp(-jnp.abs(fz)))
    tri = tri_ref[...]
    h1, h2, h3 = _split3(logf)
    cum = _dot(tri, h1) + _dot(tri, h2) + _dot(tri, h3) + carry_ref[0:1, :]
    carry_ref[...] = jnp.broadcast_to(cum[TM - 1:TM, :], carry_ref.shape)
    c1, c2, c3 = _split3(cum * LOG2E)
    for c0 in range(0, ka_ref.shape[-1], FF_CH):
        ka = (_dot(xn, wk_ref[:, c0:c0 + FF_CH]) + _dot(c1, place_ref[0, :, c0:c0 + FF_CH])
              + _dot(c2, place_ref[1, :, c0:c0 + FF_CH]) + _dot(c3, place_ref[2, :, c0:c0 + FF_CH]))
        ka_ref[:, c0:c0 + FF_CH] = ka.astype(BF16)


def _inproj1(x2, g, wfm, wk, wf, bf, tri, place, B, S):
    M = B * S
    nst = S // TM
    n_fm, n_ka = wfm.shape[0], wk.shape[1]
    return pl.pallas_call(
        functools.partial(_inproj1_body, nst=nst),
        grid=(M // TM,),
        in_specs=[
            pl.BlockSpec((TM, D_MODEL), lambda i: (i, 0)),
            _const_spec((1, D_MODEL)),
            _const_spec(wfm.shape),
            _const_spec(wk.shape),
            _const_spec(wf.shape),
            _const_spec(bf.shape),
            _const_spec(tri.shape),
            _const_spec(place.shape),
        ],
        out_specs=[
            pl.BlockSpec((1, TM // TK, n_fm, TK), lambda i: (i // nst, i % nst, 0, 0)),
            pl.BlockSpec((TM, n_ka), lambda i: (i, 0)),
        ],
        out_shape=[
            jax.ShapeDtypeStruct((B, S // TK, n_fm, TK), BF16),
            jax.ShapeDtypeStruct((M, n_ka), BF16),
        ],
        scratch_shapes=[pltpu.VMEM((8, LANES), F32)],
        compiler_params=_params(("arbitrary",)),
        name="inproj1",
    )(x2, g, wfm, wk, wf, bf, tri, place)


def _fox_body(q_ref, k_ref, v_ref, cm_ref, o_ref, s_scr, acc_scr):
    c = pl.program_id(2)
    blk = c // QPK
    q = q_ref[0, 0]
    ones = jnp.ones((LANES - HEAD_DIM, TQ), BF16)
    qas = [jnp.concatenate([q[h * HEAD_DIM:(h + 1) * HEAD_DIM, :], ones], axis=0) for h in range(FOX_HPS)]

    def qk_fn(h, n):
        rows = pl.ds(pl.multiple_of(n * TK, TK), TK)
        return _dot(k_ref[0, rows, h * LANES:(h + 1) * LANES], qas[h])

    causal = cm_ref[c % QPK]
    outs = _attend(blk + 1, lambda i: blk - i, qk_fn, None,
                   lambda h, n: v_ref[0, n, h * HEAD_DIM:(h + 1) * HEAD_DIM, :], s_scr, acc_scr, FOX_HPS,
                   first_fix=lambda h, n, s: s + causal)
    o_ref[0] = jnp.concatenate(outs, axis=0).astype(BF16)


def _fox(fm, ka, cmask, B, S):
    rows = FOX_HPS * HEAD_DIM
    return pl.pallas_call(
        _fox_body,
        grid=(B, FOX_HEADS // FOX_HPS, S // TQ),
        in_specs=[
            pl.BlockSpec((1, 1, rows, TQ), lambda b, h, c: (b, c // QPK, h, c % QPK)),
            pl.BlockSpec((1, S, FOX_HPS * LANES), lambda b, h, c: (b, 0, h), pipeline_mode=pl.Buffered(1)),
            pl.BlockSpec((1, S // TK, rows, TK), lambda b, h, c: (b, 0, FOX_HEADS // FOX_HPS + h, 0),
                         pipeline_mode=pl.Buffered(1)),
            _const_spec(cmask.shape),
        ],
        out_specs=pl.BlockSpec((1, rows, TQ), lambda b, h, c: (b, h, c)),
        out_shape=jax.ShapeDtypeStruct((B, FOX_W, S), BF16),
        scratch_shapes=[pltpu.VMEM((FOX_HPS, TK, TQ), F32), pltpu.VMEM((FOX_HPS, ACC_ROWS, TQ), F32)],
        compiler_params=_params(("parallel", "parallel", "arbitrary")),
        name="fox_attn",
    )(fm, ka, fm, cmask)


def _post_body(*refs, n_parts, final):
    o_refs = refs[:n_parts]
    h_ref, wo_ref, g_ref, w1_ref, w2_ref = refs[n_parts:n_parts + 5]
    gf_ref = refs[n_parts + 5] if final else None
    out_ref, hn_ref = refs[-2:]
    h1 = h_ref[...]
    r0 = 0
    for o_ref in o_refs:
        nf = o_ref.shape[1]
        h1 = h1 + _dot_tn(o_ref[0], wo_ref[r0:r0 + nf, :])
        r0 += nf
    out_ref[...] = h1
    hn_ref[...] = _rmsnorm(out_ref[...], g_ref[...]).astype(BF16)
    for c0 in range(0, D_FF, FF_CH):
        a = jnp.maximum(_dot(hn_ref[...], w1_ref[:, c0:c0 + FF_CH]), 0.0)
        out_ref[...] += _dot((a * a).astype(BF16), w2_ref[c0:c0 + FF_CH, :])
    if final:
        out_ref[...] = _rmsnorm(out_ref[...], gf_ref[...])


def _post(o_parts, h2, wo, g, w1, w2, gf, B, S):
    M = B * S
    nst = S // TM
    final = gf is not None
    in_specs = [pl.BlockSpec((1, o.shape[1], TM), lambda i: (i // nst, 0, i % nst)) for o in o_parts]
    in_specs += [
        pl.BlockSpec((TM, D_MODEL), lambda i: (i, 0)),
        _const_spec(wo.shape),
        _const_spec((1, D_MODEL)),
        _const_spec(w1.shape),
        _const_spec(w2.shape),
    ]
    args = list(o_parts) + [h2, wo, g, w1, w2]
    if final:
        in_specs.append(_const_spec((1, D_MODEL)))
        args.append(gf)
    return pl.pallas_call(
        functools.partial(_post_body, n_parts=len(o_parts), final=final),
        grid=(M // TM,),
        in_specs=in_specs,
        out_specs=pl.BlockSpec((TM, D_MODEL), lambda i: (i, 0)),
        out_shape=jax.ShapeDtypeStruct((M, D_MODEL), F32),
        scratch_shapes=[pltpu.VMEM((TM, D_MODEL), BF16)],
        compiler_params=_params(("parallel",)),
        name="post_final" if final else "post",
    )(*args)


def _rel_bucket(dist):
    n = jnp.maximum(dist, 0)
    max_exact = REL_BUCKETS // 2
    nf = jnp.maximum(n, 1).astype(jnp.float32)
    large = max_exact + (jnp.log(nf / max_exact) / math.log(REL_MAX_DISTANCE / max_exact)
                         * (REL_BUCKETS - max_exact)).astype(jnp.int32)
    large = jnp.minimum(large, REL_BUCKETS - 1)
    return jnp.where(n < max_exact, n, large)


def _bias_tables(rel_bias, S):
    n_heads = rel_bias.shape[1]
    table = rel_bias.T * LOG2E

    def bias_of(dist):
        tab = table.reshape((n_heads, REL_BUCKETS) + (1,) * dist.ndim)
        bkt = _rel_bucket(jnp.asarray(dist))[None]
        out = jnp.zeros((n_heads,) + dist.shape, F32)
        for b in range(REL_BUCKETS):
            out = jnp.where(bkt == b, tab[:, b], out)
        return jnp.where(jnp.asarray(dist)[None] >= 0, out, NEG_INF)

    d = (np.arange(-1, NE_BIAS)[:, None, None] * TQ + np.arange(TQ)[None, None, :] - np.arange(TB)[None, :, None])
    vals = bias_of(d)
    far = table[:, REL_BUCKETS - 1][:, None, None, None]
    tile = jnp.where(d >= 0, vals - far, NEG_INF)
    tile = jnp.concatenate([tile, jnp.zeros_like(tile[:, :1])], axis=1)
    dw = d[:NE_WIN + 2]
    twin = jnp.where((dw >= 0) & (dw < NSA_WINDOW), vals[MOBA_HEADS:, :NE_WIN + 2], NEG_INF)
    n_c = S // NSA_CMP_STRIDE
    u = np.arange(2 * n_c)[:, None]
    dc = np.arange(TQ)[None, :] + NSA_CMP_STRIDE * (u - (n_c - 1)) - (NSA_CMP_BLOCK - 1)
    fcmp = bias_of(dc)[MOBA_HEADS:]
    return tile, twin, fcmp


def _selection_constants(S):
    n_c = S // NSA_CMP_STRIDE
    n_cmp = (S - NSA_CMP_BLOCK) // NSA_CMP_STRIDE + 1
    n_sb = S // NSA_SLC_BLOCK
    ci = np.arange(n_c)[None, :] * NSA_CMP_STRIDE
    sj = np.arange(n_sb)[:, None] * NSA_SLC_BLOCK
    ovl = (ci < sj + NSA_SLC_BLOCK) & (ci + NSA_CMP_BLOCK > sj) & (np.arange(n_c)[None, :] < n_cmp)
    ovl = ovl[:, ::-1]
    return jnp.asarray(ovl, BF16)


def _causal_tiles():
    e = np.arange(QPK)[:, None, None]
    d = e * TQ + np.arange(TQ)[None, None, :] - np.arange(TK)[None, :, None]
    return jnp.asarray(np.where(d >= 0, 0.0, NEG_INF), F32)


def kernel(x, rel_bias, mix_norm, mlp_norm, even_w_in, even_w_out, cmp_pos_k, cmp_pos_v, cmp_k_w1, cmp_k_w2,
           cmp_v_w1, cmp_v_w2, odd_w_in, odd_b_forget, odd_w_out, mlp_w1, mlp_w2, final_norm):
    B, S, D = x.shape
    assert D == D_MODEL and S % TM == 0
    G, J = NSA_KV_GROUPS, NSA_HPG
    h = x.reshape(B * S, D)

    offs = np.cumsum((MOBA_W, MOBA_W, MOBA_W, NSA_W) + (NSA_KV_W,) * 6)
    mq_w, mk_w, mv_w, nq_w, kc_w, vc_w, ksl_w, vsl_w, kwn_w, vwn_w, gz_w = jnp.split(even_w_in[0], offs, axis=1)
    wrm = jnp.concatenate([mk_w, kc_w, vc_w, ksl_w, kwn_w], axis=1).astype(BF16)
    col_kcvc, col_ksl, col_kwn = MOBA_W, MOBA_W + 2 * NSA_KV_W, MOBA_W + 3 * NSA_KV_W
    qs = SCALE * LOG2E
    wfm = jnp.concatenate([mq_w * qs, mv_w, nq_w * qs, vsl_w, vwn_w], axis=1).T.astype(BF16)
    row_nq, row_vsl, row_vwn = 2 * MOBA_W, 2 * MOBA_W + NSA_W, 2 * MOBA_W + NSA_W + NSA_KV_W
    gzw = gz_w.T.reshape(G, J, 3, D).transpose(0, 2, 1, 3).reshape(G, 3 * J, D)
    gzw = jnp.pad(gzw, ((0, 0), (0, GZ_ROWS - 3 * J), (0, 0))).reshape(G * GZ_ROWS, D).astype(BF16)

    rm, fm, gz = _inproj0(h, mix_norm[0][None, :], wrm, wfm, gzw, B, S)
    rm = rm.reshape(B, S, -1)

    tile, twin, tcmp = _bias_tables(rel_bias, S)
    ovl = _selection_constants(S)

    o_moba = _moba(fm, rm, tile, B, S)

    n_c = S // NSA_CMP_STRIDE
    r = rm[:, :, col_kcvc:col_kcvc + 2 * NSA_KV_W].reshape(B, n_c, NSA_CMP_STRIDE, 2 * G, HEAD_DIM)
    r = r.transpose(0, 3, 1, 2, 4).reshape(B, 2 * G, n_c, NSA_CMP_STRIDE * HEAD_DIM)
    pos = jnp.stack([cmp_pos_k[0].reshape(1, -1), cmp_pos_v[0].reshape(1, -1)])
    pos = jnp.pad(pos, ((0, 0), (0, 7), (0, 0))).astype(BF16)
    w1c = jnp.stack([cmp_k_w1[0], cmp_v_w1[0]]).astype(BF16)
    kc, vct = _compress(r, pos, w1c, cmp_k_w2[0].astype(BF16), cmp_v_w2[0].T.astype(BF16), B, n_c)

    o_nsa = _nsa(fm, rm, gz, kc, vct, tcmp, ovl, tile, twin, B, S,
                 col_ksl, col_kwn, row_nq, row_vsl, row_vwn)

    h = _post([o_moba, o_nsa], h, even_w_out[0].astype(BF16), mlp_norm[0][None, :],
              mlp_w1[0].astype(BF16), mlp_w2[0].astype(BF16), None, B, S)

    q_w, k_w, v_w, f_w = jnp.split(odd_w_in[0], np.cumsum((FOX_W, FOX_W, FOX_W)), axis=1)
    wfm1 = jnp.concatenate([q_w * qs, v_w], axis=1).T.astype(BF16)
    wk = jnp.pad(k_w.reshape(D, FOX_HEADS, HEAD_DIM), ((0, 0), (0, 0), (0, LANES - HEAD_DIM)))
    wk = wk.reshape(D, FOX_HEADS * LANES).astype(BF16)
    f_w = jnp.pad(f_w, ((0, 0), (0, LANES - FOX_HEADS)))
    f_hi = f_w.astype(BF16)
    wf = jnp.stack([f_hi, (f_w - f_hi.astype(F32)).astype(BF16)])
    bf = jnp.pad(odd_b_forget[0], (0, LANES - FOX_HEADS))[None, :]
    tri = jnp.asarray(np.tril(np.ones((TM, TM))), BF16)
    place = np.zeros((3, LANES, FOX_HEADS * LANES), np.float32)
    for term in range(3):
        place[term, np.arange(FOX_HEADS), np.arange(FOX_HEADS) * LANES + HEAD_DIM + term] = -1.0
    fm1, ka = _inproj1(h, mix_norm[1][None, :], wfm1, wk, wf, bf, tri, jnp.asarray(place, BF16), B, S)
    o_fox = _fox(fm1, ka.reshape(B, S, -1), _causal_tiles(), B, S)

    h = _post([o_fox], h, odd_w_out[0].astype(BF16), mlp_norm[1][None, :],
              mlp_w1[1].astype(BF16), mlp_w2[1].astype(BF16), final_norm[None, :], B, S)
    return h.reshape(B, S, D)
```

```python
import functools
import math

import numpy as np
import jax
import jax.numpy as jnp
from jax import lax
from jax.experimental import pallas as pl
from jax.experimental.pallas import tpu as pltpu

D_MODEL = 1024
HEAD_DIM = 64
MOBA_HEADS = 8
MOBA_BLOCK = 256
MOBA_TOPK = 3
NSA_HEADS = 8
NSA_KV_GROUPS = 2
NSA_HPG = NSA_HEADS // NSA_KV_GROUPS
NSA_CMP_BLOCK = 32
NSA_CMP_STRIDE = 16
NSA_CMP_HIDDEN = 256
NSA_SLC_BLOCK = 64
NSA_TOPN = 16
NSA_WINDOW = 512
NSA_FORCE_SCORE = 1e6
FOX_HEADS = 16
D_FF = 4 * D_MODEL
REL_BUCKETS = 32
REL_MAX_DISTANCE = 1024
RMS_EPS = 1e-5
NEG_INF = -1e30
SCALE = HEAD_DIM ** -0.5

MOBA_W = MOBA_HEADS * HEAD_DIM
NSA_W = NSA_HEADS * HEAD_DIM
NSA_KV_W = NSA_KV_GROUPS * HEAD_DIM
FOX_W = FOX_HEADS * HEAD_DIM

LANES = 128
SUBLANES = 8
TQ = 256
TK = 256
TB = 256
QPK = TK // TQ
KSUB = TK // TB
TM = 512
CH = 256
FF_CH = 512
VMEM_LIMIT = 56 * 1024 * 1024
NE_BIAS = -(-(REL_MAX_DISTANCE + TB - 1) // TQ)
NE_WIN = -(-(NSA_WINDOW + TB - 1) // TQ)
GZ_ROWS = 16
HPS = 8
FOX_HPS = 16
SPT = TK // NSA_SLC_BLOCK
ACC_ROWS = HEAD_DIM + 16
ROW_CHUNK = 64
LOG2E = math.log2(math.e)

assert TK % TQ == 0 and TK % TB == 0 and TQ == TB and MOBA_BLOCK == TB and TB % NSA_SLC_BLOCK == 0 and SPT <= 8

F32 = jnp.float32
BF16 = jnp.bfloat16


def _dot(a, b):
    return jnp.dot(a, b, preferred_element_type=F32)


def _dot_nt(a, b):
    return lax.dot_general(a, b, (((1,), (1,)), ((), ())), preferred_element_type=F32)


def _dot_tn(a, b):
    return lax.dot_general(a, b, (((0,), (0,)), ((), ())), preferred_element_type=F32)


def _rmsnorm(x, g):
    ms = jnp.mean(x * x, axis=-1, keepdims=True)
    return x * lax.rsqrt(ms + RMS_EPS) * g


def _split3(x):
    a = x.astype(BF16)
    r = x - a.astype(F32)
    b = r.astype(BF16)
    c = (r - b.astype(F32)).astype(BF16)
    return a, b, c


def _const_spec(shape):
    nd = len(shape)
    return pl.BlockSpec(shape, lambda *_: (0,) * nd, pipeline_mode=pl.Buffered(1))


def _params(sem):
    return pltpu.CompilerParams(dimension_semantics=sem, vmem_limit_bytes=VMEM_LIMIT)


def _attend(n_tiles, tile_of, qk_fn, fix_fn, v_fn, s_scr, acc_scr, n_heads, first_fix=None):
    def put_scores(h, n, fix):
        s = qk_fn(h, n)
        s_scr[h] = s if fix is None else fix(h, n, s)

    for h in range(n_heads):
        put_scores(h, tile_of(0), fix_fn if first_fix is None else first_fix)
    acc_scr[...] = jnp.zeros_like(acc_scr)
    last = n_tiles - 1
    ones = jnp.ones((ACC_ROWS - HEAD_DIM, TK), BF16)
    chunks = range(0, TK, ROW_CHUNK)

    def body(i, ms):
        n = tile_of(i)
        n_next = tile_of(jnp.minimum(i + 1, last))
        out = []
        for h in range(n_heads):
            mx = s_scr[h, 0:ROW_CHUNK, :]
            for r0 in chunks[1:]:
                mx = jnp.maximum(mx, s_scr[h, r0:r0 + ROW_CHUNK, :])
            m_new = jnp.maximum(ms[h], jnp.max(mx, axis=0, keepdims=True))
            alpha = jnp.exp2(ms[h] - m_new)
            p = jnp.concatenate([jnp.exp2(s_scr[h, r0:r0 + ROW_CHUNK, :] - m_new).astype(BF16) for r0 in chunks],
                                axis=0)
            va = jnp.concatenate([v_fn(h, n), ones], axis=0)
            acc_scr[h] = alpha * acc_scr[h] + _dot(va, p)
            out.append(m_new)
            put_scores(h, n_next, fix_fn)
        return tuple(out)

    lax.fori_loop(0, n_tiles, body, tuple(jnp.full((1, TQ), NEG_INF, F32) for _ in range(n_heads)))
    return [acc_scr[h, :HEAD_DIM, :] / acc_scr[h, HEAD_DIM:HEAD_DIM + 1, :] for h in range(n_heads)]


def _rank_select(val, n_rows, k):
    sub = lax.broadcasted_iota(jnp.int32, (SUBLANES, val.shape[1]), 0)
    groups = [val[g0:g0 + SUBLANES, :] for g0 in range(0, val.shape[0], SUBLANES)]
    counts = [jnp.zeros(g.shape, F32) for g in groups]
    for m in range(n_rows):
        vm = val[m:m + 1, :]
        for g, vg in enumerate(groups):
            if g * SUBLANES > m:
                counts[g] = counts[g] + jnp.where(vm >= vg, 1.0, 0.0)
            elif (g + 1) * SUBLANES <= m:
                counts[g] = counts[g] + jnp.where(vm > vg, 1.0, 0.0)
            else:
                tie = jnp.where(sub > m % SUBLANES, 1.0, 0.0)
                counts[g] = counts[g] + jnp.where(vm > vg, 1.0, 0.0) + jnp.where(vm == vg, tie, 0.0)
    return jnp.concatenate(counts, axis=0) < k


def _inproj0_body(x_ref, g_ref, wrm_ref, wfm_ref, wgz_ref, wcv_ref, rm_ref, fm_ref, gz_ref, cv_ref):
    xn = _rmsnorm(x_ref[...], g_ref[...]).astype(BF16)
    cv = _dot(xn, wcv_ref[...]).astype(BF16)
    for j in range(cv_ref.shape[0]):
        cv_ref[j] = cv[:, j * HEAD_DIM:(j + 1) * HEAD_DIM]
    for c0 in range(0, rm_ref.shape[-1], CH):
        rm_ref[:, c0:c0 + CH] = _dot(xn, wrm_ref[:, c0:c0 + CH]).astype(BF16)
    for r0 in range(0, fm_ref.shape[2], CH):
        res = _dot_nt(wfm_ref[r0:r0 + CH, :], xn).astype(BF16)
        for t in range(TM // TK):
            fm_ref[0, t, r0:r0 + CH, :] = res[:, t * TK:(t + 1) * TK]
    gz_ref[0] = _dot_nt(wgz_ref[...], xn)


def _inproj0(x2, g, wrm, wfm, wgz, wcv, B, S):
    M = B * S
    nst = S // TM
    n_rm, n_fm, n_gz, n_cv = wrm.shape[1], wfm.shape[0], wgz.shape[0], wcv.shape[1] // HEAD_DIM
    return pl.pallas_call(
        _inproj0_body,
        grid=(M // TM,),
        in_specs=[
            pl.BlockSpec((TM, D_MODEL), lambda i: (i, 0)),
            _const_spec((1, D_MODEL)),
            _const_spec((D_MODEL, n_rm)),
            _const_spec((n_fm, D_MODEL)),
            _const_spec((n_gz, D_MODEL)),
            _const_spec(wcv.shape),
        ],
        out_specs=[
            pl.BlockSpec((TM, n_rm), lambda i: (i, 0)),
            pl.BlockSpec((1, TM // TK, n_fm, TK), lambda i: (i // nst, i % nst, 0, 0)),
            pl.BlockSpec((1, n_gz, TM), lambda i: (i // nst, 0, i % nst)),
            pl.BlockSpec((n_cv, TM, HEAD_DIM), lambda i: (0, i, 0)),
        ],
        out_shape=[
            jax.ShapeDtypeStruct((M, n_rm), BF16),
            jax.ShapeDtypeStruct((B, S // TK, n_fm, TK), BF16),
            jax.ShapeDtypeStruct((B, n_gz, S), F32),
            jax.ShapeDtypeStruct((n_cv, M, HEAD_DIM), BF16),
        ],
        compiler_params=_params(("parallel",)),
        name="inproj0",
    )(x2, g, wrm, wfm, wgz, wcv)


def _compress_body(rk_ref, rv_ref, pos_ref, w1_ref, w2k_ref, w2vt_ref, flip_ref, kc_ref, vct_ref):
    half = NSA_CMP_STRIDE * HEAD_DIM

    def hidden(r_ref, s):
        r = r_ref[0, 0]
        a = _dot(r, w1_ref[s, :half, :])
        b = _dot(r, w1_ref[s, half:, :])
        nxt = pltpu.roll(b, b.shape[0] - 1, axis=0)
        posb = _dot(pos_ref[s], w1_ref[s])[0:1]
        pre = a + nxt + posb
        act = (pre * jax.nn.sigmoid(pre)).astype(BF16)
        return _dot(flip_ref[...], act).astype(BF16)

    kc_ref[0, 0] = _dot(hidden(rk_ref, 0), w2k_ref[...]).astype(BF16)
    vct_ref[0, 0] = _dot_nt(w2vt_ref[...], hidden(rv_ref, 1)).astype(BF16)


def _compress(r, pos, w1, w2k, w2vt, B, NC):
    G = NSA_KV_GROUPS
    half = NSA_CMP_STRIDE * HEAD_DIM
    return pl.pallas_call(
        _compress_body,
        grid=(B, G),
        in_specs=[
            pl.BlockSpec((1, 1, NC, half), lambda b, g: (g, b, 0, 0)),
            pl.BlockSpec((1, 1, NC, half), lambda b, g: (G + g, b, 0, 0)),
            _const_spec(pos.shape),
            _const_spec(w1.shape),
            _const_spec(w2k.shape),
            _const_spec(w2vt.shape),
            _const_spec((NC, NC)),
        ],
        out_specs=[
            pl.BlockSpec((1, 1, NC, HEAD_DIM), lambda b, g: (b, g, 0, 0)),
            pl.BlockSpec((1, 1, HEAD_DIM, NC), lambda b, g: (b, g, 0, 0)),
        ],
        out_shape=[
            jax.ShapeDtypeStruct((B, G, NC, HEAD_DIM), BF16),
            jax.ShapeDtypeStruct((B, G, HEAD_DIM, NC), BF16),
        ],
        compiler_params=_params(("parallel", "parallel")),
        name="nsa_compress",
    )(r, r, pos, w1, w2k, w2vt, jnp.asarray(np.eye(NC)[::-1], BF16))


def _moba_body(q_ref, k_ref, v_ref, t_ref, o_ref, kmean_ref, mask_ref, s_scr, acc_scr, *, n_mb, topk):
    c = pl.program_id(2)
    blk = c // QPK
    qblk = c * TQ // MOBA_BLOCK

    @pl.when(c == 0)
    def _():
        kmean_ref[...] = jnp.zeros_like(kmean_ref)
        for n in range(n_mb):
            kblk = k_ref[0, n * MOBA_BLOCK:(n + 1) * MOBA_BLOCK, :].astype(F32)
            kmean_ref[n:n + 1, :] = jnp.mean(kblk, axis=0, keepdims=True)

    q = q_ref[0, 0]
    rowi = lax.broadcasted_iota(jnp.int32, (LANES, TQ), 0)
    nidx = lax.broadcasted_iota(jnp.int32, (kmean_ref.shape[0], TQ), 0)
    km = _split3(kmean_ref[...])
    qpads = []
    for h in range(HPS):
        lo = (h // 2) * LANES
        qpair = q[lo:lo + LANES, :]
        qh = jnp.where(rowi // HEAD_DIM == h % 2, qpair, jnp.zeros_like(qpair))
        route = sum(_dot(part[:, lo:lo + LANES], qh) for part in km)
        route = jnp.where(nidx < qblk, route, NEG_INF)
        sel = _rank_select(route, n_mb, topk) & (nidx < qblk)
        mask_ref[h] = jnp.where(sel | (nidx == qblk), 0.0, NEG_INF)
        qpads.append(qh)

    def qk_fn(h, n):
        rows = pl.ds(pl.multiple_of(n * TK, TK), TK)
        return _dot(k_ref[0, rows, (h // 2) * LANES:(h // 2 + 1) * LANES], qpads[h])

    def fix_fn(h, n, s):
        parts = []
        for u in range(KSUB):
            b = KSUB * n + u
            parts.append(mask_ref[h, pl.ds(b, 1), :] + t_ref[h, jnp.clip(c - b, -1, NE_BIAS) + 1])
        return s + jnp.concatenate(parts, axis=0)

    def v_fn(h, n):
        return v_ref[0, n, h * HEAD_DIM:(h + 1) * HEAD_DIM, :]

    outs = _attend(blk + 1, lambda i: blk - i, qk_fn, fix_fn, v_fn, s_scr, acc_scr, HPS)
    o_ref[0] = jnp.concatenate(outs, axis=0).astype(BF16)


def _moba(fm, rm, tab, B, S):
    n_mb = S // MOBA_BLOCK
    n_pad = -(-n_mb // 16) * 16
    ne = tab.shape[1]
    rows = HPS * HEAD_DIM
    body = functools.partial(_moba_body, n_mb=n_mb, topk=min(MOBA_TOPK, n_mb))
    return pl.pallas_call(
        body,
        grid=(B, MOBA_HEADS // HPS, S // TQ),
        in_specs=[
            pl.BlockSpec((1, 1, rows, TQ), lambda b, p, c: (b, c // QPK, p, c % QPK)),
            pl.BlockSpec((1, S, rows), lambda b, p, c: (b, 0, p)),
            pl.BlockSpec((1, S // TK, rows, TK), lambda b, p, c: (b, 0, MOBA_HEADS // HPS + p, 0)),
            pl.BlockSpec((HPS, ne, TB, TQ), lambda b, p, c: (p, 0, 0, 0), pipeline_mode=pl.Buffered(1)),
        ],
        out_specs=pl.BlockSpec((1, rows, TQ), lambda b, p, c: (b, p, c)),
        out_shape=jax.ShapeDtypeStruct((B, MOBA_W, S), BF16),
        scratch_shapes=[pltpu.VMEM((n_pad, rows), F32), pltpu.VMEM((HPS, n_pad, TQ), F32),
                        pltpu.VMEM((HPS, TK, TQ), F32), pltpu.VMEM((HPS, ACC_ROWS, TQ), F32)],
        compiler_params=_params(("parallel", "parallel", "arbitrary")),
        name="moba_attn",
    )(fm, rm, fm, tab)


def _nsa_body(q_ref, kc_ref, vct_ref, fc_ref, ovl_ref, ksl_ref, vsl_ref, kwn_ref, vwn_ref,
              tslc_ref, twin_ref, gz_ref, o_ref, s_scr, acc_scr, s_win, acc_win, sel_ref, *, n_sb, n_sel):
    c = pl.program_id(1)
    blk = c // QPK
    G, J = NSA_KV_GROUPS, NSA_HPG
    H = G * J

    q = q_ref[0, 0]
    qs = [q[h * HEAD_DIM:(h + 1) * HEAD_DIM, :] for h in range(H)]
    zero = jnp.zeros((HEAD_DIM, TQ), BF16)
    qpads = [jnp.concatenate([zero] * (h // J) + [qs[h]] + [zero] * (G - 1 - h // J), axis=0) for h in range(H)]

    c0 = pl.multiple_of(c * (TQ // NSA_CMP_STRIDE), TQ // NSA_CMP_STRIDE)
    o_cmp = []
    for g in range(G):
        kc = kc_ref[0, g]
        vct = vct_ref[0, g]
        psum = jnp.zeros((kc.shape[0], TQ), F32)
        for h in range(g * J, (g + 1) * J):
            s = _dot(kc, qs[h]) + fc_ref[h, pl.ds(c0, kc.shape[0]), :]
            m = jnp.max(s, axis=0, keepdims=True)
            p = jnp.exp2(s - m)
            l = jnp.sum(p, axis=0, keepdims=True)
            pn = p * jnp.where(m > 0.5 * NEG_INF, 1.0 / l, 0.0)
            o_cmp.append(_dot(vct, pn.astype(BF16)))
            psum = psum + pn

        ph = psum.astype(BF16)
        plo = (psum - ph.astype(F32)).astype(BF16)
        imp = _dot(ovl_ref[...], ph) + _dot(ovl_ref[...], plo)
        jb = lax.broadcasted_iota(jnp.int32, imp.shape, 0)
        t = c * TQ + lax.broadcasted_iota(jnp.int32, imp.shape, 1)
        sb = t // NSA_SLC_BLOCK
        forced = (jb == 0) | (jb == sb) | (jb == sb - 1)
        allowed = jb <= sb
        val = jnp.where(forced, imp + NSA_FORCE_SCORE, jnp.where(allowed, imp, NEG_INF))
        sel = _rank_select(val, n_sb, n_sel) & allowed
        selb = jnp.where(sel, 0.0, NEG_INF)
        for n in range(n_sb // SPT):
            slab = selb[n * SPT:(n + 1) * SPT, :]
            if SPT < sel_ref.shape[2]:
                slab = jnp.concatenate([slab, jnp.zeros((sel_ref.shape[2] - SPT, TQ), F32)], axis=0)
            sel_ref[g, n] = slab

    def slc_qk(h, n):
        return _dot(ksl_ref[0, pl.ds(pl.multiple_of(n * TK, TK), TK), :], qpads[h])

    def slc_fix(h, n, s):
        rows = sel_ref[h // J, n]
        mask = jnp.concatenate([jnp.broadcast_to(rows[b:b + 1, :], (NSA_SLC_BLOCK, TQ)) for b in range(SPT)], axis=0)
        bias = jnp.concatenate([tslc_ref[h, jnp.clip(c - (KSUB * n + u), -1, NE_BIAS) + 1] for u in range(KSUB)],
                               axis=0)
        return s + mask + bias

    def slc_v(h, n):
        return vsl_ref[0, n, (h // J) * HEAD_DIM:(h // J + 1) * HEAD_DIM, :]

    o_slc = _attend(blk + 1, lambda i: blk - i, slc_qk, slc_fix, slc_v, s_scr, acc_scr, H)

    def win_qk(h, n):
        return _dot(kwn_ref[0, pl.ds(pl.multiple_of(n * TK, TK), TK), :], qpads[h])

    def win_fix(h, n, s):
        return s + jnp.concatenate([twin_ref[h, jnp.clip(c - (KSUB * n + u), -1, NE_WIN) + 1] for u in range(KSUB)],
                                   axis=0)

    def win_v(h, n):
        return vwn_ref[0, n, (h // J) * HEAD_DIM:(h // J + 1) * HEAD_DIM, :]

    w_lo = jnp.maximum(c - NE_WIN + 1, 0) // KSUB
    o_win = _attend(blk - w_lo + 1, lambda i: blk - i, win_qk, win_fix, win_v, s_win, acc_win, H)

    gate = jax.nn.sigmoid(gz_ref[0])
    outs = []
    for h in range(H):
        r = (h // J) * GZ_ROWS + h % J
        outs.append(gate[r:r + 1, :] * o_cmp[h] + gate[r + J:r + J + 1, :] * o_slc[h]
                    + gate[r + 2 * J:r + 2 * J + 1, :] * o_win[h])
    o_ref[0] = jnp.concatenate(outs, axis=0).astype(BF16)


def _nsa(fm, rm, gz, kc, vct, tcmp, ovl, tslc, twin, B, S, col_ksl, col_kwn, row_q, row_vsl, row_vwn):
    G, J = NSA_KV_GROUPS, NSA_HPG
    NC = kc.shape[2]
    n_sb = S // NSA_SLC_BLOCK
    body = functools.partial(_nsa_body, n_sb=n_sb, n_sel=min(NSA_TOPN, n_sb))
    H = G * J
    kvrows = G * HEAD_DIM
    one = pl.Buffered(1)
    return pl.pallas_call(
        body,
        grid=(B, S // TQ),
        in_specs=[
            pl.BlockSpec((1, 1, NSA_W, TQ), lambda b, c: (b, c // QPK, row_q // NSA_W, c % QPK)),
            pl.BlockSpec((1, G, NC, HEAD_DIM), lambda b, c: (b, 0, 0, 0)),
            pl.BlockSpec((1, G, HEAD_DIM, NC), lambda b, c: (b, 0, 0, 0)),
            _const_spec(tcmp.shape),
            _const_spec(ovl.shape),
            pl.BlockSpec((1, S, LANES), lambda b, c: (b, 0, col_ksl // LANES)),
            pl.BlockSpec((1, S // TK, kvrows, TK), lambda b, c: (b, 0, row_vsl // kvrows, 0)),
            pl.BlockSpec((1, S, LANES), lambda b, c: (b, 0, col_kwn // LANES)),
            pl.BlockSpec((1, S // TK, kvrows, TK), lambda b, c: (b, 0, row_vwn // kvrows, 0)),
            pl.BlockSpec((H, tslc.shape[1], TB, TQ), lambda b, c: (MOBA_HEADS // H, 0, 0, 0), pipeline_mode=one),
            _const_spec(twin.shape),
            pl.BlockSpec((1, G * GZ_ROWS, TQ), lambda b, c: (b, 0, c)),
        ],
        out_specs=pl.BlockSpec((1, NSA_W, TQ), lambda b, c: (b, 0, c)),
        out_shape=jax.ShapeDtypeStruct((B, NSA_W, S), BF16),
        scratch_shapes=[pltpu.VMEM((H, TK, TQ), F32), pltpu.VMEM((H, ACC_ROWS, TQ), F32),
                        pltpu.VMEM((H, TK, TQ), F32), pltpu.VMEM((H, ACC_ROWS, TQ), F32),
                        pltpu.VMEM((G, S // TK, 8, TQ), F32)],
        compiler_params=_params(("parallel", "arbitrary")),
        name="nsa_attn",
    )(fm, kc, vct, tcmp, ovl, rm, fm, rm, fm, tslc, twin, gz)


def _inproj1_body(x_ref, g_ref, wfm_ref, wk_ref, wf_ref, bf_ref, tri_ref, place_ref, fm_ref, ka_ref, carry_ref,
                  *, nst):
    i = pl.program_id(0)

    @pl.when(i % nst == 0)
    def _():
        carry_ref[...] = jnp.zeros_like(carry_ref)

    xf = _rmsnorm(x_ref[...], g_ref[...])
    xn = xf.astype(BF16)
    xlo = (xf - xn.astype(F32)).astype(BF16)
    for r0 in range(0, fm_ref.shape[2], CH):
        res = _dot_nt(wfm_ref[r0:r0 + CH, :], xn).astype(BF16)
        for t in range(TM // TK):
            fm_ref[0, t, r0:r0 + CH, :] = res[:, t * TK:(t + 1) * TK]

    fz = _dot(xn, wf_ref[0]) + _dot(xlo, wf_ref[0]) + _dot(xn, wf_ref[1]) + bf_ref[...]
    logf = jnp.minimum(fz, 0.0) - jnp.log(1.0 + jnp.exp(-jnp.abs(fz)))
    tri = tri_ref[...]
    h1, h2, h3 = _split3(logf)
    cum = _dot(tri, h1) + _dot(tri, h2) + _dot(tri, h3) + carry_ref[0:1, :]
    carry_ref[...] = jnp.broadcast_to(cum[TM - 1:TM, :], carry_ref.shape)
    c1, c2, c3 = _split3(cum * LOG2E)
    for c0 in range(0, ka_ref.shape[-1], FF_CH):
        ka = (_dot(xn, wk_ref[:, c0:c0 + FF_CH]) + _dot(c1, place_ref[0, :, c0:c0 + FF_CH])
              + _dot(c2, place_ref[1, :, c0:c0 + FF_CH]) + _dot(c3, place_ref[2, :, c0:c0 + FF_CH]))
        ka_ref[:, c0:c0 + FF_CH] = ka.astype(BF16)


def _inproj1(x2, g, wfm, wk, wf, bf, tri, place, B, S):
    M = B * S
    nst = S // TM
    n_fm, n_ka = wfm.shape[0], wk.shape[1]
    return pl.pallas_call(
        functools.partial(_inproj1_body, nst=nst),
        grid=(M // TM,),
        in_specs=[
            pl.BlockSpec((TM, D_MODEL), lambda i: (i, 0)),
            _const_spec((1, D_MODEL)),
            _const_spec(wfm.shape),
            _const_spec(wk.shape),
            _const_spec(wf.shape),
            _const_spec(bf.shape),
            _const_spec(tri.shape),
            _const_spec(place.shape),
        ],
        out_specs=[
            pl.BlockSpec((1, TM // TK, n_fm, TK), lambda i: (i // nst, i % nst, 0, 0)),
            pl.BlockSpec((TM, n_ka), lambda i: (i, 0)),
        ],
        out_shape=[
            jax.ShapeDtypeStruct((B, S // TK, n_fm, TK), BF16),
            jax.ShapeDtypeStruct((M, n_ka), BF16),
        ],
        scratch_shapes=[pltpu.VMEM((8, LANES), F32)],
        compiler_params=_params(("arbitrary",)),
        name="inproj1",
    )(x2, g, wfm, wk, wf, bf, tri, place)


def _fox_body(q_ref, k_ref, v_ref, cm_ref, o_ref, s_scr, acc_scr):
    c = pl.program_id(2)
    blk = c // QPK
    q = q_ref[0, 0]
    ones = jnp.ones((LANES - HEAD_DIM, TQ), BF16)
    qas = [jnp.concatenate([q[h * HEAD_DIM:(h + 1) * HEAD_DIM, :], ones], axis=0) for h in range(FOX_HPS)]

    def qk_fn(h, n):
        rows = pl.ds(pl.multiple_of(n * TK, TK), TK)
        return _dot(k_ref[0, rows, h * LANES:(h + 1) * LANES], qas[h])

    causal = cm_ref[c % QPK]
    outs = _attend(blk + 1, lambda i: blk - i, qk_fn, None,
                   lambda h, n: v_ref[0, n, h * HEAD_DIM:(h + 1) * HEAD_DIM, :], s_scr, acc_scr, FOX_HPS,
                   first_fix=lambda h, n, s: s + causal)
    o_ref[0] = jnp.concatenate(outs, axis=0).astype(BF16)


def _fox(fm, ka, cmask, B, S):
    rows = FOX_HPS * HEAD_DIM
    return pl.pallas_call(
        _fox_body,
        grid=(B, FOX_HEADS // FOX_HPS, S // TQ),
        in_specs=[
            pl.BlockSpec((1, 1, rows, TQ), lambda b, h, c: (b, c // QPK, h, c % QPK)),
            pl.BlockSpec((1, S, FOX_HPS * LANES), lambda b, h, c: (b, 0, h), pipeline_mode=pl.Buffered(1)),
            pl.BlockSpec((1, S // TK, rows, TK), lambda b, h, c: (b, 0, FOX_HEADS // FOX_HPS + h, 0),
                         pipeline_mode=pl.Buffered(1)),
            _const_spec(cmask.shape),
        ],
        out_specs=pl.BlockSpec((1, rows, TQ), lambda b, h, c: (b, h, c)),
        out_shape=jax.ShapeDtypeStruct((B, FOX_W, S), BF16),
        scratch_shapes=[pltpu.VMEM((FOX_HPS, TK, TQ), F32), pltpu.VMEM((FOX_HPS, ACC_ROWS, TQ), F32)],
        compiler_params=_params(("parallel", "parallel", "arbitrary")),
        name="fox_attn",
    )(fm, ka, fm, cmask)


def _post_body(*refs, n_parts, final):
    o_refs = refs[:n_parts]
    h_ref, wo_ref, g_ref, w1_ref, w2_ref = refs[n_parts:n_parts + 5]
    gf_ref = refs[n_parts + 5] if final else None
    out_ref, hn_ref = refs[-2:]
    h1 = h_ref[...]
    r0 = 0
    for o_ref in o_refs:
        nf = o_ref.shape[1]
        h1 = h1 + _dot_tn(o_ref[0], wo_ref[r0:r0 + nf, :])
        r0 += nf
    out_ref[...] = h1
    hn_ref[...] = _rmsnorm(out_ref[...], g_ref[...]).astype(BF16)
    for c0 in range(0, D_FF, FF_CH):
        a = jnp.maximum(_dot(hn_ref[...], w1_ref[:, c0:c0 + FF_CH]), 0.0)
        out_ref[...] += _dot((a * a).astype(BF16), w2_ref[c0:c0 + FF_CH, :])
    if final:
        out_ref[...] = _rmsnorm(out_ref[...], gf_ref[...])


def _post(o_parts, h2, wo, g, w1, w2, gf, B, S):
    M = B * S
    nst = S // TM
    final = gf is not None
    in_specs = [pl.BlockSpec((1, o.shape[1], TM), lambda i: (i // nst, 0, i % nst)) for o in o_parts]
    in_specs += [
        pl.BlockSpec((TM, D_MODEL), lambda i: (i, 0)),
        _const_spec(wo.shape),
        _const_spec((1, D_MODEL)),
        _const_spec(w1.shape),
        _const_spec(w2.shape),
    ]
    args = list(o_parts) + [h2, wo, g, w1, w2]
    if final:
        in_specs.append(_const_spec((1, D_MODEL)))
        args.append(gf)
    return pl.pallas_call(
        functools.partial(_post_body, n_parts=len(o_parts), final=final),
        grid=(M // TM,),
        in_specs=in_specs,
        out_specs=pl.BlockSpec((TM, D_MODEL), lambda i: (i, 0)),
        out_shape=jax.ShapeDtypeStruct((M, D_MODEL), F32),
        scratch_shapes=[pltpu.VMEM((TM, D_MODEL), BF16)],
        compiler_params=_params(("parallel",)),
        name="post_final" if final else "post",
    )(*args)


def _rel_bucket(dist):
    n = jnp.maximum(dist, 0)
    max_exact = REL_BUCKETS // 2
    nf = jnp.maximum(n, 1).astype(jnp.float32)
    large = max_exact + (jnp.log(nf / max_exact) / math.log(REL_MAX_DISTANCE / max_exact)
                         * (REL_BUCKETS - max_exact)).astype(jnp.int32)
    large = jnp.minimum(large, REL_BUCKETS - 1)
    return jnp.where(n < max_exact, n, large)


def _bias_tables(rel_bias, S):
    n_heads = rel_bias.shape[1]
    table = rel_bias.T * LOG2E

    def bias_of(dist):
        tab = table.reshape((n_heads, REL_BUCKETS) + (1,) * dist.ndim)
        bkt = _rel_bucket(jnp.asarray(dist))[None]
        out = jnp.zeros((n_heads,) + dist.shape, F32)
        for b in range(REL_BUCKETS):
            out = jnp.where(bkt == b, tab[:, b], out)
        return jnp.where(jnp.asarray(dist)[None] >= 0, out, NEG_INF)

    d = (np.arange(-1, NE_BIAS)[:, None, None] * TQ + np.arange(TQ)[None, None, :] - np.arange(TB)[None, :, None])
    vals = bias_of(d)
    far = table[:, REL_BUCKETS - 1][:, None, None, None]
    tile = jnp.where(d >= 0, vals - far, NEG_INF)
    tile = jnp.concatenate([tile, jnp.zeros_like(tile[:, :1])], axis=1)
    dw = d[:NE_WIN + 2]
    twin = jnp.where((dw >= 0) & (dw < NSA_WINDOW), vals[MOBA_HEADS:, :NE_WIN + 2], NEG_INF)
    n_c = S // NSA_CMP_STRIDE
    u = np.arange(2 * n_c)[:, None]
    dc = np.arange(TQ)[None, :] + NSA_CMP_STRIDE * (u - (n_c - 1)) - (NSA_CMP_BLOCK - 1)
    fcmp = bias_of(dc)[MOBA_HEADS:]
    return tile, twin, fcmp


def _selection_constants(S):
    n_c = S // NSA_CMP_STRIDE
    n_cmp = (S - NSA_CMP_BLOCK) // NSA_CMP_STRIDE + 1
    n_sb = S // NSA_SLC_BLOCK
    ci = np.arange(n_c)[None, :] * NSA_CMP_STRIDE
    sj = np.arange(n_sb)[:, None] * NSA_SLC_BLOCK
    ovl = (ci < sj + NSA_SLC_BLOCK) & (ci + NSA_CMP_BLOCK > sj) & (np.arange(n_c)[None, :] < n_cmp)
    ovl = ovl[:, ::-1]
    return jnp.asarray(ovl, BF16)


def _causal_tiles():
    e = np.arange(QPK)[:, None, None]
    d = e * TQ + np.arange(TQ)[None, None, :] - np.arange(TK)[None, :, None]
    return jnp.asarray(np.where(d >= 0, 0.0, NEG_INF), F32)


def kernel(x, rel_bias, mix_norm, mlp_norm, even_w_in, even_w_out, cmp_pos_k, cmp_pos_v, cmp_k_w1, cmp_k_w2,
           cmp_v_w1, cmp_v_w2, odd_w_in, odd_b_forget, odd_w_out, mlp_w1, mlp_w2, final_norm):
    B, S, D = x.shape
    assert D == D_MODEL and S % TM == 0
    G, J = NSA_KV_GROUPS, NSA_HPG
    h = x.reshape(B * S, D)

    offs = np.cumsum((MOBA_W, MOBA_W, MOBA_W, NSA_W) + (NSA_KV_W,) * 6)
    mq_w, mk_w, mv_w, nq_w, kc_w, vc_w, ksl_w, vsl_w, kwn_w, vwn_w, gz_w = jnp.split(even_w_in[0], offs, axis=1)
    wrm = jnp.concatenate([mk_w, ksl_w, kwn_w], axis=1).astype(BF16)
    col_ksl, col_kwn = MOBA_W, MOBA_W + NSA_KV_W
    wcv = jnp.concatenate([kc_w, vc_w], axis=1).astype(BF16)
    qs = SCALE * LOG2E
    wfm = jnp.concatenate([mq_w * qs, mv_w, nq_w * qs, vsl_w, vwn_w], axis=1).T.astype(BF16)
    row_nq, row_vsl, row_vwn = 2 * MOBA_W, 2 * MOBA_W + NSA_W, 2 * MOBA_W + NSA_W + NSA_KV_W
    gzw = gz_w.T.reshape(G, J, 3, D).transpose(0, 2, 1, 3).reshape(G, 3 * J, D)
    gzw = jnp.pad(gzw, ((0, 0), (0, GZ_ROWS - 3 * J), (0, 0))).reshape(G * GZ_ROWS, D).astype(BF16)

    rm, fm, gz, cv = _inproj0(h, mix_norm[0][None, :], wrm, wfm, gzw, wcv, B, S)
    rm = rm.reshape(B, S, -1)

    tile, twin, tcmp = _bias_tables(rel_bias, S)
    ovl = _selection_constants(S)

    o_moba = _moba(fm, rm, tile, B, S)

    n_c = S // NSA_CMP_STRIDE
    r = cv.reshape(2 * G, B, n_c, NSA_CMP_STRIDE * HEAD_DIM)
    pos = jnp.stack([cmp_pos_k[0].reshape(1, -1), cmp_pos_v[0].reshape(1, -1)])
    pos = jnp.pad(pos, ((0, 0), (0, 7), (0, 0))).astype(BF16)
    w1c = jnp.stack([cmp_k_w1[0], cmp_v_w1[0]]).astype(BF16)
    kc, vct = _compress(r, pos, w1c, cmp_k_w2[0].astype(BF16), cmp_v_w2[0].T.astype(BF16), B, n_c)

    o_nsa = _nsa(fm, rm, gz, kc, vct, tcmp, ovl, tile, twin, B, S,
                 col_ksl, col_kwn, row_nq, row_vsl, row_vwn)

    h = _post([o_moba, o_nsa], h, even_w_out[0].astype(BF16), mlp_norm[0][None, :],
              mlp_w1[0].astype(BF16), mlp_w2[0].astype(BF16), None, B, S)

    q_w, k_w, v_w, f_w = jnp.split(odd_w_in[0], np.cumsum((FOX_W, FOX_W, FOX_W)), axis=1)
    wfm1 = jnp.concatenate([q_w * qs, v_w], axis=1).T.astype(BF16)
    wk = jnp.pad(k_w.reshape(D, FOX_HEADS, HEAD_DIM), ((0, 0), (0, 0), (0, LANES - HEAD_DIM)))
    wk = wk.reshape(D, FOX_HEADS * LANES).astype(BF16)
    f_w = jnp.pad(f_w, ((0, 0), (0, LANES - FOX_HEADS)))
    f_hi = f_w.astype(BF16)
    wf = jnp.stack([f_hi, (f_w - f_hi.astype(F32)).astype(BF16)])
    bf = jnp.pad(odd_b_forget[0], (0, LANES - FOX_HEADS))[None, :]
    tri = jnp.asarray(np.tril(np.ones((TM, TM))), BF16)
    place = np.zeros((3, LANES, FOX_HEADS * LANES), np.float32)
    for term in range(3):
        place[term, np.arange(FOX_HEADS), np.arange(FOX_HEADS) * LANES + HEAD_DIM + term] = -1.0
    fm1, ka = _inproj1(h, mix_norm[1][None, :], wfm1, wk, wf, bf, tri, jnp.asarray(place, BF16), B, S)
    o_fox = _fox(fm1, ka.reshape(B, S, -1), _causal_tiles(), B, S)

    h = _post([o_fox], h, odd_w_out[0].astype(BF16), mlp_norm[1][None, :],
              mlp_w1[1].astype(BF16), mlp_w2[1].astype(BF16), final_norm[None, :], B, S)
    return h.reshape(B, S, D)
```

```python
import functools
import math

import numpy as np
import jax
import jax.numpy as jnp
from jax import lax
from jax.experimental import pallas as pl
from jax.experimental.pallas import tpu as pltpu

D_MODEL = 1024
HEAD_DIM = 64
MOBA_HEADS = 8
MOBA_BLOCK = 256
MOBA_TOPK = 3
NSA_HEADS = 8
NSA_KV_GROUPS = 2
NSA_HPG = NSA_HEADS // NSA_KV_GROUPS
NSA_CMP_BLOCK = 32
NSA_CMP_STRIDE = 16
NSA_CMP_HIDDEN = 256
NSA_SLC_BLOCK = 64
NSA_TOPN = 16
NSA_WINDOW = 512
NSA_FORCE_SCORE = 1e6
FOX_HEADS = 16
D_FF = 4 * D_MODEL
REL_BUCKETS = 32
REL_MAX_DISTANCE = 1024
RMS_EPS = 1e-5
NEG_INF = -1e30
SCALE = HEAD_DIM ** -0.5

MOBA_W = MOBA_HEADS * HEAD_DIM
NSA_W = NSA_HEADS * HEAD_DIM
NSA_KV_W = NSA_KV_GROUPS * HEAD_DIM
FOX_W = FOX_HEADS * HEAD_DIM

LANES = 128
SUBLANES = 8
TQ = 256
TK = 256
TB = 256
QPK = TK // TQ
KSUB = TK // TB
TM = 512
CH = 256
FF_CH = 512
VMEM_LIMIT = 56 * 1024 * 1024
NE_BIAS = -(-(REL_MAX_DISTANCE + TB - 1) // TQ)
NE_WIN = -(-(NSA_WINDOW + TB - 1) // TQ)
GZ_ROWS = 16
HPS = 8
FOX_HPS = 16
SPT = TK // NSA_SLC_BLOCK
ACC_ROWS = HEAD_DIM + 16
ROW_CHUNK = 64
LOG2E = math.log2(math.e)

assert TK % TQ == 0 and TK % TB == 0 and TQ == TB and MOBA_BLOCK == TB and TB % NSA_SLC_BLOCK == 0 and SPT <= 8

F32 = jnp.float32
BF16 = jnp.bfloat16


def _dot(a, b):
    return jnp.dot(a, b, preferred_element_type=F32)


def _dot_nt(a, b):
    return lax.dot_general(a, b, (((1,), (1,)), ((), ())), preferred_element_type=F32)


def _dot_tn(a, b):
    return lax.dot_general(a, b, (((0,), (0,)), ((), ())), preferred_element_type=F32)


def _rmsnorm(x, g):
    ms = jnp.mean(x * x, axis=-1, keepdims=True)
    return x * lax.rsqrt(ms + RMS_EPS) * g


def _split3(x):
    a = x.astype(BF16)
    r = x - a.astype(F32)
    b = r.astype(BF16)
    c = (r - b.astype(F32)).astype(BF16)
    return a, b, c


def _const_spec(shape):
    nd = len(shape)
    return pl.BlockSpec(shape, lambda *_: (0,) * nd, pipeline_mode=pl.Buffered(1))


def _params(sem):
    return pltpu.CompilerParams(dimension_semantics=sem, vmem_limit_bytes=VMEM_LIMIT)


def _attend(n_tiles, tile_of, qk_fn, fix_fn, v_fn, s_scr, acc_scr, n_heads, first_fix=None):
    def put_scores(h, n, fix):
        s = qk_fn(h, n)
        s_scr[h] = s if fix is None else fix(h, n, s)

    for h in range(n_heads):
        put_scores(h, tile_of(0), fix_fn if first_fix is None else first_fix)
    acc_scr[...] = jnp.zeros_like(acc_scr)
    last = n_tiles - 1
    ones = jnp.ones((ACC_ROWS - HEAD_DIM, TK), BF16)
    chunks = range(0, TK, ROW_CHUNK)

    def body(i, ms):
        n = tile_of(i)
        n_next = tile_of(jnp.minimum(i + 1, last))
        out = []
        for h in range(n_heads):
            mx = s_scr[h, 0:ROW_CHUNK, :]
            for r0 in chunks[1:]:
                mx = jnp.maximum(mx, s_scr[h, r0:r0 + ROW_CHUNK, :])
            m_new = jnp.maximum(ms[h], jnp.max(mx, axis=0, keepdims=True))
            alpha = jnp.exp2(ms[h] - m_new)
            p = jnp.concatenate([jnp.exp2(s_scr[h, r0:r0 + ROW_CHUNK, :] - m_new).astype(BF16) for r0 in chunks],
                                axis=0)
            va = jnp.concatenate([v_fn(h, n), ones], axis=0)
            acc_scr[h] = alpha * acc_scr[h] + _dot(va, p)
            out.append(m_new)
            put_scores(h, n_next, fix_fn)
        return tuple(out)

    lax.fori_loop(0, n_tiles, body, tuple(jnp.full((1, TQ), NEG_INF, F32) for _ in range(n_heads)))
    return [acc_scr[h, :HEAD_DIM, :] / acc_scr[h, HEAD_DIM:HEAD_DIM + 1, :] for h in range(n_heads)]


def _rank_select(val, n_rows, k):
    sub = lax.broadcasted_iota(jnp.int32, (SUBLANES, val.shape[1]), 0)
    groups = [val[g0:g0 + SUBLANES, :] for g0 in range(0, val.shape[0], SUBLANES)]
    counts = [jnp.zeros(g.shape, F32) for g in groups]
    for m in range(n_rows):
        vm = val[m:m + 1, :]
        for g, vg in enumerate(groups):
            if g * SUBLANES > m:
                counts[g] = counts[g] + jnp.where(vm >= vg, 1.0, 0.0)
            elif (g + 1) * SUBLANES <= m:
                counts[g] = counts[g] + jnp.where(vm > vg, 1.0, 0.0)
            else:
                tie = jnp.where(sub > m % SUBLANES, 1.0, 0.0)
                counts[g] = counts[g] + jnp.where(vm > vg, 1.0, 0.0) + jnp.where(vm == vg, tie, 0.0)
    return jnp.concatenate(counts, axis=0) < k


def _inproj0_body(x_ref, g_ref, wrm_ref, wfm_ref, wgz_ref, wcv_ref, rm_ref, fm_ref, gz_ref, cv_ref):
    xn = _rmsnorm(x_ref[...], g_ref[...]).astype(BF16)
    cv = _dot(xn, wcv_ref[...]).astype(BF16)
    for j in range(cv_ref.shape[0]):
        cv_ref[j] = cv[:, j * HEAD_DIM:(j + 1) * HEAD_DIM]
    for c0 in range(0, rm_ref.shape[-1], CH):
        rm_ref[:, c0:c0 + CH] = _dot(xn, wrm_ref[:, c0:c0 + CH]).astype(BF16)
    for r0 in range(0, fm_ref.shape[2], CH):
        res = _dot_nt(wfm_ref[r0:r0 + CH, :], xn).astype(BF16)
        for t in range(TM // TK):
            fm_ref[0, t, r0:r0 + CH, :] = res[:, t * TK:(t + 1) * TK]
    gz_ref[0] = _dot_nt(wgz_ref[...], xn)


def _inproj0(x2, g, wrm, wfm, wgz, wcv, B, S):
    M = B * S
    nst = S // TM
    n_rm, n_fm, n_gz, n_cv = wrm.shape[1], wfm.shape[0], wgz.shape[0], wcv.shape[1] // HEAD_DIM
    return pl.pallas_call(
        _inproj0_body,
        grid=(M // TM,),
        in_specs=[
            pl.BlockSpec((TM, D_MODEL), lambda i: (i, 0)),
            _const_spec((1, D_MODEL)),
            _const_spec((D_MODEL, n_rm)),
            _const_spec((n_fm, D_MODEL)),
            _const_spec((n_gz, D_MODEL)),
            _const_spec(wcv.shape),
        ],
        out_specs=[
            pl.BlockSpec((TM, n_rm), lambda i: (i, 0)),
            pl.BlockSpec((1, TM // TK, n_fm, TK), lambda i: (i // nst, i % nst, 0, 0)),
            pl.BlockSpec((1, n_gz, TM), lambda i: (i // nst, 0, i % nst)),
            pl.BlockSpec((n_cv, TM, HEAD_DIM), lambda i: (0, i, 0)),
        ],
        out_shape=[
            jax.ShapeDtypeStruct((M, n_rm), BF16),
            jax.ShapeDtypeStruct((B, S // TK, n_fm, TK), BF16),
            jax.ShapeDtypeStruct((B, n_gz, S), F32),
            jax.ShapeDtypeStruct((n_cv, M, HEAD_DIM), BF16),
        ],
        compiler_params=_params(("parallel",)),
        name="inproj0",
    )(x2, g, wrm, wfm, wgz, wcv)


def _compress_body(rk_ref, rv_ref, pos_ref, w1_ref, w2k_ref, w2vt_ref, flip_ref, kc_ref, vct_ref):
    half = NSA_CMP_STRIDE * HEAD_DIM

    def hidden(r_ref, s):
        r = r_ref[0, 0]
        a = _dot(r, w1_ref[s, :half, :])
        b = _dot(r, w1_ref[s, half:, :])
        nxt = pltpu.roll(b, b.shape[0] - 1, axis=0)
        posb = _dot(pos_ref[s], w1_ref[s])[0:1]
        pre = a + nxt + posb
        act = (pre * jax.nn.sigmoid(pre)).astype(BF16)
        return _dot(flip_ref[...], act).astype(BF16)

    kc_ref[0, 0] = _dot(hidden(rk_ref, 0), w2k_ref[...]).astype(BF16)
    vct_ref[0, 0] = _dot_nt(w2vt_ref[...], hidden(rv_ref, 1)).astype(BF16)


def _compress(r, pos, w1, w2k, w2vt, B, NC):
    G = NSA_KV_GROUPS
    half = NSA_CMP_STRIDE * HEAD_DIM
    return pl.pallas_call(
        _compress_body,
        grid=(B, G),
        in_specs=[
            pl.BlockSpec((1, 1, NC, half), lambda b, g: (g, b, 0, 0)),
            pl.BlockSpec((1, 1, NC, half), lambda b, g: (G + g, b, 0, 0)),
            _const_spec(pos.shape),
            _const_spec(w1.shape),
            _const_spec(w2k.shape),
            _const_spec(w2vt.shape),
            _const_spec((NC, NC)),
        ],
        out_specs=[
            pl.BlockSpec((1, 1, NC, HEAD_DIM), lambda b, g: (b, g, 0, 0)),
            pl.BlockSpec((1, 1, HEAD_DIM, NC), lambda b, g: (b, g, 0, 0)),
        ],
        out_shape=[
            jax.ShapeDtypeStruct((B, G, NC, HEAD_DIM), BF16),
            jax.ShapeDtypeStruct((B, G, HEAD_DIM, NC), BF16),
        ],
        compiler_params=_params(("parallel", "parallel")),
        name="nsa_compress",
    )(r, r, pos, w1, w2k, w2vt, jnp.asarray(np.eye(NC)[::-1], BF16))


def _moba_body(q_ref, k_ref, v_ref, t_ref, o_ref, kmean_ref, mask_ref, s_scr, acc_scr, *, n_mb, topk):
    c = pl.program_id(2)
    blk = c // QPK
    qblk = c * TQ // MOBA_BLOCK

    @pl.when(c == 0)
    def _():
        kmean_ref[...] = jnp.zeros_like(kmean_ref)
        for n in range(n_mb):
            kblk = k_ref[0, n * MOBA_BLOCK:(n + 1) * MOBA_BLOCK, :].astype(F32)
            kmean_ref[n:n + 1, :] = jnp.mean(kblk, axis=0, keepdims=True)

    q = q_ref[0, 0]
    rowi = lax.broadcasted_iota(jnp.int32, (LANES, TQ), 0)
    nidx = lax.broadcasted_iota(jnp.int32, (kmean_ref.shape[0], TQ), 0)
    km = _split3(kmean_ref[...])
    qpads = []
    for h in range(HPS):
        lo = (h // 2) * LANES
        qpair = q[lo:lo + LANES, :]
        qh = jnp.where(rowi // HEAD_DIM == h % 2, qpair, jnp.zeros_like(qpair))
        route = sum(_dot(part[:, lo:lo + LANES], qh) for part in km)
        route = jnp.where(nidx < qblk, route, NEG_INF)
        sel = _rank_select(route, n_mb, topk) & (nidx < qblk)
        mask_ref[h] = jnp.where(sel | (nidx == qblk), 0.0, NEG_INF)
        qpads.append(qh)

    def qk_fn(h, n):
        rows = pl.ds(pl.multiple_of(n * TK, TK), TK)
        return _dot(k_ref[0, rows, (h // 2) * LANES:(h // 2 + 1) * LANES], qpads[h])

    def fix_fn(h, n, s):
        parts = []
        for u in range(KSUB):
            b = KSUB * n + u
            parts.append(mask_ref[h, pl.ds(b, 1), :] + t_ref[h, jnp.clip(c - b, -1, NE_BIAS) + 1])
        return s + jnp.concatenate(parts, axis=0)

    def v_fn(h, n):
        return v_ref[0, n, h * HEAD_DIM:(h + 1) * HEAD_DIM, :]

    outs = _attend(blk + 1, lambda i: blk - i, qk_fn, fix_fn, v_fn, s_scr, acc_scr, HPS)
    o_ref[0] = jnp.concatenate(outs, axis=0).astype(BF16)


def _moba(fm, rm, tab, B, S):
    n_mb = S // MOBA_BLOCK
    n_pad = -(-n_mb // 16) * 16
    ne = tab.shape[1]
    rows = HPS * HEAD_DIM
    body = functools.partial(_moba_body, n_mb=n_mb, topk=min(MOBA_TOPK, n_mb))
    return pl.pallas_call(
        body,
        grid=(B, MOBA_HEADS // HPS, S // TQ),
        in_specs=[
            pl.BlockSpec((1, 1, rows, TQ), lambda b, p, c: (b, c // QPK, p, c % QPK)),
            pl.BlockSpec((1, S, rows), lambda b, p, c: (b, 0, p)),
            pl.BlockSpec((1, S // TK, rows, TK), lambda b, p, c: (b, 0, MOBA_HEADS // HPS + p, 0)),
            pl.BlockSpec((HPS, ne, TB, TQ), lambda b, p, c: (p, 0, 0, 0), pipeline_mode=pl.Buffered(1)),
        ],
        out_specs=pl.BlockSpec((1, rows, TQ), lambda b, p, c: (b, p, c)),
        out_shape=jax.ShapeDtypeStruct((B, MOBA_W, S), BF16),
        scratch_shapes=[pltpu.VMEM((n_pad, rows), F32), pltpu.VMEM((HPS, n_pad, TQ), F32),
                        pltpu.VMEM((HPS, TK, TQ), F32), pltpu.VMEM((HPS, ACC_ROWS, TQ), F32)],
        compiler_params=_params(("parallel", "parallel", "arbitrary")),
        name="moba_attn",
    )(fm, rm, fm, tab)


def _nsa_body(q_ref, kc_ref, vct_ref, fc_ref, ovl_ref, ksl_ref, vsl_ref, kwn_ref, vwn_ref,
              tslc_ref, twin_ref, gz_ref, o_ref, s_scr, acc_scr, s_win, acc_win, sel_ref, *, n_sb, n_sel):
    c = pl.program_id(1)
    blk = c // QPK
    G, J = NSA_KV_GROUPS, NSA_HPG
    H = G * J

    q = q_ref[0, 0]
    qs = [q[h * HEAD_DIM:(h + 1) * HEAD_DIM, :] for h in range(H)]
    zero = jnp.zeros((HEAD_DIM, TQ), BF16)
    qpads = [jnp.concatenate([zero] * (h // J) + [qs[h]] + [zero] * (G - 1 - h // J), axis=0) for h in range(H)]

    c0 = pl.multiple_of(c * (TQ // NSA_CMP_STRIDE), TQ // NSA_CMP_STRIDE)
    o_cmp = []
    for g in range(G):
        kc = kc_ref[0, g]
        vct = vct_ref[0, g]
        psum = jnp.zeros((kc.shape[0], TQ), F32)
        for h in range(g * J, (g + 1) * J):
            s = _dot(kc, qs[h]) + fc_ref[h, pl.ds(c0, kc.shape[0]), :]
            m = jnp.max(s, axis=0, keepdims=True)
            p = jnp.exp2(s - m)
            l = jnp.sum(p, axis=0, keepdims=True)
            pn = p * jnp.where(m > 0.5 * NEG_INF, 1.0 / l, 0.0)
            o_cmp.append(_dot(vct, pn.astype(BF16)))
            psum = psum + pn

        ph = psum.astype(BF16)
        plo = (psum - ph.astype(F32)).astype(BF16)
        imp = _dot(ovl_ref[...], ph) + _dot(ovl_ref[...], plo)
        jb = lax.broadcasted_iota(jnp.int32, imp.shape, 0)
        t = c * TQ + lax.broadcasted_iota(jnp.int32, imp.shape, 1)
        sb = t // NSA_SLC_BLOCK
        forced = (jb == 0) | (jb == sb) | (jb == sb - 1)
        allowed = jb <= sb
        val = jnp.where(forced, imp + NSA_FORCE_SCORE, jnp.where(allowed, imp, NEG_INF))
        sel = _rank_select(val, n_sb, n_sel) & allowed
        selb = jnp.where(sel, 0.0, NEG_INF)
        for n in range(n_sb // SPT):
            slab = selb[n * SPT:(n + 1) * SPT, :]
            if SPT < sel_ref.shape[2]:
                slab = jnp.concatenate([slab, jnp.zeros((sel_ref.shape[2] - SPT, TQ), F32)], axis=0)
            sel_ref[g, n] = slab

    def slc_qk(h, n):
        return _dot(ksl_ref[0, pl.ds(pl.multiple_of(n * TK, TK), TK), :], qpads[h])

    def slc_fix(h, n, s):
        rows = sel_ref[h // J, n]
        mask = jnp.concatenate([jnp.broadcast_to(rows[b:b + 1, :], (NSA_SLC_BLOCK, TQ)) for b in range(SPT)], axis=0)
        bias = jnp.concatenate([tslc_ref[h, jnp.clip(c - (KSUB * n + u), -1, NE_BIAS) + 1] for u in range(KSUB)],
                               axis=0)
        return s + mask + bias

    def slc_v(h, n):
        return vsl_ref[0, n, (h // J) * HEAD_DIM:(h // J + 1) * HEAD_DIM, :]

    o_slc = _attend(blk + 1, lambda i: blk - i, slc_qk, slc_fix, slc_v, s_scr, acc_scr, H)

    def win_qk(h, n):
        return _dot(kwn_ref[0, pl.ds(pl.multiple_of(n * TK, TK), TK), :], qpads[h])

    def win_fix(h, n, s):
        return s + jnp.concatenate([twin_ref[h, jnp.clip(c - (KSUB * n + u), -1, NE_WIN) + 1] for u in range(KSUB)],
                                   axis=0)

    def win_v(h, n):
        return vwn_ref[0, n, (h // J) * HEAD_DIM:(h // J + 1) * HEAD_DIM, :]

    w_lo = jnp.maximum(c - NE_WIN + 1, 0) // KSUB
    o_win = _attend(blk - w_lo + 1, lambda i: blk - i, win_qk, win_fix, win_v, s_win, acc_win, H)

    gate = jax.nn.sigmoid(gz_ref[0])
    outs = []
    for h in range(H):
        r = (h // J) * GZ_ROWS + h % J
        outs.append(gate[r:r + 1, :] * o_cmp[h] + gate[r + J:r + J + 1, :] * o_slc[h]
                    + gate[r + 2 * J:r + 2 * J + 1, :] * o_win[h])
    o_ref[0] = jnp.concatenate(outs, axis=0).astype(BF16)


def _nsa(fm, rm, gz, kc, vct, tcmp, ovl, tslc, twin, B, S, col_ksl, col_kwn, row_q, row_vsl, row_vwn):
    G, J = NSA_KV_GROUPS, NSA_HPG
    NC = kc.shape[2]
    n_sb = S // NSA_SLC_BLOCK
    body = functools.partial(_nsa_body, n_sb=n_sb, n_sel=min(NSA_TOPN, n_sb))
    H = G * J
    kvrows = G * HEAD_DIM
    one = pl.Buffered(1)
    return pl.pallas_call(
        body,
        grid=(B, S // TQ),
        in_specs=[
            pl.BlockSpec((1, 1, NSA_W, TQ), lambda b, c: (b, c // QPK, row_q // NSA_W, c % QPK)),
            pl.BlockSpec((1, G, NC, HEAD_DIM), lambda b, c: (b, 0, 0, 0)),
            pl.BlockSpec((1, G, HEAD_DIM, NC), lambda b, c: (b, 0, 0, 0)),
            _const_spec(tcmp.shape),
            _const_spec(ovl.shape),
            pl.BlockSpec((1, S, LANES), lambda b, c: (b, 0, col_ksl // LANES)),
            pl.BlockSpec((1, S // TK, kvrows, TK), lambda b, c: (b, 0, row_vsl // kvrows, 0)),
            pl.BlockSpec((1, S, LANES), lambda b, c: (b, 0, col_kwn // LANES)),
            pl.BlockSpec((1, S // TK, kvrows, TK), lambda b, c: (b, 0, row_vwn // kvrows, 0)),
            pl.BlockSpec((H, tslc.shape[1], TB, TQ), lambda b, c: (MOBA_HEADS // H, 0, 0, 0), pipeline_mode=one),
            _const_spec(twin.shape),
            pl.BlockSpec((1, G * GZ_ROWS, TQ), lambda b, c: (b, 0, c)),
        ],
        out_specs=pl.BlockSpec((1, NSA_W, TQ), lambda b, c: (b, 0, c)),
        out_shape=jax.ShapeDtypeStruct((B, NSA_W, S), BF16),
        scratch_shapes=[pltpu.VMEM((H, TK, TQ), F32), pltpu.VMEM((H, ACC_ROWS, TQ), F32),
                        pltpu.VMEM((H, TK, TQ), F32), pltpu.VMEM((H, ACC_ROWS, TQ), F32),
                        pltpu.VMEM((G, S // TK, 8, TQ), F32)],
        compiler_params=_params(("parallel", "arbitrary")),
        name="nsa_attn",
    )(fm, kc, vct, tcmp, ovl, rm, fm, rm, fm, tslc, twin, gz)


def _inproj1_body(x_ref, g_ref, wfm_ref, wk_ref, wf_ref, bf_ref, tri_ref, place_ref, fm_ref, ka_ref, carry_ref,
                  *, nst):
    i = pl.program_id(0)

    @pl.when(i % nst == 0)
    def _():
        carry_ref[...] = jnp.zeros_like(carry_ref)

    xf = _rmsnorm(x_ref[...], g_ref[...])
    xn = xf.astype(BF16)
    xlo = (xf - xn.astype(F32)).astype(BF16)
    for r0 in range(0, fm_ref.shape[2], CH):
        res = _dot_nt(wfm_ref[r0:r0 + CH, :], xn).astype(BF16)
        for t in range(TM // TK):
            fm_ref[0, t, r0:r0 + CH, :] = res[:, t * TK:(t + 1) * TK]

    fz = _dot(xn, wf_ref[0]) + _dot(xlo, wf_ref[0]) + _dot(xn, wf_ref[1]) + bf_ref[...]
    logf = jnp.minimum(fz, 0.0) - jnp.log(1.0 + jnp.exp(-jnp.abs(fz)))
    tri = tri_ref[...]
    h1, h2, h3 = _split3(logf)
    cum = _dot(tri, h1) + _dot(tri, h2) + _dot(tri, h3) + carry_ref[0:1, :]
    carry_ref[...] = jnp.broadcast_to(cum[TM - 1:TM, :], carry_ref.shape)
    cc = jnp.concatenate(_split3(cum * LOG2E), axis=1)
    low = lax.broadcasted_iota(jnp.int32, (TM, LANES), 1) < HEAD_DIM
    for c0 in range(0, wk_ref.shape[-1], FF_CH):
        kp = _dot(xn, wk_ref[:, c0:c0 + FF_CH])
        dp = _dot(cc, place_ref[:, c0:c0 + FF_CH])
        for t0 in range(0, FF_CH, LANES):
            j = (c0 + t0) // LANES
            kt, dt = kp[:, t0:t0 + LANES], dp[:, t0:t0 + LANES]
            ka_ref[:, 2 * j * LANES:(2 * j + 1) * LANES] = jnp.where(low, kt, dt).astype(BF16)
            ka_ref[:, (2 * j + 1) * LANES:(2 * j + 2) * LANES] = jnp.where(low, dt, kt).astype(BF16)


def _inproj1(x2, g, wfm, wk, wf, bf, tri, place, B, S):
    M = B * S
    nst = S // TM
    n_fm, n_ka = wfm.shape[0], FOX_HEADS * LANES
    return pl.pallas_call(
        functools.partial(_inproj1_body, nst=nst),
        grid=(M // TM,),
        in_specs=[
            pl.BlockSpec((TM, D_MODEL), lambda i: (i, 0)),
            _const_spec((1, D_MODEL)),
            _const_spec(wfm.shape),
            _const_spec(wk.shape),
            _const_spec(wf.shape),
            _const_spec(bf.shape),
            _const_spec(tri.shape),
            _const_spec(place.shape),
        ],
        out_specs=[
            pl.BlockSpec((1, TM // TK, n_fm, TK), lambda i: (i // nst, i % nst, 0, 0)),
            pl.BlockSpec((TM, n_ka), lambda i: (i, 0)),
        ],
        out_shape=[
            jax.ShapeDtypeStruct((B, S // TK, n_fm, TK), BF16),
            jax.ShapeDtypeStruct((M, n_ka), BF16),
        ],
        scratch_shapes=[pltpu.VMEM((8, LANES), F32)],
        compiler_params=_params(("arbitrary",)),
        name="inproj1",
    )(x2, g, wfm, wk, wf, bf, tri, place)


def _fox_body(q_ref, k_ref, v_ref, cm_ref, o_ref, s_scr, acc_scr):
    c = pl.program_id(2)
    blk = c // QPK
    q = q_ref[0, 0]
    ones = jnp.ones((LANES - HEAD_DIM, TQ), BF16)
    qhs = [q[h * HEAD_DIM:(h + 1) * HEAD_DIM, :] for h in range(FOX_HPS)]
    qas = [jnp.concatenate([qhs[h], ones] if h % 2 == 0 else [ones, qhs[h]], axis=0) for h in range(FOX_HPS)]

    def qk_fn(h, n):
        rows = pl.ds(pl.multiple_of(n * TK, TK), TK)
        return _dot(k_ref[0, rows, h * LANES:(h + 1) * LANES], qas[h])

    causal = cm_ref[c % QPK]
    outs = _attend(blk + 1, lambda i: blk - i, qk_fn, None,
                   lambda h, n: v_ref[0, n, h * HEAD_DIM:(h + 1) * HEAD_DIM, :], s_scr, acc_scr, FOX_HPS,
                   first_fix=lambda h, n, s: s + causal)
    o_ref[0] = jnp.concatenate(outs, axis=0).astype(BF16)


def _fox(fm, ka, cmask, B, S):
    rows = FOX_HPS * HEAD_DIM
    return pl.pallas_call(
        _fox_body,
        grid=(B, FOX_HEADS // FOX_HPS, S // TQ),
        in_specs=[
            pl.BlockSpec((1, 1, rows, TQ), lambda b, h, c: (b, c // QPK, h, c % QPK)),
            pl.BlockSpec((1, S, FOX_HPS * LANES), lambda b, h, c: (b, 0, h), pipeline_mode=pl.Buffered(1)),
            pl.BlockSpec((1, S // TK, rows, TK), lambda b, h, c: (b, 0, FOX_HEADS // FOX_HPS + h, 0),
                         pipeline_mode=pl.Buffered(1)),
            _const_spec(cmask.shape),
        ],
        out_specs=pl.BlockSpec((1, rows, TQ), lambda b, h, c: (b, h, c)),
        out_shape=jax.ShapeDtypeStruct((B, FOX_W, S), BF16),
        scratch_shapes=[pltpu.VMEM((FOX_HPS, TK, TQ), F32), pltpu.VMEM((FOX_HPS, ACC_ROWS, TQ), F32)],
        compiler_params=_params(("parallel", "parallel", "arbitrary")),
        name="fox_attn",
    )(fm, ka, fm, cmask)


def _post_body(*refs, n_parts, final):
    o_refs = refs[:n_parts]
    h_ref, wo_ref, g_ref, w1_ref, w2_ref = refs[n_parts:n_parts + 5]
    gf_ref = refs[n_parts + 5] if final else None
    out_ref, hn_ref = refs[-2:]
    h1 = h_ref[...]
    r0 = 0
    for o_ref in o_refs:
        nf = o_ref.shape[1]
        h1 = h1 + _dot_tn(o_ref[0], wo_ref[r0:r0 + nf, :])
        r0 += nf
    out_ref[...] = h1
    hn_ref[...] = _rmsnorm(out_ref[...], g_ref[...]).astype(BF16)
    for c0 in range(0, D_FF, FF_CH):
        a = jnp.maximum(_dot(hn_ref[...], w1_ref[:, c0:c0 + FF_CH]), 0.0)
        out_ref[...] += _dot((a * a).astype(BF16), w2_ref[c0:c0 + FF_CH, :])
    if final:
        out_ref[...] = _rmsnorm(out_ref[...], gf_ref[...])


def _post(o_parts, h2, wo, g, w1, w2, gf, B, S):
    M = B * S
    nst = S // TM
    final = gf is not None
    in_specs = [pl.BlockSpec((1, o.shape[1], TM), lambda i: (i // nst, 0, i % nst)) for o in o_parts]
    in_specs += [
        pl.BlockSpec((TM, D_MODEL), lambda i: (i, 0)),
        _const_spec(wo.shape),
        _const_spec((1, D_MODEL)),
        _const_spec(w1.shape),
        _const_spec(w2.shape),
    ]
    args = list(o_parts) + [h2, wo, g, w1, w2]
    if final:
        in_specs.append(_const_spec((1, D_MODEL)))
        args.append(gf)
    return pl.pallas_call(
        functools.partial(_post_body, n_parts=len(o_parts), final=final),
        grid=(M // TM,),
        in_specs=in_specs,
        out_specs=pl.BlockSpec((TM, D_MODEL), lambda i: (i, 0)),
        out_shape=jax.ShapeDtypeStruct((M, D_MODEL), F32),
        scratch_shapes=[pltpu.VMEM((TM, D_MODEL), BF16)],
        compiler_params=_params(("parallel",)),
        name="post_final" if final else "post",
    )(*args)


def _rel_bucket(dist):
    n = jnp.maximum(dist, 0)
    max_exact = REL_BUCKETS // 2
    nf = jnp.maximum(n, 1).astype(jnp.float32)
    large = max_exact + (jnp.log(nf / max_exact) / math.log(REL_MAX_DISTANCE / max_exact)
                         * (REL_BUCKETS - max_exact)).astype(jnp.int32)
    large = jnp.minimum(large, REL_BUCKETS - 1)
    return jnp.where(n < max_exact, n, large)


def _bias_tables(rel_bias, S):
    n_heads = rel_bias.shape[1]
    table = rel_bias.T * LOG2E

    def bias_of(dist):
        tab = table.reshape((n_heads, REL_BUCKETS) + (1,) * dist.ndim)
        bkt = _rel_bucket(jnp.asarray(dist))[None]
        out = jnp.zeros((n_heads,) + dist.shape, F32)
        for b in range(REL_BUCKETS):
            out = jnp.where(bkt == b, tab[:, b], out)
        return jnp.where(jnp.asarray(dist)[None] >= 0, out, NEG_INF)

    d = (np.arange(-1, NE_BIAS)[:, None, None] * TQ + np.arange(TQ)[None, None, :] - np.arange(TB)[None, :, None])
    vals = bias_of(d)
    far = table[:, REL_BUCKETS - 1][:, None, None, None]
    tile = jnp.where(d >= 0, vals - far, NEG_INF)
    tile = jnp.concatenate([tile, jnp.zeros_like(tile[:, :1])], axis=1)
    dw = d[:NE_WIN + 2]
    twin = jnp.where((dw >= 0) & (dw < NSA_WINDOW), vals[MOBA_HEADS:, :NE_WIN + 2], NEG_INF)
    n_c = S // NSA_CMP_STRIDE
    u = np.arange(2 * n_c)[:, None]
    dc = np.arange(TQ)[None, :] + NSA_CMP_STRIDE * (u - (n_c - 1)) - (NSA_CMP_BLOCK - 1)
    fcmp = bias_of(dc)[MOBA_HEADS:]
    return tile, twin, fcmp


def _selection_constants(S):
    n_c = S // NSA_CMP_STRIDE
    n_cmp = (S - NSA_CMP_BLOCK) // NSA_CMP_STRIDE + 1
    n_sb = S // NSA_SLC_BLOCK
    ci = np.arange(n_c)[None, :] * NSA_CMP_STRIDE
    sj = np.arange(n_sb)[:, None] * NSA_SLC_BLOCK
    ovl = (ci < sj + NSA_SLC_BLOCK) & (ci + NSA_CMP_BLOCK > sj) & (np.arange(n_c)[None, :] < n_cmp)
    ovl = ovl[:, ::-1]
    return jnp.asarray(ovl, BF16)


def _causal_tiles():
    e = np.arange(QPK)[:, None, None]
    d = e * TQ + np.arange(TQ)[None, None, :] - np.arange(TK)[None, :, None]
    return jnp.asarray(np.where(d >= 0, 0.0, NEG_INF), F32)


def kernel(x, rel_bias, mix_norm, mlp_norm, even_w_in, even_w_out, cmp_pos_k, cmp_pos_v, cmp_k_w1, cmp_k_w2,
           cmp_v_w1, cmp_v_w2, odd_w_in, odd_b_forget, odd_w_out, mlp_w1, mlp_w2, final_norm):
    B, S, D = x.shape
    assert D == D_MODEL and S % TM == 0
    G, J = NSA_KV_GROUPS, NSA_HPG
    h = x.reshape(B * S, D)

    offs = np.cumsum((MOBA_W, MOBA_W, MOBA_W, NSA_W) + (NSA_KV_W,) * 6)
    mq_w, mk_w, mv_w, nq_w, kc_w, vc_w, ksl_w, vsl_w, kwn_w, vwn_w, gz_w = jnp.split(even_w_in[0], offs, axis=1)
    wrm = jnp.concatenate([mk_w, ksl_w, kwn_w], axis=1).astype(BF16)
    col_ksl, col_kwn = MOBA_W, MOBA_W + NSA_KV_W
    wcv = jnp.concatenate([kc_w, vc_w], axis=1).astype(BF16)
    qs = SCALE * LOG2E
    wfm = jnp.concatenate([mq_w * qs, mv_w, nq_w * qs, vsl_w, vwn_w], axis=1).T.astype(BF16)
    row_nq, row_vsl, row_vwn = 2 * MOBA_W, 2 * MOBA_W + NSA_W, 2 * MOBA_W + NSA_W + NSA_KV_W
    gzw = gz_w.T.reshape(G, J, 3, D).transpose(0, 2, 1, 3).reshape(G, 3 * J, D)
    gzw = jnp.pad(gzw, ((0, 0), (0, GZ_ROWS - 3 * J), (0, 0))).reshape(G * GZ_ROWS, D).astype(BF16)

    rm, fm, gz, cv = _inproj0(h, mix_norm[0][None, :], wrm, wfm, gzw, wcv, B, S)
    rm = rm.reshape(B, S, -1)

    tile, twin, tcmp = _bias_tables(rel_bias, S)
    ovl = _selection_constants(S)

    o_moba = _moba(fm, rm, tile, B, S)

    n_c = S // NSA_CMP_STRIDE
    r = cv.reshape(2 * G, B, n_c, NSA_CMP_STRIDE * HEAD_DIM)
    pos = jnp.stack([cmp_pos_k[0].reshape(1, -1), cmp_pos_v[0].reshape(1, -1)])
    pos = jnp.pad(pos, ((0, 0), (0, 7), (0, 0))).astype(BF16)
    w1c = jnp.stack([cmp_k_w1[0], cmp_v_w1[0]]).astype(BF16)
    kc, vct = _compress(r, pos, w1c, cmp_k_w2[0].astype(BF16), cmp_v_w2[0].T.astype(BF16), B, n_c)

    o_nsa = _nsa(fm, rm, gz, kc, vct, tcmp, ovl, tile, twin, B, S,
                 col_ksl, col_kwn, row_nq, row_vsl, row_vwn)

    h = _post([o_moba, o_nsa], h, even_w_out[0].astype(BF16), mlp_norm[0][None, :],
              mlp_w1[0].astype(BF16), mlp_w2[0].astype(BF16), None, B, S)

    q_w, k_w, v_w, f_w = jnp.split(odd_w_in[0], np.cumsum((FOX_W, FOX_W, FOX_W)), axis=1)
    wfm1 = jnp.concatenate([q_w * qs, v_w], axis=1).T.astype(BF16)
    wk = k_w.astype(BF16)
    f_w = jnp.pad(f_w, ((0, 0), (0, LANES - FOX_HEADS)))
    f_hi = f_w.astype(BF16)
    wf = jnp.stack([f_hi, (f_w - f_hi.astype(F32)).astype(BF16)])
    bf = jnp.pad(odd_b_forget[0], (0, LANES - FOX_HEADS))[None, :]
    tri = jnp.asarray(np.tril(np.ones((TM, TM))), BF16)
    place = np.zeros((3 * LANES, FOX_W), np.float32)
    heads = np.arange(FOX_HEADS)
    for term in range(3):
        place[term * LANES + heads, (heads // 2) * LANES + (1 - heads % 2) * HEAD_DIM + term] = -1.0
    fm1, ka = _inproj1(h, mix_norm[1][None, :], wfm1, wk, wf, bf, tri, jnp.asarray(place, BF16), B, S)
    o_fox = _fox(fm1, ka.reshape(B, S, -1), _causal_tiles(), B, S)

    h = _post([o_fox], h, odd_w_out[0].astype(BF16), mlp_norm[1][None, :],
              mlp_w1[1].astype(BF16), mlp_w2[1].astype(BF16), final_norm[None, :], B, S)
    return h.reshape(B, S, D)
```

```python
import functools
import math

import numpy as np
import jax
import jax.numpy as jnp
from jax import lax
from jax.experimental import pallas as pl
from jax.experimental.pallas import tpu as pltpu

D_MODEL = 1024
HEAD_DIM = 64
MOBA_HEADS = 8
MOBA_BLOCK = 256
MOBA_TOPK = 3
NSA_HEADS = 8
NSA_KV_GROUPS = 2
NSA_HPG = NSA_HEADS // NSA_KV_GROUPS
NSA_CMP_BLOCK = 32
NSA_CMP_STRIDE = 16
NSA_CMP_HIDDEN = 256
NSA_SLC_BLOCK = 64
NSA_TOPN = 16
NSA_WINDOW = 512
NSA_FORCE_SCORE = 1e6
FOX_HEADS = 16
D_FF = 4 * D_MODEL
REL_BUCKETS = 32
REL_MAX_DISTANCE = 1024
RMS_EPS = 1e-5
NEG_INF = -1e30
SCALE = HEAD_DIM ** -0.5

MOBA_W = MOBA_HEADS * HEAD_DIM
NSA_W = NSA_HEADS * HEAD_DIM
NSA_KV_W = NSA_KV_GROUPS * HEAD_DIM
FOX_W = FOX_HEADS * HEAD_DIM

LANES = 128
SUBLANES = 8
TQ = 256
TK = 256
TB = 256
QPK = TK // TQ
KSUB = TK // TB
TM = 512
CH = 256
FF_CH = 512
VMEM_LIMIT = 56 * 1024 * 1024
NE_BIAS = -(-(REL_MAX_DISTANCE + TB - 1) // TQ)
NE_WIN = -(-(NSA_WINDOW + TB - 1) // TQ)
GZ_ROWS = 16
HPS = 8
FOX_HPS = 16
SPT = TK // NSA_SLC_BLOCK
ACC_ROWS = HEAD_DIM + 16
ROW_CHUNK = 64
LOG2E = math.log2(math.e)

assert TK % TQ == 0 and TK % TB == 0 and TQ == TB and MOBA_BLOCK == TB and TB % NSA_SLC_BLOCK == 0 and SPT <= 8

F32 = jnp.float32
BF16 = jnp.bfloat16


def _dot(a, b):
    return jnp.dot(a, b, preferred_element_type=F32)


def _dot_nt(a, b):
    return lax.dot_general(a, b, (((1,), (1,)), ((), ())), preferred_element_type=F32)


def _dot_tn(a, b):
    return lax.dot_general(a, b, (((0,), (0,)), ((), ())), preferred_element_type=F32)


def _rmsnorm(x, g):
    ms = jnp.mean(x * x, axis=-1, keepdims=True)
    return x * lax.rsqrt(ms + RMS_EPS) * g


def _split3(x):
    a = x.astype(BF16)
    r = x - a.astype(F32)
    b = r.astype(BF16)
    c = (r - b.astype(F32)).astype(BF16)
    return a, b, c


def _const_spec(shape):
    nd = len(shape)
    return pl.BlockSpec(shape, lambda *_: (0,) * nd, pipeline_mode=pl.Buffered(1))


def _params(sem):
    return pltpu.CompilerParams(dimension_semantics=sem, vmem_limit_bytes=VMEM_LIMIT)


def _attend(n_tiles, tile_of, qk_fn, fix_fn, v_fn, s_scr, acc_scr, n_heads, first_fix=None):
    def put_scores(h, n, fix):
        s = qk_fn(h, n)
        s_scr[h] = s if fix is None else fix(h, n, s)

    for h in range(n_heads):
        put_scores(h, tile_of(0), fix_fn if first_fix is None else first_fix)
    acc_scr[...] = jnp.zeros_like(acc_scr)
    last = n_tiles - 1
    ones = jnp.ones((ACC_ROWS - HEAD_DIM, TK), BF16)
    chunks = range(0, TK, ROW_CHUNK)

    def body(i, ms):
        n = tile_of(i)
        n_next = tile_of(jnp.minimum(i + 1, last))
        out = []
        for h in range(n_heads):
            mx = s_scr[h, 0:ROW_CHUNK, :]
            for r0 in chunks[1:]:
                mx = jnp.maximum(mx, s_scr[h, r0:r0 + ROW_CHUNK, :])
            m_new = jnp.maximum(ms[h], jnp.max(mx, axis=0, keepdims=True))
            alpha = jnp.exp2(ms[h] - m_new)
            p = jnp.concatenate([jnp.exp2(s_scr[h, r0:r0 + ROW_CHUNK, :] - m_new).astype(BF16) for r0 in chunks],
                                axis=0)
            va = jnp.concatenate([v_fn(h, n), ones], axis=0)
            acc_scr[h] = alpha * acc_scr[h] + _dot(va, p)
            out.append(m_new)
            put_scores(h, n_next, fix_fn)
        return tuple(out)

    lax.fori_loop(0, n_tiles, body, tuple(jnp.full((1, TQ), NEG_INF, F32) for _ in range(n_heads)))
    return [acc_scr[h, :HEAD_DIM, :] / acc_scr[h, HEAD_DIM:HEAD_DIM + 1, :] for h in range(n_heads)]


def _rank_select(val, n_rows, k):
    sub = lax.broadcasted_iota(jnp.int32, (SUBLANES, val.shape[1]), 0)
    groups = [val[g0:g0 + SUBLANES, :] for g0 in range(0, val.shape[0], SUBLANES)]
    counts = [jnp.zeros(g.shape, F32) for g in groups]
    for m in range(n_rows):
        vm = val[m:m + 1, :]
        for g, vg in enumerate(groups):
            if g * SUBLANES > m:
                counts[g] = counts[g] + jnp.where(vm >= vg, 1.0, 0.0)
            elif (g + 1) * SUBLANES <= m:
                counts[g] = counts[g] + jnp.where(vm > vg, 1.0, 0.0)
            else:
                tie = jnp.where(sub > m % SUBLANES, 1.0, 0.0)
                counts[g] = counts[g] + jnp.where(vm > vg, 1.0, 0.0) + jnp.where(vm == vg, tie, 0.0)
    return jnp.concatenate(counts, axis=0) < k


def _inproj0_body(x_ref, g_ref, wrm_ref, wfm_ref, wgz_ref, wcv_ref, rm_ref, fm_ref, gz_ref, cv_ref):
    xn = _rmsnorm(x_ref[...], g_ref[...]).astype(BF16)
    cv = _dot(xn, wcv_ref[...]).astype(BF16)
    for j in range(cv_ref.shape[0]):
        cv_ref[j] = cv[:, j * HEAD_DIM:(j + 1) * HEAD_DIM]
    for c0 in range(0, rm_ref.shape[-1], CH):
        rm_ref[:, c0:c0 + CH] = _dot(xn, wrm_ref[:, c0:c0 + CH]).astype(BF16)
    for r0 in range(0, fm_ref.shape[2], CH):
        res = _dot_nt(wfm_ref[r0:r0 + CH, :], xn).astype(BF16)
        for t in range(TM // TK):
            fm_ref[0, t, r0:r0 + CH, :] = res[:, t * TK:(t + 1) * TK]
    gz_ref[0] = _dot_nt(wgz_ref[...], xn)


def _inproj0(x2, g, wrm, wfm, wgz, wcv, B, S):
    M = B * S
    nst = S // TM
    n_rm, n_fm, n_gz, n_cv = wrm.shape[1], wfm.shape[0], wgz.shape[0], wcv.shape[1] // HEAD_DIM
    return pl.pallas_call(
        _inproj0_body,
        grid=(M // TM,),
        in_specs=[
            pl.BlockSpec((TM, D_MODEL), lambda i: (i, 0)),
            _const_spec((1, D_MODEL)),
            _const_spec((D_MODEL, n_rm)),
            _const_spec((n_fm, D_MODEL)),
            _const_spec((n_gz, D_MODEL)),
            _const_spec(wcv.shape),
        ],
        out_specs=[
            pl.BlockSpec((TM, n_rm), lambda i: (i, 0)),
            pl.BlockSpec((1, TM // TK, n_fm, TK), lambda i: (i // nst, i % nst, 0, 0)),
            pl.BlockSpec((1, n_gz, TM), lambda i: (i // nst, 0, i % nst)),
            pl.BlockSpec((n_cv, TM, HEAD_DIM), lambda i: (0, i, 0)),
        ],
        out_shape=[
            jax.ShapeDtypeStruct((M, n_rm), BF16),
            jax.ShapeDtypeStruct((B, S // TK, n_fm, TK), BF16),
            jax.ShapeDtypeStruct((B, n_gz, S), F32),
            jax.ShapeDtypeStruct((n_cv, M, HEAD_DIM), BF16),
        ],
        compiler_params=_params(("parallel",)),
        name="inproj0",
    )(x2, g, wrm, wfm, wgz, wcv)


def _compress_body(rk_ref, rv_ref, pos_ref, w1_ref, w2k_ref, w2vt_ref, flip_ref, kc_ref, vct_ref):
    half = NSA_CMP_STRIDE * HEAD_DIM

    def hidden(r_ref, s):
        r = r_ref[0, 0]
        a = _dot(r, w1_ref[s, :half, :])
        b = _dot(r, w1_ref[s, half:, :])
        nxt = pltpu.roll(b, b.shape[0] - 1, axis=0)
        posb = _dot(pos_ref[s], w1_ref[s])[0:1]
        pre = a + nxt + posb
        act = (pre * jax.nn.sigmoid(pre)).astype(BF16)
        return _dot(flip_ref[...], act).astype(BF16)

    kc_ref[0, 0] = _dot(hidden(rk_ref, 0), w2k_ref[...]).astype(BF16)
    vct_ref[0, 0] = _dot_nt(w2vt_ref[...], hidden(rv_ref, 1)).astype(BF16)


def _compress(r, pos, w1, w2k, w2vt, B, NC):
    G = NSA_KV_GROUPS
    half = NSA_CMP_STRIDE * HEAD_DIM
    return pl.pallas_call(
        _compress_body,
        grid=(B, G),
        in_specs=[
            pl.BlockSpec((1, 1, NC, half), lambda b, g: (g, b, 0, 0)),
            pl.BlockSpec((1, 1, NC, half), lambda b, g: (G + g, b, 0, 0)),
            _const_spec(pos.shape),
            _const_spec(w1.shape),
            _const_spec(w2k.shape),
            _const_spec(w2vt.shape),
            _const_spec((NC, NC)),
        ],
        out_specs=[
            pl.BlockSpec((1, 1, NC, HEAD_DIM), lambda b, g: (b, g, 0, 0)),
            pl.BlockSpec((1, 1, HEAD_DIM, NC), lambda b, g: (b, g, 0, 0)),
        ],
        out_shape=[
            jax.ShapeDtypeStruct((B, G, NC, HEAD_DIM), BF16),
            jax.ShapeDtypeStruct((B, G, HEAD_DIM, NC), BF16),
        ],
        compiler_params=_params(("parallel", "parallel")),
        name="nsa_compress",
    )(r, r, pos, w1, w2k, w2vt, jnp.asarray(np.eye(NC)[::-1], BF16))


def _moba_body(q_ref, k_ref, v_ref, t_ref, o_ref, kmean_ref, mask_ref, s_scr, acc_scr, *, n_mb, topk):
    c = pl.program_id(2)
    blk = c // QPK
    qblk = c * TQ // MOBA_BLOCK

    @pl.when(c == 0)
    def _():
        kmean_ref[...] = jnp.zeros_like(kmean_ref)
        for n in range(n_mb):
            kblk = k_ref[0, n * MOBA_BLOCK:(n + 1) * MOBA_BLOCK, :].astype(F32)
            kmean_ref[n:n + 1, :] = jnp.mean(kblk, axis=0, keepdims=True)

    q = q_ref[0, 0]
    rowi = lax.broadcasted_iota(jnp.int32, (LANES, TQ), 0)
    nidx = lax.broadcasted_iota(jnp.int32, (kmean_ref.shape[0], TQ), 0)
    km = _split3(kmean_ref[...])
    qpads = []
    for h in range(HPS):
        lo = (h // 2) * LANES
        qpair = q[lo:lo + LANES, :]
        qh = jnp.where(rowi // HEAD_DIM == h % 2, qpair, jnp.zeros_like(qpair))
        route = sum(_dot(part[:, lo:lo + LANES], qh) for part in km)
        route = jnp.where(nidx < qblk, route, NEG_INF)
        sel = _rank_select(route, n_mb, topk) & (nidx < qblk)
        mask_ref[h] = jnp.where(sel | (nidx == qblk), 0.0, NEG_INF)
        qpads.append(qh)

    def qk_fn(h, n):
        rows = pl.ds(pl.multiple_of(n * TK, TK), TK)
        return _dot(k_ref[0, rows, (h // 2) * LANES:(h // 2 + 1) * LANES], qpads[h])

    def fix_fn(h, n, s):
        parts = []
        for u in range(KSUB):
            b = KSUB * n + u
            parts.append(mask_ref[h, pl.ds(b, 1), :] + t_ref[h, jnp.clip(c - b, -1, NE_BIAS) + 1])
        return s + jnp.concatenate(parts, axis=0)

    def v_fn(h, n):
        return v_ref[0, n, h * HEAD_DIM:(h + 1) * HEAD_DIM, :]

    outs = _attend(blk + 1, lambda i: blk - i, qk_fn, fix_fn, v_fn, s_scr, acc_scr, HPS)
    o_ref[0] = jnp.concatenate(outs, axis=0).astype(BF16)


def _moba(fm, rm, tab, B, S):
    n_mb = S // MOBA_BLOCK
    n_pad = -(-n_mb // 16) * 16
    ne = tab.shape[1]
    rows = HPS * HEAD_DIM
    body = functools.partial(_moba_body, n_mb=n_mb, topk=min(MOBA_TOPK, n_mb))
    return pl.pallas_call(
        body,
        grid=(B, MOBA_HEADS // HPS, S // TQ),
        in_specs=[
            pl.BlockSpec((1, 1, rows, TQ), lambda b, p, c: (b, c // QPK, p, c % QPK)),
            pl.BlockSpec((1, S, rows), lambda b, p, c: (b, 0, p)),
            pl.BlockSpec((1, S // TK, rows, TK), lambda b, p, c: (b, 0, MOBA_HEADS // HPS + p, 0)),
            pl.BlockSpec((HPS, ne, TB, TQ), lambda b, p, c: (p, 0, 0, 0), pipeline_mode=pl.Buffered(1)),
        ],
        out_specs=pl.BlockSpec((1, rows, TQ), lambda b, p, c: (b, p, c)),
        out_shape=jax.ShapeDtypeStruct((B, MOBA_W, S), BF16),
        scratch_shapes=[pltpu.VMEM((n_pad, rows), F32), pltpu.VMEM((HPS, n_pad, TQ), F32),
                        pltpu.VMEM((HPS, TK, TQ), F32), pltpu.VMEM((HPS, ACC_ROWS, TQ), F32)],
        compiler_params=_params(("parallel", "parallel", "arbitrary")),
        name="moba_attn",
    )(fm, rm, fm, tab)


def _nsa_body(q_ref, kc_ref, vct_ref, fc_ref, ovl_ref, ksl_ref, vsl_ref, kwn_ref, vwn_ref,
              tslc_ref, twin_ref, gz_ref, o_ref, s_scr, acc_scr, s_win, acc_win, sel_ref, *, n_sb, n_sel):
    c = pl.program_id(1)
    blk = c // QPK
    G, J = NSA_KV_GROUPS, NSA_HPG
    H = G * J

    q = q_ref[0, 0]
    qs = [q[h * HEAD_DIM:(h + 1) * HEAD_DIM, :] for h in range(H)]
    zero = jnp.zeros((HEAD_DIM, TQ), BF16)
    qpads = [jnp.concatenate([zero] * (h // J) + [qs[h]] + [zero] * (G - 1 - h // J), axis=0) for h in range(H)]

    c0 = pl.multiple_of(c * (TQ // NSA_CMP_STRIDE), TQ // NSA_CMP_STRIDE)
    o_cmp = []
    for g in range(G):
        kc = kc_ref[0, g]
        vct = vct_ref[0, g]
        psum = jnp.zeros((kc.shape[0], TQ), F32)
        for h in range(g * J, (g + 1) * J):
            s = _dot(kc, qs[h]) + fc_ref[h, pl.ds(c0, kc.shape[0]), :]
            m = jnp.max(s, axis=0, keepdims=True)
            p = jnp.exp2(s - m)
            l = jnp.sum(p, axis=0, keepdims=True)
            pn = p * jnp.where(m > 0.5 * NEG_INF, 1.0 / l, 0.0)
            o_cmp.append(_dot(vct, pn.astype(BF16)))
            psum = psum + pn

        ph = psum.astype(BF16)
        plo = (psum - ph.astype(F32)).astype(BF16)
        imp = _dot(ovl_ref[...], ph) + _dot(ovl_ref[...], plo)
        jb = lax.broadcasted_iota(jnp.int32, imp.shape, 0)
        t = c * TQ + lax.broadcasted_iota(jnp.int32, imp.shape, 1)
        sb = t // NSA_SLC_BLOCK
        forced = (jb == 0) | (jb == sb) | (jb == sb - 1)
        allowed = jb <= sb
        val = jnp.where(forced, imp + NSA_FORCE_SCORE, jnp.where(allowed, imp, NEG_INF))
        sel = _rank_select(val, n_sb, n_sel) & allowed
        selb = jnp.where(sel, 0.0, NEG_INF)
        for n in range(n_sb // SPT):
            slab = selb[n * SPT:(n + 1) * SPT, :]
            if SPT < sel_ref.shape[2]:
                slab = jnp.concatenate([slab, jnp.zeros((sel_ref.shape[2] - SPT, TQ), F32)], axis=0)
            sel_ref[g, n] = slab

    def slc_qk(h, n):
        return _dot(ksl_ref[0, pl.ds(pl.multiple_of(n * TK, TK), TK), :], qpads[h])

    def slc_fix(h, n, s):
        rows = sel_ref[h // J, n]
        mask = jnp.concatenate([jnp.broadcast_to(rows[b:b + 1, :], (NSA_SLC_BLOCK, TQ)) for b in range(SPT)], axis=0)
        bias = jnp.concatenate([tslc_ref[h, jnp.clip(c - (KSUB * n + u), -1, NE_BIAS) + 1] for u in range(KSUB)],
                               axis=0)
        return s + mask + bias

    def slc_v(h, n):
        return vsl_ref[0, n, (h // J) * HEAD_DIM:(h // J + 1) * HEAD_DIM, :]

    o_slc = _attend(blk + 1, lambda i: blk - i, slc_qk, slc_fix, slc_v, s_scr, acc_scr, H)

    def win_qk(h, n):
        return _dot(kwn_ref[0, pl.ds(pl.multiple_of(n * TK, TK), TK), :], qpads[h])

    def win_fix(h, n, s):
        return s + jnp.concatenate([twin_ref[h, jnp.clip(c - (KSUB * n + u), -1, NE_WIN) + 1] for u in range(KSUB)],
                                   axis=0)

    def win_v(h, n):
        return vwn_ref[0, n, (h // J) * HEAD_DIM:(h // J + 1) * HEAD_DIM, :]

    w_lo = jnp.maximum(c - NE_WIN + 1, 0) // KSUB
    o_win = _attend(blk - w_lo + 1, lambda i: blk - i, win_qk, win_fix, win_v, s_win, acc_win, H)

    gate = jax.nn.sigmoid(gz_ref[0])
    outs = []
    for h in range(H):
        r = (h // J) * GZ_ROWS + h % J
        outs.append(gate[r:r + 1, :] * o_cmp[h] + gate[r + J:r + J + 1, :] * o_slc[h]
                    + gate[r + 2 * J:r + 2 * J + 1, :] * o_win[h])
    o_ref[0] = jnp.concatenate(outs, axis=0).astype(BF16)


def _nsa(fm, rm, gz, kc, vct, tcmp, ovl, tslc, twin, B, S, col_ksl, col_kwn, row_q, row_vsl, row_vwn):
    G, J = NSA_KV_GROUPS, NSA_HPG
    NC = kc.shape[2]
    n_sb = S // NSA_SLC_BLOCK
    body = functools.partial(_nsa_body, n_sb=n_sb, n_sel=min(NSA_TOPN, n_sb))
    H = G * J
    kvrows = G * HEAD_DIM
    one = pl.Buffered(1)
    return pl.pallas_call(
        body,
        grid=(B, S // TQ),
        in_specs=[
            pl.BlockSpec((1, 1, NSA_W, TQ), lambda b, c: (b, c // QPK, row_q // NSA_W, c % QPK)),
            pl.BlockSpec((1, G, NC, HEAD_DIM), lambda b, c: (b, 0, 0, 0)),
            pl.BlockSpec((1, G, HEAD_DIM, NC), lambda b, c: (b, 0, 0, 0)),
            _const_spec(tcmp.shape),
            _const_spec(ovl.shape),
            pl.BlockSpec((1, S, LANES), lambda b, c: (b, 0, col_ksl // LANES)),
            pl.BlockSpec((1, S // TK, kvrows, TK), lambda b, c: (b, 0, row_vsl // kvrows, 0)),
            pl.BlockSpec((1, S, LANES), lambda b, c: (b, 0, col_kwn // LANES)),
            pl.BlockSpec((1, S // TK, kvrows, TK), lambda b, c: (b, 0, row_vwn // kvrows, 0)),
            pl.BlockSpec((H, tslc.shape[1], TB, TQ), lambda b, c: (MOBA_HEADS // H, 0, 0, 0), pipeline_mode=one),
            _const_spec(twin.shape),
            pl.BlockSpec((1, G * GZ_ROWS, TQ), lambda b, c: (b, 0, c)),
        ],
        out_specs=pl.BlockSpec((1, NSA_W, TQ), lambda b, c: (b, 0, c)),
        out_shape=jax.ShapeDtypeStruct((B, NSA_W, S), BF16),
        scratch_shapes=[pltpu.VMEM((H, TK, TQ), F32), pltpu.VMEM((H, ACC_ROWS, TQ), F32),
                        pltpu.VMEM((H, TK, TQ), F32), pltpu.VMEM((H, ACC_ROWS, TQ), F32),
                        pltpu.VMEM((G, S // TK, 8, TQ), F32)],
        compiler_params=_params(("parallel", "arbitrary")),
        name="nsa_attn",
    )(fm, kc, vct, tcmp, ovl, rm, fm, rm, fm, tslc, twin, gz)


def _inproj1_body(x_ref, g_ref, wfm_ref, wk_ref, wf_ref, bf_ref, tri_ref, place_ref, fm_ref, ka_ref, carry_ref,
                  *, nst):
    i = pl.program_id(0)

    @pl.when(i % nst == 0)
    def _():
        carry_ref[...] = jnp.zeros_like(carry_ref)

    xf = _rmsnorm(x_ref[...], g_ref[...])
    xn = xf.astype(BF16)
    xlo = (xf - xn.astype(F32)).astype(BF16)
    for r0 in range(0, fm_ref.shape[2], CH):
        res = _dot_nt(wfm_ref[r0:r0 + CH, :], xn).astype(BF16)
        for t in range(TM // TK):
            fm_ref[0, t, r0:r0 + CH, :] = res[:, t * TK:(t + 1) * TK]

    fz = _dot(xn, wf_ref[0]) + _dot(xlo, wf_ref[0]) + _dot(xn, wf_ref[1]) + bf_ref[...]
    logf = jnp.minimum(fz, 0.0) - jnp.log(1.0 + jnp.exp(-jnp.abs(fz)))
    tri = tri_ref[...]
    h1, h2, h3 = _split3(logf)
    cum = _dot(tri, h1) + _dot(tri, h2) + _dot(tri, h3) + carry_ref[0:1, :]
    carry_ref[...] = jnp.broadcast_to(cum[TM - 1:TM, :], carry_ref.shape)
    cc = jnp.concatenate(_split3(cum * LOG2E), axis=1)
    low = lax.broadcasted_iota(jnp.int32, (TM, LANES), 1) < HEAD_DIM
    for c0 in range(0, wk_ref.shape[-1], FF_CH):
        kp = _dot(xn, wk_ref[:, c0:c0 + FF_CH])
        dp = _dot(cc, place_ref[:, c0:c0 + FF_CH])
        for t0 in range(0, FF_CH, LANES):
            j = (c0 + t0) // LANES
            kt, dt = kp[:, t0:t0 + LANES], dp[:, t0:t0 + LANES]
            ka_ref[:, 2 * j * LANES:(2 * j + 1) * LANES] = jnp.where(low, kt, dt).astype(BF16)
            ka_ref[:, (2 * j + 1) * LANES:(2 * j + 2) * LANES] = jnp.where(low, dt, kt).astype(BF16)


def _inproj1(x2, g, wfm, wk, wf, bf, tri, place, B, S):
    M = B * S
    nst = S // TM
    n_fm, n_ka = wfm.shape[0], FOX_HEADS * LANES
    return pl.pallas_call(
        functools.partial(_inproj1_body, nst=nst),
        grid=(M // TM,),
        in_specs=[
            pl.BlockSpec((TM, D_MODEL), lambda i: (i, 0)),
            _const_spec((1, D_MODEL)),
            _const_spec(wfm.shape),
            _const_spec(wk.shape),
            _const_spec(wf.shape),
            _const_spec(bf.shape),
            _const_spec(tri.shape),
            _const_spec(place.shape),
        ],
        out_specs=[
            pl.BlockSpec((1, TM // TK, n_fm, TK), lambda i: (i // nst, i % nst, 0, 0)),
            pl.BlockSpec((TM, n_ka), lambda i: (i, 0)),
        ],
        out_shape=[
            jax.ShapeDtypeStruct((B, S // TK, n_fm, TK), BF16),
            jax.ShapeDtypeStruct((M, n_ka), BF16),
        ],
        scratch_shapes=[pltpu.VMEM((8, LANES), F32)],
        compiler_params=_params(("arbitrary",)),
        name="inproj1",
    )(x2, g, wfm, wk, wf, bf, tri, place)


def _fox_body(q_ref, k_ref, v_ref, cm_ref, o_ref, s_scr, acc_scr):
    c = pl.program_id(2)
    blk = c // QPK
    q = q_ref[0, 0]
    ones = jnp.ones((LANES - HEAD_DIM, TQ), BF16)
    qhs = [q[h * HEAD_DIM:(h + 1) * HEAD_DIM, :] for h in range(FOX_HPS)]
    qas = [jnp.concatenate([qhs[h], ones] if h % 2 == 0 else [ones, qhs[h]], axis=0) for h in range(FOX_HPS)]

    def qk_fn(h, n):
        rows = pl.ds(pl.multiple_of(n * TK, TK), TK)
        return _dot(k_ref[0, rows, h * LANES:(h + 1) * LANES], qas[h])

    causal = cm_ref[c % QPK]
    outs = _attend(blk + 1, lambda i: blk - i, qk_fn, None,
                   lambda h, n: v_ref[0, n, h * HEAD_DIM:(h + 1) * HEAD_DIM, :], s_scr, acc_scr, FOX_HPS,
                   first_fix=lambda h, n, s: s + causal)
    o_ref[0] = jnp.concatenate(outs, axis=0).astype(BF16)


def _fox(fm, ka, cmask, B, S):
    rows = FOX_HPS * HEAD_DIM
    return pl.pallas_call(
        _fox_body,
        grid=(B, FOX_HEADS // FOX_HPS, S // TQ),
        in_specs=[
            pl.BlockSpec((1, 1, rows, TQ), lambda b, h, c: (b, c // QPK, h, c % QPK)),
            pl.BlockSpec((1, S, FOX_HPS * LANES), lambda b, h, c: (b, 0, h), pipeline_mode=pl.Buffered(1)),
            pl.BlockSpec((1, S // TK, rows, TK), lambda b, h, c: (b, 0, FOX_HEADS // FOX_HPS + h, 0),
                         pipeline_mode=pl.Buffered(1)),
            _const_spec(cmask.shape),
        ],
        out_specs=pl.BlockSpec((1, rows, TQ), lambda b, h, c: (b, h, c)),
        out_shape=jax.ShapeDtypeStruct((B, FOX_W, S), BF16),
        scratch_shapes=[pltpu.VMEM((FOX_HPS, TK, TQ), F32), pltpu.VMEM((FOX_HPS, ACC_ROWS, TQ), F32)],
        compiler_params=_params(("parallel", "parallel", "arbitrary")),
        name="fox_attn",
    )(fm, ka, fm, cmask)


def _post_body(*refs, n_parts, final):
    o_refs = refs[:n_parts]
    h_ref, wo_ref, g_ref, w1_ref, w2_ref = refs[n_parts:n_parts + 5]
    gf_ref = refs[n_parts + 5] if final else None
    out_ref, hn_ref = refs[-2:]
    h1 = h_ref[...]
    r0 = 0
    for o_ref in o_refs:
        nf = o_ref.shape[1]
        h1 = h1 + _dot_tn(o_ref[0], wo_ref[r0:r0 + nf, :])
        r0 += nf
    out_ref[...] = h1
    hn_ref[...] = _rmsnorm(out_ref[...], g_ref[...]).astype(BF16)
    for c0 in range(0, D_FF, FF_CH):
        a = jnp.maximum(_dot(hn_ref[...], w1_ref[:, c0:c0 + FF_CH]), 0.0)
        out_ref[...] += _dot((a * a).astype(BF16), w2_ref[c0:c0 + FF_CH, :])
    if final:
        out_ref[...] = _rmsnorm(out_ref[...], gf_ref[...])


def _post(o_parts, h2, wo, g, w1, w2, gf, B, S):
    M = B * S
    nst = S // TM
    final = gf is not None
    in_specs = [pl.BlockSpec((1, o.shape[1], TM), lambda i: (i // nst, 0, i % nst)) for o in o_parts]
    in_specs += [
        pl.BlockSpec((TM, D_MODEL), lambda i: (i, 0)),
        _const_spec(wo.shape),
        _const_spec((1, D_MODEL)),
        _const_spec(w1.shape),
        _const_spec(w2.shape),
    ]
    args = list(o_parts) + [h2, wo, g, w1, w2]
    if final:
        in_specs.append(_const_spec((1, D_MODEL)))
        args.append(gf)
    return pl.pallas_call(
        functools.partial(_post_body, n_parts=len(o_parts), final=final),
        grid=(M // TM,),
        in_specs=in_specs,
        out_specs=pl.BlockSpec((TM, D_MODEL), lambda i: (i, 0)),
        out_shape=jax.ShapeDtypeStruct((M, D_MODEL), F32),
        scratch_shapes=[pltpu.VMEM((TM, D_MODEL), BF16)],
        compiler_params=_params(("parallel",)),
        name="post_final" if final else "post",
    )(*args)


def _rel_bucket(dist):
    n = jnp.maximum(dist, 0)
    max_exact = REL_BUCKETS // 2
    nf = jnp.maximum(n, 1).astype(jnp.float32)
    large = max_exact + (jnp.log(nf / max_exact) / math.log(REL_MAX_DISTANCE / max_exact)
                         * (REL_BUCKETS - max_exact)).astype(jnp.int32)
    large = jnp.minimum(large, REL_BUCKETS - 1)
    return jnp.where(n < max_exact, n, large)


def _bias_tables(rel_bias, S):
    n_heads = rel_bias.shape[1]
    table = rel_bias.T * LOG2E

    def bias_of(dist):
        tab = table.reshape((n_heads, REL_BUCKETS) + (1,) * dist.ndim)
        bkt = _rel_bucket(jnp.asarray(dist))[None]
        out = jnp.zeros((n_heads,) + dist.shape, F32)
        for b in range(REL_BUCKETS):
            out = jnp.where(bkt == b, tab[:, b], out)
        return jnp.where(jnp.asarray(dist)[None] >= 0, out, NEG_INF)

    d = (np.arange(-1, NE_BIAS)[:, None, None] * TQ + np.arange(TQ)[None, None, :] - np.arange(TB)[None, :, None])
    far = table[:, REL_BUCKETS - 1][:, None, None, None]
    d_far = np.concatenate([d, np.full_like(d[:1], REL_MAX_DISTANCE)], axis=0)
    tile = bias_of(d_far) - far
    dw = d[:NE_WIN + 2]
    twin = jnp.where(dw < NSA_WINDOW, bias_of(dw)[MOBA_HEADS:], NEG_INF)
    n_c = S // NSA_CMP_STRIDE
    u = np.arange(2 * n_c)[:, None]
    dc = np.arange(TQ)[None, :] + NSA_CMP_STRIDE * (u - (n_c - 1)) - (NSA_CMP_BLOCK - 1)
    fcmp = bias_of(dc)[MOBA_HEADS:]
    return tile, twin, fcmp


def _selection_constants(S):
    n_c = S // NSA_CMP_STRIDE
    n_cmp = (S - NSA_CMP_BLOCK) // NSA_CMP_STRIDE + 1
    n_sb = S // NSA_SLC_BLOCK
    ci = np.arange(n_c)[None, :] * NSA_CMP_STRIDE
    sj = np.arange(n_sb)[:, None] * NSA_SLC_BLOCK
    ovl = (ci < sj + NSA_SLC_BLOCK) & (ci + NSA_CMP_BLOCK > sj) & (np.arange(n_c)[None, :] < n_cmp)
    ovl = ovl[:, ::-1]
    return jnp.asarray(ovl, BF16)


def _causal_tiles():
    e = np.arange(QPK)[:, None, None]
    d = e * TQ + np.arange(TQ)[None, None, :] - np.arange(TK)[None, :, None]
    return jnp.asarray(np.where(d >= 0, 0.0, NEG_INF), F32)


def kernel(x, rel_bias, mix_norm, mlp_norm, even_w_in, even_w_out, cmp_pos_k, cmp_pos_v, cmp_k_w1, cmp_k_w2,
           cmp_v_w1, cmp_v_w2, odd_w_in, odd_b_forget, odd_w_out, mlp_w1, mlp_w2, final_norm):
    B, S, D = x.shape
    assert D == D_MODEL and S % TM == 0
    G, J = NSA_KV_GROUPS, NSA_HPG
    h = x.reshape(B * S, D)

    offs = np.cumsum((MOBA_W, MOBA_W, MOBA_W, NSA_W) + (NSA_KV_W,) * 6)
    mq_w, mk_w, mv_w, nq_w, kc_w, vc_w, ksl_w, vsl_w, kwn_w, vwn_w, gz_w = jnp.split(even_w_in[0], offs, axis=1)
    wrm = jnp.concatenate([mk_w, ksl_w, kwn_w], axis=1).astype(BF16)
    col_ksl, col_kwn = MOBA_W, MOBA_W + NSA_KV_W
    wcv = jnp.concatenate([kc_w, vc_w], axis=1).astype(BF16)
    qs = SCALE * LOG2E
    wfm = jnp.concatenate([mq_w * qs, mv_w, nq_w * qs, vsl_w, vwn_w], axis=1).T.astype(BF16)
    row_nq, row_vsl, row_vwn = 2 * MOBA_W, 2 * MOBA_W + NSA_W, 2 * MOBA_W + NSA_W + NSA_KV_W
    gzw = gz_w.T.reshape(G, J, 3, D).transpose(0, 2, 1, 3).reshape(G, 3 * J, D)
    gzw = jnp.pad(gzw, ((0, 0), (0, GZ_ROWS - 3 * J), (0, 0))).reshape(G * GZ_ROWS, D).astype(BF16)

    rm, fm, gz, cv = _inproj0(h, mix_norm[0][None, :], wrm, wfm, gzw, wcv, B, S)
    rm = rm.reshape(B, S, -1)

    tile, twin, tcmp = _bias_tables(rel_bias, S)
    ovl = _selection_constants(S)

    o_moba = _moba(fm, rm, tile, B, S)

    n_c = S // NSA_CMP_STRIDE
    r = cv.reshape(2 * G, B, n_c, NSA_CMP_STRIDE * HEAD_DIM)
    pos = jnp.stack([cmp_pos_k[0].reshape(1, -1), cmp_pos_v[0].reshape(1, -1)])
    pos = jnp.pad(pos, ((0, 0), (0, 7), (0, 0))).astype(BF16)
    w1c = jnp.stack([cmp_k_w1[0], cmp_v_w1[0]]).astype(BF16)
    kc, vct = _compress(r, pos, w1c, cmp_k_w2[0].astype(BF16), cmp_v_w2[0].T.astype(BF16), B, n_c)

    o_nsa = _nsa(fm, rm, gz, kc, vct, tcmp, ovl, tile, twin, B, S,
                 col_ksl, col_kwn, row_nq, row_vsl, row_vwn)

    h = _post([o_moba, o_nsa], h, even_w_out[0].astype(BF16), mlp_norm[0][None, :],
              mlp_w1[0].astype(BF16), mlp_w2[0].astype(BF16), None, B, S)

    q_w, k_w, v_w, f_w = jnp.split(odd_w_in[0], np.cumsum((FOX_W, FOX_W, FOX_W)), axis=1)
    wfm1 = jnp.concatenate([q_w * qs, v_w], axis=1).T.astype(BF16)
    wk = k_w.astype(BF16)
    f_w = jnp.pad(f_w, ((0, 0), (0, LANES - FOX_HEADS)))
    f_hi = f_w.astype(BF16)
    wf = jnp.stack([f_hi, (f_w - f_hi.astype(F32)).astype(BF16)])
    bf = jnp.pad(odd_b_forget[0], (0, LANES - FOX_HEADS))[None, :]
    tri = jnp.asarray(np.tril(np.ones((TM, TM))), BF16)
    place = np.zeros((3 * LANES, FOX_W), np.float32)
    heads = np.arange(FOX_HEADS)
    for term in range(3):
        place[term * LANES + heads, (heads // 2) * LANES + (1 - heads % 2) * HEAD_DIM + term] = -1.0
    fm1, ka = _inproj1(h, mix_norm[1][None, :], wfm1, wk, wf, bf, tri, jnp.asarray(place, BF16), B, S)
    o_fox = _fox(fm1, ka.reshape(B, S, -1), _causal_tiles(), B, S)

    h = _post([o_fox], h, odd_w_out[0].astype(BF16), mlp_norm[1][None, :],
              mlp_w1[1].astype(BF16), mlp_w2[1].astype(BF16), final_norm[None, :], B, S)
    return h.reshape(B, S, D)
```

```python
import functools
import math

import numpy as np
import jax
import jax.numpy as jnp
from jax import lax
from jax.experimental import pallas as pl
from jax.experimental.pallas import tpu as pltpu

D_MODEL = 1024
HEAD_DIM = 64
MOBA_HEADS = 8
MOBA_BLOCK = 256
MOBA_TOPK = 3
NSA_HEADS = 8
NSA_KV_GROUPS = 2
NSA_HPG = NSA_HEADS // NSA_KV_GROUPS
NSA_CMP_BLOCK = 32
NSA_CMP_STRIDE = 16
NSA_CMP_HIDDEN = 256
NSA_SLC_BLOCK = 64
NSA_TOPN = 16
NSA_WINDOW = 512
NSA_FORCE_SCORE = 1e6
FOX_HEADS = 16
D_FF = 4 * D_MODEL
REL_BUCKETS = 32
REL_MAX_DISTANCE = 1024
RMS_EPS = 1e-5
NEG_INF = -1e30
SCALE = HEAD_DIM ** -0.5

MOBA_W = MOBA_HEADS * HEAD_DIM
NSA_W = NSA_HEADS * HEAD_DIM
NSA_KV_W = NSA_KV_GROUPS * HEAD_DIM
FOX_W = FOX_HEADS * HEAD_DIM

LANES = 128
SUBLANES = 8
TQ = 256
TK = 256
TB = 256
QPK = TK // TQ
KSUB = TK // TB
TM = 512
CH = 256
FF_CH = 512
VMEM_LIMIT = 56 * 1024 * 1024
NE_BIAS = -(-(REL_MAX_DISTANCE + TB - 1) // TQ)
NE_WIN = -(-(NSA_WINDOW + TB - 1) // TQ)
GZ_ROWS = 16
HPS = 8
FOX_HPS = 16
SPT = TK // NSA_SLC_BLOCK
ACC_ROWS = HEAD_DIM + 16
ROW_CHUNK = 64
LOG2E = math.log2(math.e)

assert TK % TQ == 0 and TK % TB == 0 and TQ == TB and MOBA_BLOCK == TB and TB % NSA_SLC_BLOCK == 0 and SPT <= 8

F32 = jnp.float32
BF16 = jnp.bfloat16


def _dot(a, b):
    return jnp.dot(a, b, preferred_element_type=F32)


def _dot_nt(a, b):
    return lax.dot_general(a, b, (((1,), (1,)), ((), ())), preferred_element_type=F32)


def _dot_tn(a, b):
    return lax.dot_general(a, b, (((0,), (0,)), ((), ())), preferred_element_type=F32)


def _rmsnorm(x, g):
    ms = jnp.mean(x * x, axis=-1, keepdims=True)
    return x * lax.rsqrt(ms + RMS_EPS) * g


def _split3(x):
    a = x.astype(BF16)
    r = x - a.astype(F32)
    b = r.astype(BF16)
    c = (r - b.astype(F32)).astype(BF16)
    return a, b, c


def _const_spec(shape):
    nd = len(shape)
    return pl.BlockSpec(shape, lambda *_: (0,) * nd, pipeline_mode=pl.Buffered(1))


def _params(sem):
    return pltpu.CompilerParams(dimension_semantics=sem, vmem_limit_bytes=VMEM_LIMIT)


def _attend(n_tiles, tile_of, qk_fn, fix_fn, v_fn, s_scr, acc_scr, n_heads, first_fix=None):
    def put_scores(h, n, fix):
        s = qk_fn(h, n)
        s_scr[h] = s if fix is None else fix(h, n, s)

    for h in range(n_heads):
        put_scores(h, tile_of(0), fix_fn if first_fix is None else first_fix)
    acc_scr[...] = jnp.zeros_like(acc_scr)
    last = n_tiles - 1
    ones = jnp.ones((ACC_ROWS - HEAD_DIM, TK), BF16)
    chunks = range(0, TK, ROW_CHUNK)

    def body(i, ms):
        n = tile_of(i)
        n_next = tile_of(jnp.minimum(i + 1, last))
        out = []
        for h in range(n_heads):
            mx = s_scr[h, 0:ROW_CHUNK, :]
            for r0 in chunks[1:]:
                mx = jnp.maximum(mx, s_scr[h, r0:r0 + ROW_CHUNK, :])
            m_new = jnp.maximum(ms[h], jnp.max(mx, axis=0, keepdims=True))
            alpha = jnp.exp2(ms[h] - m_new)
            p = jnp.concatenate([jnp.exp2(s_scr[h, r0:r0 + ROW_CHUNK, :] - m_new).astype(BF16) for r0 in chunks],
                                axis=0)
            va = jnp.concatenate([v_fn(h, n), ones], axis=0)
            acc_scr[h] = alpha * acc_scr[h] + _dot(va, p)
            out.append(m_new)
            put_scores(h, n_next, fix_fn)
        return tuple(out)

    lax.fori_loop(0, n_tiles, body, tuple(jnp.full((1, TQ), NEG_INF, F32) for _ in range(n_heads)))
    return [acc_scr[h, :HEAD_DIM, :] / acc_scr[h, HEAD_DIM:HEAD_DIM + 1, :] for h in range(n_heads)]


def _rank_select(val, n_rows, k):
    sub = lax.broadcasted_iota(jnp.int32, (SUBLANES, val.shape[1]), 0)
    groups = [val[g0:g0 + SUBLANES, :] for g0 in range(0, val.shape[0], SUBLANES)]
    counts = [jnp.zeros(g.shape, F32) for g in groups]
    for m in range(n_rows):
        vm = val[m:m + 1, :]
        for g, vg in enumerate(groups):
            if g * SUBLANES > m:
                counts[g] = counts[g] + jnp.where(vm >= vg, 1.0, 0.0)
            elif (g + 1) * SUBLANES <= m:
                counts[g] = counts[g] + jnp.where(vm > vg, 1.0, 0.0)
            else:
                tie = jnp.where(sub > m % SUBLANES, 1.0, 0.0)
                counts[g] = counts[g] + jnp.where(vm > vg, 1.0, 0.0) + jnp.where(vm == vg, tie, 0.0)
    return jnp.concatenate(counts, axis=0) < k


def _inproj0_body(x_ref, g_ref, wrm_ref, wfm_ref, wgz_ref, wcv_ref, rm_ref, fm_ref, gz_ref, cv_ref):
    xn = _rmsnorm(x_ref[...], g_ref[...]).astype(BF16)
    cv = _dot(xn, wcv_ref[...]).astype(BF16)
    for j in range(cv_ref.shape[0]):
        cv_ref[j] = cv[:, j * HEAD_DIM:(j + 1) * HEAD_DIM]
    for c0 in range(0, rm_ref.shape[-1], CH):
        rm_ref[:, c0:c0 + CH] = _dot(xn, wrm_ref[:, c0:c0 + CH]).astype(BF16)
    for r0 in range(0, fm_ref.shape[2], CH):
        res = _dot_nt(wfm_ref[r0:r0 + CH, :], xn).astype(BF16)
        for t in range(TM // TK):
            fm_ref[0, t, r0:r0 + CH, :] = res[:, t * TK:(t + 1) * TK]
    gz_ref[0] = _dot_nt(wgz_ref[...], xn)


def _inproj0(x2, g, wrm, wfm, wgz, wcv, B, S):
    M = B * S
    nst = S // TM
    n_rm, n_fm, n_gz, n_cv = wrm.shape[1], wfm.shape[0], wgz.shape[0], wcv.shape[1] // HEAD_DIM
    return pl.pallas_call(
        _inproj0_body,
        grid=(M // TM,),
        in_specs=[
            pl.BlockSpec((TM, D_MODEL), lambda i: (i, 0)),
            _const_spec((1, D_MODEL)),
            _const_spec((D_MODEL, n_rm)),
            _const_spec((n_fm, D_MODEL)),
            _const_spec((n_gz, D_MODEL)),
            _const_spec(wcv.shape),
        ],
        out_specs=[
            pl.BlockSpec((TM, n_rm), lambda i: (i, 0)),
            pl.BlockSpec((1, TM // TK, n_fm, TK), lambda i: (i // nst, i % nst, 0, 0)),
            pl.BlockSpec((1, n_gz, TM), lambda i: (i // nst, 0, i % nst)),
            pl.BlockSpec((n_cv, TM, HEAD_DIM), lambda i: (0, i, 0)),
        ],
        out_shape=[
            jax.ShapeDtypeStruct((M, n_rm), BF16),
            jax.ShapeDtypeStruct((B, S // TK, n_fm, TK), BF16),
            jax.ShapeDtypeStruct((B, n_gz, S), F32),
            jax.ShapeDtypeStruct((n_cv, M, HEAD_DIM), BF16),
        ],
        compiler_params=_params(("parallel",)),
        name="inproj0",
    )(x2, g, wrm, wfm, wgz, wcv)


def _compress_body(rk_ref, rv_ref, pos_ref, w1_ref, w2k_ref, w2vt_ref, flip_ref, kc_ref, vct_ref):
    half = NSA_CMP_STRIDE * HEAD_DIM

    def hidden(r_ref, s):
        r = r_ref[0, 0]
        a = _dot(r, w1_ref[s, :half, :])
        b = _dot(r, w1_ref[s, half:, :])
        nxt = pltpu.roll(b, b.shape[0] - 1, axis=0)
        posb = _dot(pos_ref[s], w1_ref[s])[0:1]
        pre = a + nxt + posb
        act = (pre * jax.nn.sigmoid(pre)).astype(BF16)
        return _dot(flip_ref[...], act).astype(BF16)

    kc_ref[0, 0] = _dot(hidden(rk_ref, 0), w2k_ref[...]).astype(BF16)
    vct_ref[0, 0] = _dot_nt(w2vt_ref[...], hidden(rv_ref, 1)).astype(BF16)


def _compress(r, pos, w1, w2k, w2vt, B, NC):
    G = NSA_KV_GROUPS
    half = NSA_CMP_STRIDE * HEAD_DIM
    return pl.pallas_call(
        _compress_body,
        grid=(B, G),
        in_specs=[
            pl.BlockSpec((1, 1, NC, half), lambda b, g: (g, b, 0, 0)),
            pl.BlockSpec((1, 1, NC, half), lambda b, g: (G + g, b, 0, 0)),
            _const_spec(pos.shape),
            _const_spec(w1.shape),
            _const_spec(w2k.shape),
            _const_spec(w2vt.shape),
            _const_spec((NC, NC)),
        ],
        out_specs=[
            pl.BlockSpec((1, 1, NC, HEAD_DIM), lambda b, g: (b, g, 0, 0)),
            pl.BlockSpec((1, 1, HEAD_DIM, NC), lambda b, g: (b, g, 0, 0)),
        ],
        out_shape=[
            jax.ShapeDtypeStruct((B, G, NC, HEAD_DIM), BF16),
            jax.ShapeDtypeStruct((B, G, HEAD_DIM, NC), BF16),
        ],
        compiler_params=_params(("parallel", "parallel")),
        name="nsa_compress",
    )(r, r, pos, w1, w2k, w2vt, jnp.asarray(np.eye(NC)[::-1], BF16))


def _moba_body(q_ref, k_ref, v_ref, t_ref, o_ref, kmean_ref, mask_ref, s_scr, acc_scr, *, n_mb, topk):
    c = pl.program_id(2)
    blk = c // QPK
    qblk = c * TQ // MOBA_BLOCK

    @pl.when(c == 0)
    def _():
        kmean_ref[...] = jnp.zeros_like(kmean_ref)
        for n in range(n_mb):
            kblk = k_ref[0, n * MOBA_BLOCK:(n + 1) * MOBA_BLOCK, :].astype(F32)
            kmean_ref[n:n + 1, :] = jnp.mean(kblk, axis=0, keepdims=True)

    q = q_ref[0, 0]
    rowi = lax.broadcasted_iota(jnp.int32, (LANES, TQ), 0)
    nidx = lax.broadcasted_iota(jnp.int32, (kmean_ref.shape[0], TQ), 0)
    km = _split3(kmean_ref[...])
    qpads = []
    for h in range(HPS):
        lo = (h // 2) * LANES
        qpair = q[lo:lo + LANES, :]
        qh = jnp.where(rowi // HEAD_DIM == h % 2, qpair, jnp.zeros_like(qpair))
        route = sum(_dot(part[:, lo:lo + LANES], qh) for part in km)
        route = jnp.where(nidx < qblk, route, NEG_INF)
        sel = _rank_select(route, n_mb, topk) & (nidx < qblk)
        mask_ref[h] = jnp.where(sel | (nidx == qblk), 0.0, NEG_INF)
        qpads.append(qh)

    def qk_fn(h, n):
        rows = pl.ds(pl.multiple_of(n * TK, TK), TK)
        return _dot(k_ref[0, rows, (h // 2) * LANES:(h // 2 + 1) * LANES], qpads[h])

    def fix_fn(h, n, s):
        parts = []
        for u in range(KSUB):
            b = KSUB * n + u
            parts.append(mask_ref[h, pl.ds(b, 1), :] + t_ref[h, jnp.clip(c - b, -1, NE_BIAS) + 1])
        return s + jnp.concatenate(parts, axis=0)

    def v_fn(h, n):
        return v_ref[0, n, h * HEAD_DIM:(h + 1) * HEAD_DIM, :]

    outs = _attend(blk + 1, lambda i: blk - i, qk_fn, fix_fn, v_fn, s_scr, acc_scr, HPS)
    o_ref[0] = jnp.concatenate(outs, axis=0).astype(BF16)


def _moba(fm, rm, tab, B, S):
    n_mb = S // MOBA_BLOCK
    n_pad = -(-n_mb // 16) * 16
    ne = tab.shape[1]
    rows = HPS * HEAD_DIM
    body = functools.partial(_moba_body, n_mb=n_mb, topk=min(MOBA_TOPK, n_mb))
    return pl.pallas_call(
        body,
        grid=(B, MOBA_HEADS // HPS, S // TQ),
        in_specs=[
            pl.BlockSpec((1, 1, rows, TQ), lambda b, p, c: (b, c // QPK, p, c % QPK)),
            pl.BlockSpec((1, S, rows), lambda b, p, c: (b, 0, p)),
            pl.BlockSpec((1, S // TK, rows, TK), lambda b, p, c: (b, 0, MOBA_HEADS // HPS + p, 0)),
            pl.BlockSpec((HPS, ne, TB, TQ), lambda b, p, c: (p, 0, 0, 0), pipeline_mode=pl.Buffered(1)),
        ],
        out_specs=pl.BlockSpec((1, rows, TQ), lambda b, p, c: (b, p, c)),
        out_shape=jax.ShapeDtypeStruct((B, MOBA_W, S), BF16),
        scratch_shapes=[pltpu.VMEM((n_pad, rows), F32), pltpu.VMEM((HPS, n_pad, TQ), F32),
                        pltpu.VMEM((HPS, TK, TQ), F32), pltpu.VMEM((HPS, ACC_ROWS, TQ), F32)],
        compiler_params=_params(("parallel", "parallel", "arbitrary")),
        name="moba_attn",
    )(fm, rm, fm, tab)


def _nsa_body(q_ref, kc_ref, vct_ref, fc_ref, ovl_ref, ksl_ref, vsl_ref, kwn_ref, vwn_ref,
              tslc_ref, twin_ref, gz_ref, o_ref, s_scr, acc_scr, s_win, acc_win, sel_ref, out_scr, *, n_sb, n_sel):
    c = pl.program_id(1)
    blk = c // QPK
    G, J = NSA_KV_GROUPS, NSA_HPG
    H = G * J

    q = q_ref[0, 0]
    qs = [q[h * HEAD_DIM:(h + 1) * HEAD_DIM, :] for h in range(H)]
    zero = jnp.zeros((HEAD_DIM, TQ), BF16)
    qpads = [jnp.concatenate([zero] * (h // J) + [qs[h]] + [zero] * (G - 1 - h // J), axis=0) for h in range(H)]

    c0 = pl.multiple_of(c * (TQ // NSA_CMP_STRIDE), TQ // NSA_CMP_STRIDE)
    gate = jax.nn.sigmoid(gz_ref[0])

    def gated(h, branch, o):
        r = (h // J) * GZ_ROWS + branch * J + h % J
        return gate[r:r + 1, :] * o

    for g in range(G):
        kc = kc_ref[0, g]
        vct = vct_ref[0, g]
        psum = jnp.zeros((kc.shape[0], TQ), F32)
        for h in range(g * J, (g + 1) * J):
            s = _dot(kc, qs[h]) + fc_ref[h, pl.ds(c0, kc.shape[0]), :]
            m = jnp.max(s, axis=0, keepdims=True)
            p = jnp.exp2(s - m)
            l = jnp.sum(p, axis=0, keepdims=True)
            pn = p * jnp.where(m > 0.5 * NEG_INF, 1.0 / l, 0.0)
            out_scr[h] = gated(h, 0, _dot(vct, pn.astype(BF16)))
            psum = psum + pn

        ph = psum.astype(BF16)
        plo = (psum - ph.astype(F32)).astype(BF16)
        imp = _dot(ovl_ref[...], ph) + _dot(ovl_ref[...], plo)
        jb = lax.broadcasted_iota(jnp.int32, imp.shape, 0)
        t = c * TQ + lax.broadcasted_iota(jnp.int32, imp.shape, 1)
        sb = t // NSA_SLC_BLOCK
        forced = (jb == 0) | (jb == sb) | (jb == sb - 1)
        allowed = jb <= sb
        val = jnp.where(forced, imp + NSA_FORCE_SCORE, jnp.where(allowed, imp, NEG_INF))
        sel = _rank_select(val, n_sb, n_sel) & allowed
        selb = jnp.where(sel, 0.0, NEG_INF)
        for n in range(n_sb // SPT):
            slab = selb[n * SPT:(n + 1) * SPT, :]
            if SPT < sel_ref.shape[2]:
                slab = jnp.concatenate([slab, jnp.zeros((sel_ref.shape[2] - SPT, TQ), F32)], axis=0)
            sel_ref[g, n] = slab

    def slc_qk(h, n):
        return _dot(ksl_ref[0, pl.ds(pl.multiple_of(n * TK, TK), TK), :], qpads[h])

    def slc_fix(h, n, s):
        rows = sel_ref[h // J, n]
        mask = jnp.concatenate([jnp.broadcast_to(rows[b:b + 1, :], (NSA_SLC_BLOCK, TQ)) for b in range(SPT)], axis=0)
        bias = jnp.concatenate([tslc_ref[h, jnp.clip(c - (KSUB * n + u), -1, NE_BIAS) + 1] for u in range(KSUB)],
                               axis=0)
        return s + mask + bias

    def slc_v(h, n):
        return vsl_ref[0, n, (h // J) * HEAD_DIM:(h // J + 1) * HEAD_DIM, :]

    for h, o in enumerate(_attend(blk + 1, lambda i: blk - i, slc_qk, slc_fix, slc_v, s_scr, acc_scr, H)):
        out_scr[h] += gated(h, 1, o)

    def win_qk(h, n):
        return _dot(kwn_ref[0, pl.ds(pl.multiple_of(n * TK, TK), TK), :], qpads[h])

    def win_fix(h, n, s):
        return s + jnp.concatenate([twin_ref[h, jnp.clip(c - (KSUB * n + u), -1, NE_WIN) + 1] for u in range(KSUB)],
                                   axis=0)

    def win_v(h, n):
        return vwn_ref[0, n, (h // J) * HEAD_DIM:(h // J + 1) * HEAD_DIM, :]

    w_lo = jnp.maximum(c - NE_WIN + 1, 0) // KSUB
    o_win = _attend(blk - w_lo + 1, lambda i: blk - i, win_qk, win_fix, win_v, s_win, acc_win, H)

    for h in range(H):
        o_ref[0, h * HEAD_DIM:(h + 1) * HEAD_DIM, :] = (out_scr[h] + gated(h, 2, o_win[h])).astype(BF16)


def _nsa(fm, rm, gz, kc, vct, tcmp, ovl, tslc, twin, B, S, col_ksl, col_kwn, row_q, row_vsl, row_vwn):
    G, J = NSA_KV_GROUPS, NSA_HPG
    NC = kc.shape[2]
    n_sb = S // NSA_SLC_BLOCK
    body = functools.partial(_nsa_body, n_sb=n_sb, n_sel=min(NSA_TOPN, n_sb))
    H = G * J
    kvrows = G * HEAD_DIM
    one = pl.Buffered(1)
    return pl.pallas_call(
        body,
        grid=(B, S // TQ),
        in_specs=[
            pl.BlockSpec((1, 1, NSA_W, TQ), lambda b, c: (b, c // QPK, row_q // NSA_W, c % QPK)),
            pl.BlockSpec((1, G, NC, HEAD_DIM), lambda b, c: (b, 0, 0, 0)),
            pl.BlockSpec((1, G, HEAD_DIM, NC), lambda b, c: (b, 0, 0, 0)),
            _const_spec(tcmp.shape),
            _const_spec(ovl.shape),
            pl.BlockSpec((1, S, LANES), lambda b, c: (b, 0, col_ksl // LANES)),
            pl.BlockSpec((1, S // TK, kvrows, TK), lambda b, c: (b, 0, row_vsl // kvrows, 0)),
            pl.BlockSpec((1, S, LANES), lambda b, c: (b, 0, col_kwn // LANES)),
            pl.BlockSpec((1, S // TK, kvrows, TK), lambda b, c: (b, 0, row_vwn // kvrows, 0)),
            pl.BlockSpec((H, tslc.shape[1], TB, TQ), lambda b, c: (MOBA_HEADS // H, 0, 0, 0), pipeline_mode=one),
            _const_spec(twin.shape),
            pl.BlockSpec((1, G * GZ_ROWS, TQ), lambda b, c: (b, 0, c)),
        ],
        out_specs=pl.BlockSpec((1, NSA_W, TQ), lambda b, c: (b, 0, c)),
        out_shape=jax.ShapeDtypeStruct((B, NSA_W, S), BF16),
        scratch_shapes=[pltpu.VMEM((H, TK, TQ), F32), pltpu.VMEM((H, ACC_ROWS, TQ), F32),
                        pltpu.VMEM((H, TK, TQ), F32), pltpu.VMEM((H, ACC_ROWS, TQ), F32),
                        pltpu.VMEM((G, S // TK, 8, TQ), F32), pltpu.VMEM((H, HEAD_DIM, TQ), F32)],
        compiler_params=_params(("parallel", "arbitrary")),
        name="nsa_attn",
    )(fm, kc, vct, tcmp, ovl, rm, fm, rm, fm, tslc, twin, gz)


def _inproj1_body(x_ref, g_ref, wfm_ref, wk_ref, wf_ref, bf_ref, tri_ref, place_ref, fm_ref, ka_ref, carry_ref,
                  *, nst):
    i = pl.program_id(0)

    @pl.when(i % nst == 0)
    def _():
        carry_ref[...] = jnp.zeros_like(carry_ref)

    xf = _rmsnorm(x_ref[...], g_ref[...])
    xn = xf.astype(BF16)
    xlo = (xf - xn.astype(F32)).astype(BF16)
    for r0 in range(0, fm_ref.shape[2], CH):
        res = _dot_nt(wfm_ref[r0:r0 + CH, :], xn).astype(BF16)
        for t in range(TM // TK):
            fm_ref[0, t, r0:r0 + CH, :] = res[:, t * TK:(t + 1) * TK]

    fz = _dot(xn, wf_ref[0]) + _dot(xlo, wf_ref[0]) + _dot(xn, wf_ref[1]) + bf_ref[...]
    logf = jnp.minimum(fz, 0.0) - jnp.log(1.0 + jnp.exp(-jnp.abs(fz)))
    tri = tri_ref[...]
    h1, h2, h3 = _split3(logf)
    cum = _dot(tri, h1) + _dot(tri, h2) + _dot(tri, h3) + carry_ref[0:1, :]
    carry_ref[...] = jnp.broadcast_to(cum[TM - 1:TM, :], carry_ref.shape)
    cc = jnp.concatenate(_split3(cum * LOG2E), axis=1)
    low = lax.broadcasted_iota(jnp.int32, (TM, LANES), 1) < HEAD_DIM
    for c0 in range(0, wk_ref.shape[-1], FF_CH):
        kp = _dot(xn, wk_ref[:, c0:c0 + FF_CH])
        dp = _dot(cc, place_ref[:, c0:c0 + FF_CH])
        for t0 in range(0, FF_CH, LANES):
            j = (c0 + t0) // LANES
            kt, dt = kp[:, t0:t0 + LANES], dp[:, t0:t0 + LANES]
            ka_ref[:, 2 * j * LANES:(2 * j + 1) * LANES] = jnp.where(low, kt, dt).astype(BF16)
            ka_ref[:, (2 * j + 1) * LANES:(2 * j + 2) * LANES] = jnp.where(low, dt, kt).astype(BF16)


def _inproj1(x2, g, wfm, wk, wf, bf, tri, place, B, S):
    M = B * S
    nst = S // TM
    n_fm, n_ka = wfm.shape[0], FOX_HEADS * LANES
    return pl.pallas_call(
        functools.partial(_inproj1_body, nst=nst),
        grid=(M // TM,),
        in_specs=[
            pl.BlockSpec((TM, D_MODEL), lambda i: (i, 0)),
            _const_spec((1, D_MODEL)),
            _const_spec(wfm.shape),
            _const_spec(wk.shape),
            _const_spec(wf.shape),
            _const_spec(bf.shape),
            _const_spec(tri.shape),
            _const_spec(place.shape),
        ],
        out_specs=[
            pl.BlockSpec((1, TM // TK, n_fm, TK), lambda i: (i // nst, i % nst, 0, 0)),
            pl.BlockSpec((TM, n_ka), lambda i: (i, 0)),
        ],
        out_shape=[
            jax.ShapeDtypeStruct((B, S // TK, n_fm, TK), BF16),
            jax.ShapeDtypeStruct((M, n_ka), BF16),
        ],
        scratch_shapes=[pltpu.VMEM((8, LANES), F32)],
        compiler_params=_params(("arbitrary",)),
        name="inproj1",
    )(x2, g, wfm, wk, wf, bf, tri, place)


def _fox_body(q_ref, k_ref, v_ref, cm_ref, o_ref, s_scr, acc_scr):
    c = pl.program_id(2)
    blk = c // QPK
    q = q_ref[0, 0]
    ones = jnp.ones((LANES - HEAD_DIM, TQ), BF16)
    qhs = [q[h * HEAD_DIM:(h + 1) * HEAD_DIM, :] for h in range(FOX_HPS)]
    qas = [jnp.concatenate([qhs[h], ones] if h % 2 == 0 else [ones, qhs[h]], axis=0) for h in range(FOX_HPS)]

    def qk_fn(h, n):
        rows = pl.ds(pl.multiple_of(n * TK, TK), TK)
        return _dot(k_ref[0, rows, h * LANES:(h + 1) * LANES], qas[h])

    causal = cm_ref[c % QPK]
    outs = _attend(blk + 1, lambda i: blk - i, qk_fn, None,
                   lambda h, n: v_ref[0, n, h * HEAD_DIM:(h + 1) * HEAD_DIM, :], s_scr, acc_scr, FOX_HPS,
                   first_fix=lambda h, n, s: s + causal)
    o_ref[0] = jnp.concatenate(outs, axis=0).astype(BF16)


def _fox(fm, ka, cmask, B, S):
    rows = FOX_HPS * HEAD_DIM
    return pl.pallas_call(
        _fox_body,
        grid=(B, FOX_HEADS // FOX_HPS, S // TQ),
        in_specs=[
            pl.BlockSpec((1, 1, rows, TQ), lambda b, h, c: (b, c // QPK, h, c % QPK)),
            pl.BlockSpec((1, S, FOX_HPS * LANES), lambda b, h, c: (b, 0, h), pipeline_mode=pl.Buffered(1)),
            pl.BlockSpec((1, S // TK, rows, TK), lambda b, h, c: (b, 0, FOX_HEADS // FOX_HPS + h, 0),
                         pipeline_mode=pl.Buffered(1)),
            _const_spec(cmask.shape),
        ],
        out_specs=pl.BlockSpec((1, rows, TQ), lambda b, h, c: (b, h, c)),
        out_shape=jax.ShapeDtypeStruct((B, FOX_W, S), BF16),
        scratch_shapes=[pltpu.VMEM((FOX_HPS, TK, TQ), F32), pltpu.VMEM((FOX_HPS, ACC_ROWS, TQ), F32)],
        compiler_params=_params(("parallel", "parallel", "arbitrary")),
        name="fox_attn",
    )(fm, ka, fm, cmask)


def _post_body(*refs, n_parts, final):
    o_refs = refs[:n_parts]
    h_ref, wo_ref, g_ref, w1_ref, w2_ref = refs[n_parts:n_parts + 5]
    gf_ref = refs[n_parts + 5] if final else None
    out_ref, hn_ref = refs[-2:]
    h1 = h_ref[...]
    r0 = 0
    for o_ref in o_refs:
        nf = o_ref.shape[1]
        h1 = h1 + _dot_tn(o_ref[0], wo_ref[r0:r0 + nf, :])
        r0 += nf
    out_ref[...] = h1
    hn_ref[...] = _rmsnorm(out_ref[...], g_ref[...]).astype(BF16)
    for c0 in range(0, D_FF, FF_CH):
        a = jnp.maximum(_dot(hn_ref[...], w1_ref[:, c0:c0 + FF_CH]), 0.0)
        out_ref[...] += _dot((a * a).astype(BF16), w2_ref[c0:c0 + FF_CH, :])
    if final:
        out_ref[...] = _rmsnorm(out_ref[...], gf_ref[...])


def _post(o_parts, h2, wo, g, w1, w2, gf, B, S):
    M = B * S
    nst = S // TM
    final = gf is not None
    in_specs = [pl.BlockSpec((1, o.shape[1], TM), lambda i: (i // nst, 0, i % nst)) for o in o_parts]
    in_specs += [
        pl.BlockSpec((TM, D_MODEL), lambda i: (i, 0)),
        _const_spec(wo.shape),
        _const_spec((1, D_MODEL)),
        _const_spec(w1.shape),
        _const_spec(w2.shape),
    ]
    args = list(o_parts) + [h2, wo, g, w1, w2]
    if final:
        in_specs.append(_const_spec((1, D_MODEL)))
        args.append(gf)
    return pl.pallas_call(
        functools.partial(_post_body, n_parts=len(o_parts), final=final),
        grid=(M // TM,),
        in_specs=in_specs,
        out_specs=pl.BlockSpec((TM, D_MODEL), lambda i: (i, 0)),
        out_shape=jax.ShapeDtypeStruct((M, D_MODEL), F32),
        scratch_shapes=[pltpu.VMEM((TM, D_MODEL), BF16)],
        compiler_params=_params(("parallel",)),
        name="post_final" if final else "post",
    )(*args)


def _rel_bucket(dist):
    n = jnp.maximum(dist, 0)
    max_exact = REL_BUCKETS // 2
    nf = jnp.maximum(n, 1).astype(jnp.float32)
    large = max_exact + (jnp.log(nf / max_exact) / math.log(REL_MAX_DISTANCE / max_exact)
                         * (REL_BUCKETS - max_exact)).astype(jnp.int32)
    large = jnp.minimum(large, REL_BUCKETS - 1)
    return jnp.where(n < max_exact, n, large)


def _bias_tables(rel_bias, S):
    n_heads = rel_bias.shape[1]
    table = rel_bias.T * LOG2E

    def bias_of(dist):
        tab = table.reshape((n_heads, REL_BUCKETS) + (1,) * dist.ndim)
        bkt = _rel_bucket(jnp.asarray(dist))[None]
        out = jnp.zeros((n_heads,) + dist.shape, F32)
        for b in range(REL_BUCKETS):
            out = jnp.where(bkt == b, tab[:, b], out)
        return jnp.where(jnp.asarray(dist)[None] >= 0, out, NEG_INF)

    d = (np.arange(-1, NE_BIAS)[:, None, None] * TQ + np.arange(TQ)[None, None, :] - np.arange(TB)[None, :, None])
    vals = bias_of(d)
    far = table[:, REL_BUCKETS - 1][:, None, None, None]
    tile = jnp.where(d >= 0, vals - far, NEG_INF)
    tile = jnp.concatenate([tile, jnp.zeros_like(tile[:, :1])], axis=1)
    dw = d[:NE_WIN + 2]
    twin = jnp.where((dw >= 0) & (dw < NSA_WINDOW), vals[MOBA_HEADS:, :NE_WIN + 2], NEG_INF)
    n_c = S // NSA_CMP_STRIDE
    u = np.arange(2 * n_c)[:, None]
    dc = np.arange(TQ)[None, :] + NSA_CMP_STRIDE * (u - (n_c - 1)) - (NSA_CMP_BLOCK - 1)
    fcmp = bias_of(dc)[MOBA_HEADS:]
    return tile, twin, fcmp


def _selection_constants(S):
    n_c = S // NSA_CMP_STRIDE
    n_cmp = (S - NSA_CMP_BLOCK) // NSA_CMP_STRIDE + 1
    n_sb = S // NSA_SLC_BLOCK
    ci = np.arange(n_c)[None, :] * NSA_CMP_STRIDE
    sj = np.arange(n_sb)[:, None] * NSA_SLC_BLOCK
    ovl = (ci < sj + NSA_SLC_BLOCK) & (ci + NSA_CMP_BLOCK > sj) & (np.arange(n_c)[None, :] < n_cmp)
    ovl = ovl[:, ::-1]
    return jnp.asarray(ovl, BF16)


def _causal_tiles():
    e = np.arange(QPK)[:, None, None]
    d = e * TQ + np.arange(TQ)[None, None, :] - np.arange(TK)[None, :, None]
    return jnp.asarray(np.where(d >= 0, 0.0, NEG_INF), F32)


def kernel(x, rel_bias, mix_norm, mlp_norm, even_w_in, even_w_out, cmp_pos_k, cmp_pos_v, cmp_k_w1, cmp_k_w2,
           cmp_v_w1, cmp_v_w2, odd_w_in, odd_b_forget, odd_w_out, mlp_w1, mlp_w2, final_norm):
    B, S, D = x.shape
    assert D == D_MODEL and S % TM == 0
    G, J = NSA_KV_GROUPS, NSA_HPG
    h = x.reshape(B * S, D)

    offs = np.cumsum((MOBA_W, MOBA_W, MOBA_W, NSA_W) + (NSA_KV_W,) * 6)
    mq_w, mk_w, mv_w, nq_w, kc_w, vc_w, ksl_w, vsl_w, kwn_w, vwn_w, gz_w = jnp.split(even_w_in[0], offs, axis=1)
    wrm = jnp.concatenate([mk_w, ksl_w, kwn_w], axis=1).astype(BF16)
    col_ksl, col_kwn = MOBA_W, MOBA_W + NSA_KV_W
    wcv = jnp.concatenate([kc_w, vc_w], axis=1).astype(BF16)
    qs = SCALE * LOG2E
    wfm = jnp.concatenate([mq_w * qs, mv_w, nq_w * qs, vsl_w, vwn_w], axis=1).T.astype(BF16)
    row_nq, row_vsl, row_vwn = 2 * MOBA_W, 2 * MOBA_W + NSA_W, 2 * MOBA_W + NSA_W + NSA_KV_W
    gzw = gz_w.T.reshape(G, J, 3, D).transpose(0, 2, 1, 3).reshape(G, 3 * J, D)
    gzw = jnp.pad(gzw, ((0, 0), (0, GZ_ROWS - 3 * J), (0, 0))).reshape(G * GZ_ROWS, D).astype(BF16)

    rm, fm, gz, cv = _inproj0(h, mix_norm[0][None, :], wrm, wfm, gzw, wcv, B, S)
    rm = rm.reshape(B, S, -1)

    tile, twin, tcmp = _bias_tables(rel_bias, S)
    ovl = _selection_constants(S)

    o_moba = _moba(fm, rm, tile, B, S)

    n_c = S // NSA_CMP_STRIDE
    r = cv.reshape(2 * G, B, n_c, NSA_CMP_STRIDE * HEAD_DIM)
    pos = jnp.stack([cmp_pos_k[0].reshape(1, -1), cmp_pos_v[0].reshape(1, -1)])
    pos = jnp.pad(pos, ((0, 0), (0, 7), (0, 0))).astype(BF16)
    w1c = jnp.stack([cmp_k_w1[0], cmp_v_w1[0]]).astype(BF16)
    kc, vct = _compress(r, pos, w1c, cmp_k_w2[0].astype(BF16), cmp_v_w2[0].T.astype(BF16), B, n_c)

    o_nsa = _nsa(fm, rm, gz, kc, vct, tcmp, ovl, tile, twin, B, S,
                 col_ksl, col_kwn, row_nq, row_vsl, row_vwn)

    h = _post([o_moba, o_nsa], h, even_w_out[0].astype(BF16), mlp_norm[0][None, :],
              mlp_w1[0].astype(BF16), mlp_w2[0].astype(BF16), None, B, S)

    q_w, k_w, v_w, f_w = jnp.split(odd_w_in[0], np.cumsum((FOX_W, FOX_W, FOX_W)), axis=1)
    wfm1 = jnp.concatenate([q_w * qs, v_w], axis=1).T.astype(BF16)
    wk = k_w.astype(BF16)
    f_w = jnp.pad(f_w, ((0, 0), (0, LANES - FOX_HEADS)))
    f_hi = f_w.astype(BF16)
    wf = jnp.stack([f_hi, (f_w - f_hi.astype(F32)).astype(BF16)])
    bf = jnp.pad(odd_b_forget[0], (0, LANES - FOX_HEADS))[None, :]
    tri = jnp.asarray(np.tril(np.ones((TM, TM))), BF16)
    place = np.zeros((3 * LANES, FOX_W), np.float32)
    heads = np.arange(FOX_HEADS)
    for term in range(3):
        place[term * LANES + heads, (heads // 2) * LANES + (1 - heads % 2) * HEAD_DIM + term] = -1.0
    fm1, ka = _inproj1(h, mix_norm[1][None, :], wfm1, wk, wf, bf, tri, jnp.asarray(place, BF16), B, S)
    o_fox = _fox(fm1, ka.reshape(B, S, -1), _causal_tiles(), B, S)

    h = _post([o_fox], h, odd_w_out[0].astype(BF16), mlp_norm[1][None, :],
              mlp_w1[1].astype(BF16), mlp_w2[1].astype(BF16), final_norm[None, :], B, S)
    return h.reshape(B, S, D)
```

```python
import functools
import math

import numpy as np
import jax
import jax.numpy as jnp
from jax import lax
from jax.experimental import pallas as pl
from jax.experimental.pallas import tpu as pltpu

D_MODEL = 1024
HEAD_DIM = 64
MOBA_HEADS = 8
MOBA_BLOCK = 256
MOBA_TOPK = 3
NSA_HEADS = 8
NSA_KV_GROUPS = 2
NSA_HPG = NSA_HEADS // NSA_KV_GROUPS
NSA_CMP_BLOCK = 32
NSA_CMP_STRIDE = 16
NSA_CMP_HIDDEN = 256
NSA_SLC_BLOCK = 64
NSA_TOPN = 16
NSA_WINDOW = 512
NSA_FORCE_SCORE = 1e6
FOX_HEADS = 16
D_FF = 4 * D_MODEL
REL_BUCKETS = 32
REL_MAX_DISTANCE = 1024
RMS_EPS = 1e-5
NEG_INF = -1e30
SCALE = HEAD_DIM ** -0.5

MOBA_W = MOBA_HEADS * HEAD_DIM
NSA_W = NSA_HEADS * HEAD_DIM
NSA_KV_W = NSA_KV_GROUPS * HEAD_DIM
FOX_W = FOX_HEADS * HEAD_DIM

LANES = 128
SUBLANES = 8
TQ = 256
TK = 256
TB = 256
QPK = TK // TQ
KSUB = TK // TB
TM = 512
CH = 256
FF_CH = 512
VMEM_LIMIT = 56 * 1024 * 1024
NE_BIAS = -(-(REL_MAX_DISTANCE + TB - 1) // TQ)
NE_WIN = -(-(NSA_WINDOW + TB - 1) // TQ)
GZ_ROWS = 16
HPS = 8
FOX_HPS = 16
SPT = TK // NSA_SLC_BLOCK
BF16_ROWS = 2 * SUBLANES
ACC_ROWS = HEAD_DIM + BF16_ROWS
ROW_CHUNK = 64
LOG2E = math.log2(math.e)

assert TK % TQ == 0 and TK % TB == 0 and TQ == TB and MOBA_BLOCK == TB and TB % NSA_SLC_BLOCK == 0 and SPT <= 8

F32 = jnp.float32
BF16 = jnp.bfloat16


def _dot(a, b):
    return jnp.dot(a, b, preferred_element_type=F32)


def _dot_nt(a, b):
    return lax.dot_general(a, b, (((1,), (1,)), ((), ())), preferred_element_type=F32)


def _dot_tn(a, b):
    return lax.dot_general(a, b, (((0,), (0,)), ((), ())), preferred_element_type=F32)


def _rmsnorm(x, g):
    ms = jnp.mean(x * x, axis=-1, keepdims=True)
    return x * lax.rsqrt(ms + RMS_EPS) * g


def _split3(x):
    a = x.astype(BF16)
    r = x - a.astype(F32)
    b = r.astype(BF16)
    c = (r - b.astype(F32)).astype(BF16)
    return a, b, c


def _const_spec(shape):
    nd = len(shape)
    return pl.BlockSpec(shape, lambda *_: (0,) * nd, pipeline_mode=pl.Buffered(1))


def _params(sem):
    return pltpu.CompilerParams(dimension_semantics=sem, vmem_limit_bytes=VMEM_LIMIT)


def _attend(n_tiles, tile_of, qk_fn, fix_fn, v_fn, s_scr, acc_scr, n_heads, first_fix=None):
    def put_scores(h, n, fix):
        s = qk_fn(h, n)
        s_scr[h] = s if fix is None else fix(h, n, s)

    for h in range(n_heads):
        put_scores(h, tile_of(0), fix_fn if first_fix is None else first_fix)
    acc_scr[...] = jnp.zeros_like(acc_scr)
    last = n_tiles - 1
    ones = jnp.ones((ACC_ROWS - HEAD_DIM, TK), BF16)
    chunks = range(0, TK, ROW_CHUNK)

    def body(i, ms):
        n = tile_of(i)
        n_next = tile_of(jnp.minimum(i + 1, last))
        out = []
        for h in range(n_heads):
            mx = s_scr[h, 0:ROW_CHUNK, :]
            for r0 in chunks[1:]:
                mx = jnp.maximum(mx, s_scr[h, r0:r0 + ROW_CHUNK, :])
            m_new = jnp.maximum(ms[h], jnp.max(mx, axis=0, keepdims=True))
            alpha = jnp.exp2(ms[h] - m_new)
            p = jnp.concatenate([jnp.exp2(s_scr[h, r0:r0 + ROW_CHUNK, :] - m_new).astype(BF16) for r0 in chunks],
                                axis=0)
            va = jnp.concatenate([v_fn(h, n), ones], axis=0)
            acc_scr[h] = alpha * acc_scr[h] + _dot(va, p)
            out.append(m_new)
            put_scores(h, n_next, fix_fn)
        return tuple(out)

    lax.fori_loop(0, n_tiles, body, tuple(jnp.full((1, TQ), NEG_INF, F32) for _ in range(n_heads)))
    return [acc_scr[h, :HEAD_DIM, :] / acc_scr[h, HEAD_DIM:HEAD_DIM + 1, :] for h in range(n_heads)]


def _rank_select(val, n_rows, k):
    sub = lax.broadcasted_iota(jnp.int32, (SUBLANES, val.shape[1]), 0)
    groups = [val[g0:g0 + SUBLANES, :] for g0 in range(0, val.shape[0], SUBLANES)]
    counts = [jnp.zeros(g.shape, F32) for g in groups]
    for m in range(n_rows):
        vm = val[m:m + 1, :]
        for g, vg in enumerate(groups):
            if g * SUBLANES > m:
                counts[g] = counts[g] + jnp.where(vm >= vg, 1.0, 0.0)
            elif (g + 1) * SUBLANES <= m:
                counts[g] = counts[g] + jnp.where(vm > vg, 1.0, 0.0)
            else:
                tie = jnp.where(sub > m % SUBLANES, 1.0, 0.0)
                counts[g] = counts[g] + jnp.where(vm > vg, 1.0, 0.0) + jnp.where(vm == vg, tie, 0.0)
    return jnp.concatenate(counts, axis=0) < k


def _inproj0_body(x_ref, g_ref, wrm_ref, wfm_ref, wgz_ref, wcv_ref, rm_ref, fm_ref, gz_ref, cv_ref):
    xn = _rmsnorm(x_ref[...], g_ref[...]).astype(BF16)
    cv = _dot(xn, wcv_ref[...]).astype(BF16)
    for j in range(cv_ref.shape[0]):
        cv_ref[j] = cv[:, j * HEAD_DIM:(j + 1) * HEAD_DIM]
    for c0 in range(0, rm_ref.shape[-1], CH):
        rm_ref[:, c0:c0 + CH] = _dot(xn, wrm_ref[:, c0:c0 + CH]).astype(BF16)
    for r0 in range(0, fm_ref.shape[2], CH):
        res = _dot_nt(wfm_ref[r0:r0 + CH, :], xn).astype(BF16)
        for t in range(TM // TK):
            fm_ref[0, t, r0:r0 + CH, :] = res[:, t * TK:(t + 1) * TK]
    gz_ref[0] = _dot_nt(wgz_ref[...], xn)


def _inproj0(x2, g, wrm, wfm, wgz, wcv, B, S):
    M = B * S
    nst = S // TM
    n_rm, n_fm, n_gz, n_cv = wrm.shape[1], wfm.shape[0], wgz.shape[0], wcv.shape[1] // HEAD_DIM
    return pl.pallas_call(
        _inproj0_body,
        grid=(M // TM,),
        in_specs=[
            pl.BlockSpec((TM, D_MODEL), lambda i: (i, 0)),
            _const_spec((1, D_MODEL)),
            _const_spec((D_MODEL, n_rm)),
            _const_spec((n_fm, D_MODEL)),
            _const_spec((n_gz, D_MODEL)),
            _const_spec(wcv.shape),
        ],
        out_specs=[
            pl.BlockSpec((TM, n_rm), lambda i: (i, 0)),
            pl.BlockSpec((1, TM // TK, n_fm, TK), lambda i: (i // nst, i % nst, 0, 0)),
            pl.BlockSpec((1, n_gz, TM), lambda i: (i // nst, 0, i % nst)),
            pl.BlockSpec((n_cv, TM, HEAD_DIM), lambda i: (0, i, 0)),
        ],
        out_shape=[
            jax.ShapeDtypeStruct((M, n_rm), BF16),
            jax.ShapeDtypeStruct((B, S // TK, n_fm, TK), BF16),
            jax.ShapeDtypeStruct((B, n_gz, S), F32),
            jax.ShapeDtypeStruct((n_cv, M, HEAD_DIM), BF16),
        ],
        compiler_params=_params(("parallel",)),
        name="inproj0",
    )(x2, g, wrm, wfm, wgz, wcv)


def _compress_body(rk_ref, rv_ref, pos_ref, w1_ref, w2k_ref, w2vt_ref, flip_ref, kc_ref, vct_ref):
    half = NSA_CMP_STRIDE * HEAD_DIM

    def hidden(r_ref, s):
        r = r_ref[0, 0]
        a = _dot(r, w1_ref[s, :half, :])
        b = _dot(r, w1_ref[s, half:, :])
        nxt = pltpu.roll(b, b.shape[0] - 1, axis=0)
        posb = _dot(pos_ref[s], w1_ref[s])[0:1]
        pre = a + nxt + posb
        act = (pre * jax.nn.sigmoid(pre)).astype(BF16)
        return _dot(flip_ref[...], act).astype(BF16)

    kc_ref[0, 0] = _dot(hidden(rk_ref, 0), w2k_ref[...]).astype(BF16)
    vct_ref[0, 0] = _dot_nt(w2vt_ref[...], hidden(rv_ref, 1)).astype(BF16)


def _compress(r, pos, w1, w2k, w2vt, B, NC):
    G = NSA_KV_GROUPS
    half = NSA_CMP_STRIDE * HEAD_DIM
    return pl.pallas_call(
        _compress_body,
        grid=(B, G),
        in_specs=[
            pl.BlockSpec((1, 1, NC, half), lambda b, g: (g, b, 0, 0)),
            pl.BlockSpec((1, 1, NC, half), lambda b, g: (G + g, b, 0, 0)),
            _const_spec(pos.shape),
            _const_spec(w1.shape),
            _const_spec(w2k.shape),
            _const_spec(w2vt.shape),
            _const_spec((NC, NC)),
        ],
        out_specs=[
            pl.BlockSpec((1, 1, NC, HEAD_DIM), lambda b, g: (b, g, 0, 0)),
            pl.BlockSpec((1, 1, HEAD_DIM, NC), lambda b, g: (b, g, 0, 0)),
        ],
        out_shape=[
            jax.ShapeDtypeStruct((B, G, NC, HEAD_DIM), BF16),
            jax.ShapeDtypeStruct((B, G, HEAD_DIM, NC), BF16),
        ],
        compiler_params=_params(("parallel", "parallel")),
        name="nsa_compress",
    )(r, r, pos, w1, w2k, w2vt, jnp.asarray(np.eye(NC)[::-1], BF16))


def _moba_body(q_ref, k_ref, v_ref, t_ref, o_ref, kmean_ref, mask_ref, s_scr, acc_scr, *, n_mb, topk):
    c = pl.program_id(2)
    blk = c // QPK
    qblk = c * TQ // MOBA_BLOCK

    @pl.when(c == 0)
    def _():
        kmean_ref[...] = jnp.zeros_like(kmean_ref)
        for n in range(n_mb):
            kblk = k_ref[0, n * MOBA_BLOCK:(n + 1) * MOBA_BLOCK, :].astype(F32)
            kmean_ref[n:n + 1, :] = jnp.mean(kblk, axis=0, keepdims=True)

    q = q_ref[0, 0]
    rowi = lax.broadcasted_iota(jnp.int32, (LANES, TQ), 0)
    nidx = lax.broadcasted_iota(jnp.int32, (kmean_ref.shape[0], TQ), 0)
    n_pad = kmean_ref.shape[0]
    km = jnp.concatenate(_split3(kmean_ref[...]), axis=0)
    qpads = []
    for h in range(HPS):
        lo = (h // 2) * LANES
        qpair = q[lo:lo + LANES, :]
        qh = jnp.where(rowi // HEAD_DIM == h % 2, qpair, jnp.zeros_like(qpair))
        terms = _dot(km[:, lo:lo + LANES], qh)
        route = terms[:n_pad] + terms[n_pad:2 * n_pad] + terms[2 * n_pad:]
        route = jnp.where(nidx < qblk, route, NEG_INF)
        sel = _rank_select(route, n_mb, topk) & (nidx < qblk)
        mask_ref[h] = jnp.where(sel | (nidx == qblk), 0.0, NEG_INF)
        qpads.append(qh)

    def qk_fn(h, n):
        rows = pl.ds(pl.multiple_of(n * TK, TK), TK)
        return _dot(k_ref[0, rows, (h // 2) * LANES:(h // 2 + 1) * LANES], qpads[h])

    def fix_fn(h, n, s):
        parts = []
        for u in range(KSUB):
            b = KSUB * n + u
            parts.append(mask_ref[h, pl.ds(b, 1), :] + t_ref[h, jnp.clip(c - b, -1, NE_BIAS) + 1])
        return s + jnp.concatenate(parts, axis=0)

    def v_fn(h, n):
        return v_ref[0, n, h * HEAD_DIM:(h + 1) * HEAD_DIM, :]

    outs = _attend(blk + 1, lambda i: blk - i, qk_fn, fix_fn, v_fn, s_scr, acc_scr, HPS)
    o_ref[0] = jnp.concatenate(outs, axis=0).astype(BF16)


def _moba(fm, rm, tab, B, S):
    n_mb = S // MOBA_BLOCK
    n_pad = -(-n_mb // BF16_ROWS) * BF16_ROWS
    ne = tab.shape[1]
    rows = HPS * HEAD_DIM
    body = functools.partial(_moba_body, n_mb=n_mb, topk=min(MOBA_TOPK, n_mb))
    return pl.pallas_call(
        body,
        grid=(B, MOBA_HEADS // HPS, S // TQ),
        in_specs=[
            pl.BlockSpec((1, 1, rows, TQ), lambda b, p, c: (b, c // QPK, p, c % QPK)),
            pl.BlockSpec((1, S, rows), lambda b, p, c: (b, 0, p)),
            pl.BlockSpec((1, S // TK, rows, TK), lambda b, p, c: (b, 0, MOBA_HEADS // HPS + p, 0)),
            pl.BlockSpec((HPS, ne, TB, TQ), lambda b, p, c: (p, 0, 0, 0), pipeline_mode=pl.Buffered(1)),
        ],
        out_specs=pl.BlockSpec((1, rows, TQ), lambda b, p, c: (b, p, c)),
        out_shape=jax.ShapeDtypeStruct((B, MOBA_W, S), BF16),
        scratch_shapes=[pltpu.VMEM((n_pad, rows), F32), pltpu.VMEM((HPS, n_pad, TQ), F32),
                        pltpu.VMEM((HPS, TK, TQ), F32), pltpu.VMEM((HPS, ACC_ROWS, TQ), F32)],
        compiler_params=_params(("parallel", "parallel", "arbitrary")),
        name="moba_attn",
    )(fm, rm, fm, tab)


def _nsa_body(q_ref, kc_ref, vct_ref, fc_ref, ovl_ref, ksl_ref, vsl_ref, kwn_ref, vwn_ref,
              tslc_ref, twin_ref, gz_ref, o_ref, s_scr, acc_scr, s_win, acc_win, sel_ref, out_scr, *, n_sb, n_sel):
    c = pl.program_id(1)
    blk = c // QPK
    G, J = NSA_KV_GROUPS, NSA_HPG
    H = G * J

    q = q_ref[0, 0]
    qs = [q[h * HEAD_DIM:(h + 1) * HEAD_DIM, :] for h in range(H)]
    zero = jnp.zeros((HEAD_DIM, TQ), BF16)
    qpads = [jnp.concatenate([zero] * (h // J) + [qs[h]] + [zero] * (G - 1 - h // J), axis=0) for h in range(H)]

    c0 = pl.multiple_of(c * (TQ // NSA_CMP_STRIDE), TQ // NSA_CMP_STRIDE)
    gate = jax.nn.sigmoid(gz_ref[0])

    def gated(h, branch, o):
        r = (h // J) * GZ_ROWS + branch * J + h % J
        return gate[r:r + 1, :] * o

    for g in range(G):
        kc = kc_ref[0, g]
        vct = vct_ref[0, g]
        psum = jnp.zeros((kc.shape[0], TQ), F32)
        for h in range(g * J, (g + 1) * J):
            s = _dot(kc, qs[h]) + fc_ref[h, pl.ds(c0, kc.shape[0]), :]
            m = jnp.max(s, axis=0, keepdims=True)
            p = jnp.exp2(s - m)
            l = jnp.sum(p, axis=0, keepdims=True)
            pn = p * jnp.where(m > 0.5 * NEG_INF, 1.0 / l, 0.0)
            out_scr[h] = gated(h, 0, _dot(vct, pn.astype(BF16)))
            psum = psum + pn

        ph = psum.astype(BF16)
        plo = (psum - ph.astype(F32)).astype(BF16)
        imp = _dot(ovl_ref[...], ph) + _dot(ovl_ref[...], plo)
        jb = lax.broadcasted_iota(jnp.int32, imp.shape, 0)
        t = c * TQ + lax.broadcasted_iota(jnp.int32, imp.shape, 1)
        sb = t // NSA_SLC_BLOCK
        forced = (jb == 0) | (jb == sb) | (jb == sb - 1)
        allowed = jb <= sb
        val = jnp.where(forced, imp + NSA_FORCE_SCORE, jnp.where(allowed, imp, NEG_INF))
        sel = _rank_select(val, n_sb, n_sel) & allowed
        selb = jnp.where(sel, 0.0, NEG_INF)
        for n in range(n_sb // SPT):
            slab = selb[n * SPT:(n + 1) * SPT, :]
            if SPT < sel_ref.shape[2]:
                slab = jnp.concatenate([slab, jnp.zeros((sel_ref.shape[2] - SPT, TQ), F32)], axis=0)
            sel_ref[g, n] = slab

    def slc_qk(h, n):
        return _dot(ksl_ref[0, pl.ds(pl.multiple_of(n * TK, TK), TK), :], qpads[h])

    def slc_fix(h, n, s):
        rows = sel_ref[h // J, n]
        mask = jnp.concatenate([jnp.broadcast_to(rows[b:b + 1, :], (NSA_SLC_BLOCK, TQ)) for b in range(SPT)], axis=0)
        bias = jnp.concatenate([tslc_ref[h, jnp.clip(c - (KSUB * n + u), -1, NE_BIAS) + 1] for u in range(KSUB)],
                               axis=0)
        return s + mask + bias

    def slc_v(h, n):
        return vsl_ref[0, n, (h // J) * HEAD_DIM:(h // J + 1) * HEAD_DIM, :]

    for h, o in enumerate(_attend(blk + 1, lambda i: blk - i, slc_qk, slc_fix, slc_v, s_scr, acc_scr, H)):
        out_scr[h] += gated(h, 1, o)

    def win_qk(h, n):
        return _dot(kwn_ref[0, pl.ds(pl.multiple_of(n * TK, TK), TK), :], qpads[h])

    def win_fix(h, n, s):
        return s + jnp.concatenate([twin_ref[h, jnp.clip(c - (KSUB * n + u), -1, NE_WIN) + 1] for u in range(KSUB)],
                                   axis=0)

    def win_v(h, n):
        return vwn_ref[0, n, (h // J) * HEAD_DIM:(h // J + 1) * HEAD_DIM, :]

    w_lo = jnp.maximum(c - NE_WIN + 1, 0) // KSUB
    o_win = _attend(blk - w_lo + 1, lambda i: blk - i, win_qk, win_fix, win_v, s_win, acc_win, H)

    for h in range(H):
        o_ref[0, h * HEAD_DIM:(h + 1) * HEAD_DIM, :] = (out_scr[h] + gated(h, 2, o_win[h])).astype(BF16)


def _nsa(fm, rm, gz, kc, vct, tcmp, ovl, tslc, twin, B, S, col_ksl, col_kwn, row_q, row_vsl, row_vwn):
    G, J = NSA_KV_GROUPS, NSA_HPG
    NC = kc.shape[2]
    n_sb = S // NSA_SLC_BLOCK
    body = functools.partial(_nsa_body, n_sb=n_sb, n_sel=min(NSA_TOPN, n_sb))
    H = G * J
    kvrows = G * HEAD_DIM
    one = pl.Buffered(1)
    return pl.pallas_call(
        body,
        grid=(B, S // TQ),
        in_specs=[
            pl.BlockSpec((1, 1, NSA_W, TQ), lambda b, c: (b, c // QPK, row_q // NSA_W, c % QPK)),
            pl.BlockSpec((1, G, NC, HEAD_DIM), lambda b, c: (b, 0, 0, 0)),
            pl.BlockSpec((1, G, HEAD_DIM, NC), lambda b, c: (b, 0, 0, 0)),
            _const_spec(tcmp.shape),
            _const_spec(ovl.shape),
            pl.BlockSpec((1, S, LANES), lambda b, c: (b, 0, col_ksl // LANES)),
            pl.BlockSpec((1, S // TK, kvrows, TK), lambda b, c: (b, 0, row_vsl // kvrows, 0)),
            pl.BlockSpec((1, S, LANES), lambda b, c: (b, 0, col_kwn // LANES)),
            pl.BlockSpec((1, S // TK, kvrows, TK), lambda b, c: (b, 0, row_vwn // kvrows, 0)),
            pl.BlockSpec((H, tslc.shape[1], TB, TQ), lambda b, c: (MOBA_HEADS // H, 0, 0, 0), pipeline_mode=one),
            _const_spec(twin.shape),
            pl.BlockSpec((1, G * GZ_ROWS, TQ), lambda b, c: (b, 0, c)),
        ],
        out_specs=pl.BlockSpec((1, NSA_W, TQ), lambda b, c: (b, 0, c)),
        out_shape=jax.ShapeDtypeStruct((B, NSA_W, S), BF16),
        scratch_shapes=[pltpu.VMEM((H, TK, TQ), F32), pltpu.VMEM((H, ACC_ROWS, TQ), F32),
                        pltpu.VMEM((H, TK, TQ), F32), pltpu.VMEM((H, ACC_ROWS, TQ), F32),
                        pltpu.VMEM((G, S // TK, SUBLANES, TQ), F32), pltpu.VMEM((H, HEAD_DIM, TQ), F32)],
        compiler_params=_params(("parallel", "arbitrary")),
        name="nsa_attn",
    )(fm, kc, vct, tcmp, ovl, rm, fm, rm, fm, tslc, twin, gz)


def _inproj1_body(x_ref, g_ref, wfm_ref, wk_ref, wf_ref, bf_ref, tri_ref, place_ref, fm_ref, ka_ref, carry_ref,
                  *, nst):
    i = pl.program_id(0)

    @pl.when(i % nst == 0)
    def _():
        carry_ref[...] = jnp.zeros_like(carry_ref)

    xf = _rmsnorm(x_ref[...], g_ref[...])
    xn = xf.astype(BF16)
    xlo = (xf - xn.astype(F32)).astype(BF16)
    for r0 in range(0, fm_ref.shape[2], CH):
        res = _dot_nt(wfm_ref[r0:r0 + CH, :], xn).astype(BF16)
        for t in range(TM // TK):
            fm_ref[0, t, r0:r0 + CH, :] = res[:, t * TK:(t + 1) * TK]

    fz = _dot(xn, wf_ref[0]) + _dot(xlo, wf_ref[0]) + _dot(xn, wf_ref[1]) + bf_ref[...]
    logf = jnp.minimum(fz, 0.0) - jnp.log(1.0 + jnp.exp(-jnp.abs(fz)))
    tri = tri_ref[...]
    h1, h2, h3 = _split3(logf)
    cum = _dot(tri, h1) + _dot(tri, h2) + _dot(tri, h3) + carry_ref[0:1, :]
    carry_ref[...] = jnp.broadcast_to(cum[TM - 1:TM, :], carry_ref.shape)
    cc = jnp.concatenate(_split3(cum * LOG2E), axis=1)
    low = lax.broadcasted_iota(jnp.int32, (TM, LANES), 1) < HEAD_DIM
    for c0 in range(0, wk_ref.shape[-1], FF_CH):
        kp = _dot(xn, wk_ref[:, c0:c0 + FF_CH])
        dp = _dot(cc, place_ref[:, c0:c0 + FF_CH])
        for t0 in range(0, FF_CH, LANES):
            j = (c0 + t0) // LANES
            kt, dt = kp[:, t0:t0 + LANES], dp[:, t0:t0 + LANES]
            ka_ref[:, 2 * j * LANES:(2 * j + 1) * LANES] = jnp.where(low, kt, dt).astype(BF16)
            ka_ref[:, (2 * j + 1) * LANES:(2 * j + 2) * LANES] = jnp.where(low, dt, kt).astype(BF16)


def _inproj1(x2, g, wfm, wk, wf, bf, tri, place, B, S):
    M = B * S
    nst = S // TM
    n_fm, n_ka = wfm.shape[0], FOX_HEADS * LANES
    return pl.pallas_call(
        functools.partial(_inproj1_body, nst=nst),
        grid=(M // TM,),
        in_specs=[
            pl.BlockSpec((TM, D_MODEL), lambda i: (i, 0)),
            _const_spec((1, D_MODEL)),
            _const_spec(wfm.shape),
            _const_spec(wk.shape),
            _const_spec(wf.shape),
            _const_spec(bf.shape),
            _const_spec(tri.shape),
            _const_spec(place.shape),
        ],
        out_specs=[
            pl.BlockSpec((1, TM // TK, n_fm, TK), lambda i: (i // nst, i % nst, 0, 0)),
            pl.BlockSpec((TM, n_ka), lambda i: (i, 0)),
        ],
        out_shape=[
            jax.ShapeDtypeStruct((B, S // TK, n_fm, TK), BF16),
            jax.ShapeDtypeStruct((M, n_ka), BF16),
        ],
        scratch_shapes=[pltpu.VMEM((8, LANES), F32)],
        compiler_params=_params(("arbitrary",)),
        name="inproj1",
    )(x2, g, wfm, wk, wf, bf, tri, place)


def _fox_body(q_ref, k_ref, v_ref, cm_ref, o_ref, s_scr, acc_scr):
    c = pl.program_id(2)
    blk = c // QPK
    q = q_ref[0, 0]
    ones = jnp.ones((LANES - HEAD_DIM, TQ), BF16)
    qhs = [q[h * HEAD_DIM:(h + 1) * HEAD_DIM, :] for h in range(FOX_HPS)]
    qas = [jnp.concatenate([qhs[h], ones] if h % 2 == 0 else [ones, qhs[h]], axis=0) for h in range(FOX_HPS)]

    def qk_fn(h, n):
        rows = pl.ds(pl.multiple_of(n * TK, TK), TK)
        return _dot(k_ref[0, rows, h * LANES:(h + 1) * LANES], qas[h])

    causal = cm_ref[c % QPK]
    outs = _attend(blk + 1, lambda i: blk - i, qk_fn, None,
                   lambda h, n: v_ref[0, n, h * HEAD_DIM:(h + 1) * HEAD_DIM, :], s_scr, acc_scr, FOX_HPS,
                   first_fix=lambda h, n, s: s + causal)
    o_ref[0] = jnp.concatenate(outs, axis=0).astype(BF16)


def _fox(fm, ka, cmask, B, S):
    rows = FOX_HPS * HEAD_DIM
    return pl.pallas_call(
        _fox_body,
        grid=(B, FOX_HEADS // FOX_HPS, S // TQ),
        in_specs=[
            pl.BlockSpec((1, 1, rows, TQ), lambda b, h, c: (b, c // QPK, h, c % QPK)),
            pl.BlockSpec((1, S, FOX_HPS * LANES), lambda b, h, c: (b, 0, h), pipeline_mode=pl.Buffered(1)),
            pl.BlockSpec((1, S // TK, rows, TK), lambda b, h, c: (b, 0, FOX_HEADS // FOX_HPS + h, 0),
                         pipeline_mode=pl.Buffered(1)),
            _const_spec(cmask.shape),
        ],
        out_specs=pl.BlockSpec((1, rows, TQ), lambda b, h, c: (b, h, c)),
        out_shape=jax.ShapeDtypeStruct((B, FOX_W, S), BF16),
        scratch_shapes=[pltpu.VMEM((FOX_HPS, TK, TQ), F32), pltpu.VMEM((FOX_HPS, ACC_ROWS, TQ), F32)],
        compiler_params=_params(("parallel", "parallel", "arbitrary")),
        name="fox_attn",
    )(fm, ka, fm, cmask)


def _post_body(*refs, n_parts, final):
    o_refs = refs[:n_parts]
    h_ref, wo_ref, g_ref, w1_ref, w2_ref = refs[n_parts:n_parts + 5]
    gf_ref = refs[n_parts + 5] if final else None
    out_ref, hn_ref = refs[-2:]
    h1 = h_ref[...]
    r0 = 0
    for o_ref in o_refs:
        nf = o_ref.shape[1]
        h1 = h1 + _dot_tn(o_ref[0], wo_ref[r0:r0 + nf, :])
        r0 += nf
    out_ref[...] = h1
    hn_ref[...] = _rmsnorm(out_ref[...], g_ref[...]).astype(BF16)
    for c0 in range(0, D_FF, FF_CH):
        a = jnp.maximum(_dot(hn_ref[...], w1_ref[:, c0:c0 + FF_CH]), 0.0)
        out_ref[...] += _dot((a * a).astype(BF16), w2_ref[c0:c0 + FF_CH, :])
    if final:
        out_ref[...] = _rmsnorm(out_ref[...], gf_ref[...])


def _post(o_parts, h2, wo, g, w1, w2, gf, B, S):
    M = B * S
    nst = S // TM
    final = gf is not None
    in_specs = [pl.BlockSpec((1, o.shape[1], TM), lambda i: (i // nst, 0, i % nst)) for o in o_parts]
    in_specs += [
        pl.BlockSpec((TM, D_MODEL), lambda i: (i, 0)),
        _const_spec(wo.shape),
        _const_spec((1, D_MODEL)),
        _const_spec(w1.shape),
        _const_spec(w2.shape),
    ]
    args = list(o_parts) + [h2, wo, g, w1, w2]
    if final:
        in_specs.append(_const_spec((1, D_MODEL)))
        args.append(gf)
    return pl.pallas_call(
        functools.partial(_post_body, n_parts=len(o_parts), final=final),
        grid=(M // TM,),
        in_specs=in_specs,
        out_specs=pl.BlockSpec((TM, D_MODEL), lambda i: (i, 0)),
        out_shape=jax.ShapeDtypeStruct((M, D_MODEL), F32),
        scratch_shapes=[pltpu.VMEM((TM, D_MODEL), BF16)],
        compiler_params=_params(("parallel",)),
        name="post_final" if final else "post",
    )(*args)


def _rel_bucket(dist):
    n = jnp.maximum(dist, 0)
    max_exact = REL_BUCKETS // 2
    nf = jnp.maximum(n, 1).astype(jnp.float32)
    large = max_exact + (jnp.log(nf / max_exact) / math.log(REL_MAX_DISTANCE / max_exact)
                         * (REL_BUCKETS - max_exact)).astype(jnp.int32)
    large = jnp.minimum(large, REL_BUCKETS - 1)
    return jnp.where(n < max_exact, n, large)


def _bias_tables(rel_bias, S):
    n_heads = rel_bias.shape[1]
    table = rel_bias.T * LOG2E

    def bias_of(dist):
        tab = table.reshape((n_heads, REL_BUCKETS) + (1,) * dist.ndim)
        bkt = _rel_bucket(jnp.asarray(dist))[None]
        out = jnp.zeros((n_heads,) + dist.shape, F32)
        for b in range(REL_BUCKETS):
            out = jnp.where(bkt == b, tab[:, b], out)
        return jnp.where(jnp.asarray(dist)[None] >= 0, out, NEG_INF)

    d = (np.arange(-1, NE_BIAS)[:, None, None] * TQ + np.arange(TQ)[None, None, :] - np.arange(TB)[None, :, None])
    vals = bias_of(d)
    far = table[:, REL_BUCKETS - 1][:, None, None, None]
    tile = jnp.where(d >= 0, vals - far, NEG_INF)
    tile = jnp.concatenate([tile, jnp.zeros_like(tile[:, :1])], axis=1)
    dw = d[:NE_WIN + 2]
    twin = jnp.where((dw >= 0) & (dw < NSA_WINDOW), vals[MOBA_HEADS:, :NE_WIN + 2], NEG_INF)
    n_c = S // NSA_CMP_STRIDE
    u = np.arange(2 * n_c)[:, None]
    dc = np.arange(TQ)[None, :] + NSA_CMP_STRIDE * (u - (n_c - 1)) - (NSA_CMP_BLOCK - 1)
    fcmp = bias_of(dc)[MOBA_HEADS:]
    return tile, twin, fcmp


def _selection_constants(S):
    n_c = S // NSA_CMP_STRIDE
    n_cmp = (S - NSA_CMP_BLOCK) // NSA_CMP_STRIDE + 1
    n_sb = S // NSA_SLC_BLOCK
    ci = np.arange(n_c)[None, :] * NSA_CMP_STRIDE
    sj = np.arange(n_sb)[:, None] * NSA_SLC_BLOCK
    ovl = (ci < sj + NSA_SLC_BLOCK) & (ci + NSA_CMP_BLOCK > sj) & (np.arange(n_c)[None, :] < n_cmp)
    ovl = ovl[:, ::-1]
    return jnp.asarray(ovl, BF16)


def _causal_tiles():
    e = np.arange(QPK)[:, None, None]
    d = e * TQ + np.arange(TQ)[None, None, :] - np.arange(TK)[None, :, None]
    return jnp.asarray(np.where(d >= 0, 0.0, NEG_INF), F32)


def kernel(x, rel_bias, mix_norm, mlp_norm, even_w_in, even_w_out, cmp_pos_k, cmp_pos_v, cmp_k_w1, cmp_k_w2,
           cmp_v_w1, cmp_v_w2, odd_w_in, odd_b_forget, odd_w_out, mlp_w1, mlp_w2, final_norm):
    B, S, D = x.shape
    assert D == D_MODEL and S % TM == 0
    G, J = NSA_KV_GROUPS, NSA_HPG
    h = x.reshape(B * S, D)

    offs = np.cumsum((MOBA_W, MOBA_W, MOBA_W, NSA_W) + (NSA_KV_W,) * 6)
    mq_w, mk_w, mv_w, nq_w, kc_w, vc_w, ksl_w, vsl_w, kwn_w, vwn_w, gz_w = jnp.split(even_w_in[0], offs, axis=1)
    wrm = jnp.concatenate([mk_w, ksl_w, kwn_w], axis=1).astype(BF16)
    col_ksl, col_kwn = MOBA_W, MOBA_W + NSA_KV_W
    wcv = jnp.concatenate([kc_w, vc_w], axis=1).astype(BF16)
    qs = SCALE * LOG2E
    wfm = jnp.concatenate([mq_w * qs, mv_w, nq_w * qs, vsl_w, vwn_w], axis=1).T.astype(BF16)
    row_nq, row_vsl, row_vwn = 2 * MOBA_W, 2 * MOBA_W + NSA_W, 2 * MOBA_W + NSA_W + NSA_KV_W
    gzw = gz_w.T.reshape(G, J, 3, D).transpose(0, 2, 1, 3).reshape(G, 3 * J, D)
    gzw = jnp.pad(gzw, ((0, 0), (0, GZ_ROWS - 3 * J), (0, 0))).reshape(G * GZ_ROWS, D).astype(BF16)

    rm, fm, gz, cv = _inproj0(h, mix_norm[0][None, :], wrm, wfm, gzw, wcv, B, S)
    rm = rm.reshape(B, S, -1)

    tile, twin, tcmp = _bias_tables(rel_bias, S)
    ovl = _selection_constants(S)

    o_moba = _moba(fm, rm, tile, B, S)

    n_c = S // NSA_CMP_STRIDE
    r = cv.reshape(2 * G, B, n_c, NSA_CMP_STRIDE * HEAD_DIM)
    pos = jnp.stack([cmp_pos_k[0].reshape(1, -1), cmp_pos_v[0].reshape(1, -1)])
    pos = jnp.pad(pos, ((0, 0), (0, 7), (0, 0))).astype(BF16)
    w1c = jnp.stack([cmp_k_w1[0], cmp_v_w1[0]]).astype(BF16)
    kc, vct = _compress(r, pos, w1c, cmp_k_w2[0].astype(BF16), cmp_v_w2[0].T.astype(BF16), B, n_c)

    o_nsa = _nsa(fm, rm, gz, kc, vct, tcmp, ovl, tile, twin, B, S,
                 col_ksl, col_kwn, row_nq, row_vsl, row_vwn)

    h = _post([o_moba, o_nsa], h, even_w_out[0].astype(BF16), mlp_norm[0][None, :],
              mlp_w1[0].astype(BF16), mlp_w2[0].astype(BF16), None, B, S)

    q_w, k_w, v_w, f_w = jnp.split(odd_w_in[0], np.cumsum((FOX_W, FOX_W, FOX_W)), axis=1)
    wfm1 = jnp.concatenate([q_w * qs, v_w], axis=1).T.astype(BF16)
    wk = k_w.astype(BF16)
    f_w = jnp.pad(f_w, ((0, 0), (0, LANES - FOX_HEADS)))
    f_hi = f_w.astype(BF16)
    wf = jnp.stack([f_hi, (f_w - f_hi.astype(F32)).astype(BF16)])
    bf = jnp.pad(odd_b_forget[0], (0, LANES - FOX_HEADS))[None, :]
    tri = jnp.asarray(np.tril(np.ones((TM, TM))), BF16)
    place = np.zeros((3 * LANES, FOX_W), np.float32)
    heads = np.arange(FOX_HEADS)
    for term in range(3):
        place[term * LANES + heads, (heads // 2) * LANES + (1 - heads % 2) * HEAD_DIM + term] = -1.0
    fm1, ka = _inproj1(h, mix_norm[1][None, :], wfm1, wk, wf, bf, tri, jnp.asarray(place, BF16), B, S)
    o_fox = _fox(fm1, ka.reshape(B, S, -1), _causal_tiles(), B, S)

    h = _post([o_fox], h, odd_w_out[0].astype(BF16), mlp_norm[1][None, :],
              mlp_w1[1].astype(BF16), mlp_w2[1].astype(BF16), final_norm[None, :], B, S)
    return h.reshape(B, S, D)
```

```python
import functools
import math

import numpy as np
import jax
import jax.numpy as jnp
from jax import lax
from jax.experimental import pallas as pl
from jax.experimental.pallas import tpu as pltpu

D_MODEL = 1024
HEAD_DIM = 64
MOBA_HEADS = 8
MOBA_BLOCK = 256
MOBA_TOPK = 3
NSA_HEADS = 8
NSA_KV_GROUPS = 2
NSA_HPG = NSA_HEADS // NSA_KV_GROUPS
NSA_CMP_BLOCK = 32
NSA_CMP_STRIDE = 16
NSA_CMP_HIDDEN = 256
NSA_SLC_BLOCK = 64
NSA_TOPN = 16
NSA_WINDOW = 512
NSA_FORCE_SCORE = 1e6
FOX_HEADS = 16
D_FF = 4 * D_MODEL
REL_BUCKETS = 32
REL_MAX_DISTANCE = 1024
RMS_EPS = 1e-5
NEG_INF = -1e30
SCALE = HEAD_DIM ** -0.5

MOBA_W = MOBA_HEADS * HEAD_DIM
NSA_W = NSA_HEADS * HEAD_DIM
NSA_KV_W = NSA_KV_GROUPS * HEAD_DIM
FOX_W = FOX_HEADS * HEAD_DIM

LANES = 128
SUBLANES = 8
TQ = 256
TK = 256
TB = 256
QPK = TK // TQ
KSUB = TK // TB
TM = 512
CH = 256
FF_CH = 1024
VMEM_LIMIT = 56 * 1024 * 1024
NE_BIAS = -(-(REL_MAX_DISTANCE + TB - 1) // TQ)
NE_WIN = -(-(NSA_WINDOW + TB - 1) // TQ)
GZ_ROWS = 16
HPS = 8
FOX_HPS = 16
SPT = TK // NSA_SLC_BLOCK
BF16_ROWS = 2 * SUBLANES
ACC_ROWS = HEAD_DIM + BF16_ROWS
ROW_CHUNK = 64
LOG2E = math.log2(math.e)

assert TK % TQ == 0 and TK % TB == 0 and TQ == TB and MOBA_BLOCK == TB and TB % NSA_SLC_BLOCK == 0 and SPT <= 8

F32 = jnp.float32
BF16 = jnp.bfloat16


def _dot(a, b):
    return jnp.dot(a, b, preferred_element_type=F32)


def _dot_nt(a, b):
    return lax.dot_general(a, b, (((1,), (1,)), ((), ())), preferred_element_type=F32)


def _dot_tn(a, b):
    return lax.dot_general(a, b, (((0,), (0,)), ((), ())), preferred_element_type=F32)


def _rmsnorm(x, g):
    ms = jnp.mean(x * x, axis=-1, keepdims=True)
    return x * lax.rsqrt(ms + RMS_EPS) * g


def _split3(x):
    a = x.astype(BF16)
    r = x - a.astype(F32)
    b = r.astype(BF16)
    c = (r - b.astype(F32)).astype(BF16)
    return a, b, c


def _const_spec(shape):
    nd = len(shape)
    return pl.BlockSpec(shape, lambda *_: (0,) * nd, pipeline_mode=pl.Buffered(1))


def _params(sem):
    return pltpu.CompilerParams(dimension_semantics=sem, vmem_limit_bytes=VMEM_LIMIT)


def _attend(n_tiles, tile_of, qk_fn, fix_fn, v_fn, s_scr, acc_scr, n_heads, first_fix=None):
    def put_scores(h, n, fix):
        s = qk_fn(h, n)
        s_scr[h] = s if fix is None else fix(h, n, s)

    for h in range(n_heads):
        put_scores(h, tile_of(0), fix_fn if first_fix is None else first_fix)
    acc_scr[...] = jnp.zeros_like(acc_scr)
    last = n_tiles - 1
    ones = jnp.ones((ACC_ROWS - HEAD_DIM, TK), BF16)
    chunks = range(0, TK, ROW_CHUNK)

    def body(i, ms):
        n = tile_of(i)
        n_next = tile_of(jnp.minimum(i + 1, last))
        out = []
        for h in range(n_heads):
            mx = s_scr[h, 0:ROW_CHUNK, :]
            for r0 in chunks[1:]:
                mx = jnp.maximum(mx, s_scr[h, r0:r0 + ROW_CHUNK, :])
            m_new = jnp.maximum(ms[h], jnp.max(mx, axis=0, keepdims=True))
            alpha = jnp.exp2(ms[h] - m_new)
            p = jnp.concatenate([jnp.exp2(s_scr[h, r0:r0 + ROW_CHUNK, :] - m_new).astype(BF16) for r0 in chunks],
                                axis=0)
            va = jnp.concatenate([v_fn(h, n), ones], axis=0)
            acc_scr[h] = alpha * acc_scr[h] + _dot(va, p)
            out.append(m_new)
            put_scores(h, n_next, fix_fn)
        return tuple(out)

    lax.fori_loop(0, n_tiles, body, tuple(jnp.full((1, TQ), NEG_INF, F32) for _ in range(n_heads)))
    return [acc_scr[h, :HEAD_DIM, :] / acc_scr[h, HEAD_DIM:HEAD_DIM + 1, :] for h in range(n_heads)]


def _rank_select(val, n_rows, k):
    sub = lax.broadcasted_iota(jnp.int32, (SUBLANES, val.shape[1]), 0)
    groups = [val[g0:g0 + SUBLANES, :] for g0 in range(0, val.shape[0], SUBLANES)]
    counts = [jnp.zeros(g.shape, F32) for g in groups]
    for m in range(n_rows):
        vm = val[m:m + 1, :]
        for g, vg in enumerate(groups):
            if g * SUBLANES > m:
                counts[g] = counts[g] + jnp.where(vm >= vg, 1.0, 0.0)
            elif (g + 1) * SUBLANES <= m:
                counts[g] = counts[g] + jnp.where(vm > vg, 1.0, 0.0)
            else:
                tie = jnp.where(sub > m % SUBLANES, 1.0, 0.0)
                counts[g] = counts[g] + jnp.where(vm > vg, 1.0, 0.0) + jnp.where(vm == vg, tie, 0.0)
    return jnp.concatenate(counts, axis=0) < k


def _inproj0_body(x_ref, g_ref, wrm_ref, wfm_ref, wgz_ref, wcv_ref, rm_ref, fm_ref, gz_ref, cv_ref):
    xn = _rmsnorm(x_ref[...], g_ref[...]).astype(BF16)
    cv = _dot(xn, wcv_ref[...]).astype(BF16)
    for j in range(cv_ref.shape[0]):
        cv_ref[j] = cv[:, j * HEAD_DIM:(j + 1) * HEAD_DIM]
    for c0 in range(0, rm_ref.shape[-1], CH):
        rm_ref[:, c0:c0 + CH] = _dot(xn, wrm_ref[:, c0:c0 + CH]).astype(BF16)
    for r0 in range(0, fm_ref.shape[2], CH):
        res = _dot_nt(wfm_ref[r0:r0 + CH, :], xn).astype(BF16)
        for t in range(TM // TK):
            fm_ref[0, t, r0:r0 + CH, :] = res[:, t * TK:(t + 1) * TK]
    gz_ref[0] = _dot_nt(wgz_ref[...], xn)


def _inproj0(x2, g, wrm, wfm, wgz, wcv, B, S):
    M = B * S
    nst = S // TM
    n_rm, n_fm, n_gz, n_cv = wrm.shape[1], wfm.shape[0], wgz.shape[0], wcv.shape[1] // HEAD_DIM
    return pl.pallas_call(
        _inproj0_body,
        grid=(M // TM,),
        in_specs=[
            pl.BlockSpec((TM, D_MODEL), lambda i: (i, 0)),
            _const_spec((1, D_MODEL)),
            _const_spec((D_MODEL, n_rm)),
            _const_spec((n_fm, D_MODEL)),
            _const_spec((n_gz, D_MODEL)),
            _const_spec(wcv.shape),
        ],
        out_specs=[
            pl.BlockSpec((TM, n_rm), lambda i: (i, 0)),
            pl.BlockSpec((1, TM // TK, n_fm, TK), lambda i: (i // nst, i % nst, 0, 0)),
            pl.BlockSpec((1, n_gz, TM), lambda i: (i // nst, 0, i % nst)),
            pl.BlockSpec((n_cv, TM, HEAD_DIM), lambda i: (0, i, 0)),
        ],
        out_shape=[
            jax.ShapeDtypeStruct((M, n_rm), BF16),
            jax.ShapeDtypeStruct((B, S // TK, n_fm, TK), BF16),
            jax.ShapeDtypeStruct((B, n_gz, S), F32),
            jax.ShapeDtypeStruct((n_cv, M, HEAD_DIM), BF16),
        ],
        compiler_params=_params(("parallel",)),
        name="inproj0",
    )(x2, g, wrm, wfm, wgz, wcv)


def _compress_body(rk_ref, rv_ref, pos_ref, w1_ref, w2k_ref, w2vt_ref, flip_ref, kc_ref, vct_ref):
    half = NSA_CMP_STRIDE * HEAD_DIM

    def hidden(r_ref, s):
        r = r_ref[0, 0]
        a = _dot(r, w1_ref[s, :half, :])
        b = _dot(r, w1_ref[s, half:, :])
        nxt = pltpu.roll(b, b.shape[0] - 1, axis=0)
        posb = _dot(pos_ref[s], w1_ref[s])[0:1]
        pre = a + nxt + posb
        act = (pre * jax.nn.sigmoid(pre)).astype(BF16)
        return _dot(flip_ref[...], act).astype(BF16)

    kc_ref[0, 0] = _dot(hidden(rk_ref, 0), w2k_ref[...]).astype(BF16)
    vct_ref[0, 0] = _dot_nt(w2vt_ref[...], hidden(rv_ref, 1)).astype(BF16)


def _compress(r, pos, w1, w2k, w2vt, B, NC):
    G = NSA_KV_GROUPS
    half = NSA_CMP_STRIDE * HEAD_DIM
    return pl.pallas_call(
        _compress_body,
        grid=(B, G),
        in_specs=[
            pl.BlockSpec((1, 1, NC, half), lambda b, g: (g, b, 0, 0)),
            pl.BlockSpec((1, 1, NC, half), lambda b, g: (G + g, b, 0, 0)),
            _const_spec(pos.shape),
            _const_spec(w1.shape),
            _const_spec(w2k.shape),
            _const_spec(w2vt.shape),
            _const_spec((NC, NC)),
        ],
        out_specs=[
            pl.BlockSpec((1, 1, NC, HEAD_DIM), lambda b, g: (b, g, 0, 0)),
            pl.BlockSpec((1, 1, HEAD_DIM, NC), lambda b, g: (b, g, 0, 0)),
        ],
        out_shape=[
            jax.ShapeDtypeStruct((B, G, NC, HEAD_DIM), BF16),
            jax.ShapeDtypeStruct((B, G, HEAD_DIM, NC), BF16),
        ],
        compiler_params=_params(("parallel", "parallel")),
        name="nsa_compress",
    )(r, r, pos, w1, w2k, w2vt, jnp.asarray(np.eye(NC)[::-1], BF16))


def _moba_body(q_ref, k_ref, v_ref, t_ref, o_ref, kmean_ref, mask_ref, s_scr, acc_scr, *, n_mb, topk):
    c = pl.program_id(2)
    blk = c // QPK
    qblk = c * TQ // MOBA_BLOCK

    @pl.when(c == 0)
    def _():
        kmean_ref[...] = jnp.zeros_like(kmean_ref)
        for n in range(n_mb):
            kblk = k_ref[0, n * MOBA_BLOCK:(n + 1) * MOBA_BLOCK, :].astype(F32)
            kmean_ref[n:n + 1, :] = jnp.mean(kblk, axis=0, keepdims=True)

    q = q_ref[0, 0]
    rowi = lax.broadcasted_iota(jnp.int32, (LANES, TQ), 0)
    nidx = lax.broadcasted_iota(jnp.int32, (kmean_ref.shape[0], TQ), 0)
    n_pad = kmean_ref.shape[0]
    km = jnp.concatenate(_split3(kmean_ref[...]), axis=0)
    qpads = []
    for h in range(HPS):
        lo = (h // 2) * LANES
        qpair = q[lo:lo + LANES, :]
        qh = jnp.where(rowi // HEAD_DIM == h % 2, qpair, jnp.zeros_like(qpair))
        terms = _dot(km[:, lo:lo + LANES], qh)
        route = terms[:n_pad] + terms[n_pad:2 * n_pad] + terms[2 * n_pad:]
        route = jnp.where(nidx < qblk, route, NEG_INF)
        sel = _rank_select(route, n_mb, topk) & (nidx < qblk)
        mask_ref[h] = jnp.where(sel | (nidx == qblk), 0.0, NEG_INF)
        qpads.append(qh)

    def qk_fn(h, n):
        rows = pl.ds(pl.multiple_of(n * TK, TK), TK)
        return _dot(k_ref[0, rows, (h // 2) * LANES:(h // 2 + 1) * LANES], qpads[h])

    def fix_fn(h, n, s):
        parts = []
        for u in range(KSUB):
            b = KSUB * n + u
            parts.append(mask_ref[h, pl.ds(b, 1), :] + t_ref[h, jnp.clip(c - b, -1, NE_BIAS) + 1])
        return s + jnp.concatenate(parts, axis=0)

    def v_fn(h, n):
        return v_ref[0, n, h * HEAD_DIM:(h + 1) * HEAD_DIM, :]

    outs = _attend(blk + 1, lambda i: blk - i, qk_fn, fix_fn, v_fn, s_scr, acc_scr, HPS)
    o_ref[0] = jnp.concatenate(outs, axis=0).astype(BF16)


def _moba(fm, rm, tab, B, S):
    n_mb = S // MOBA_BLOCK
    n_pad = -(-n_mb // BF16_ROWS) * BF16_ROWS
    ne = tab.shape[1]
    rows = HPS * HEAD_DIM
    body = functools.partial(_moba_body, n_mb=n_mb, topk=min(MOBA_TOPK, n_mb))
    return pl.pallas_call(
        body,
        grid=(B, MOBA_HEADS // HPS, S // TQ),
        in_specs=[
            pl.BlockSpec((1, 1, rows, TQ), lambda b, p, c: (b, c // QPK, p, c % QPK)),
            pl.BlockSpec((1, S, rows), lambda b, p, c: (b, 0, p)),
            pl.BlockSpec((1, S // TK, rows, TK), lambda b, p, c: (b, 0, MOBA_HEADS // HPS + p, 0)),
            pl.BlockSpec((HPS, ne, TB, TQ), lambda b, p, c: (p, 0, 0, 0), pipeline_mode=pl.Buffered(1)),
        ],
        out_specs=pl.BlockSpec((1, rows, TQ), lambda b, p, c: (b, p, c)),
        out_shape=jax.ShapeDtypeStruct((B, MOBA_W, S), BF16),
        scratch_shapes=[pltpu.VMEM((n_pad, rows), F32), pltpu.VMEM((HPS, n_pad, TQ), F32),
                        pltpu.VMEM((HPS, TK, TQ), F32), pltpu.VMEM((HPS, ACC_ROWS, TQ), F32)],
        compiler_params=_params(("parallel", "parallel", "arbitrary")),
        name="moba_attn",
    )(fm, rm, fm, tab)


def _nsa_body(q_ref, kc_ref, vct_ref, fc_ref, ovl_ref, ksl_ref, vsl_ref, kwn_ref, vwn_ref,
              tslc_ref, twin_ref, gz_ref, o_ref, s_scr, acc_scr, s_win, acc_win, sel_ref, out_scr, *, n_sb, n_sel):
    c = pl.program_id(1)
    blk = c // QPK
    G, J = NSA_KV_GROUPS, NSA_HPG
    H = G * J

    q = q_ref[0, 0]
    qs = [q[h * HEAD_DIM:(h + 1) * HEAD_DIM, :] for h in range(H)]
    zero = jnp.zeros((HEAD_DIM, TQ), BF16)
    qpads = [jnp.concatenate([zero] * (h // J) + [qs[h]] + [zero] * (G - 1 - h // J), axis=0) for h in range(H)]

    c0 = pl.multiple_of(c * (TQ // NSA_CMP_STRIDE), TQ // NSA_CMP_STRIDE)
    gate = jax.nn.sigmoid(gz_ref[0])

    def gated(h, branch, o):
        r = (h // J) * GZ_ROWS + branch * J + h % J
        return gate[r:r + 1, :] * o

    for g in range(G):
        kc = kc_ref[0, g]
        vct = vct_ref[0, g]
        psum = jnp.zeros((kc.shape[0], TQ), F32)
        for h in range(g * J, (g + 1) * J):
            s = _dot(kc, qs[h]) + fc_ref[h, pl.ds(c0, kc.shape[0]), :]
            m = jnp.max(s, axis=0, keepdims=True)
            p = jnp.exp2(s - m)
            l = jnp.sum(p, axis=0, keepdims=True)
            pn = p * jnp.where(m > 0.5 * NEG_INF, 1.0 / l, 0.0)
            out_scr[h] = gated(h, 0, _dot(vct, pn.astype(BF16)))
            psum = psum + pn

        ph = psum.astype(BF16)
        plo = (psum - ph.astype(F32)).astype(BF16)
        imp = _dot(ovl_ref[...], ph) + _dot(ovl_ref[...], plo)
        jb = lax.broadcasted_iota(jnp.int32, imp.shape, 0)
        t = c * TQ + lax.broadcasted_iota(jnp.int32, imp.shape, 1)
        sb = t // NSA_SLC_BLOCK
        forced = (jb == 0) | (jb == sb) | (jb == sb - 1)
        allowed = jb <= sb
        val = jnp.where(forced, imp + NSA_FORCE_SCORE, jnp.where(allowed, imp, NEG_INF))
        sel = _rank_select(val, n_sb, n_sel) & allowed
        selb = jnp.where(sel, 0.0, NEG_INF)
        for n in range(n_sb // SPT):
            slab = selb[n * SPT:(n + 1) * SPT, :]
            if SPT < sel_ref.shape[2]:
                slab = jnp.concatenate([slab, jnp.zeros((sel_ref.shape[2] - SPT, TQ), F32)], axis=0)
            sel_ref[g, n] = slab

    def slc_qk(h, n):
        return _dot(ksl_ref[0, pl.ds(pl.multiple_of(n * TK, TK), TK), :], qpads[h])

    def slc_fix(h, n, s):
        rows = sel_ref[h // J, n]
        mask = jnp.concatenate([jnp.broadcast_to(rows[b:b + 1, :], (NSA_SLC_BLOCK, TQ)) for b in range(SPT)], axis=0)
        bias = jnp.concatenate([tslc_ref[h, jnp.clip(c - (KSUB * n + u), -1, NE_BIAS) + 1] for u in range(KSUB)],
                               axis=0)
        return s + mask + bias

    def slc_v(h, n):
        return vsl_ref[0, n, (h // J) * HEAD_DIM:(h // J + 1) * HEAD_DIM, :]

    for h, o in enumerate(_attend(blk + 1, lambda i: blk - i, slc_qk, slc_fix, slc_v, s_scr, acc_scr, H)):
        out_scr[h] += gated(h, 1, o)

    def win_qk(h, n):
        return _dot(kwn_ref[0, pl.ds(pl.multiple_of(n * TK, TK), TK), :], qpads[h])

    def win_fix(h, n, s):
        return s + jnp.concatenate([twin_ref[h, jnp.clip(c - (KSUB * n + u), -1, NE_WIN) + 1] for u in range(KSUB)],
                                   axis=0)

    def win_v(h, n):
        return vwn_ref[0, n, (h // J) * HEAD_DIM:(h // J + 1) * HEAD_DIM, :]

    w_lo = jnp.maximum(c - NE_WIN + 1, 0) // KSUB
    o_win = _attend(blk - w_lo + 1, lambda i: blk - i, win_qk, win_fix, win_v, s_win, acc_win, H)

    for h in range(H):
        o_ref[0, h * HEAD_DIM:(h + 1) * HEAD_DIM, :] = (out_scr[h] + gated(h, 2, o_win[h])).astype(BF16)


def _nsa(fm, rm, gz, kc, vct, tcmp, ovl, tslc, twin, B, S, col_ksl, col_kwn, row_q, row_vsl, row_vwn):
    G, J = NSA_KV_GROUPS, NSA_HPG
    NC = kc.shape[2]
    n_sb = S // NSA_SLC_BLOCK
    body = functools.partial(_nsa_body, n_sb=n_sb, n_sel=min(NSA_TOPN, n_sb))
    H = G * J
    kvrows = G * HEAD_DIM
    one = pl.Buffered(1)
    return pl.pallas_call(
        body,
        grid=(B, S // TQ),
        in_specs=[
            pl.BlockSpec((1, 1, NSA_W, TQ), lambda b, c: (b, c // QPK, row_q // NSA_W, c % QPK)),
            pl.BlockSpec((1, G, NC, HEAD_DIM), lambda b, c: (b, 0, 0, 0)),
            pl.BlockSpec((1, G, HEAD_DIM, NC), lambda b, c: (b, 0, 0, 0)),
            _const_spec(tcmp.shape),
            _const_spec(ovl.shape),
            pl.BlockSpec((1, S, LANES), lambda b, c: (b, 0, col_ksl // LANES)),
            pl.BlockSpec((1, S // TK, kvrows, TK), lambda b, c: (b, 0, row_vsl // kvrows, 0)),
            pl.BlockSpec((1, S, LANES), lambda b, c: (b, 0, col_kwn // LANES)),
            pl.BlockSpec((1, S // TK, kvrows, TK), lambda b, c: (b, 0, row_vwn // kvrows, 0)),
            pl.BlockSpec((H, tslc.shape[1], TB, TQ), lambda b, c: (MOBA_HEADS // H, 0, 0, 0), pipeline_mode=one),
            _const_spec(twin.shape),
            pl.BlockSpec((1, G * GZ_ROWS, TQ), lambda b, c: (b, 0, c)),
        ],
        out_specs=pl.BlockSpec((1, NSA_W, TQ), lambda b, c: (b, 0, c)),
        out_shape=jax.ShapeDtypeStruct((B, NSA_W, S), BF16),
        scratch_shapes=[pltpu.VMEM((H, TK, TQ), F32), pltpu.VMEM((H, ACC_ROWS, TQ), F32),
                        pltpu.VMEM((H, TK, TQ), F32), pltpu.VMEM((H, ACC_ROWS, TQ), F32),
                        pltpu.VMEM((G, S // TK, SUBLANES, TQ), F32), pltpu.VMEM((H, HEAD_DIM, TQ), F32)],
        compiler_params=_params(("parallel", "arbitrary")),
        name="nsa_attn",
    )(fm, kc, vct, tcmp, ovl, rm, fm, rm, fm, tslc, twin, gz)


def _inproj1_body(x_ref, g_ref, wfm_ref, wk_ref, wf_ref, bf_ref, tri_ref, place_ref, fm_ref, ka_ref, carry_ref,
                  *, nst):
    i = pl.program_id(0)

    @pl.when(i % nst == 0)
    def _():
        carry_ref[...] = jnp.zeros_like(carry_ref)

    xf = _rmsnorm(x_ref[...], g_ref[...])
    xn = xf.astype(BF16)
    xlo = (xf - xn.astype(F32)).astype(BF16)
    for r0 in range(0, fm_ref.shape[2], CH):
        res = _dot_nt(wfm_ref[r0:r0 + CH, :], xn).astype(BF16)
        for t in range(TM // TK):
            fm_ref[0, t, r0:r0 + CH, :] = res[:, t * TK:(t + 1) * TK]

    fz = _dot(xn, wf_ref[0]) + _dot(xlo, wf_ref[0]) + _dot(xn, wf_ref[1]) + bf_ref[...]
    logf = jnp.minimum(fz, 0.0) - jnp.log(1.0 + jnp.exp(-jnp.abs(fz)))
    tri = tri_ref[...]
    h1, h2, h3 = _split3(logf)
    cum = _dot(tri, h1) + _dot(tri, h2) + _dot(tri, h3) + carry_ref[0:1, :]
    carry_ref[...] = jnp.broadcast_to(cum[TM - 1:TM, :], carry_ref.shape)
    cc = jnp.concatenate(_split3(cum * LOG2E), axis=1)
    low = lax.broadcasted_iota(jnp.int32, (TM, LANES), 1) < HEAD_DIM
    for c0 in range(0, wk_ref.shape[-1], FF_CH):
        kp = _dot(xn, wk_ref[:, c0:c0 + FF_CH])
        dp = _dot(cc, place_ref[:, c0:c0 + FF_CH])
        for t0 in range(0, FF_CH, LANES):
            j = (c0 + t0) // LANES
            kt, dt = kp[:, t0:t0 + LANES], dp[:, t0:t0 + LANES]
            ka_ref[:, 2 * j * LANES:(2 * j + 1) * LANES] = jnp.where(low, kt, dt).astype(BF16)
            ka_ref[:, (2 * j + 1) * LANES:(2 * j + 2) * LANES] = jnp.where(low, dt, kt).astype(BF16)


def _inproj1(x2, g, wfm, wk, wf, bf, tri, place, B, S):
    M = B * S
    nst = S // TM
    n_fm, n_ka = wfm.shape[0], FOX_HEADS * LANES
    return pl.pallas_call(
        functools.partial(_inproj1_body, nst=nst),
        grid=(M // TM,),
        in_specs=[
            pl.BlockSpec((TM, D_MODEL), lambda i: (i, 0)),
            _const_spec((1, D_MODEL)),
            _const_spec(wfm.shape),
            _const_spec(wk.shape),
            _const_spec(wf.shape),
            _const_spec(bf.shape),
            _const_spec(tri.shape),
            _const_spec(place.shape),
        ],
        out_specs=[
            pl.BlockSpec((1, TM // TK, n_fm, TK), lambda i: (i // nst, i % nst, 0, 0)),
            pl.BlockSpec((TM, n_ka), lambda i: (i, 0)),
        ],
        out_shape=[
            jax.ShapeDtypeStruct((B, S // TK, n_fm, TK), BF16),
            jax.ShapeDtypeStruct((M, n_ka), BF16),
        ],
        scratch_shapes=[pltpu.VMEM((8, LANES), F32)],
        compiler_params=_params(("arbitrary",)),
        name="inproj1",
    )(x2, g, wfm, wk, wf, bf, tri, place)


def _fox_body(q_ref, k_ref, v_ref, cm_ref, o_ref, s_scr, acc_scr):
    c = pl.program_id(2)
    blk = c // QPK
    q = q_ref[0, 0]
    ones = jnp.ones((LANES - HEAD_DIM, TQ), BF16)
    qhs = [q[h * HEAD_DIM:(h + 1) * HEAD_DIM, :] for h in range(FOX_HPS)]
    qas = [jnp.concatenate([qhs[h], ones] if h % 2 == 0 else [ones, qhs[h]], axis=0) for h in range(FOX_HPS)]

    def qk_fn(h, n):
        rows = pl.ds(pl.multiple_of(n * TK, TK), TK)
        return _dot(k_ref[0, rows, h * LANES:(h + 1) * LANES], qas[h])

    causal = cm_ref[c % QPK]
    outs = _attend(blk + 1, lambda i: blk - i, qk_fn, None,
                   lambda h, n: v_ref[0, n, h * HEAD_DIM:(h + 1) * HEAD_DIM, :], s_scr, acc_scr, FOX_HPS,
                   first_fix=lambda h, n, s: s + causal)
    o_ref[0] = jnp.concatenate(outs, axis=0).astype(BF16)


def _fox(fm, ka, cmask, B, S):
    rows = FOX_HPS * HEAD_DIM
    return pl.pallas_call(
        _fox_body,
        grid=(B, FOX_HEADS // FOX_HPS, S // TQ),
        in_specs=[
            pl.BlockSpec((1, 1, rows, TQ), lambda b, h, c: (b, c // QPK, h, c % QPK)),
            pl.BlockSpec((1, S, FOX_HPS * LANES), lambda b, h, c: (b, 0, h), pipeline_mode=pl.Buffered(1)),
            pl.BlockSpec((1, S // TK, rows, TK), lambda b, h, c: (b, 0, FOX_HEADS // FOX_HPS + h, 0),
                         pipeline_mode=pl.Buffered(1)),
            _const_spec(cmask.shape),
        ],
        out_specs=pl.BlockSpec((1, rows, TQ), lambda b, h, c: (b, h, c)),
        out_shape=jax.ShapeDtypeStruct((B, FOX_W, S), BF16),
        scratch_shapes=[pltpu.VMEM((FOX_HPS, TK, TQ), F32), pltpu.VMEM((FOX_HPS, ACC_ROWS, TQ), F32)],
        compiler_params=_params(("parallel", "parallel", "arbitrary")),
        name="fox_attn",
    )(fm, ka, fm, cmask)


def _post_body(*refs, n_parts, final):
    o_refs = refs[:n_parts]
    h_ref, wo_ref, g_ref, w1_ref, w2_ref = refs[n_parts:n_parts + 5]
    gf_ref = refs[n_parts + 5] if final else None
    out_ref, hn_ref = refs[-2:]
    h1 = h_ref[...]
    r0 = 0
    for o_ref in o_refs:
        nf = o_ref.shape[1]
        h1 = h1 + _dot_tn(o_ref[0], wo_ref[r0:r0 + nf, :])
        r0 += nf
    out_ref[...] = h1
    hn_ref[...] = _rmsnorm(out_ref[...], g_ref[...]).astype(BF16)
    for c0 in range(0, D_FF, FF_CH):
        a = jnp.maximum(_dot(hn_ref[...], w1_ref[:, c0:c0 + FF_CH]), 0.0)
        out_ref[...] += _dot((a * a).astype(BF16), w2_ref[c0:c0 + FF_CH, :])
    if final:
        out_ref[...] = _rmsnorm(out_ref[...], gf_ref[...])


def _post(o_parts, h2, wo, g, w1, w2, gf, B, S):
    M = B * S
    nst = S // TM
    final = gf is not None
    in_specs = [pl.BlockSpec((1, o.shape[1], TM), lambda i: (i // nst, 0, i % nst)) for o in o_parts]
    in_specs += [
        pl.BlockSpec((TM, D_MODEL), lambda i: (i, 0)),
        _const_spec(wo.shape),
        _const_spec((1, D_MODEL)),
        _const_spec(w1.shape),
        _const_spec(w2.shape),
    ]
    args = list(o_parts) + [h2, wo, g, w1, w2]
    if final:
        in_specs.append(_const_spec((1, D_MODEL)))
        args.append(gf)
    return pl.pallas_call(
        functools.partial(_post_body, n_parts=len(o_parts), final=final),
        grid=(M // TM,),
        in_specs=in_specs,
        out_specs=pl.BlockSpec((TM, D_MODEL), lambda i: (i, 0)),
        out_shape=jax.ShapeDtypeStruct((M, D_MODEL), F32),
        scratch_shapes=[pltpu.VMEM((TM, D_MODEL), BF16)],
        compiler_params=_params(("parallel",)),
        name="post_final" if final else "post",
    )(*args)


def _rel_bucket(dist):
    n = jnp.maximum(dist, 0)
    max_exact = REL_BUCKETS // 2
    nf = jnp.maximum(n, 1).astype(jnp.float32)
    large = max_exact + (jnp.log(nf / max_exact) / math.log(REL_MAX_DISTANCE / max_exact)
                         * (REL_BUCKETS - max_exact)).astype(jnp.int32)
    large = jnp.minimum(large, REL_BUCKETS - 1)
    return jnp.where(n < max_exact, n, large)


def _bias_tables(rel_bias, S):
    n_heads = rel_bias.shape[1]
    table = rel_bias.T * LOG2E

    def bias_of(dist):
        tab = table.reshape((n_heads, REL_BUCKETS) + (1,) * dist.ndim)
        bkt = _rel_bucket(jnp.asarray(dist))[None]
        out = jnp.zeros((n_heads,) + dist.shape, F32)
        for b in range(REL_BUCKETS):
            out = jnp.where(bkt == b, tab[:, b], out)
        return jnp.where(jnp.asarray(dist)[None] >= 0, out, NEG_INF)

    d = (np.arange(-1, NE_BIAS)[:, None, None] * TQ + np.arange(TQ)[None, None, :] - np.arange(TB)[None, :, None])
    vals = bias_of(d)
    far = table[:, REL_BUCKETS - 1][:, None, None, None]
    tile = jnp.where(d >= 0, vals - far, NEG_INF)
    tile = jnp.concatenate([tile, jnp.zeros_like(tile[:, :1])], axis=1)
    dw = d[:NE_WIN + 2]
    twin = jnp.where((dw >= 0) & (dw < NSA_WINDOW), vals[MOBA_HEADS:, :NE_WIN + 2], NEG_INF)
    n_c = S // NSA_CMP_STRIDE
    u = np.arange(2 * n_c)[:, None]
    dc = np.arange(TQ)[None, :] + NSA_CMP_STRIDE * (u - (n_c - 1)) - (NSA_CMP_BLOCK - 1)
    fcmp = bias_of(dc)[MOBA_HEADS:]
    return tile, twin, fcmp


def _selection_constants(S):
    n_c = S // NSA_CMP_STRIDE
    n_cmp = (S - NSA_CMP_BLOCK) // NSA_CMP_STRIDE + 1
    n_sb = S // NSA_SLC_BLOCK
    ci = np.arange(n_c)[None, :] * NSA_CMP_STRIDE
    sj = np.arange(n_sb)[:, None] * NSA_SLC_BLOCK
    ovl = (ci < sj + NSA_SLC_BLOCK) & (ci + NSA_CMP_BLOCK > sj) & (np.arange(n_c)[None, :] < n_cmp)
    ovl = ovl[:, ::-1]
    return jnp.asarray(ovl, BF16)


def _causal_tiles():
    e = np.arange(QPK)[:, None, None]
    d = e * TQ + np.arange(TQ)[None, None, :] - np.arange(TK)[None, :, None]
    return jnp.asarray(np.where(d >= 0, 0.0, NEG_INF), F32)


def kernel(x, rel_bias, mix_norm, mlp_norm, even_w_in, even_w_out, cmp_pos_k, cmp_pos_v, cmp_k_w1, cmp_k_w2,
           cmp_v_w1, cmp_v_w2, odd_w_in, odd_b_forget, odd_w_out, mlp_w1, mlp_w2, final_norm):
    B, S, D = x.shape
    assert D == D_MODEL and S % TM == 0
    G, J = NSA_KV_GROUPS, NSA_HPG
    h = x.reshape(B * S, D)

    offs = np.cumsum((MOBA_W, MOBA_W, MOBA_W, NSA_W) + (NSA_KV_W,) * 6)
    mq_w, mk_w, mv_w, nq_w, kc_w, vc_w, ksl_w, vsl_w, kwn_w, vwn_w, gz_w = jnp.split(even_w_in[0], offs, axis=1)
    wrm = jnp.concatenate([mk_w, ksl_w, kwn_w], axis=1).astype(BF16)
    col_ksl, col_kwn = MOBA_W, MOBA_W + NSA_KV_W
    wcv = jnp.concatenate([kc_w, vc_w], axis=1).astype(BF16)
    qs = SCALE * LOG2E
    wfm = jnp.concatenate([mq_w * qs, mv_w, nq_w * qs, vsl_w, vwn_w], axis=1).T.astype(BF16)
    row_nq, row_vsl, row_vwn = 2 * MOBA_W, 2 * MOBA_W + NSA_W, 2 * MOBA_W + NSA_W + NSA_KV_W
    gzw = gz_w.T.reshape(G, J, 3, D).transpose(0, 2, 1, 3).reshape(G, 3 * J, D)
    gzw = jnp.pad(gzw, ((0, 0), (0, GZ_ROWS - 3 * J), (0, 0))).reshape(G * GZ_ROWS, D).astype(BF16)

    rm, fm, gz, cv = _inproj0(h, mix_norm[0][None, :], wrm, wfm, gzw, wcv, B, S)
    rm = rm.reshape(B, S, -1)

    tile, twin, tcmp = _bias_tables(rel_bias, S)
    ovl = _selection_constants(S)

    o_moba = _moba(fm, rm, tile, B, S)

    n_c = S // NSA_CMP_STRIDE
    r = cv.reshape(2 * G, B, n_c, NSA_CMP_STRIDE * HEAD_DIM)
    pos = jnp.stack([cmp_pos_k[0].reshape(1, -1), cmp_pos_v[0].reshape(1, -1)])
    pos = jnp.pad(pos, ((0, 0), (0, 7), (0, 0))).astype(BF16)
    w1c = jnp.stack([cmp_k_w1[0], cmp_v_w1[0]]).astype(BF16)
    kc, vct = _compress(r, pos, w1c, cmp_k_w2[0].astype(BF16), cmp_v_w2[0].T.astype(BF16), B, n_c)

    o_nsa = _nsa(fm, rm, gz, kc, vct, tcmp, ovl, tile, twin, B, S,
                 col_ksl, col_kwn, row_nq, row_vsl, row_vwn)

    h = _post([o_moba, o_nsa], h, even_w_out[0].astype(BF16), mlp_norm[0][None, :],
              mlp_w1[0].astype(BF16), mlp_w2[0].astype(BF16), None, B, S)

    q_w, k_w, v_w, f_w = jnp.split(odd_w_in[0], np.cumsum((FOX_W, FOX_W, FOX_W)), axis=1)
    wfm1 = jnp.concatenate([q_w * qs, v_w], axis=1).T.astype(BF16)
    wk = k_w.astype(BF16)
    f_w = jnp.pad(f_w, ((0, 0), (0, LANES - FOX_HEADS)))
    f_hi = f_w.astype(BF16)
    wf = jnp.stack([f_hi, (f_w - f_hi.astype(F32)).astype(BF16)])
    bf = jnp.pad(odd_b_forget[0], (0, LANES - FOX_HEADS))[None, :]
    tri = jnp.asarray(np.tril(np.ones((TM, TM))), BF16)
    place = np.zeros((3 * LANES, FOX_W), np.float32)
    heads = np.arange(FOX_HEADS)
    for term in range(3):
        place[term * LANES + heads, (heads // 2) * LANES + (1 - heads % 2) * HEAD_DIM + term] = -1.0
    fm1, ka = _inproj1(h, mix_norm[1][None, :], wfm1, wk, wf, bf, tri, jnp.asarray(place, BF16), B, S)
    o_fox = _fox(fm1, ka.reshape(B, S, -1), _causal_tiles(), B, S)

    h = _post([o_fox], h, odd_w_out[0].astype(BF16), mlp_norm[1][None, :],
              mlp_w1[1].astype(BF16), mlp_w2[1].astype(BF16), final_norm[None, :], B, S)
    return h.reshape(B, S, D)
```

```python
import functools
import math

import numpy as np
import jax
import jax.numpy as jnp
from jax import lax
from jax.experimental import pallas as pl
from jax.experimental.pallas import tpu as pltpu

D_MODEL = 1024
HEAD_DIM = 64
MOBA_HEADS = 8
MOBA_BLOCK = 256
MOBA_TOPK = 3
NSA_HEADS = 8
NSA_KV_GROUPS = 2
NSA_HPG = NSA_HEADS // NSA_KV_GROUPS
NSA_CMP_BLOCK = 32
NSA_CMP_STRIDE = 16
NSA_CMP_HIDDEN = 256
NSA_SLC_BLOCK = 64
NSA_TOPN = 16
NSA_WINDOW = 512
NSA_FORCE_SCORE = 1e6
FOX_HEADS = 16
D_FF = 4 * D_MODEL
REL_BUCKETS = 32
REL_MAX_DISTANCE = 1024
RMS_EPS = 1e-5
NEG_INF = -1e30
SCALE = HEAD_DIM ** -0.5

MOBA_W = MOBA_HEADS * HEAD_DIM
NSA_W = NSA_HEADS * HEAD_DIM
NSA_KV_W = NSA_KV_GROUPS * HEAD_DIM
FOX_W = FOX_HEADS * HEAD_DIM

LANES = 128
SUBLANES = 8
TQ = 256
TK = 256
TB = 256
QPK = TK // TQ
KSUB = TK // TB
TM = 512
CH = 256
FF_CH = 1024
VMEM_LIMIT = 56 * 1024 * 1024
NE_BIAS = -(-(REL_MAX_DISTANCE + TB - 1) // TQ)
NE_WIN = -(-(NSA_WINDOW + TB - 1) // TQ)
GZ_ROWS = 16
HPS = 8
FOX_HPS = 16
SPT = TK // NSA_SLC_BLOCK
BF16_ROWS = 2 * SUBLANES
ACC_ROWS = HEAD_DIM + BF16_ROWS
ROW_CHUNK = 32
LOG2E = math.log2(math.e)

assert TK % TQ == 0 and TK % TB == 0 and TQ == TB and MOBA_BLOCK == TB and TB % NSA_SLC_BLOCK == 0 and SPT <= 8

F32 = jnp.float32
BF16 = jnp.bfloat16


def _dot(a, b):
    return jnp.dot(a, b, preferred_element_type=F32)


def _dot_nt(a, b):
    return lax.dot_general(a, b, (((1,), (1,)), ((), ())), preferred_element_type=F32)


def _dot_tn(a, b):
    return lax.dot_general(a, b, (((0,), (0,)), ((), ())), preferred_element_type=F32)


def _rmsnorm(x, g):
    ms = jnp.mean(x * x, axis=-1, keepdims=True)
    return x * lax.rsqrt(ms + RMS_EPS) * g


def _split3(x):
    a = x.astype(BF16)
    r = x - a.astype(F32)
    b = r.astype(BF16)
    c = (r - b.astype(F32)).astype(BF16)
    return a, b, c


def _const_spec(shape):
    nd = len(shape)
    return pl.BlockSpec(shape, lambda *_: (0,) * nd, pipeline_mode=pl.Buffered(1))


def _params(sem):
    return pltpu.CompilerParams(dimension_semantics=sem, vmem_limit_bytes=VMEM_LIMIT)


def _attend(n_tiles, tile_of, qk_fn, fix_fn, v_fn, s_scr, acc_scr, n_heads, first_fix=None):
    def put_scores(h, n, fix):
        s = qk_fn(h, n)
        s_scr[h] = s if fix is None else fix(h, n, s)

    for h in range(n_heads):
        put_scores(h, tile_of(0), fix_fn if first_fix is None else first_fix)
    acc_scr[...] = jnp.zeros_like(acc_scr)
    last = n_tiles - 1
    ones = jnp.ones((ACC_ROWS - HEAD_DIM, TK), BF16)
    chunks = range(0, TK, ROW_CHUNK)

    def body(i, ms):
        n = tile_of(i)
        n_next = tile_of(jnp.minimum(i + 1, last))
        out = []
        for h in range(n_heads):
            mx = s_scr[h, 0:ROW_CHUNK, :]
            for r0 in chunks[1:]:
                mx = jnp.maximum(mx, s_scr[h, r0:r0 + ROW_CHUNK, :])
            m_new = jnp.maximum(ms[h], jnp.max(mx, axis=0, keepdims=True))
            alpha = jnp.exp2(ms[h] - m_new)
            p = jnp.concatenate([jnp.exp2(s_scr[h, r0:r0 + ROW_CHUNK, :] - m_new).astype(BF16) for r0 in chunks],
                                axis=0)
            va = jnp.concatenate([v_fn(h, n), ones], axis=0)
            acc_scr[h] = alpha * acc_scr[h] + _dot(va, p)
            out.append(m_new)
            put_scores(h, n_next, fix_fn)
        return tuple(out)

    lax.fori_loop(0, n_tiles, body, tuple(jnp.full((1, TQ), NEG_INF, F32) for _ in range(n_heads)))
    return [acc_scr[h, :HEAD_DIM, :] / acc_scr[h, HEAD_DIM:HEAD_DIM + 1, :] for h in range(n_heads)]


def _rank_select(val, n_rows, k):
    sub = lax.broadcasted_iota(jnp.int32, (SUBLANES, val.shape[1]), 0)
    groups = [val[g0:g0 + SUBLANES, :] for g0 in range(0, val.shape[0], SUBLANES)]
    counts = [jnp.zeros(g.shape, F32) for g in groups]
    for m in range(n_rows):
        vm = val[m:m + 1, :]
        for g, vg in enumerate(groups):
            if g * SUBLANES > m:
                counts[g] = counts[g] + jnp.where(vm >= vg, 1.0, 0.0)
            elif (g + 1) * SUBLANES <= m:
                counts[g] = counts[g] + jnp.where(vm > vg, 1.0, 0.0)
            else:
                tie = jnp.where(sub > m % SUBLANES, 1.0, 0.0)
                counts[g] = counts[g] + jnp.where(vm > vg, 1.0, 0.0) + jnp.where(vm == vg, tie, 0.0)
    return jnp.concatenate(counts, axis=0) < k


def _inproj0_body(x_ref, g_ref, wrm_ref, wfm_ref, wgz_ref, wcv_ref, rm_ref, fm_ref, gz_ref, cv_ref):
    xn = _rmsnorm(x_ref[...], g_ref[...]).astype(BF16)
    cv = _dot(xn, wcv_ref[...]).astype(BF16)
    for j in range(cv_ref.shape[0]):
        cv_ref[j] = cv[:, j * HEAD_DIM:(j + 1) * HEAD_DIM]
    for c0 in range(0, rm_ref.shape[-1], CH):
        rm_ref[:, c0:c0 + CH] = _dot(xn, wrm_ref[:, c0:c0 + CH]).astype(BF16)
    for r0 in range(0, fm_ref.shape[2], CH):
        res = _dot_nt(wfm_ref[r0:r0 + CH, :], xn).astype(BF16)
        for t in range(TM // TK):
            fm_ref[0, t, r0:r0 + CH, :] = res[:, t * TK:(t + 1) * TK]
    gz_ref[0] = _dot_nt(wgz_ref[...], xn)


def _inproj0(x2, g, wrm, wfm, wgz, wcv, B, S):
    M = B * S
    nst = S // TM
    n_rm, n_fm, n_gz, n_cv = wrm.shape[1], wfm.shape[0], wgz.shape[0], wcv.shape[1] // HEAD_DIM
    return pl.pallas_call(
        _inproj0_body,
        grid=(M // TM,),
        in_specs=[
            pl.BlockSpec((TM, D_MODEL), lambda i: (i, 0)),
            _const_spec((1, D_MODEL)),
            _const_spec((D_MODEL, n_rm)),
            _const_spec((n_fm, D_MODEL)),
            _const_spec((n_gz, D_MODEL)),
            _const_spec(wcv.shape),
        ],
        out_specs=[
            pl.BlockSpec((TM, n_rm), lambda i: (i, 0)),
            pl.BlockSpec((1, TM // TK, n_fm, TK), lambda i: (i // nst, i % nst, 0, 0)),
            pl.BlockSpec((1, n_gz, TM), lambda i: (i // nst, 0, i % nst)),
            pl.BlockSpec((n_cv, TM, HEAD_DIM), lambda i: (0, i, 0)),
        ],
        out_shape=[
            jax.ShapeDtypeStruct((M, n_rm), BF16),
            jax.ShapeDtypeStruct((B, S // TK, n_fm, TK), BF16),
            jax.ShapeDtypeStruct((B, n_gz, S), F32),
            jax.ShapeDtypeStruct((n_cv, M, HEAD_DIM), BF16),
        ],
        compiler_params=_params(("parallel",)),
        name="inproj0",
    )(x2, g, wrm, wfm, wgz, wcv)


def _compress_body(rk_ref, rv_ref, pos_ref, w1_ref, w2k_ref, w2vt_ref, flip_ref, kc_ref, vct_ref):
    half = NSA_CMP_STRIDE * HEAD_DIM

    def hidden(r_ref, s):
        r = r_ref[0, 0]
        a = _dot(r, w1_ref[s, :half, :])
        b = _dot(r, w1_ref[s, half:, :])
        nxt = pltpu.roll(b, b.shape[0] - 1, axis=0)
        posb = _dot(pos_ref[s], w1_ref[s])[0:1]
        pre = a + nxt + posb
        act = (pre * jax.nn.sigmoid(pre)).astype(BF16)
        return _dot(flip_ref[...], act).astype(BF16)

    kc_ref[0, 0] = _dot(hidden(rk_ref, 0), w2k_ref[...]).astype(BF16)
    vct_ref[0, 0] = _dot_nt(w2vt_ref[...], hidden(rv_ref, 1)).astype(BF16)


def _compress(r, pos, w1, w2k, w2vt, B, NC):
    G = NSA_KV_GROUPS
    half = NSA_CMP_STRIDE * HEAD_DIM
    return pl.pallas_call(
        _compress_body,
        grid=(B, G),
        in_specs=[
            pl.BlockSpec((1, 1, NC, half), lambda b, g: (g, b, 0, 0)),
            pl.BlockSpec((1, 1, NC, half), lambda b, g: (G + g, b, 0, 0)),
            _const_spec(pos.shape),
            _const_spec(w1.shape),
            _const_spec(w2k.shape),
            _const_spec(w2vt.shape),
            _const_spec((NC, NC)),
        ],
        out_specs=[
            pl.BlockSpec((1, 1, NC, HEAD_DIM), lambda b, g: (b, g, 0, 0)),
            pl.BlockSpec((1, 1, HEAD_DIM, NC), lambda b, g: (b, g, 0, 0)),
        ],
        out_shape=[
            jax.ShapeDtypeStruct((B, G, NC, HEAD_DIM), BF16),
            jax.ShapeDtypeStruct((B, G, HEAD_DIM, NC), BF16),
        ],
        compiler_params=_params(("parallel", "parallel")),
        name="nsa_compress",
    )(r, r, pos, w1, w2k, w2vt, jnp.asarray(np.eye(NC)[::-1], BF16))


def _moba_body(q_ref, k_ref, v_ref, t_ref, o_ref, kmean_ref, mask_ref, s_scr, acc_scr, *, n_mb, topk):
    c = pl.program_id(2)
    blk = c // QPK
    qblk = c * TQ // MOBA_BLOCK

    @pl.when(c == 0)
    def _():
        kmean_ref[...] = jnp.zeros_like(kmean_ref)
        for n in range(n_mb):
            kblk = k_ref[0, n * MOBA_BLOCK:(n + 1) * MOBA_BLOCK, :].astype(F32)
            kmean_ref[n:n + 1, :] = jnp.mean(kblk, axis=0, keepdims=True)

    q = q_ref[0, 0]
    rowi = lax.broadcasted_iota(jnp.int32, (LANES, TQ), 0)
    nidx = lax.broadcasted_iota(jnp.int32, (kmean_ref.shape[0], TQ), 0)
    n_pad = kmean_ref.shape[0]
    km = jnp.concatenate(_split3(kmean_ref[...]), axis=0)
    qpads = []
    for h in range(HPS):
        lo = (h // 2) * LANES
        qpair = q[lo:lo + LANES, :]
        qh = jnp.where(rowi // HEAD_DIM == h % 2, qpair, jnp.zeros_like(qpair))
        terms = _dot(km[:, lo:lo + LANES], qh)
        route = terms[:n_pad] + terms[n_pad:2 * n_pad] + terms[2 * n_pad:]
        route = jnp.where(nidx < qblk, route, NEG_INF)
        sel = _rank_select(route, n_mb, topk) & (nidx < qblk)
        mask_ref[h] = jnp.where(sel | (nidx == qblk), 0.0, NEG_INF)
        qpads.append(qh)

    def qk_fn(h, n):
        rows = pl.ds(pl.multiple_of(n * TK, TK), TK)
        return _dot(k_ref[0, rows, (h // 2) * LANES:(h // 2 + 1) * LANES], qpads[h])

    def fix_fn(h, n, s):
        parts = []
        for u in range(KSUB):
            b = KSUB * n + u
            parts.append(mask_ref[h, pl.ds(b, 1), :] + t_ref[h, jnp.clip(c - b, -1, NE_BIAS) + 1])
        return s + jnp.concatenate(parts, axis=0)

    def v_fn(h, n):
        return v_ref[0, n, h * HEAD_DIM:(h + 1) * HEAD_DIM, :]

    outs = _attend(blk + 1, lambda i: blk - i, qk_fn, fix_fn, v_fn, s_scr, acc_scr, HPS)
    o_ref[0] = jnp.concatenate(outs, axis=0).astype(BF16)


def _moba(fm, rm, tab, B, S):
    n_mb = S // MOBA_BLOCK
    n_pad = -(-n_mb // BF16_ROWS) * BF16_ROWS
    ne = tab.shape[1]
    rows = HPS * HEAD_DIM
    body = functools.partial(_moba_body, n_mb=n_mb, topk=min(MOBA_TOPK, n_mb))
    return pl.pallas_call(
        body,
        grid=(B, MOBA_HEADS // HPS, S // TQ),
        in_specs=[
            pl.BlockSpec((1, 1, rows, TQ), lambda b, p, c: (b, c // QPK, p, c % QPK)),
            pl.BlockSpec((1, S, rows), lambda b, p, c: (b, 0, p)),
            pl.BlockSpec((1, S // TK, rows, TK), lambda b, p, c: (b, 0, MOBA_HEADS // HPS + p, 0)),
            pl.BlockSpec((HPS, ne, TB, TQ), lambda b, p, c: (p, 0, 0, 0), pipeline_mode=pl.Buffered(1)),
        ],
        out_specs=pl.BlockSpec((1, rows, TQ), lambda b, p, c: (b, p, c)),
        out_shape=jax.ShapeDtypeStruct((B, MOBA_W, S), BF16),
        scratch_shapes=[pltpu.VMEM((n_pad, rows), F32), pltpu.VMEM((HPS, n_pad, TQ), F32),
                        pltpu.VMEM((HPS, TK, TQ), F32), pltpu.VMEM((HPS, ACC_ROWS, TQ), F32)],
        compiler_params=_params(("parallel", "parallel", "arbitrary")),
        name="moba_attn",
    )(fm, rm, fm, tab)


def _nsa_body(q_ref, kc_ref, vct_ref, fc_ref, ovl_ref, ksl_ref, vsl_ref, kwn_ref, vwn_ref,
              tslc_ref, twin_ref, gz_ref, o_ref, s_scr, acc_scr, s_win, acc_win, sel_ref, out_scr, *, n_sb, n_sel):
    c = pl.program_id(1)
    blk = c // QPK
    G, J = NSA_KV_GROUPS, NSA_HPG
    H = G * J

    q = q_ref[0, 0]
    qs = [q[h * HEAD_DIM:(h + 1) * HEAD_DIM, :] for h in range(H)]
    zero = jnp.zeros((HEAD_DIM, TQ), BF16)
    qpads = [jnp.concatenate([zero] * (h // J) + [qs[h]] + [zero] * (G - 1 - h // J), axis=0) for h in range(H)]

    c0 = pl.multiple_of(c * (TQ // NSA_CMP_STRIDE), TQ // NSA_CMP_STRIDE)
    gate = jax.nn.sigmoid(gz_ref[0])

    def gated(h, branch, o):
        r = (h // J) * GZ_ROWS + branch * J + h % J
        return gate[r:r + 1, :] * o

    for g in range(G):
        kc = kc_ref[0, g]
        vct = vct_ref[0, g]
        psum = jnp.zeros((kc.shape[0], TQ), F32)
        for h in range(g * J, (g + 1) * J):
            s = _dot(kc, qs[h]) + fc_ref[h, pl.ds(c0, kc.shape[0]), :]
            m = jnp.max(s, axis=0, keepdims=True)
            p = jnp.exp2(s - m)
            l = jnp.sum(p, axis=0, keepdims=True)
            pn = p * jnp.where(m > 0.5 * NEG_INF, 1.0 / l, 0.0)
            out_scr[h] = gated(h, 0, _dot(vct, pn.astype(BF16)))
            psum = psum + pn

        ph = psum.astype(BF16)
        plo = (psum - ph.astype(F32)).astype(BF16)
        imp = _dot(ovl_ref[...], ph) + _dot(ovl_ref[...], plo)
        jb = lax.broadcasted_iota(jnp.int32, imp.shape, 0)
        t = c * TQ + lax.broadcasted_iota(jnp.int32, imp.shape, 1)
        sb = t // NSA_SLC_BLOCK
        forced = (jb == 0) | (jb == sb) | (jb == sb - 1)
        allowed = jb <= sb
        val = jnp.where(forced, imp + NSA_FORCE_SCORE, jnp.where(allowed, imp, NEG_INF))
        sel = _rank_select(val, n_sb, n_sel) & allowed
        selb = jnp.where(sel, 0.0, NEG_INF)
        for n in range(n_sb // SPT):
            slab = selb[n * SPT:(n + 1) * SPT, :]
            if SPT < sel_ref.shape[2]:
                slab = jnp.concatenate([slab, jnp.zeros((sel_ref.shape[2] - SPT, TQ), F32)], axis=0)
            sel_ref[g, n] = slab

    def slc_qk(h, n):
        return _dot(ksl_ref[0, pl.ds(pl.multiple_of(n * TK, TK), TK), :], qpads[h])

    def slc_fix(h, n, s):
        rows = sel_ref[h // J, n]
        mask = jnp.concatenate([jnp.broadcast_to(rows[b:b + 1, :], (NSA_SLC_BLOCK, TQ)) for b in range(SPT)], axis=0)
        bias = jnp.concatenate([tslc_ref[h, jnp.clip(c - (KSUB * n + u), -1, NE_BIAS) + 1] for u in range(KSUB)],
                               axis=0)
        return s + mask + bias

    def slc_v(h, n):
        return vsl_ref[0, n, (h // J) * HEAD_DIM:(h // J + 1) * HEAD_DIM, :]

    for h, o in enumerate(_attend(blk + 1, lambda i: blk - i, slc_qk, slc_fix, slc_v, s_scr, acc_scr, H)):
        out_scr[h] += gated(h, 1, o)

    def win_qk(h, n):
        return _dot(kwn_ref[0, pl.ds(pl.multiple_of(n * TK, TK), TK), :], qpads[h])

    def win_fix(h, n, s):
        return s + jnp.concatenate([twin_ref[h, jnp.clip(c - (KSUB * n + u), -1, NE_WIN) + 1] for u in range(KSUB)],
                                   axis=0)

    def win_v(h, n):
        return vwn_ref[0, n, (h // J) * HEAD_DIM:(h // J + 1) * HEAD_DIM, :]

    w_lo = jnp.maximum(c - NE_WIN + 1, 0) // KSUB
    o_win = _attend(blk - w_lo + 1, lambda i: blk - i, win_qk, win_fix, win_v, s_win, acc_win, H)

    for h in range(H):
        o_ref[0, h * HEAD_DIM:(h + 1) * HEAD_DIM, :] = (out_scr[h] + gated(h, 2, o_win[h])).astype(BF16)


def _nsa(fm, rm, gz, kc, vct, tcmp, ovl, tslc, twin, B, S, col_ksl, col_kwn, row_q, row_vsl, row_vwn):
    G, J = NSA_KV_GROUPS, NSA_HPG
    NC = kc.shape[2]
    n_sb = S // NSA_SLC_BLOCK
    body = functools.partial(_nsa_body, n_sb=n_sb, n_sel=min(NSA_TOPN, n_sb))
    H = G * J
    kvrows = G * HEAD_DIM
    one = pl.Buffered(1)
    return pl.pallas_call(
        body,
        grid=(B, S // TQ),
        in_specs=[
            pl.BlockSpec((1, 1, NSA_W, TQ), lambda b, c: (b, c // QPK, row_q // NSA_W, c % QPK)),
            pl.BlockSpec((1, G, NC, HEAD_DIM), lambda b, c: (b, 0, 0, 0)),
            pl.BlockSpec((1, G, HEAD_DIM, NC), lambda b, c: (b, 0, 0, 0)),
            _const_spec(tcmp.shape),
            _const_spec(ovl.shape),
            pl.BlockSpec((1, S, LANES), lambda b, c: (b, 0, col_ksl // LANES)),
            pl.BlockSpec((1, S // TK, kvrows, TK), lambda b, c: (b, 0, row_vsl // kvrows, 0)),
            pl.BlockSpec((1, S, LANES), lambda b, c: (b, 0, col_kwn // LANES)),
            pl.BlockSpec((1, S // TK, kvrows, TK), lambda b, c: (b, 0, row_vwn // kvrows, 0)),
            pl.BlockSpec((H, tslc.shape[1], TB, TQ), lambda b, c: (MOBA_HEADS // H, 0, 0, 0), pipeline_mode=one),
            _const_spec(twin.shape),
            pl.BlockSpec((1, G * GZ_ROWS, TQ), lambda b, c: (b, 0, c)),
        ],
        out_specs=pl.BlockSpec((1, NSA_W, TQ), lambda b, c: (b, 0, c)),
        out_shape=jax.ShapeDtypeStruct((B, NSA_W, S), BF16),
        scratch_shapes=[pltpu.VMEM((H, TK, TQ), F32), pltpu.VMEM((H, ACC_ROWS, TQ), F32),
                        pltpu.VMEM((H, TK, TQ), F32), pltpu.VMEM((H, ACC_ROWS, TQ), F32),
                        pltpu.VMEM((G, S // TK, SUBLANES, TQ), F32), pltpu.VMEM((H, HEAD_DIM, TQ), F32)],
        compiler_params=_params(("parallel", "arbitrary")),
        name="nsa_attn",
    )(fm, kc, vct, tcmp, ovl, rm, fm, rm, fm, tslc, twin, gz)


def _inproj1_body(x_ref, g_ref, wfm_ref, wk_ref, wf_ref, bf_ref, tri_ref, place_ref, fm_ref, ka_ref, carry_ref,
                  *, nst):
    i = pl.program_id(0)

    @pl.when(i % nst == 0)
    def _():
        carry_ref[...] = jnp.zeros_like(carry_ref)

    xf = _rmsnorm(x_ref[...], g_ref[...])
    xn = xf.astype(BF16)
    xlo = (xf - xn.astype(F32)).astype(BF16)
    for r0 in range(0, fm_ref.shape[2], CH):
        res = _dot_nt(wfm_ref[r0:r0 + CH, :], xn).astype(BF16)
        for t in range(TM // TK):
            fm_ref[0, t, r0:r0 + CH, :] = res[:, t * TK:(t + 1) * TK]

    fz = _dot(xn, wf_ref[0]) + _dot(xlo, wf_ref[0]) + _dot(xn, wf_ref[1]) + bf_ref[...]
    logf = jnp.minimum(fz, 0.0) - jnp.log(1.0 + jnp.exp(-jnp.abs(fz)))
    tri = tri_ref[...]
    h1, h2, h3 = _split3(logf)
    cum = _dot(tri, h1) + _dot(tri, h2) + _dot(tri, h3) + carry_ref[0:1, :]
    carry_ref[...] = jnp.broadcast_to(cum[TM - 1:TM, :], carry_ref.shape)
    cc = jnp.concatenate(_split3(cum * LOG2E), axis=1)
    low = lax.broadcasted_iota(jnp.int32, (TM, LANES), 1) < HEAD_DIM
    for c0 in range(0, wk_ref.shape[-1], FF_CH):
        kp = _dot(xn, wk_ref[:, c0:c0 + FF_CH])
        dp = _dot(cc, place_ref[:, c0:c0 + FF_CH])
        for t0 in range(0, FF_CH, LANES):
            j = (c0 + t0) // LANES
            kt, dt = kp[:, t0:t0 + LANES], dp[:, t0:t0 + LANES]
            ka_ref[:, 2 * j * LANES:(2 * j + 1) * LANES] = jnp.where(low, kt, dt).astype(BF16)
            ka_ref[:, (2 * j + 1) * LANES:(2 * j + 2) * LANES] = jnp.where(low, dt, kt).astype(BF16)


def _inproj1(x2, g, wfm, wk, wf, bf, tri, place, B, S):
    M = B * S
    nst = S // TM
    n_fm, n_ka = wfm.shape[0], FOX_HEADS * LANES
    return pl.pallas_call(
        functools.partial(_inproj1_body, nst=nst),
        grid=(M // TM,),
        in_specs=[
            pl.BlockSpec((TM, D_MODEL), lambda i: (i, 0)),
            _const_spec((1, D_MODEL)),
            _const_spec(wfm.shape),
            _const_spec(wk.shape),
            _const_spec(wf.shape),
            _const_spec(bf.shape),
            _const_spec(tri.shape),
            _const_spec(place.shape),
        ],
        out_specs=[
            pl.BlockSpec((1, TM // TK, n_fm, TK), lambda i: (i // nst, i % nst, 0, 0)),
            pl.BlockSpec((TM, n_ka), lambda i: (i, 0)),
        ],
        out_shape=[
            jax.ShapeDtypeStruct((B, S // TK, n_fm, TK), BF16),
            jax.ShapeDtypeStruct((M, n_ka), BF16),
        ],
        scratch_shapes=[pltpu.VMEM((8, LANES), F32)],
        compiler_params=_params(("arbitrary",)),
        name="inproj1",
    )(x2, g, wfm, wk, wf, bf, tri, place)


def _fox_body(q_ref, k_ref, v_ref, cm_ref, o_ref, s_scr, acc_scr):
    c = pl.program_id(2)
    blk = c // QPK
    q = q_ref[0, 0]
    ones = jnp.ones((LANES - HEAD_DIM, TQ), BF16)
    qhs = [q[h * HEAD_DIM:(h + 1) * HEAD_DIM, :] for h in range(FOX_HPS)]
    qas = [jnp.concatenate([qhs[h], ones] if h % 2 == 0 else [ones, qhs[h]], axis=0) for h in range(FOX_HPS)]

    def qk_fn(h, n):
        rows = pl.ds(pl.multiple_of(n * TK, TK), TK)
        return _dot(k_ref[0, rows, h * LANES:(h + 1) * LANES], qas[h])

    causal = cm_ref[c % QPK]
    outs = _attend(blk + 1, lambda i: blk - i, qk_fn, None,
                   lambda h, n: v_ref[0, n, h * HEAD_DIM:(h + 1) * HEAD_DIM, :], s_scr, acc_scr, FOX_HPS,
                   first_fix=lambda h, n, s: s + causal)
    o_ref[0] = jnp.concatenate(outs, axis=0).astype(BF16)


def _fox(fm, ka, cmask, B, S):
    rows = FOX_HPS * HEAD_DIM
    return pl.pallas_call(
        _fox_body,
        grid=(B, FOX_HEADS // FOX_HPS, S // TQ),
        in_specs=[
            pl.BlockSpec((1, 1, rows, TQ), lambda b, h, c: (b, c // QPK, h, c % QPK)),
            pl.BlockSpec((1, S, FOX_HPS * LANES), lambda b, h, c: (b, 0, h), pipeline_mode=pl.Buffered(1)),
            pl.BlockSpec((1, S // TK, rows, TK), lambda b, h, c: (b, 0, FOX_HEADS // FOX_HPS + h, 0),
                         pipeline_mode=pl.Buffered(1)),
            _const_spec(cmask.shape),
        ],
        out_specs=pl.BlockSpec((1, rows, TQ), lambda b, h, c: (b, h, c)),
        out_shape=jax.ShapeDtypeStruct((B, FOX_W, S), BF16),
        scratch_shapes=[pltpu.VMEM((FOX_HPS, TK, TQ), F32), pltpu.VMEM((FOX_HPS, ACC_ROWS, TQ), F32)],
        compiler_params=_params(("parallel", "parallel", "arbitrary")),
        name="fox_attn",
    )(fm, ka, fm, cmask)


def _post_body(*refs, n_parts, final):
    o_refs = refs[:n_parts]
    h_ref, wo_ref, g_ref, w1_ref, w2_ref = refs[n_parts:n_parts + 5]
    gf_ref = refs[n_parts + 5] if final else None
    out_ref, hn_ref = refs[-2:]
    h1 = h_ref[...]
    r0 = 0
    for o_ref in o_refs:
        nf = o_ref.shape[1]
        h1 = h1 + _dot_tn(o_ref[0], wo_ref[r0:r0 + nf, :])
        r0 += nf
    out_ref[...] = h1
    hn_ref[...] = _rmsnorm(out_ref[...], g_ref[...]).astype(BF16)
    for c0 in range(0, D_FF, FF_CH):
        a = jnp.maximum(_dot(hn_ref[...], w1_ref[:, c0:c0 + FF_CH]), 0.0)
        out_ref[...] += _dot((a * a).astype(BF16), w2_ref[c0:c0 + FF_CH, :])
    if final:
        out_ref[...] = _rmsnorm(out_ref[...], gf_ref[...])


def _post(o_parts, h2, wo, g, w1, w2, gf, B, S):
    M = B * S
    nst = S // TM
    final = gf is not None
    in_specs = [pl.BlockSpec((1, o.shape[1], TM), lambda i: (i // nst, 0, i % nst)) for o in o_parts]
    in_specs += [
        pl.BlockSpec((TM, D_MODEL), lambda i: (i, 0)),
        _const_spec(wo.shape),
        _const_spec((1, D_MODEL)),
        _const_spec(w1.shape),
        _const_spec(w2.shape),
    ]
    args = list(o_parts) + [h2, wo, g, w1, w2]
    if final:
        in_specs.append(_const_spec((1, D_MODEL)))
        args.append(gf)
    return pl.pallas_call(
        functools.partial(_post_body, n_parts=len(o_parts), final=final),
        grid=(M // TM,),
        in_specs=in_specs,
        out_specs=pl.BlockSpec((TM, D_MODEL), lambda i: (i, 0)),
        out_shape=jax.ShapeDtypeStruct((M, D_MODEL), F32),
        scratch_shapes=[pltpu.VMEM((TM, D_MODEL), BF16)],
        compiler_params=_params(("parallel",)),
        name="post_final" if final else "post",
    )(*args)


def _rel_bucket(dist):
    n = jnp.maximum(dist, 0)
    max_exact = REL_BUCKETS // 2
    nf = jnp.maximum(n, 1).astype(jnp.float32)
    large = max_exact + (jnp.log(nf / max_exact) / math.log(REL_MAX_DISTANCE / max_exact)
                         * (REL_BUCKETS - max_exact)).astype(jnp.int32)
    large = jnp.minimum(large, REL_BUCKETS - 1)
    return jnp.where(n < max_exact, n, large)


def _bias_tables(rel_bias, S):
    n_heads = rel_bias.shape[1]
    table = rel_bias.T * LOG2E

    def bias_of(dist):
        tab = table.reshape((n_heads, REL_BUCKETS) + (1,) * dist.ndim)
        bkt = _rel_bucket(jnp.asarray(dist))[None]
        out = jnp.zeros((n_heads,) + dist.shape, F32)
        for b in range(REL_BUCKETS):
            out = jnp.where(bkt == b, tab[:, b], out)
        return jnp.where(jnp.asarray(dist)[None] >= 0, out, NEG_INF)

    d = (np.arange(-1, NE_BIAS)[:, None, None] * TQ + np.arange(TQ)[None, None, :] - np.arange(TB)[None, :, None])
    vals = bias_of(d)
    far = table[:, REL_BUCKETS - 1][:, None, None, None]
    tile = jnp.where(d >= 0, vals - far, NEG_INF)
    tile = jnp.concatenate([tile, jnp.zeros_like(tile[:, :1])], axis=1)
    dw = d[:NE_WIN + 2]
    twin = jnp.where((dw >= 0) & (dw < NSA_WINDOW), vals[MOBA_HEADS:, :NE_WIN + 2], NEG_INF)
    n_c = S // NSA_CMP_STRIDE
    u = np.arange(2 * n_c)[:, None]
    dc = np.arange(TQ)[None, :] + NSA_CMP_STRIDE * (u - (n_c - 1)) - (NSA_CMP_BLOCK - 1)
    fcmp = bias_of(dc)[MOBA_HEADS:]
    return tile, twin, fcmp


def _selection_constants(S):
    n_c = S // NSA_CMP_STRIDE
    n_cmp = (S - NSA_CMP_BLOCK) // NSA_CMP_STRIDE + 1
    n_sb = S // NSA_SLC_BLOCK
    ci = np.arange(n_c)[None, :] * NSA_CMP_STRIDE
    sj = np.arange(n_sb)[:, None] * NSA_SLC_BLOCK
    ovl = (ci < sj + NSA_SLC_BLOCK) & (ci + NSA_CMP_BLOCK > sj) & (np.arange(n_c)[None, :] < n_cmp)
    ovl = ovl[:, ::-1]
    return jnp.asarray(ovl, BF16)


def _causal_tiles():
    e = np.arange(QPK)[:, None, None]
    d = e * TQ + np.arange(TQ)[None, None, :] - np.arange(TK)[None, :, None]
    return jnp.asarray(np.where(d >= 0, 0.0, NEG_INF), F32)


def kernel(x, rel_bias, mix_norm, mlp_norm, even_w_in, even_w_out, cmp_pos_k, cmp_pos_v, cmp_k_w1, cmp_k_w2,
           cmp_v_w1, cmp_v_w2, odd_w_in, odd_b_forget, odd_w_out, mlp_w1, mlp_w2, final_norm):
    B, S, D = x.shape
    assert D == D_MODEL and S % TM == 0
    G, J = NSA_KV_GROUPS, NSA_HPG
    h = x.reshape(B * S, D)

    offs = np.cumsum((MOBA_W, MOBA_W, MOBA_W, NSA_W) + (NSA_KV_W,) * 6)
    mq_w, mk_w, mv_w, nq_w, kc_w, vc_w, ksl_w, vsl_w, kwn_w, vwn_w, gz_w = jnp.split(even_w_in[0], offs, axis=1)
    wrm = jnp.concatenate([mk_w, ksl_w, kwn_w], axis=1).astype(BF16)
    col_ksl, col_kwn = MOBA_W, MOBA_W + NSA_KV_W
    wcv = jnp.concatenate([kc_w, vc_w], axis=1).astype(BF16)
    qs = SCALE * LOG2E
    wfm = jnp.concatenate([mq_w * qs, mv_w, nq_w * qs, vsl_w, vwn_w], axis=1).T.astype(BF16)
    row_nq, row_vsl, row_vwn = 2 * MOBA_W, 2 * MOBA_W + NSA_W, 2 * MOBA_W + NSA_W + NSA_KV_W
    gzw = gz_w.T.reshape(G, J, 3, D).transpose(0, 2, 1, 3).reshape(G, 3 * J, D)
    gzw = jnp.pad(gzw, ((0, 0), (0, GZ_ROWS - 3 * J), (0, 0))).reshape(G * GZ_ROWS, D).astype(BF16)

    rm, fm, gz, cv = _inproj0(h, mix_norm[0][None, :], wrm, wfm, gzw, wcv, B, S)
    rm = rm.reshape(B, S, -1)

    tile, twin, tcmp = _bias_tables(rel_bias, S)
    ovl = _selection_constants(S)

    o_moba = _moba(fm, rm, tile, B, S)

    n_c = S // NSA_CMP_STRIDE
    r = cv.reshape(2 * G, B, n_c, NSA_CMP_STRIDE * HEAD_DIM)
    pos = jnp.stack([cmp_pos_k[0].reshape(1, -1), cmp_pos_v[0].reshape(1, -1)])
    pos = jnp.pad(pos, ((0, 0), (0, 7), (0, 0))).astype(BF16)
    w1c = jnp.stack([cmp_k_w1[0], cmp_v_w1[0]]).astype(BF16)
    kc, vct = _compress(r, pos, w1c, cmp_k_w2[0].astype(BF16), cmp_v_w2[0].T.astype(BF16), B, n_c)

    o_nsa = _nsa(fm, rm, gz, kc, vct, tcmp, ovl, tile, twin, B, S,
                 col_ksl, col_kwn, row_nq, row_vsl, row_vwn)

    h = _post([o_moba, o_nsa], h, even_w_out[0].astype(BF16), mlp_norm[0][None, :],
              mlp_w1[0].astype(BF16), mlp_w2[0].astype(BF16), None, B, S)

    q_w, k_w, v_w, f_w = jnp.split(odd_w_in[0], np.cumsum((FOX_W, FOX_W, FOX_W)), axis=1)
    wfm1 = jnp.concatenate([q_w * qs, v_w], axis=1).T.astype(BF16)
    wk = k_w.astype(BF16)
    f_w = jnp.pad(f_w, ((0, 0), (0, LANES - FOX_HEADS)))
    f_hi = f_w.astype(BF16)
    wf = jnp.stack([f_hi, (f_w - f_hi.astype(F32)).astype(BF16)])
    bf = jnp.pad(odd_b_forget[0], (0, LANES - FOX_HEADS))[None, :]
    tri = jnp.asarray(np.tril(np.ones((TM, TM))), BF16)
    place = np.zeros((3 * LANES, FOX_W), np.float32)
    heads = np.arange(FOX_HEADS)
    for term in range(3):
        place[term * LANES + heads, (heads // 2) * LANES + (1 - heads % 2) * HEAD_DIM + term] = -1.0
    fm1, ka = _inproj1(h, mix_norm[1][None, :], wfm1, wk, wf, bf, tri, jnp.asarray(place, BF16), B, S)
    o_fox = _fox(fm1, ka.reshape(B, S, -1), _causal_tiles(), B, S)

    h = _post([o_fox], h, odd_w_out[0].astype(BF16), mlp_norm[1][None, :],
              mlp_w1[1].astype(BF16), mlp_w2[1].astype(BF16), final_norm[None, :], B, S)
    return h.reshape(B, S, D)
```

```python
import functools
import math

import numpy as np
import jax
import jax.numpy as jnp
from jax import lax
from jax.experimental import pallas as pl
from jax.experimental.pallas import tpu as pltpu

D_MODEL = 1024
HEAD_DIM = 64
MOBA_HEADS = 8
MOBA_BLOCK = 256
MOBA_TOPK = 3
NSA_HEADS = 8
NSA_KV_GROUPS = 2
NSA_HPG = NSA_HEADS // NSA_KV_GROUPS
NSA_CMP_BLOCK = 32
NSA_CMP_STRIDE = 16
NSA_CMP_HIDDEN = 256
NSA_SLC_BLOCK = 64
NSA_TOPN = 16
NSA_WINDOW = 512
NSA_FORCE_SCORE = 1e6
FOX_HEADS = 16
D_FF = 4 * D_MODEL
REL_BUCKETS = 32
REL_MAX_DISTANCE = 1024
RMS_EPS = 1e-5
NEG_INF = -1e30
SCALE = HEAD_DIM ** -0.5

MOBA_W = MOBA_HEADS * HEAD_DIM
NSA_W = NSA_HEADS * HEAD_DIM
NSA_KV_W = NSA_KV_GROUPS * HEAD_DIM
FOX_W = FOX_HEADS * HEAD_DIM

LANES = 128
SUBLANES = 8
TQ = 256
TK = 256
TB = 256
QPK = TK // TQ
KSUB = TK // TB
TM = 512
CH = 256
FF_CH = 1024
VMEM_LIMIT = 56 * 1024 * 1024
NE_BIAS = -(-(REL_MAX_DISTANCE + TB - 1) // TQ)
NE_WIN = -(-(NSA_WINDOW + TB - 1) // TQ)
GZ_ROWS = 16
HPS = 8
FOX_HPS = 16
SPT = TK // NSA_SLC_BLOCK
BF16_ROWS = 2 * SUBLANES
ACC_ROWS = HEAD_DIM + BF16_ROWS
ROW_CHUNK = 16
LOG2E = math.log2(math.e)

assert TK % TQ == 0 and TK % TB == 0 and TQ == TB and MOBA_BLOCK == TB and TB % NSA_SLC_BLOCK == 0 and SPT <= 8

F32 = jnp.float32
BF16 = jnp.bfloat16


def _dot(a, b):
    return jnp.dot(a, b, preferred_element_type=F32)


def _dot_nt(a, b):
    return lax.dot_general(a, b, (((1,), (1,)), ((), ())), preferred_element_type=F32)


def _dot_tn(a, b):
    return lax.dot_general(a, b, (((0,), (0,)), ((), ())), preferred_element_type=F32)


def _rmsnorm(x, g):
    ms = jnp.mean(x * x, axis=-1, keepdims=True)
    return x * lax.rsqrt(ms + RMS_EPS) * g


def _split3(x):
    a = x.astype(BF16)
    r = x - a.astype(F32)
    b = r.astype(BF16)
    c = (r - b.astype(F32)).astype(BF16)
    return a, b, c


def _const_spec(shape):
    nd = len(shape)
    return pl.BlockSpec(shape, lambda *_: (0,) * nd, pipeline_mode=pl.Buffered(1))


def _params(sem):
    return pltpu.CompilerParams(dimension_semantics=sem, vmem_limit_bytes=VMEM_LIMIT)


def _attend(n_tiles, tile_of, qk_fn, fix_fn, v_fn, s_scr, acc_scr, n_heads, first_fix=None):
    def put_scores(h, n, fix):
        s = qk_fn(h, n)
        s_scr[h] = s if fix is None else fix(h, n, s)

    for h in range(n_heads):
        put_scores(h, tile_of(0), fix_fn if first_fix is None else first_fix)
    acc_scr[...] = jnp.zeros_like(acc_scr)
    last = n_tiles - 1
    ones = jnp.ones((ACC_ROWS - HEAD_DIM, TK), BF16)
    chunks = range(0, TK, ROW_CHUNK)

    def body(i, ms):
        n = tile_of(i)
        n_next = tile_of(jnp.minimum(i + 1, last))
        out = []
        for h in range(n_heads):
            mx = s_scr[h, 0:ROW_CHUNK, :]
            for r0 in chunks[1:]:
                mx = jnp.maximum(mx, s_scr[h, r0:r0 + ROW_CHUNK, :])
            m_new = jnp.maximum(ms[h], jnp.max(mx, axis=0, keepdims=True))
            alpha = jnp.exp2(ms[h] - m_new)
            p = jnp.concatenate([jnp.exp2(s_scr[h, r0:r0 + ROW_CHUNK, :] - m_new).astype(BF16) for r0 in chunks],
                                axis=0)
            va = jnp.concatenate([v_fn(h, n), ones], axis=0)
            acc_scr[h] = alpha * acc_scr[h] + _dot(va, p)
            out.append(m_new)
            put_scores(h, n_next, fix_fn)
        return tuple(out)

    lax.fori_loop(0, n_tiles, body, tuple(jnp.full((1, TQ), NEG_INF, F32) for _ in range(n_heads)))
    return [acc_scr[h, :HEAD_DIM, :] / acc_scr[h, HEAD_DIM:HEAD_DIM + 1, :] for h in range(n_heads)]


def _rank_select(val, n_rows, k):
    sub = lax.broadcasted_iota(jnp.int32, (SUBLANES, val.shape[1]), 0)
    groups = [val[g0:g0 + SUBLANES, :] for g0 in range(0, val.shape[0], SUBLANES)]
    counts = [jnp.zeros(g.shape, F32) for g in groups]
    for m in range(n_rows):
        vm = val[m:m + 1, :]
        for g, vg in enumerate(groups):
            if g * SUBLANES > m:
                counts[g] = counts[g] + jnp.where(vm >= vg, 1.0, 0.0)
            elif (g + 1) * SUBLANES <= m:
                counts[g] = counts[g] + jnp.where(vm > vg, 1.0, 0.0)
            else:
                tie = jnp.where(sub > m % SUBLANES, 1.0, 0.0)
                counts[g] = counts[g] + jnp.where(vm > vg, 1.0, 0.0) + jnp.where(vm == vg, tie, 0.0)
    return jnp.concatenate(counts, axis=0) < k


def _inproj0_body(x_ref, g_ref, wrm_ref, wfm_ref, wgz_ref, wcv_ref, rm_ref, fm_ref, gz_ref, cv_ref):
    xn = _rmsnorm(x_ref[...], g_ref[...]).astype(BF16)
    cv = _dot(xn, wcv_ref[...]).astype(BF16)
    for j in range(cv_ref.shape[0]):
        cv_ref[j] = cv[:, j * HEAD_DIM:(j + 1) * HEAD_DIM]
    for c0 in range(0, rm_ref.shape[-1], CH):
        rm_ref[:, c0:c0 + CH] = _dot(xn, wrm_ref[:, c0:c0 + CH]).astype(BF16)
    for r0 in range(0, fm_ref.shape[2], CH):
        res = _dot_nt(wfm_ref[r0:r0 + CH, :], xn).astype(BF16)
        for t in range(TM // TK):
            fm_ref[0, t, r0:r0 + CH, :] = res[:, t * TK:(t + 1) * TK]
    gz_ref[0] = _dot_nt(wgz_ref[...], xn)


def _inproj0(x2, g, wrm, wfm, wgz, wcv, B, S):
    M = B * S
    nst = S // TM
    n_rm, n_fm, n_gz, n_cv = wrm.shape[1], wfm.shape[0], wgz.shape[0], wcv.shape[1] // HEAD_DIM
    return pl.pallas_call(
        _inproj0_body,
        grid=(M // TM,),
        in_specs=[
            pl.BlockSpec((TM, D_MODEL), lambda i: (i, 0)),
            _const_spec((1, D_MODEL)),
            _const_spec((D_MODEL, n_rm)),
            _const_spec((n_fm, D_MODEL)),
            _const_spec((n_gz, D_MODEL)),
            _const_spec(wcv.shape),
        ],
        out_specs=[
            pl.BlockSpec((TM, n_rm), lambda i: (i, 0)),
            pl.BlockSpec((1, TM // TK, n_fm, TK), lambda i: (i // nst, i % nst, 0, 0)),
            pl.BlockSpec((1, n_gz, TM), lambda i: (i // nst, 0, i % nst)),
            pl.BlockSpec((n_cv, TM, HEAD_DIM), lambda i: (0, i, 0)),
        ],
        out_shape=[
            jax.ShapeDtypeStruct((M, n_rm), BF16),
            jax.ShapeDtypeStruct((B, S // TK, n_fm, TK), BF16),
            jax.ShapeDtypeStruct((B, n_gz, S), F32),
            jax.ShapeDtypeStruct((n_cv, M, HEAD_DIM), BF16),
        ],
        compiler_params=_params(("parallel",)),
        name="inproj0",
    )(x2, g, wrm, wfm, wgz, wcv)


def _compress_body(rk_ref, rv_ref, pos_ref, w1_ref, w2k_ref, w2vt_ref, flip_ref, kc_ref, vct_ref):
    half = NSA_CMP_STRIDE * HEAD_DIM

    def hidden(r_ref, s):
        r = r_ref[0, 0]
        a = _dot(r, w1_ref[s, :half, :])
        b = _dot(r, w1_ref[s, half:, :])
        nxt = pltpu.roll(b, b.shape[0] - 1, axis=0)
        posb = _dot(pos_ref[s], w1_ref[s])[0:1]
        pre = a + nxt + posb
        act = (pre * jax.nn.sigmoid(pre)).astype(BF16)
        return _dot(flip_ref[...], act).astype(BF16)

    kc_ref[0, 0] = _dot(hidden(rk_ref, 0), w2k_ref[...]).astype(BF16)
    vct_ref[0, 0] = _dot_nt(w2vt_ref[...], hidden(rv_ref, 1)).astype(BF16)


def _compress(r, pos, w1, w2k, w2vt, B, NC):
    G = NSA_KV_GROUPS
    half = NSA_CMP_STRIDE * HEAD_DIM
    return pl.pallas_call(
        _compress_body,
        grid=(B, G),
        in_specs=[
            pl.BlockSpec((1, 1, NC, half), lambda b, g: (g, b, 0, 0)),
            pl.BlockSpec((1, 1, NC, half), lambda b, g: (G + g, b, 0, 0)),
            _const_spec(pos.shape),
            _const_spec(w1.shape),
            _const_spec(w2k.shape),
            _const_spec(w2vt.shape),
            _const_spec((NC, NC)),
        ],
        out_specs=[
            pl.BlockSpec((1, 1, NC, HEAD_DIM), lambda b, g: (b, g, 0, 0)),
            pl.BlockSpec((1, 1, HEAD_DIM, NC), lambda b, g: (b, g, 0, 0)),
        ],
        out_shape=[
            jax.ShapeDtypeStruct((B, G, NC, HEAD_DIM), BF16),
            jax.ShapeDtypeStruct((B, G, HEAD_DIM, NC), BF16),
        ],
        compiler_params=_params(("parallel", "parallel")),
        name="nsa_compress",
    )(r, r, pos, w1, w2k, w2vt, jnp.asarray(np.eye(NC)[::-1], BF16))


def _moba_body(q_ref, k_ref, v_ref, t_ref, o_ref, kmean_ref, mask_ref, s_scr, acc_scr, *, n_mb, topk):
    c = pl.program_id(2)
    blk = c // QPK
    qblk = c * TQ // MOBA_BLOCK

    @pl.when(c == 0)
    def _():
        kmean_ref[...] = jnp.zeros_like(kmean_ref)
        for n in range(n_mb):
            kblk = k_ref[0, n * MOBA_BLOCK:(n + 1) * MOBA_BLOCK, :].astype(F32)
            kmean_ref[n:n + 1, :] = jnp.mean(kblk, axis=0, keepdims=True)

    q = q_ref[0, 0]
    rowi = lax.broadcasted_iota(jnp.int32, (LANES, TQ), 0)
    nidx = lax.broadcasted_iota(jnp.int32, (kmean_ref.shape[0], TQ), 0)
    n_pad = kmean_ref.shape[0]
    km = jnp.concatenate(_split3(kmean_ref[...]), axis=0)
    qpads = []
    for h in range(HPS):
        lo = (h // 2) * LANES
        qpair = q[lo:lo + LANES, :]
        qh = jnp.where(rowi // HEAD_DIM == h % 2, qpair, jnp.zeros_like(qpair))
        terms = _dot(km[:, lo:lo + LANES], qh)
        route = terms[:n_pad] + terms[n_pad:2 * n_pad] + terms[2 * n_pad:]
        route = jnp.where(nidx < qblk, route, NEG_INF)
        sel = _rank_select(route, n_mb, topk) & (nidx < qblk)
        mask_ref[h] = jnp.where(sel | (nidx == qblk), 0.0, NEG_INF)
        qpads.append(qh)

    def qk_fn(h, n):
        rows = pl.ds(pl.multiple_of(n * TK, TK), TK)
        return _dot(k_ref[0, rows, (h // 2) * LANES:(h // 2 + 1) * LANES], qpads[h])

    def fix_fn(h, n, s):
        parts = []
        for u in range(KSUB):
            b = KSUB * n + u
            parts.append(mask_ref[h, pl.ds(b, 1), :] + t_ref[h, jnp.clip(c - b, -1, NE_BIAS) + 1])
        return s + jnp.concatenate(parts, axis=0)

    def v_fn(h, n):
        return v_ref[0, n, h * HEAD_DIM:(h + 1) * HEAD_DIM, :]

    outs = _attend(blk + 1, lambda i: blk - i, qk_fn, fix_fn, v_fn, s_scr, acc_scr, HPS)
    o_ref[0] = jnp.concatenate(outs, axis=0).astype(BF16)


def _moba(fm, rm, tab, B, S):
    n_mb = S // MOBA_BLOCK
    n_pad = -(-n_mb // BF16_ROWS) * BF16_ROWS
    ne = tab.shape[1]
    rows = HPS * HEAD_DIM
    body = functools.partial(_moba_body, n_mb=n_mb, topk=min(MOBA_TOPK, n_mb))
    return pl.pallas_call(
        body,
        grid=(B, MOBA_HEADS // HPS, S // TQ),
        in_specs=[
            pl.BlockSpec((1, 1, rows, TQ), lambda b, p, c: (b, c // QPK, p, c % QPK)),
            pl.BlockSpec((1, S, rows), lambda b, p, c: (b, 0, p)),
            pl.BlockSpec((1, S // TK, rows, TK), lambda b, p, c: (b, 0, MOBA_HEADS // HPS + p, 0)),
            pl.BlockSpec((HPS, ne, TB, TQ), lambda b, p, c: (p, 0, 0, 0), pipeline_mode=pl.Buffered(1)),
        ],
        out_specs=pl.BlockSpec((1, rows, TQ), lambda b, p, c: (b, p, c)),
        out_shape=jax.ShapeDtypeStruct((B, MOBA_W, S), BF16),
        scratch_shapes=[pltpu.VMEM((n_pad, rows), F32), pltpu.VMEM((HPS, n_pad, TQ), F32),
                        pltpu.VMEM((HPS, TK, TQ), F32), pltpu.VMEM((HPS, ACC_ROWS, TQ), F32)],
        compiler_params=_params(("parallel", "parallel", "arbitrary")),
        name="moba_attn",
    )(fm, rm, fm, tab)


def _nsa_body(q_ref, kc_ref, vct_ref, fc_ref, ovl_ref, ksl_ref, vsl_ref, kwn_ref, vwn_ref,
              tslc_ref, twin_ref, gz_ref, o_ref, s_scr, acc_scr, s_win, acc_win, sel_ref, out_scr, *, n_sb, n_sel):
    c = pl.program_id(1)
    blk = c // QPK
    G, J = NSA_KV_GROUPS, NSA_HPG
    H = G * J

    q = q_ref[0, 0]
    qs = [q[h * HEAD_DIM:(h + 1) * HEAD_DIM, :] for h in range(H)]
    zero = jnp.zeros((HEAD_DIM, TQ), BF16)
    qpads = [jnp.concatenate([zero] * (h // J) + [qs[h]] + [zero] * (G - 1 - h // J), axis=0) for h in range(H)]

    c0 = pl.multiple_of(c * (TQ // NSA_CMP_STRIDE), TQ // NSA_CMP_STRIDE)
    gate = jax.nn.sigmoid(gz_ref[0])

    def gated(h, branch, o):
        r = (h // J) * GZ_ROWS + branch * J + h % J
        return gate[r:r + 1, :] * o

    for g in range(G):
        kc = kc_ref[0, g]
        vct = vct_ref[0, g]
        psum = jnp.zeros((kc.shape[0], TQ), F32)
        for h in range(g * J, (g + 1) * J):
            s = _dot(kc, qs[h]) + fc_ref[h, pl.ds(c0, kc.shape[0]), :]
            m = jnp.max(s, axis=0, keepdims=True)
            p = jnp.exp2(s - m)
            l = jnp.sum(p, axis=0, keepdims=True)
            pn = p * jnp.where(m > 0.5 * NEG_INF, 1.0 / l, 0.0)
            out_scr[h] = gated(h, 0, _dot(vct, pn.astype(BF16)))
            psum = psum + pn

        ph = psum.astype(BF16)
        plo = (psum - ph.astype(F32)).astype(BF16)
        imp = _dot(ovl_ref[...], ph) + _dot(ovl_ref[...], plo)
        jb = lax.broadcasted_iota(jnp.int32, imp.shape, 0)
        t = c * TQ + lax.broadcasted_iota(jnp.int32, imp.shape, 1)
        sb = t // NSA_SLC_BLOCK
        forced = (jb == 0) | (jb == sb) | (jb == sb - 1)
        allowed = jb <= sb
        val = jnp.where(forced, imp + NSA_FORCE_SCORE, jnp.where(allowed, imp, NEG_INF))
        sel = _rank_select(val, n_sb, n_sel) & allowed
        selb = jnp.where(sel, 0.0, NEG_INF)
        for n in range(n_sb // SPT):
            slab = selb[n * SPT:(n + 1) * SPT, :]
            if SPT < sel_ref.shape[2]:
                slab = jnp.concatenate([slab, jnp.zeros((sel_ref.shape[2] - SPT, TQ), F32)], axis=0)
            sel_ref[g, n] = slab

    def slc_qk(h, n):
        return _dot(ksl_ref[0, pl.ds(pl.multiple_of(n * TK, TK), TK), :], qpads[h])

    def slc_fix(h, n, s):
        rows = sel_ref[h // J, n]
        mask = jnp.concatenate([jnp.broadcast_to(rows[b:b + 1, :], (NSA_SLC_BLOCK, TQ)) for b in range(SPT)], axis=0)
        bias = jnp.concatenate([tslc_ref[h, jnp.clip(c - (KSUB * n + u), -1, NE_BIAS) + 1] for u in range(KSUB)],
                               axis=0)
        return s + mask + bias

    def slc_v(h, n):
        return vsl_ref[0, n, (h // J) * HEAD_DIM:(h // J + 1) * HEAD_DIM, :]

    for h, o in enumerate(_attend(blk + 1, lambda i: blk - i, slc_qk, slc_fix, slc_v, s_scr, acc_scr, H)):
        out_scr[h] += gated(h, 1, o)

    def win_qk(h, n):
        return _dot(kwn_ref[0, pl.ds(pl.multiple_of(n * TK, TK), TK), :], qpads[h])

    def win_fix(h, n, s):
        return s + jnp.concatenate([twin_ref[h, jnp.clip(c - (KSUB * n + u), -1, NE_WIN) + 1] for u in range(KSUB)],
                                   axis=0)

    def win_v(h, n):
        return vwn_ref[0, n, (h // J) * HEAD_DIM:(h // J + 1) * HEAD_DIM, :]

    w_lo = jnp.maximum(c - NE_WIN + 1, 0) // KSUB
    o_win = _attend(blk - w_lo + 1, lambda i: blk - i, win_qk, win_fix, win_v, s_win, acc_win, H)

    for h in range(H):
        o_ref[0, h * HEAD_DIM:(h + 1) * HEAD_DIM, :] = (out_scr[h] + gated(h, 2, o_win[h])).astype(BF16)


def _nsa(fm, rm, gz, kc, vct, tcmp, ovl, tslc, twin, B, S, col_ksl, col_kwn, row_q, row_vsl, row_vwn):
    G, J = NSA_KV_GROUPS, NSA_HPG
    NC = kc.shape[2]
    n_sb = S // NSA_SLC_BLOCK
    body = functools.partial(_nsa_body, n_sb=n_sb, n_sel=min(NSA_TOPN, n_sb))
    H = G * J
    kvrows = G * HEAD_DIM
    one = pl.Buffered(1)
    return pl.pallas_call(
        body,
        grid=(B, S // TQ),
        in_specs=[
            pl.BlockSpec((1, 1, NSA_W, TQ), lambda b, c: (b, c // QPK, row_q // NSA_W, c % QPK)),
            pl.BlockSpec((1, G, NC, HEAD_DIM), lambda b, c: (b, 0, 0, 0)),
            pl.BlockSpec((1, G, HEAD_DIM, NC), lambda b, c: (b, 0, 0, 0)),
            _const_spec(tcmp.shape),
            _const_spec(ovl.shape),
            pl.BlockSpec((1, S, LANES), lambda b, c: (b, 0, col_ksl // LANES)),
            pl.BlockSpec((1, S // TK, kvrows, TK), lambda b, c: (b, 0, row_vsl // kvrows, 0)),
            pl.BlockSpec((1, S, LANES), lambda b, c: (b, 0, col_kwn // LANES)),
            pl.BlockSpec((1, S // TK, kvrows, TK), lambda b, c: (b, 0, row_vwn // kvrows, 0)),
            pl.BlockSpec((H, tslc.shape[1], TB, TQ), lambda b, c: (MOBA_HEADS // H, 0, 0, 0), pipeline_mode=one),
            _const_spec(twin.shape),
            pl.BlockSpec((1, G * GZ_ROWS, TQ), lambda b, c: (b, 0, c)),
        ],
        out_specs=pl.BlockSpec((1, NSA_W, TQ), lambda b, c: (b, 0, c)),
        out_shape=jax.ShapeDtypeStruct((B, NSA_W, S), BF16),
        scratch_shapes=[pltpu.VMEM((H, TK, TQ), F32), pltpu.VMEM((H, ACC_ROWS, TQ), F32),
                        pltpu.VMEM((H, TK, TQ), F32), pltpu.VMEM((H, ACC_ROWS, TQ), F32),
                        pltpu.VMEM((G, S // TK, SUBLANES, TQ), F32), pltpu.VMEM((H, HEAD_DIM, TQ), F32)],
        compiler_params=_params(("parallel", "arbitrary")),
        name="nsa_attn",
    )(fm, kc, vct, tcmp, ovl, rm, fm, rm, fm, tslc, twin, gz)


def _inproj1_body(x_ref, g_ref, wfm_ref, wk_ref, wf_ref, bf_ref, tri_ref, place_ref, fm_ref, ka_ref, carry_ref,
                  *, nst):
    i = pl.program_id(0)

    @pl.when(i % nst == 0)
    def _():
        carry_ref[...] = jnp.zeros_like(carry_ref)

    xf = _rmsnorm(x_ref[...], g_ref[...])
    xn = xf.astype(BF16)
    xlo = (xf - xn.astype(F32)).astype(BF16)
    for r0 in range(0, fm_ref.shape[2], CH):
        res = _dot_nt(wfm_ref[r0:r0 + CH, :], xn).astype(BF16)
        for t in range(TM // TK):
            fm_ref[0, t, r0:r0 + CH, :] = res[:, t * TK:(t + 1) * TK]

    fz = _dot(xn, wf_ref[0]) + _dot(xlo, wf_ref[0]) + _dot(xn, wf_ref[1]) + bf_ref[...]
    logf = jnp.minimum(fz, 0.0) - jnp.log(1.0 + jnp.exp(-jnp.abs(fz)))
    tri = tri_ref[...]
    h1, h2, h3 = _split3(logf)
    cum = _dot(tri, h1) + _dot(tri, h2) + _dot(tri, h3) + carry_ref[0:1, :]
    carry_ref[...] = jnp.broadcast_to(cum[TM - 1:TM, :], carry_ref.shape)
    cc = jnp.concatenate(_split3(cum * LOG2E), axis=1)
    low = lax.broadcasted_iota(jnp.int32, (TM, LANES), 1) < HEAD_DIM
    for c0 in range(0, wk_ref.shape[-1], FF_CH):
        kp = _dot(xn, wk_ref[:, c0:c0 + FF_CH])
        dp = _dot(cc, place_ref[:, c0:c0 + FF_CH])
        for t0 in range(0, FF_CH, LANES):
            j = (c0 + t0) // LANES
            kt, dt = kp[:, t0:t0 + LANES], dp[:, t0:t0 + LANES]
            ka_ref[:, 2 * j * LANES:(2 * j + 1) * LANES] = jnp.where(low, kt, dt).astype(BF16)
            ka_ref[:, (2 * j + 1) * LANES:(2 * j + 2) * LANES] = jnp.where(low, dt, kt).astype(BF16)


def _inproj1(x2, g, wfm, wk, wf, bf, tri, place, B, S):
    M = B * S
    nst = S // TM
    n_fm, n_ka = wfm.shape[0], FOX_HEADS * LANES
    return pl.pallas_call(
        functools.partial(_inproj1_body, nst=nst),
        grid=(M // TM,),
        in_specs=[
            pl.BlockSpec((TM, D_MODEL), lambda i: (i, 0)),
            _const_spec((1, D_MODEL)),
            _const_spec(wfm.shape),
            _const_spec(wk.shape),
            _const_spec(wf.shape),
            _const_spec(bf.shape),
            _const_spec(tri.shape),
            _const_spec(place.shape),
        ],
        out_specs=[
            pl.BlockSpec((1, TM // TK, n_fm, TK), lambda i: (i // nst, i % nst, 0, 0)),
            pl.BlockSpec((TM, n_ka), lambda i: (i, 0)),
        ],
        out_shape=[
            jax.ShapeDtypeStruct((B, S // TK, n_fm, TK), BF16),
            jax.ShapeDtypeStruct((M, n_ka), BF16),
        ],
        scratch_shapes=[pltpu.VMEM((8, LANES), F32)],
        compiler_params=_params(("arbitrary",)),
        name="inproj1",
    )(x2, g, wfm, wk, wf, bf, tri, place)


def _fox_body(q_ref, k_ref, v_ref, cm_ref, o_ref, s_scr, acc_scr):
    c = pl.program_id(2)
    blk = c // QPK
    q = q_ref[0, 0]
    ones = jnp.ones((LANES - HEAD_DIM, TQ), BF16)
    qhs = [q[h * HEAD_DIM:(h + 1) * HEAD_DIM, :] for h in range(FOX_HPS)]
    qas = [jnp.concatenate([qhs[h], ones] if h % 2 == 0 else [ones, qhs[h]], axis=0) for h in range(FOX_HPS)]

    def qk_fn(h, n):
        rows = pl.ds(pl.multiple_of(n * TK, TK), TK)
        return _dot(k_ref[0, rows, h * LANES:(h + 1) * LANES], qas[h])

    causal = cm_ref[c % QPK]
    outs = _attend(blk + 1, lambda i: blk - i, qk_fn, None,
                   lambda h, n: v_ref[0, n, h * HEAD_DIM:(h + 1) * HEAD_DIM, :], s_scr, acc_scr, FOX_HPS,
                   first_fix=lambda h, n, s: s + causal)
    o_ref[0] = jnp.concatenate(outs, axis=0).astype(BF16)


def _fox(fm, ka, cmask, B, S):
    rows = FOX_HPS * HEAD_DIM
    return pl.pallas_call(
        _fox_body,
        grid=(B, FOX_HEADS // FOX_HPS, S // TQ),
        in_specs=[
            pl.BlockSpec((1, 1, rows, TQ), lambda b, h, c: (b, c // QPK, h, c % QPK)),
            pl.BlockSpec((1, S, FOX_HPS * LANES), lambda b, h, c: (b, 0, h), pipeline_mode=pl.Buffered(1)),
            pl.BlockSpec((1, S // TK, rows, TK), lambda b, h, c: (b, 0, FOX_HEADS // FOX_HPS + h, 0),
                         pipeline_mode=pl.Buffered(1)),
            _const_spec(cmask.shape),
        ],
        out_specs=pl.BlockSpec((1, rows, TQ), lambda b, h, c: (b, h, c)),
        out_shape=jax.ShapeDtypeStruct((B, FOX_W, S), BF16),
        scratch_shapes=[pltpu.VMEM((FOX_HPS, TK, TQ), F32), pltpu.VMEM((FOX_HPS, ACC_ROWS, TQ), F32)],
        compiler_params=_params(("parallel", "parallel", "arbitrary")),
        name="fox_attn",
    )(fm, ka, fm, cmask)


def _post_body(*refs, n_parts, final):
    o_refs = refs[:n_parts]
    h_ref, wo_ref, g_ref, w1_ref, w2_ref = refs[n_parts:n_parts + 5]
    gf_ref = refs[n_parts + 5] if final else None
    out_ref, hn_ref = refs[-2:]
    h1 = h_ref[...]
    r0 = 0
    for o_ref in o_refs:
        nf = o_ref.shape[1]
        h1 = h1 + _dot_tn(o_ref[0], wo_ref[r0:r0 + nf, :])
        r0 += nf
    out_ref[...] = h1
    hn_ref[...] = _rmsnorm(out_ref[...], g_ref[...]).astype(BF16)
    for c0 in range(0, D_FF, FF_CH):
        a = jnp.maximum(_dot(hn_ref[...], w1_ref[:, c0:c0 + FF_CH]), 0.0)
        out_ref[...] += _dot((a * a).astype(BF16), w2_ref[c0:c0 + FF_CH, :])
    if final:
        out_ref[...] = _rmsnorm(out_ref[...], gf_ref[...])


def _post(o_parts, h2, wo, g, w1, w2, gf, B, S):
    M = B * S
    nst = S // TM
    final = gf is not None
    in_specs = [pl.BlockSpec((1, o.shape[1], TM), lambda i: (i // nst, 0, i % nst)) for o in o_parts]
    in_specs += [
        pl.BlockSpec((TM, D_MODEL), lambda i: (i, 0)),
        _const_spec(wo.shape),
        _const_spec((1, D_MODEL)),
        _const_spec(w1.shape),
        _const_spec(w2.shape),
    ]
    args = list(o_parts) + [h2, wo, g, w1, w2]
    if final:
        in_specs.append(_const_spec((1, D_MODEL)))
        args.append(gf)
    return pl.pallas_call(
        functools.partial(_post_body, n_parts=len(o_parts), final=final),
        grid=(M // TM,),
        in_specs=in_specs,
        out_specs=pl.BlockSpec((TM, D_MODEL), lambda i: (i, 0)),
        out_shape=jax.ShapeDtypeStruct((M, D_MODEL), F32),
        scratch_shapes=[pltpu.VMEM((TM, D_MODEL), BF16)],
        compiler_params=_params(("parallel",)),
        name="post_final" if final else "post",
    )(*args)


def _rel_bucket(dist):
    n = jnp.maximum(dist, 0)
    max_exact = REL_BUCKETS // 2
    nf = jnp.maximum(n, 1).astype(jnp.float32)
    large = max_exact + (jnp.log(nf / max_exact) / math.log(REL_MAX_DISTANCE / max_exact)
                         * (REL_BUCKETS - max_exact)).astype(jnp.int32)
    large = jnp.minimum(large, REL_BUCKETS - 1)
    return jnp.where(n < max_exact, n, large)


def _bias_tables(rel_bias, S):
    n_heads = rel_bias.shape[1]
    table = rel_bias.T * LOG2E

    def bias_of(dist):
        tab = table.reshape((n_heads, REL_BUCKETS) + (1,) * dist.ndim)
        bkt = _rel_bucket(jnp.asarray(dist))[None]
        out = jnp.zeros((n_heads,) + dist.shape, F32)
        for b in range(REL_BUCKETS):
            out = jnp.where(bkt == b, tab[:, b], out)
        return jnp.where(jnp.asarray(dist)[None] >= 0, out, NEG_INF)

    d = (np.arange(-1, NE_BIAS)[:, None, None] * TQ + np.arange(TQ)[None, None, :] - np.arange(TB)[None, :, None])
    vals = bias_of(d)
    far = table[:, REL_BUCKETS - 1][:, None, None, None]
    tile = jnp.where(d >= 0, vals - far, NEG_INF)
    tile = jnp.concatenate([tile, jnp.zeros_like(tile[:, :1])], axis=1)
    dw = d[:NE_WIN + 2]
    twin = jnp.where((dw >= 0) & (dw < NSA_WINDOW), vals[MOBA_HEADS:, :NE_WIN + 2], NEG_INF)
    n_c = S // NSA_CMP_STRIDE
    u = np.arange(2 * n_c)[:, None]
    dc = np.arange(TQ)[None, :] + NSA_CMP_STRIDE * (u - (n_c - 1)) - (NSA_CMP_BLOCK - 1)
    fcmp = bias_of(dc)[MOBA_HEADS:]
    return tile, twin, fcmp


def _selection_constants(S):
    n_c = S // NSA_CMP_STRIDE
    n_cmp = (S - NSA_CMP_BLOCK) // NSA_CMP_STRIDE + 1
    n_sb = S // NSA_SLC_BLOCK
    ci = np.arange(n_c)[None, :] * NSA_CMP_STRIDE
    sj = np.arange(n_sb)[:, None] * NSA_SLC_BLOCK
    ovl = (ci < sj + NSA_SLC_BLOCK) & (ci + NSA_CMP_BLOCK > sj) & (np.arange(n_c)[None, :] < n_cmp)
    ovl = ovl[:, ::-1]
    return jnp.asarray(ovl, BF16)


def _causal_tiles():
    e = np.arange(QPK)[:, None, None]
    d = e * TQ + np.arange(TQ)[None, None, :] - np.arange(TK)[None, :, None]
    return jnp.asarray(np.where(d >= 0, 0.0, NEG_INF), F32)


def kernel(x, rel_bias, mix_norm, mlp_norm, even_w_in, even_w_out, cmp_pos_k, cmp_pos_v, cmp_k_w1, cmp_k_w2,
           cmp_v_w1, cmp_v_w2, odd_w_in, odd_b_forget, odd_w_out, mlp_w1, mlp_w2, final_norm):
    B, S, D = x.shape
    assert D == D_MODEL and S % TM == 0
    G, J = NSA_KV_GROUPS, NSA_HPG
    h = x.reshape(B * S, D)

    offs = np.cumsum((MOBA_W, MOBA_W, MOBA_W, NSA_W) + (NSA_KV_W,) * 6)
    mq_w, mk_w, mv_w, nq_w, kc_w, vc_w, ksl_w, vsl_w, kwn_w, vwn_w, gz_w = jnp.split(even_w_in[0], offs, axis=1)
    wrm = jnp.concatenate([mk_w, ksl_w, kwn_w], axis=1).astype(BF16)
    col_ksl, col_kwn = MOBA_W, MOBA_W + NSA_KV_W
    wcv = jnp.concatenate([kc_w, vc_w], axis=1).astype(BF16)
    qs = SCALE * LOG2E
    wfm = jnp.concatenate([mq_w * qs, mv_w, nq_w * qs, vsl_w, vwn_w], axis=1).T.astype(BF16)
    row_nq, row_vsl, row_vwn = 2 * MOBA_W, 2 * MOBA_W + NSA_W, 2 * MOBA_W + NSA_W + NSA_KV_W
    gzw = gz_w.T.reshape(G, J, 3, D).transpose(0, 2, 1, 3).reshape(G, 3 * J, D)
    gzw = jnp.pad(gzw, ((0, 0), (0, GZ_ROWS - 3 * J), (0, 0))).reshape(G * GZ_ROWS, D).astype(BF16)

    rm, fm, gz, cv = _inproj0(h, mix_norm[0][None, :], wrm, wfm, gzw, wcv, B, S)
    rm = rm.reshape(B, S, -1)

    tile, twin, tcmp = _bias_tables(rel_bias, S)
    ovl = _selection_constants(S)

    o_moba = _moba(fm, rm, tile, B, S)

    n_c = S // NSA_CMP_STRIDE
    r = cv.reshape(2 * G, B, n_c, NSA_CMP_STRIDE * HEAD_DIM)
    pos = jnp.stack([cmp_pos_k[0].reshape(1, -1), cmp_pos_v[0].reshape(1, -1)])
    pos = jnp.pad(pos, ((0, 0), (0, 7), (0, 0))).astype(BF16)
    w1c = jnp.stack([cmp_k_w1[0], cmp_v_w1[0]]).astype(BF16)
    kc, vct = _compress(r, pos, w1c, cmp_k_w2[0].astype(BF16), cmp_v_w2[0].T.astype(BF16), B, n_c)

    o_nsa = _nsa(fm, rm, gz, kc, vct, tcmp, ovl, tile, twin, B, S,
                 col_ksl, col_kwn, row_nq, row_vsl, row_vwn)

    h = _post([o_moba, o_nsa], h, even_w_out[0].astype(BF16), mlp_norm[0][None, :],
              mlp_w1[0].astype(BF16), mlp_w2[0].astype(BF16), None, B, S)

    q_w, k_w, v_w, f_w = jnp.split(odd_w_in[0], np.cumsum((FOX_W, FOX_W, FOX_W)), axis=1)
    wfm1 = jnp.concatenate([q_w * qs, v_w], axis=1).T.astype(BF16)
    wk = k_w.astype(BF16)
    f_w = jnp.pad(f_w, ((0, 0), (0, LANES - FOX_HEADS)))
    f_hi = f_w.astype(BF16)
    wf = jnp.stack([f_hi, (f_w - f_hi.astype(F32)).astype(BF16)])
    bf = jnp.pad(odd_b_forget[0], (0, LANES - FOX_HEADS))[None, :]
    tri = jnp.asarray(np.tril(np.ones((TM, TM))), BF16)
    place = np.zeros((3 * LANES, FOX_W), np.float32)
    heads = np.arange(FOX_HEADS)
    for term in range(3):
        place[term * LANES + heads, (heads // 2) * LANES + (1 - heads % 2) * HEAD_DIM + term] = -1.0
    fm1, ka = _inproj1(h, mix_norm[1][None, :], wfm1, wk, wf, bf, tri, jnp.asarray(place, BF16), B, S)
    o_fox = _fox(fm1, ka.reshape(B, S, -1), _causal_tiles(), B, S)

    h = _post([o_fox], h, odd_w_out[0].astype(BF16), mlp_norm[1][None, :],
              mlp_w1[1].astype(BF16), mlp_w2[1].astype(BF16), final_norm[None, :], B, S)
    return h.reshape(B, S, D)
```
